```python
import math, functools
import jax, jax.numpy as jnp
from jax import lax
import numpy as np

D_MODEL = 2048
BATCH = 8
SEQ = 4096
DEPTH = 4

N_MIXERS = 3
EPS = 1e-6
CONF_KERNEL_WIDTH = 31
POOL_WINDOWS = (2, 4, 8, 16)
N_POOL_GROUPS = len(POOL_WINDOWS)
POOL_GROUP_DIM = D_MODEL // N_POOL_GROUPS
HEAD_DIM = 64
N_HEADS = D_MODEL // HEAD_DIM
N_KV_HEADS = N_HEADS // 8
GROUP_SIZE = N_HEADS // N_KV_HEADS
WINDOW = 128
BLOCK = 128
ROT_DIM = HEAD_DIM // 4
ROPE_THETA = 500000.0
D_FF = 5632
FFN_CONV_WIDTH = 3

kernel_name = "hybrid_conv_pool_swa_trunk"


def rms_norm(x, g):
    xf = x.astype(jnp.float32)
    y = xf * lax.rsqrt(jnp.mean(xf * xf, axis=-1, keepdims=True) + EPS)
    return (y * g.astype(jnp.float32)).astype(x.dtype)


def layer_norm(x, g, b):
    xf = x.astype(jnp.float32)
    mu = jnp.mean(xf, axis=-1, keepdims=True)
    xc = xf - mu
    var = jnp.mean(xc * xc, axis=-1, keepdims=True)
    y = xc * lax.rsqrt(var + EPS) * g.astype(jnp.float32) + b.astype(jnp.float32)
    return y.astype(x.dtype)


def causal_depthwise_conv(x, w, b):
    width, channels = w.shape
    y = lax.conv_general_dilated(
        x, w[:, None, :].astype(x.dtype), window_strides=(1,), padding=[(width - 1, 0)],
        dimension_numbers=("NWC", "WIO", "NWC"), feature_group_count=channels)
    return y + b.astype(x.dtype)


def conformer_conv_module(h, w_in, b_in, dw_w, dw_b, ln_g, ln_b, w_out, b_out):
    u = h @ w_in + b_in
    a, gate = jnp.split(u, 2, axis=-1)
    u = a * jax.nn.sigmoid(gate)
    u = causal_depthwise_conv(u, dw_w, dw_b)
    u = jax.nn.silu(layer_norm(u, ln_g, ln_b))
    return u @ w_out + b_out


def multiscale_pool_mixer(h, w_group, scale):
    bsz, seq, dim = h.shape
    hf = h.astype(jnp.float32)
    cs = jnp.concatenate([jnp.zeros((bsz, 1, dim), jnp.float32), jnp.cumsum(hf, axis=1)], axis=1)
    upper = cs[:, 1:]
    t = jnp.arange(seq)
    pooled = []
    for g, w in enumerate(POOL_WINDOWS):
        sl = slice(g * POOL_GROUP_DIM, (g + 1) * POOL_GROUP_DIM)
        lower = jnp.concatenate(
            [jnp.zeros((bsz, w - 1, POOL_GROUP_DIM), jnp.float32), cs[:, :seq + 1 - w, sl]], axis=1)
        count = jnp.minimum(t + 1, w).astype(jnp.float32)[None, :, None]
        pooled.append((upper[..., sl] - lower) / count)
    mixed = (jnp.concatenate(pooled, axis=-1) - hf).astype(h.dtype)
    mixed = mixed.reshape(bsz, seq, N_POOL_GROUPS, POOL_GROUP_DIM)
    y = jnp.einsum("bsgc,gcd->bsgd", mixed, w_group).reshape(bsz, seq, dim)
    return y * scale


def apply_partial_rotary(x, cos, sin):
    half = ROT_DIM // 2
    xr = x[..., :ROT_DIM].astype(jnp.float32)
    x1, x2 = xr[..., :half], xr[..., half:]
    rot = jnp.concatenate([x1 * cos - x2 * sin, x2 * cos + x1 * sin], axis=-1)
    return jnp.concatenate([rot.astype(x.dtype), x[..., ROT_DIM:]], axis=-1)


def banded_sink_attention(q, k, v, sinks):
    bsz, seq = q.shape[:2]
    nb = seq // BLOCK
    scale = 1.0 / math.sqrt(HEAD_DIM)
    qb = q.reshape(bsz, nb, BLOCK, N_KV_HEADS, GROUP_SIZE, HEAD_DIM).transpose(1, 0, 3, 4, 2, 5)

    def band(t):
        tb = t.reshape(bsz, nb, BLOCK, N_KV_HEADS, HEAD_DIM)
        prev = jnp.concatenate([jnp.zeros_like(tb[:, :1]), tb[:, :-1]], axis=1)
        return jnp.concatenate([prev, tb], axis=2).transpose(1, 0, 3, 2, 4)

    kb, vb = band(k), band(v)
    qi = jnp.arange(BLOCK)[:, None]
    kj = jnp.arange(2 * BLOCK)[None, :]
    diff = qi + BLOCK - kj
    in_window = (diff >= 0) & (diff < WINDOW)
    sink = sinks.astype(jnp.float32).reshape(N_KV_HEADS, GROUP_SIZE)[None, :, :, None, None]

    def block_fn(args):
        n, qn, kn, vn = args
        s = jnp.einsum("bkgqd,bkjd->bkgqj", qn.astype(jnp.float32), kn.astype(jnp.float32)) * scale
        valid = in_window & ((n * BLOCK + kj - BLOCK) >= 0)
        s = jnp.where(valid, s, -jnp.inf)
        m = jnp.maximum(jnp.max(s, axis=-1, keepdims=True), sink)
        p = jnp.exp(s - m)
        denom = jnp.sum(p, axis=-1, keepdims=True) + jnp.exp(sink - m)
        o = jnp.einsum("bkgqj,bkjd->bkgqd", p, vn.astype(jnp.float32)) / denom
        return o.astype(q.dtype)

    out = lax.map(block_fn, (jnp.arange(nb), qb, kb, vb))
    return out.transpose(1, 0, 4, 2, 3, 5).reshape(bsz, seq, N_HEADS * HEAD_DIM)


def swa_sink_attention(h, positions, w_qkv, q_norm_g, k_norm_g, sinks, w_o):
    bsz, seq, _ = h.shape
    qkv = h @ w_qkv
    q, k, v = jnp.split(qkv, [N_HEADS * HEAD_DIM, (N_HEADS + N_KV_HEADS) * HEAD_DIM], axis=-1)
    q = rms_norm(q.reshape(bsz, seq, N_HEADS, HEAD_DIM), q_norm_g)
    k = rms_norm(k.reshape(bsz, seq, N_KV_HEADS, HEAD_DIM), k_norm_g)
    v = v.reshape(bsz, seq, N_KV_HEADS, HEAD_DIM)
    inv_freq = ROPE_THETA ** (-jnp.arange(0, ROT_DIM, 2, dtype=jnp.float32) / ROT_DIM)
    ang = positions.astype(jnp.float32)[..., None] * inv_freq
    cos, sin = jnp.cos(ang)[:, :, None, :], jnp.sin(ang)[:, :, None, :]
    q = apply_partial_rotary(q, cos, sin)
    k = apply_partial_rotary(k, cos, sin)
    o = banded_sink_attention(q, k, v, sinks)
    return o @ w_o


def conv_gated_mlp(h, w_up, dw_w, dw_b, w_down):
    u = causal_depthwise_conv(h @ w_up, dw_w, dw_b)
    gate, val = jnp.split(u, 2, axis=-1)
    return (jax.nn.silu(gate) * val) @ w_down


def _fwd_setup_inputs(seed: int = 0) -> dict:
    key = jax.random.key(seed)
    keys = iter(jax.random.split(key, 64))

    def nrm(shape, scale):
        return jax.random.normal(next(keys), shape, jnp.float32) * scale

    def gain(shape):
        return 1.0 + nrm(shape, 0.02)

    d = D_MODEL
    out = {}
    out["x"] = nrm((BATCH, SEQ, d), 1.0)
    offset = jax.random.randint(next(keys), (BATCH, 1), 0, 4096, dtype=jnp.int32)
    out["positions"] = (jnp.arange(SEQ, dtype=jnp.int32)[None, :] + offset).astype(jnp.int32)

    def add_conformer(p):
        out[p + "norm_g"] = gain((d,))
        out[p + "a_w_in"] = nrm((d, 2 * d), d ** -0.5)
        out[p + "a_b_in"] = nrm((2 * d,), 0.02)
        out[p + "a_dw_w"] = nrm((CONF_KERNEL_WIDTH, d), CONF_KERNEL_WIDTH ** -0.5)
        out[p + "a_dw_b"] = nrm((d,), 0.02)
        out[p + "a_ln_g"] = gain((d,))
        out[p + "a_ln_b"] = nrm((d,), 0.02)
        out[p + "a_w_out"] = nrm((d, d), d ** -0.5)
        out[p + "a_b_out"] = nrm((d,), 0.02)

    def add_ffn(p):
        out[p + "ffn_norm_g"] = gain((d,))
        out[p + "ffn_w_up"] = nrm((d, 2 * D_FF), d ** -0.5)
        out[p + "ffn_dw_w"] = nrm((FFN_CONV_WIDTH, 2 * D_FF), FFN_CONV_WIDTH ** -0.5)
        out[p + "ffn_dw_b"] = nrm((2 * D_FF,), 0.02)
        out[p + "ffn_w_down"] = nrm((D_FF, d), D_FF ** -0.5)

    add_conformer("l0_")
    add_ffn("l0_")
    out["l1_norm_g"] = gain((d,))
    out["l1_b_w_group"] = nrm((N_POOL_GROUPS, POOL_GROUP_DIM, POOL_GROUP_DIM), POOL_GROUP_DIM ** -0.5)
    out["l1_b_scale"] = 1.0 + nrm((d,), 0.1)
    add_ffn("l1_")
    out["l2_norm_g"] = gain((d,))
    out["l2_c_w_qkv"] = nrm((d, (N_HEADS + 2 * N_KV_HEADS) * HEAD_DIM), d ** -0.5)
    out["l2_c_q_norm_g"] = gain((HEAD_DIM,))
    out["l2_c_k_norm_g"] = gain((HEAD_DIM,))
    out["l2_c_sinks"] = nrm((N_HEADS,), 1.0)
    out["l2_c_w_o"] = nrm((N_HEADS * HEAD_DIM, d), (N_HEADS * HEAD_DIM) ** -0.5)
    add_ffn("l2_")
    add_conformer("l3_")
    add_ffn("l3_")
    return out


def _fwd_reference(x, positions,
              l0_norm_g, l0_a_w_in, l0_a_b_in, l0_a_dw_w, l0_a_dw_b, l0_a_ln_g, l0_a_ln_b, l0_a_w_out, l0_a_b_out,
              l0_ffn_norm_g, l0_ffn_w_up, l0_ffn_dw_w, l0_ffn_dw_b, l0_ffn_w_down,
              l1_norm_g, l1_b_w_group, l1_b_scale,
              l1_ffn_norm_g, l1_ffn_w_up, l1_ffn_dw_w, l1_ffn_dw_b, l1_ffn_w_down,
              l2_norm_g, l2_c_w_qkv, l2_c_q_norm_g, l2_c_k_norm_g, l2_c_sinks, l2_c_w_o,
              l2_ffn_norm_g, l2_ffn_w_up, l2_ffn_dw_w, l2_ffn_dw_b, l2_ffn_w_down,
              l3_norm_g, l3_a_w_in, l3_a_b_in, l3_a_dw_w, l3_a_dw_b, l3_a_ln_g, l3_a_ln_b, l3_a_w_out, l3_a_b_out,
              l3_ffn_norm_g, l3_ffn_w_up, l3_ffn_dw_w, l3_ffn_dw_b, l3_ffn_w_down):
    mixers = [
        lambda h: conformer_conv_module(h, l0_a_w_in, l0_a_b_in, l0_a_dw_w, l0_a_dw_b,
                                        l0_a_ln_g, l0_a_ln_b, l0_a_w_out, l0_a_b_out),
        lambda h: multiscale_pool_mixer(h, l1_b_w_group, l1_b_scale),
        lambda h: swa_sink_attention(h, positions, l2_c_w_qkv, l2_c_q_norm_g, l2_c_k_norm_g,
                                     l2_c_sinks, l2_c_w_o),
        lambda h: conformer_conv_module(h, l3_a_w_in, l3_a_b_in, l3_a_dw_w, l3_a_dw_b,
                                        l3_a_ln_g, l3_a_ln_b, l3_a_w_out, l3_a_b_out),
    ]
    mixer_norms = [l0_norm_g, l1_norm_g, l2_norm_g, l3_norm_g]
    ffns = [
        (l0_ffn_norm_g, l0_ffn_w_up, l0_ffn_dw_w, l0_ffn_dw_b, l0_ffn_w_down),
        (l1_ffn_norm_g, l1_ffn_w_up, l1_ffn_dw_w, l1_ffn_dw_b, l1_ffn_w_down),
        (l2_ffn_norm_g, l2_ffn_w_up, l2_ffn_dw_w, l2_ffn_dw_b, l2_ffn_w_down),
        (l3_ffn_norm_g, l3_ffn_w_up, l3_ffn_dw_w, l3_ffn_dw_b, l3_ffn_w_down),
    ]
    for i in range(DEPTH):
        x = x + mixers[i](rms_norm(x, mixer_norms[i]))
        g, w_up, dw_w, dw_b, w_down = ffns[i]
        x = x + conv_gated_mlp(rms_norm(x, g), w_up, dw_w, dw_b, w_down)
    return x


import jax as _jax
import jax.numpy as _jnp

TWIN_FORMAT = 'train_step'
FWD_PARAMS = ['x', 'positions', 'l0_norm_g', 'l0_a_w_in', 'l0_a_b_in', 'l0_a_dw_w', 'l0_a_dw_b', 'l0_a_ln_g', 'l0_a_ln_b', 'l0_a_w_out', 'l0_a_b_out', 'l0_ffn_norm_g', 'l0_ffn_w_up', 'l0_ffn_dw_w', 'l0_ffn_dw_b', 'l0_ffn_w_down', 'l1_norm_g', 'l1_b_w_group', 'l1_b_scale', 'l1_ffn_norm_g', 'l1_ffn_w_up', 'l1_ffn_dw_w', 'l1_ffn_dw_b', 'l1_ffn_w_down', 'l2_norm_g', 'l2_c_w_qkv', 'l2_c_q_norm_g', 'l2_c_k_norm_g', 'l2_c_sinks', 'l2_c_w_o', 'l2_ffn_norm_g', 'l2_ffn_w_up', 'l2_ffn_dw_w', 'l2_ffn_dw_b', 'l2_ffn_w_down', 'l3_norm_g', 'l3_a_w_in', 'l3_a_b_in', 'l3_a_dw_w', 'l3_a_dw_b', 'l3_a_ln_g', 'l3_a_ln_b', 'l3_a_w_out', 'l3_a_b_out', 'l3_ffn_norm_g', 'l3_ffn_w_up', 'l3_ffn_dw_w', 'l3_ffn_dw_b', 'l3_ffn_w_down']
TWIN_WEIGHTS = ['l0_norm_g', 'l0_a_w_in', 'l0_a_b_in', 'l0_a_dw_w', 'l0_a_dw_b', 'l0_a_ln_g', 'l0_a_ln_b', 'l0_a_w_out', 'l0_a_b_out', 'l0_ffn_norm_g', 'l0_ffn_w_up', 'l0_ffn_dw_w', 'l0_ffn_dw_b', 'l0_ffn_w_down', 'l1_norm_g', 'l1_b_w_group', 'l1_b_scale', 'l1_ffn_norm_g', 'l1_ffn_w_up', 'l1_ffn_dw_w', 'l1_ffn_dw_b', 'l1_ffn_w_down', 'l2_norm_g', 'l2_c_w_qkv', 'l2_c_q_norm_g', 'l2_c_k_norm_g', 'l2_c_sinks', 'l2_c_w_o', 'l2_ffn_norm_g', 'l2_ffn_w_up', 'l2_ffn_dw_w', 'l2_ffn_dw_b', 'l2_ffn_w_down', 'l3_norm_g', 'l3_a_w_in', 'l3_a_b_in', 'l3_a_dw_w', 'l3_a_dw_b', 'l3_a_ln_g', 'l3_a_ln_b', 'l3_a_w_out', 'l3_a_b_out', 'l3_ffn_norm_g', 'l3_ffn_w_up', 'l3_ffn_dw_w', 'l3_ffn_dw_b', 'l3_ffn_w_down']
TWIN_DIFF_INPUT = 'x'
TWIN_INPUTS = ['x', 'positions', 'l0_norm_g', 'l0_a_w_in', 'l0_a_b_in', 'l0_a_dw_w', 'l0_a_dw_b', 'l0_a_ln_g', 'l0_a_ln_b', 'l0_a_w_out', 'l0_a_b_out', 'l0_ffn_norm_g', 'l0_ffn_w_up', 'l0_ffn_dw_w', 'l0_ffn_dw_b', 'l0_ffn_w_down', 'l1_norm_g', 'l1_b_w_group', 'l1_b_scale', 'l1_ffn_norm_g', 'l1_ffn_w_up', 'l1_ffn_dw_w', 'l1_ffn_dw_b', 'l1_ffn_w_down', 'l2_norm_g', 'l2_c_w_qkv', 'l2_c_q_norm_g', 'l2_c_k_norm_g', 'l2_c_sinks', 'l2_c_w_o', 'l2_ffn_norm_g', 'l2_ffn_w_up', 'l2_ffn_dw_w', 'l2_ffn_dw_b', 'l2_ffn_w_down', 'l3_norm_g', 'l3_a_w_in', 'l3_a_b_in', 'l3_a_dw_w', 'l3_a_dw_b', 'l3_a_ln_g', 'l3_a_ln_b', 'l3_a_w_out', 'l3_a_b_out', 'l3_ffn_norm_g', 'l3_ffn_w_up', 'l3_ffn_dw_w', 'l3_ffn_dw_b', 'l3_ffn_w_down', 'loss_target', 'm_l0_norm_g', 'm_l0_a_w_in', 'm_l0_a_b_in', 'm_l0_a_dw_w', 'm_l0_a_dw_b', 'm_l0_a_ln_g', 'm_l0_a_ln_b', 'm_l0_a_w_out', 'm_l0_a_b_out', 'm_l0_ffn_norm_g', 'm_l0_ffn_w_up', 'm_l0_ffn_dw_w', 'm_l0_ffn_dw_b', 'm_l0_ffn_w_down', 'm_l1_norm_g', 'm_l1_b_w_group', 'm_l1_b_scale', 'm_l1_ffn_norm_g', 'm_l1_ffn_w_up', 'm_l1_ffn_dw_w', 'm_l1_ffn_dw_b', 'm_l1_ffn_w_down', 'm_l2_norm_g', 'm_l2_c_w_qkv', 'm_l2_c_q_norm_g', 'm_l2_c_k_norm_g', 'm_l2_c_sinks', 'm_l2_c_w_o', 'm_l2_ffn_norm_g', 'm_l2_ffn_w_up', 'm_l2_ffn_dw_w', 'm_l2_ffn_dw_b', 'm_l2_ffn_w_down', 'm_l3_norm_g', 'm_l3_a_w_in', 'm_l3_a_b_in', 'm_l3_a_dw_w', 'm_l3_a_dw_b', 'm_l3_a_ln_g', 'm_l3_a_ln_b', 'm_l3_a_w_out', 'm_l3_a_b_out', 'm_l3_ffn_norm_g', 'm_l3_ffn_w_up', 'm_l3_ffn_dw_w', 'm_l3_ffn_dw_b', 'm_l3_ffn_w_down', 'v_l0_norm_g', 'v_l0_a_w_in', 'v_l0_a_b_in', 'v_l0_a_dw_w', 'v_l0_a_dw_b', 'v_l0_a_ln_g', 'v_l0_a_ln_b', 'v_l0_a_w_out', 'v_l0_a_b_out', 'v_l0_ffn_norm_g', 'v_l0_ffn_w_up', 'v_l0_ffn_dw_w', 'v_l0_ffn_dw_b', 'v_l0_ffn_w_down', 'v_l1_norm_g', 'v_l1_b_w_group', 'v_l1_b_scale', 'v_l1_ffn_norm_g', 'v_l1_ffn_w_up', 'v_l1_ffn_dw_w', 'v_l1_ffn_dw_b', 'v_l1_ffn_w_down', 'v_l2_norm_g', 'v_l2_c_w_qkv', 'v_l2_c_q_norm_g', 'v_l2_c_k_norm_g', 'v_l2_c_sinks', 'v_l2_c_w_o', 'v_l2_ffn_norm_g', 'v_l2_ffn_w_up', 'v_l2_ffn_dw_w', 'v_l2_ffn_dw_b', 'v_l2_ffn_w_down', 'v_l3_norm_g', 'v_l3_a_w_in', 'v_l3_a_b_in', 'v_l3_a_dw_w', 'v_l3_a_dw_b', 'v_l3_a_ln_g', 'v_l3_a_ln_b', 'v_l3_a_w_out', 'v_l3_a_b_out', 'v_l3_ffn_norm_g', 'v_l3_ffn_w_up', 'v_l3_ffn_dw_w', 'v_l3_ffn_dw_b', 'v_l3_ffn_w_down']
TWIN_OUTPUTS = ['loss', 'grad_x', 'grad_l0_norm_g', 'grad_l0_a_w_in', 'grad_l0_a_b_in', 'grad_l0_a_dw_w', 'grad_l0_a_dw_b', 'grad_l0_a_ln_g', 'grad_l0_a_ln_b', 'grad_l0_a_w_out', 'grad_l0_a_b_out', 'grad_l0_ffn_norm_g', 'grad_l0_ffn_w_up', 'grad_l0_ffn_dw_w', 'grad_l0_ffn_dw_b', 'grad_l0_ffn_w_down', 'grad_l1_norm_g', 'grad_l1_b_w_group', 'grad_l1_b_scale', 'grad_l1_ffn_norm_g', 'grad_l1_ffn_w_up', 'grad_l1_ffn_dw_w', 'grad_l1_ffn_dw_b', 'grad_l1_ffn_w_down', 'grad_l2_norm_g', 'grad_l2_c_w_qkv', 'grad_l2_c_q_norm_g', 'grad_l2_c_k_norm_g', 'grad_l2_c_sinks', 'grad_l2_c_w_o', 'grad_l2_ffn_norm_g', 'grad_l2_ffn_w_up', 'grad_l2_ffn_dw_w', 'grad_l2_ffn_dw_b', 'grad_l2_ffn_w_down', 'grad_l3_norm_g', 'grad_l3_a_w_in', 'grad_l3_a_b_in', 'grad_l3_a_dw_w', 'grad_l3_a_dw_b', 'grad_l3_a_ln_g', 'grad_l3_a_ln_b', 'grad_l3_a_w_out', 'grad_l3_a_b_out', 'grad_l3_ffn_norm_g', 'grad_l3_ffn_w_up', 'grad_l3_ffn_dw_w', 'grad_l3_ffn_dw_b', 'grad_l3_ffn_w_down', 'delta_l0_norm_g', 'delta_l0_a_w_in', 'delta_l0_a_b_in', 'delta_l0_a_dw_w', 'delta_l0_a_dw_b', 'delta_l0_a_ln_g', 'delta_l0_a_ln_b', 'delta_l0_a_w_out', 'delta_l0_a_b_out', 'delta_l0_ffn_norm_g', 'delta_l0_ffn_w_up', 'delta_l0_ffn_dw_w', 'delta_l0_ffn_dw_b', 'delta_l0_ffn_w_down', 'delta_l1_norm_g', 'delta_l1_b_w_group', 'delta_l1_b_scale', 'delta_l1_ffn_norm_g', 'delta_l1_ffn_w_up', 'delta_l1_ffn_dw_w', 'delta_l1_ffn_dw_b', 'delta_l1_ffn_w_down', 'delta_l2_norm_g', 'delta_l2_c_w_qkv', 'delta_l2_c_q_norm_g', 'delta_l2_c_k_norm_g', 'delta_l2_c_sinks', 'delta_l2_c_w_o', 'delta_l2_ffn_norm_g', 'delta_l2_ffn_w_up', 'delta_l2_ffn_dw_w', 'delta_l2_ffn_dw_b', 'delta_l2_ffn_w_down', 'delta_l3_norm_g', 'delta_l3_a_w_in', 'delta_l3_a_b_in', 'delta_l3_a_dw_w', 'delta_l3_a_dw_b', 'delta_l3_a_ln_g', 'delta_l3_a_ln_b', 'delta_l3_a_w_out', 'delta_l3_a_b_out', 'delta_l3_ffn_norm_g', 'delta_l3_ffn_w_up', 'delta_l3_ffn_dw_w', 'delta_l3_ffn_dw_b', 'delta_l3_ffn_w_down', 'new_m_l0_norm_g', 'new_m_l0_a_w_in', 'new_m_l0_a_b_in', 'new_m_l0_a_dw_w', 'new_m_l0_a_dw_b', 'new_m_l0_a_ln_g', 'new_m_l0_a_ln_b', 'new_m_l0_a_w_out', 'new_m_l0_a_b_out', 'new_m_l0_ffn_norm_g', 'new_m_l0_ffn_w_up', 'new_m_l0_ffn_dw_w', 'new_m_l0_ffn_dw_b', 'new_m_l0_ffn_w_down', 'new_m_l1_norm_g', 'new_m_l1_b_w_group', 'new_m_l1_b_scale', 'new_m_l1_ffn_norm_g', 'new_m_l1_ffn_w_up', 'new_m_l1_ffn_dw_w', 'new_m_l1_ffn_dw_b', 'new_m_l1_ffn_w_down', 'new_m_l2_norm_g', 'new_m_l2_c_w_qkv', 'new_m_l2_c_q_norm_g', 'new_m_l2_c_k_norm_g', 'new_m_l2_c_sinks', 'new_m_l2_c_w_o', 'new_m_l2_ffn_norm_g', 'new_m_l2_ffn_w_up', 'new_m_l2_ffn_dw_w', 'new_m_l2_ffn_dw_b', 'new_m_l2_ffn_w_down', 'new_m_l3_norm_g', 'new_m_l3_a_w_in', 'new_m_l3_a_b_in', 'new_m_l3_a_dw_w', 'new_m_l3_a_dw_b', 'new_m_l3_a_ln_g', 'new_m_l3_a_ln_b', 'new_m_l3_a_w_out', 'new_m_l3_a_b_out', 'new_m_l3_ffn_norm_g', 'new_m_l3_ffn_w_up', 'new_m_l3_ffn_dw_w', 'new_m_l3_ffn_dw_b', 'new_m_l3_ffn_w_down', 'new_v_l0_norm_g', 'new_v_l0_a_w_in', 'new_v_l0_a_b_in', 'new_v_l0_a_dw_w', 'new_v_l0_a_dw_b', 'new_v_l0_a_ln_g', 'new_v_l0_a_ln_b', 'new_v_l0_a_w_out', 'new_v_l0_a_b_out', 'new_v_l0_ffn_norm_g', 'new_v_l0_ffn_w_up', 'new_v_l0_ffn_dw_w', 'new_v_l0_ffn_dw_b', 'new_v_l0_ffn_w_down', 'new_v_l1_norm_g', 'new_v_l1_b_w_group', 'new_v_l1_b_scale', 'new_v_l1_ffn_norm_g', 'new_v_l1_ffn_w_up', 'new_v_l1_ffn_dw_w', 'new_v_l1_ffn_dw_b', 'new_v_l1_ffn_w_down', 'new_v_l2_norm_g', 'new_v_l2_c_w_qkv', 'new_v_l2_c_q_norm_g', 'new_v_l2_c_k_norm_g', 'new_v_l2_c_sinks', 'new_v_l2_c_w_o', 'new_v_l2_ffn_norm_g', 'new_v_l2_ffn_w_up', 'new_v_l2_ffn_dw_w', 'new_v_l2_ffn_dw_b', 'new_v_l2_ffn_w_down', 'new_v_l3_norm_g', 'new_v_l3_a_w_in', 'new_v_l3_a_b_in', 'new_v_l3_a_dw_w', 'new_v_l3_a_dw_b', 'new_v_l3_a_ln_g', 'new_v_l3_a_ln_b', 'new_v_l3_a_w_out', 'new_v_l3_a_b_out', 'new_v_l3_ffn_norm_g', 'new_v_l3_ffn_w_up', 'new_v_l3_ffn_dw_w', 'new_v_l3_ffn_dw_b', 'new_v_l3_ffn_w_down']
TWIN_LEAF_KINDS = {'loss': 'loss', 'grad_x': 'grad_x', 'grad_l0_norm_g': 'grad_w', 'grad_l0_a_w_in': 'grad_w', 'grad_l0_a_b_in': 'grad_w', 'grad_l0_a_dw_w': 'grad_w', 'grad_l0_a_dw_b': 'grad_w', 'grad_l0_a_ln_g': 'grad_w', 'grad_l0_a_ln_b': 'grad_w', 'grad_l0_a_w_out': 'grad_w', 'grad_l0_a_b_out': 'grad_w', 'grad_l0_ffn_norm_g': 'grad_w', 'grad_l0_ffn_w_up': 'grad_w', 'grad_l0_ffn_dw_w': 'grad_w', 'grad_l0_ffn_dw_b': 'grad_w', 'grad_l0_ffn_w_down': 'grad_w', 'grad_l1_norm_g': 'grad_w', 'grad_l1_b_w_group': 'grad_w', 'grad_l1_b_scale': 'grad_w', 'grad_l1_ffn_norm_g': 'grad_w', 'grad_l1_ffn_w_up': 'grad_w', 'grad_l1_ffn_dw_w': 'grad_w', 'grad_l1_ffn_dw_b': 'grad_w', 'grad_l1_ffn_w_down': 'grad_w', 'grad_l2_norm_g': 'grad_w', 'grad_l2_c_w_qkv': 'grad_w', 'grad_l2_c_q_norm_g': 'grad_w', 'grad_l2_c_k_norm_g': 'grad_w', 'grad_l2_c_sinks': 'grad_w', 'grad_l2_c_w_o': 'grad_w', 'grad_l2_ffn_norm_g': 'grad_w', 'grad_l2_ffn_w_up': 'grad_w', 'grad_l2_ffn_dw_w': 'grad_w', 'grad_l2_ffn_dw_b': 'grad_w', 'grad_l2_ffn_w_down': 'grad_w', 'grad_l3_norm_g': 'grad_w', 'grad_l3_a_w_in': 'grad_w', 'grad_l3_a_b_in': 'grad_w', 'grad_l3_a_dw_w': 'grad_w', 'grad_l3_a_dw_b': 'grad_w', 'grad_l3_a_ln_g': 'grad_w', 'grad_l3_a_ln_b': 'grad_w', 'grad_l3_a_w_out': 'grad_w', 'grad_l3_a_b_out': 'grad_w', 'grad_l3_ffn_norm_g': 'grad_w', 'grad_l3_ffn_w_up': 'grad_w', 'grad_l3_ffn_dw_w': 'grad_w', 'grad_l3_ffn_dw_b': 'grad_w', 'grad_l3_ffn_w_down': 'grad_w', 'delta_l0_norm_g': 'delta_w', 'delta_l0_a_w_in': 'delta_w', 'delta_l0_a_b_in': 'delta_w', 'delta_l0_a_dw_w': 'delta_w', 'delta_l0_a_dw_b': 'delta_w', 'delta_l0_a_ln_g': 'delta_w', 'delta_l0_a_ln_b': 'delta_w', 'delta_l0_a_w_out': 'delta_w', 'delta_l0_a_b_out': 'delta_w', 'delta_l0_ffn_norm_g': 'delta_w', 'delta_l0_ffn_w_up': 'delta_w', 'delta_l0_ffn_dw_w': 'delta_w', 'delta_l0_ffn_dw_b': 'delta_w', 'delta_l0_ffn_w_down': 'delta_w', 'delta_l1_norm_g': 'delta_w', 'delta_l1_b_w_group': 'delta_w', 'delta_l1_b_scale': 'delta_w', 'delta_l1_ffn_norm_g': 'delta_w', 'delta_l1_ffn_w_up': 'delta_w', 'delta_l1_ffn_dw_w': 'delta_w', 'delta_l1_ffn_dw_b': 'delta_w', 'delta_l1_ffn_w_down': 'delta_w', 'delta_l2_norm_g': 'delta_w', 'delta_l2_c_w_qkv': 'delta_w', 'delta_l2_c_q_norm_g': 'delta_w', 'delta_l2_c_k_norm_g': 'delta_w', 'delta_l2_c_sinks': 'delta_w', 'delta_l2_c_w_o': 'delta_w', 'delta_l2_ffn_norm_g': 'delta_w', 'delta_l2_ffn_w_up': 'delta_w', 'delta_l2_ffn_dw_w': 'delta_w', 'delta_l2_ffn_dw_b': 'delta_w', 'delta_l2_ffn_w_down': 'delta_w', 'delta_l3_norm_g': 'delta_w', 'delta_l3_a_w_in': 'delta_w', 'delta_l3_a_b_in': 'delta_w', 'delta_l3_a_dw_w': 'delta_w', 'delta_l3_a_dw_b': 'delta_w', 'delta_l3_a_ln_g': 'delta_w', 'delta_l3_a_ln_b': 'delta_w', 'delta_l3_a_w_out': 'delta_w', 'delta_l3_a_b_out': 'delta_w', 'delta_l3_ffn_norm_g': 'delta_w', 'delta_l3_ffn_w_up': 'delta_w', 'delta_l3_ffn_dw_w': 'delta_w', 'delta_l3_ffn_dw_b': 'delta_w', 'delta_l3_ffn_w_down': 'delta_w', 'new_m_l0_norm_g': 'new_m', 'new_m_l0_a_w_in': 'new_m', 'new_m_l0_a_b_in': 'new_m', 'new_m_l0_a_dw_w': 'new_m', 'new_m_l0_a_dw_b': 'new_m', 'new_m_l0_a_ln_g': 'new_m', 'new_m_l0_a_ln_b': 'new_m', 'new_m_l0_a_w_out': 'new_m', 'new_m_l0_a_b_out': 'new_m', 'new_m_l0_ffn_norm_g': 'new_m', 'new_m_l0_ffn_w_up': 'new_m', 'new_m_l0_ffn_dw_w': 'new_m', 'new_m_l0_ffn_dw_b': 'new_m', 'new_m_l0_ffn_w_down': 'new_m', 'new_m_l1_norm_g': 'new_m', 'new_m_l1_b_w_group': 'new_m', 'new_m_l1_b_scale': 'new_m', 'new_m_l1_ffn_norm_g': 'new_m', 'new_m_l1_ffn_w_up': 'new_m', 'new_m_l1_ffn_dw_w': 'new_m', 'new_m_l1_ffn_dw_b': 'new_m', 'new_m_l1_ffn_w_down': 'new_m', 'new_m_l2_norm_g': 'new_m', 'new_m_l2_c_w_qkv': 'new_m', 'new_m_l2_c_q_norm_g': 'new_m', 'new_m_l2_c_k_norm_g': 'new_m', 'new_m_l2_c_sinks': 'new_m', 'new_m_l2_c_w_o': 'new_m', 'new_m_l2_ffn_norm_g': 'new_m', 'new_m_l2_ffn_w_up': 'new_m', 'new_m_l2_ffn_dw_w': 'new_m', 'new_m_l2_ffn_dw_b': 'new_m', 'new_m_l2_ffn_w_down': 'new_m', 'new_m_l3_norm_g': 'new_m', 'new_m_l3_a_w_in': 'new_m', 'new_m_l3_a_b_in': 'new_m', 'new_m_l3_a_dw_w': 'new_m', 'new_m_l3_a_dw_b': 'new_m', 'new_m_l3_a_ln_g': 'new_m', 'new_m_l3_a_ln_b': 'new_m', 'new_m_l3_a_w_out': 'new_m', 'new_m_l3_a_b_out': 'new_m', 'new_m_l3_ffn_norm_g': 'new_m', 'new_m_l3_ffn_w_up': 'new_m', 'new_m_l3_ffn_dw_w': 'new_m', 'new_m_l3_ffn_dw_b': 'new_m', 'new_m_l3_ffn_w_down': 'new_m', 'new_v_l0_norm_g': 'new_v', 'new_v_l0_a_w_in': 'new_v', 'new_v_l0_a_b_in': 'new_v', 'new_v_l0_a_dw_w': 'new_v', 'new_v_l0_a_dw_b': 'new_v', 'new_v_l0_a_ln_g': 'new_v', 'new_v_l0_a_ln_b': 'new_v', 'new_v_l0_a_w_out': 'new_v', 'new_v_l0_a_b_out': 'new_v', 'new_v_l0_ffn_norm_g': 'new_v', 'new_v_l0_ffn_w_up': 'new_v', 'new_v_l0_ffn_dw_w': 'new_v', 'new_v_l0_ffn_dw_b': 'new_v', 'new_v_l0_ffn_w_down': 'new_v', 'new_v_l1_norm_g': 'new_v', 'new_v_l1_b_w_group': 'new_v', 'new_v_l1_b_scale': 'new_v', 'new_v_l1_ffn_norm_g': 'new_v', 'new_v_l1_ffn_w_up': 'new_v', 'new_v_l1_ffn_dw_w': 'new_v', 'new_v_l1_ffn_dw_b': 'new_v', 'new_v_l1_ffn_w_down': 'new_v', 'new_v_l2_norm_g': 'new_v', 'new_v_l2_c_w_qkv': 'new_v', 'new_v_l2_c_q_norm_g': 'new_v', 'new_v_l2_c_k_norm_g': 'new_v', 'new_v_l2_c_sinks': 'new_v', 'new_v_l2_c_w_o': 'new_v', 'new_v_l2_ffn_norm_g': 'new_v', 'new_v_l2_ffn_w_up': 'new_v', 'new_v_l2_ffn_dw_w': 'new_v', 'new_v_l2_ffn_dw_b': 'new_v', 'new_v_l2_ffn_w_down': 'new_v', 'new_v_l3_norm_g': 'new_v', 'new_v_l3_a_w_in': 'new_v', 'new_v_l3_a_b_in': 'new_v', 'new_v_l3_a_dw_w': 'new_v', 'new_v_l3_a_dw_b': 'new_v', 'new_v_l3_a_ln_g': 'new_v', 'new_v_l3_a_ln_b': 'new_v', 'new_v_l3_a_w_out': 'new_v', 'new_v_l3_a_b_out': 'new_v', 'new_v_l3_ffn_norm_g': 'new_v', 'new_v_l3_ffn_w_up': 'new_v', 'new_v_l3_ffn_dw_w': 'new_v', 'new_v_l3_ffn_dw_b': 'new_v', 'new_v_l3_ffn_w_down': 'new_v'}


def _forward(args):
    return _fwd_reference(*[args[k] for k in FWD_PARAMS])


def _output_shape():
    out = _jax.eval_shape(lambda: _forward(_fwd_setup_inputs(0)))
    return out.shape, out.dtype

N_MICROBATCH = 1
ADAM_LR = 0.001
ADAM_B1 = 0.9
ADAM_B2 = 0.999
ADAM_EPS = 1e-08
ADAM_WD = 0.01
ADAM_STEP = 10
PER_EXAMPLE_BATCH_AXIS = {'x': 0, 'positions': 0, 'loss_target': 0}
SHARED_INPUTS = []
_WEIGHT_DTYPES = {'l0_norm_g': _jnp.float32, 'l0_a_w_in': _jnp.float32, 'l0_a_b_in': _jnp.float32, 'l0_a_dw_w': _jnp.float32, 'l0_a_dw_b': _jnp.float32, 'l0_a_ln_g': _jnp.float32, 'l0_a_ln_b': _jnp.float32, 'l0_a_w_out': _jnp.float32, 'l0_a_b_out': _jnp.float32, 'l0_ffn_norm_g': _jnp.float32, 'l0_ffn_w_up': _jnp.float32, 'l0_ffn_dw_w': _jnp.float32, 'l0_ffn_dw_b': _jnp.float32, 'l0_ffn_w_down': _jnp.float32, 'l1_norm_g': _jnp.float32, 'l1_b_w_group': _jnp.float32, 'l1_b_scale': _jnp.float32, 'l1_ffn_norm_g': _jnp.float32, 'l1_ffn_w_up': _jnp.float32, 'l1_ffn_dw_w': _jnp.float32, 'l1_ffn_dw_b': _jnp.float32, 'l1_ffn_w_down': _jnp.float32, 'l2_norm_g': _jnp.float32, 'l2_c_w_qkv': _jnp.float32, 'l2_c_q_norm_g': _jnp.float32, 'l2_c_k_norm_g': _jnp.float32, 'l2_c_sinks': _jnp.float32, 'l2_c_w_o': _jnp.float32, 'l2_ffn_norm_g': _jnp.float32, 'l2_ffn_w_up': _jnp.float32, 'l2_ffn_dw_w': _jnp.float32, 'l2_ffn_dw_b': _jnp.float32, 'l2_ffn_w_down': _jnp.float32, 'l3_norm_g': _jnp.float32, 'l3_a_w_in': _jnp.float32, 'l3_a_b_in': _jnp.float32, 'l3_a_dw_w': _jnp.float32, 'l3_a_dw_b': _jnp.float32, 'l3_a_ln_g': _jnp.float32, 'l3_a_ln_b': _jnp.float32, 'l3_a_w_out': _jnp.float32, 'l3_a_b_out': _jnp.float32, 'l3_ffn_norm_g': _jnp.float32, 'l3_ffn_w_up': _jnp.float32, 'l3_ffn_dw_w': _jnp.float32, 'l3_ffn_dw_b': _jnp.float32, 'l3_ffn_w_down': _jnp.float32}
MOMENT_SCALE = {'l0_norm_g': 3.734017e-01, 'l0_a_w_in': 2.635946e-01, 'l0_a_b_in': 3.875297e+00, 'l0_a_dw_w': 5.330397e-01, 'l0_a_dw_b': 8.697595e+00, 'l0_a_ln_g': 7.553400e+00, 'l0_a_ln_b': 6.802537e+00, 'l0_a_w_out': 1.921344e+00, 'l0_a_b_out': 1.064095e+01, 'l0_ffn_norm_g': 1.362274e+01, 'l0_ffn_w_up': 3.562304e-01, 'l0_ffn_dw_w': 1.835672e+00, 'l0_ffn_dw_b': 1.933922e+00, 'l0_ffn_w_down': 3.979989e-01, 'l1_norm_g': 1.228178e+01, 'l1_b_w_group': 6.689851e-01, 'l1_b_scale': 1.222229e+01, 'l1_ffn_norm_g': 1.305401e+01, 'l1_ffn_w_up': 2.707476e-01, 'l1_ffn_dw_w': 1.767746e+00, 'l1_ffn_dw_b': 1.894940e+00, 'l1_ffn_w_down': 2.759277e-01, 'l2_norm_g': 9.851811e-01, 'l2_c_w_qkv': 8.587879e-01, 'l2_c_q_norm_g': 5.577415e+00, 'l2_c_k_norm_g': 5.591144e+00, 'l2_c_sinks': 3.955945e-01, 'l2_c_w_o': 8.278902e-01, 'l2_ffn_norm_g': 1.299241e+01, 'l2_ffn_w_up': 2.784562e-01, 'l2_ffn_dw_w': 1.790735e+00, 'l2_ffn_dw_b': 1.741912e+00, 'l2_ffn_w_down': 2.254130e-01, 'l3_norm_g': 3.103196e-01, 'l3_a_w_in': 2.171548e-01, 'l3_a_b_in': 2.139238e+00, 'l3_a_dw_w': 4.345345e-01, 'l3_a_dw_b': 5.394621e+00, 'l3_a_ln_g': 7.485854e+00, 'l3_a_ln_b': 5.457803e+00, 'l3_a_w_out': 1.200715e+00, 'l3_a_b_out': 5.970515e+00, 'l3_ffn_norm_g': 1.335448e+01, 'l3_ffn_w_up': 3.375107e-01, 'l3_ffn_dw_w': 1.932203e+00, 'l3_ffn_dw_b': 1.815370e+00, 'l3_ffn_w_down': 2.212543e-01}


def _to_microbatches(a, axis):
    t = _jnp.moveaxis(a, axis, 0)
    t = t.reshape((N_MICROBATCH, t.shape[0] // N_MICROBATCH) + t.shape[1:])
    return _jnp.moveaxis(t, 1, axis + 1)


def setup_inputs(seed: int = 0) -> dict:
    inp = _fwd_setup_inputs(seed)
    key = _jax.random.fold_in(_jax.random.key(seed), 7919)
    shape, _ = _output_shape()
    out = dict(inp)
    out["loss_target"] = _jax.random.normal(_jax.random.fold_in(key, 0), shape, _jnp.float32)
    for i, name in enumerate(TWIN_WEIGHTS):
        w = inp[name].astype(_jnp.float32)
        if MOMENT_SCALE is None:
            s = _jnp.sqrt(_jnp.mean(_jnp.square(w)) + 1e-30)
        else:
            s = MOMENT_SCALE[name]
        km, kv = _jax.random.split(_jax.random.fold_in(key, i + 1))
        out[name] = w
        out["m_" + name] = s * _jax.random.normal(km, w.shape, _jnp.float32)
        out["v_" + name] = (s * s) * _jax.random.uniform(kv, w.shape, _jnp.float32, 0.5, 1.5)
    if N_MICROBATCH > 1:
        for name, axis in PER_EXAMPLE_BATCH_AXIS.items():
            out[name] = _to_microbatches(out[name], axis)
    return {'x': out['x'], 'positions': out['positions'], 'l0_norm_g': out['l0_norm_g'], 'l0_a_w_in': out['l0_a_w_in'], 'l0_a_b_in': out['l0_a_b_in'], 'l0_a_dw_w': out['l0_a_dw_w'], 'l0_a_dw_b': out['l0_a_dw_b'], 'l0_a_ln_g': out['l0_a_ln_g'], 'l0_a_ln_b': out['l0_a_ln_b'], 'l0_a_w_out': out['l0_a_w_out'], 'l0_a_b_out': out['l0_a_b_out'], 'l0_ffn_norm_g': out['l0_ffn_norm_g'], 'l0_ffn_w_up': out['l0_ffn_w_up'], 'l0_ffn_dw_w': out['l0_ffn_dw_w'], 'l0_ffn_dw_b': out['l0_ffn_dw_b'], 'l0_ffn_w_down': out['l0_ffn_w_down'], 'l1_norm_g': out['l1_norm_g'], 'l1_b_w_group': out['l1_b_w_group'], 'l1_b_scale': out['l1_b_scale'], 'l1_ffn_norm_g': out['l1_ffn_norm_g'], 'l1_ffn_w_up': out['l1_ffn_w_up'], 'l1_ffn_dw_w': out['l1_ffn_dw_w'], 'l1_ffn_dw_b': out['l1_ffn_dw_b'], 'l1_ffn_w_down': out['l1_ffn_w_down'], 'l2_norm_g': out['l2_norm_g'], 'l2_c_w_qkv': out['l2_c_w_qkv'], 'l2_c_q_norm_g': out['l2_c_q_norm_g'], 'l2_c_k_norm_g': out['l2_c_k_norm_g'], 'l2_c_sinks': out['l2_c_sinks'], 'l2_c_w_o': out['l2_c_w_o'], 'l2_ffn_norm_g': out['l2_ffn_norm_g'], 'l2_ffn_w_up': out['l2_ffn_w_up'], 'l2_ffn_dw_w': out['l2_ffn_dw_w'], 'l2_ffn_dw_b': out['l2_ffn_dw_b'], 'l2_ffn_w_down': out['l2_ffn_w_down'], 'l3_norm_g': out['l3_norm_g'], 'l3_a_w_in': out['l3_a_w_in'], 'l3_a_b_in': out['l3_a_b_in'], 'l3_a_dw_w': out['l3_a_dw_w'], 'l3_a_dw_b': out['l3_a_dw_b'], 'l3_a_ln_g': out['l3_a_ln_g'], 'l3_a_ln_b': out['l3_a_ln_b'], 'l3_a_w_out': out['l3_a_w_out'], 'l3_a_b_out': out['l3_a_b_out'], 'l3_ffn_norm_g': out['l3_ffn_norm_g'], 'l3_ffn_w_up': out['l3_ffn_w_up'], 'l3_ffn_dw_w': out['l3_ffn_dw_w'], 'l3_ffn_dw_b': out['l3_ffn_dw_b'], 'l3_ffn_w_down': out['l3_ffn_w_down'], 'loss_target': out['loss_target'], 'm_l0_norm_g': out['m_l0_norm_g'], 'm_l0_a_w_in': out['m_l0_a_w_in'], 'm_l0_a_b_in': out['m_l0_a_b_in'], 'm_l0_a_dw_w': out['m_l0_a_dw_w'], 'm_l0_a_dw_b': out['m_l0_a_dw_b'], 'm_l0_a_ln_g': out['m_l0_a_ln_g'], 'm_l0_a_ln_b': out['m_l0_a_ln_b'], 'm_l0_a_w_out': out['m_l0_a_w_out'], 'm_l0_a_b_out': out['m_l0_a_b_out'], 'm_l0_ffn_norm_g': out['m_l0_ffn_norm_g'], 'm_l0_ffn_w_up': out['m_l0_ffn_w_up'], 'm_l0_ffn_dw_w': out['m_l0_ffn_dw_w'], 'm_l0_ffn_dw_b': out['m_l0_ffn_dw_b'], 'm_l0_ffn_w_down': out['m_l0_ffn_w_down'], 'm_l1_norm_g': out['m_l1_norm_g'], 'm_l1_b_w_group': out['m_l1_b_w_group'], 'm_l1_b_scale': out['m_l1_b_scale'], 'm_l1_ffn_norm_g': out['m_l1_ffn_norm_g'], 'm_l1_ffn_w_up': out['m_l1_ffn_w_up'], 'm_l1_ffn_dw_w': out['m_l1_ffn_dw_w'], 'm_l1_ffn_dw_b': out['m_l1_ffn_dw_b'], 'm_l1_ffn_w_down': out['m_l1_ffn_w_down'], 'm_l2_norm_g': out['m_l2_norm_g'], 'm_l2_c_w_qkv': out['m_l2_c_w_qkv'], 'm_l2_c_q_norm_g': out['m_l2_c_q_norm_g'], 'm_l2_c_k_norm_g': out['m_l2_c_k_norm_g'], 'm_l2_c_sinks': out['m_l2_c_sinks'], 'm_l2_c_w_o': out['m_l2_c_w_o'], 'm_l2_ffn_norm_g': out['m_l2_ffn_norm_g'], 'm_l2_ffn_w_up': out['m_l2_ffn_w_up'], 'm_l2_ffn_dw_w': out['m_l2_ffn_dw_w'], 'm_l2_ffn_dw_b': out['m_l2_ffn_dw_b'], 'm_l2_ffn_w_down': out['m_l2_ffn_w_down'], 'm_l3_norm_g': out['m_l3_norm_g'], 'm_l3_a_w_in': out['m_l3_a_w_in'], 'm_l3_a_b_in': out['m_l3_a_b_in'], 'm_l3_a_dw_w': out['m_l3_a_dw_w'], 'm_l3_a_dw_b': out['m_l3_a_dw_b'], 'm_l3_a_ln_g': out['m_l3_a_ln_g'], 'm_l3_a_ln_b': out['m_l3_a_ln_b'], 'm_l3_a_w_out': out['m_l3_a_w_out'], 'm_l3_a_b_out': out['m_l3_a_b_out'], 'm_l3_ffn_norm_g': out['m_l3_ffn_norm_g'], 'm_l3_ffn_w_up': out['m_l3_ffn_w_up'], 'm_l3_ffn_dw_w': out['m_l3_ffn_dw_w'], 'm_l3_ffn_dw_b': out['m_l3_ffn_dw_b'], 'm_l3_ffn_w_down': out['m_l3_ffn_w_down'], 'v_l0_norm_g': out['v_l0_norm_g'], 'v_l0_a_w_in': out['v_l0_a_w_in'], 'v_l0_a_b_in': out['v_l0_a_b_in'], 'v_l0_a_dw_w': out['v_l0_a_dw_w'], 'v_l0_a_dw_b': out['v_l0_a_dw_b'], 'v_l0_a_ln_g': out['v_l0_a_ln_g'], 'v_l0_a_ln_b': out['v_l0_a_ln_b'], 'v_l0_a_w_out': out['v_l0_a_w_out'], 'v_l0_a_b_out': out['v_l0_a_b_out'], 'v_l0_ffn_norm_g': out['v_l0_ffn_norm_g'], 'v_l0_ffn_w_up': out['v_l0_ffn_w_up'], 'v_l0_ffn_dw_w': out['v_l0_ffn_dw_w'], 'v_l0_ffn_dw_b': out['v_l0_ffn_dw_b'], 'v_l0_ffn_w_down': out['v_l0_ffn_w_down'], 'v_l1_norm_g': out['v_l1_norm_g'], 'v_l1_b_w_group': out['v_l1_b_w_group'], 'v_l1_b_scale': out['v_l1_b_scale'], 'v_l1_ffn_norm_g': out['v_l1_ffn_norm_g'], 'v_l1_ffn_w_up': out['v_l1_ffn_w_up'], 'v_l1_ffn_dw_w': out['v_l1_ffn_dw_w'], 'v_l1_ffn_dw_b': out['v_l1_ffn_dw_b'], 'v_l1_ffn_w_down': out['v_l1_ffn_w_down'], 'v_l2_norm_g': out['v_l2_norm_g'], 'v_l2_c_w_qkv': out['v_l2_c_w_qkv'], 'v_l2_c_q_norm_g': out['v_l2_c_q_norm_g'], 'v_l2_c_k_norm_g': out['v_l2_c_k_norm_g'], 'v_l2_c_sinks': out['v_l2_c_sinks'], 'v_l2_c_w_o': out['v_l2_c_w_o'], 'v_l2_ffn_norm_g': out['v_l2_ffn_norm_g'], 'v_l2_ffn_w_up': out['v_l2_ffn_w_up'], 'v_l2_ffn_dw_w': out['v_l2_ffn_dw_w'], 'v_l2_ffn_dw_b': out['v_l2_ffn_dw_b'], 'v_l2_ffn_w_down': out['v_l2_ffn_w_down'], 'v_l3_norm_g': out['v_l3_norm_g'], 'v_l3_a_w_in': out['v_l3_a_w_in'], 'v_l3_a_b_in': out['v_l3_a_b_in'], 'v_l3_a_dw_w': out['v_l3_a_dw_w'], 'v_l3_a_dw_b': out['v_l3_a_dw_b'], 'v_l3_a_ln_g': out['v_l3_a_ln_g'], 'v_l3_a_ln_b': out['v_l3_a_ln_b'], 'v_l3_a_w_out': out['v_l3_a_w_out'], 'v_l3_a_b_out': out['v_l3_a_b_out'], 'v_l3_ffn_norm_g': out['v_l3_ffn_norm_g'], 'v_l3_ffn_w_up': out['v_l3_ffn_w_up'], 'v_l3_ffn_dw_w': out['v_l3_ffn_dw_w'], 'v_l3_ffn_dw_b': out['v_l3_ffn_dw_b'], 'v_l3_ffn_w_down': out['v_l3_ffn_w_down']}


def _loss(weights, diff, rest, loss_target):
    with _jax.named_scope("forward"):
        args = {**rest, TWIN_DIFF_INPUT: diff, **{k: w.astype(_WEIGHT_DTYPES[k]) for k, w in weights.items()}}
        y = _forward(args)
    with _jax.named_scope("loss_head"):
        err = _jnp.square(y.astype(_jnp.float32) - loss_target)
        return 0.5 * _jnp.sum(_jnp.mean(err, axis=-1)) if err.ndim else 0.5 * err


def _adamw(w, g, m, v):
    m = ADAM_B1 * m + (1.0 - ADAM_B1) * g
    v = ADAM_B2 * v + (1.0 - ADAM_B2) * _jnp.square(g)
    m_hat = m / (1.0 - ADAM_B1 ** ADAM_STEP)
    v_hat = v / (1.0 - ADAM_B2 ** ADAM_STEP)
    delta = -ADAM_LR * (m_hat / (_jnp.sqrt(v_hat) + ADAM_EPS) + ADAM_WD * w)
    return delta, m, v


def reference(x, positions, l0_norm_g, l0_a_w_in, l0_a_b_in, l0_a_dw_w, l0_a_dw_b, l0_a_ln_g, l0_a_ln_b, l0_a_w_out, l0_a_b_out, l0_ffn_norm_g, l0_ffn_w_up, l0_ffn_dw_w, l0_ffn_dw_b, l0_ffn_w_down, l1_norm_g, l1_b_w_group, l1_b_scale, l1_ffn_norm_g, l1_ffn_w_up, l1_ffn_dw_w, l1_ffn_dw_b, l1_ffn_w_down, l2_norm_g, l2_c_w_qkv, l2_c_q_norm_g, l2_c_k_norm_g, l2_c_sinks, l2_c_w_o, l2_ffn_norm_g, l2_ffn_w_up, l2_ffn_dw_w, l2_ffn_dw_b, l2_ffn_w_down, l3_norm_g, l3_a_w_in, l3_a_b_in, l3_a_dw_w, l3_a_dw_b, l3_a_ln_g, l3_a_ln_b, l3_a_w_out, l3_a_b_out, l3_ffn_norm_g, l3_ffn_w_up, l3_ffn_dw_w, l3_ffn_dw_b, l3_ffn_w_down, loss_target, m_l0_norm_g, m_l0_a_w_in, m_l0_a_b_in, m_l0_a_dw_w, m_l0_a_dw_b, m_l0_a_ln_g, m_l0_a_ln_b, m_l0_a_w_out, m_l0_a_b_out, m_l0_ffn_norm_g, m_l0_ffn_w_up, m_l0_ffn_dw_w, m_l0_ffn_dw_b, m_l0_ffn_w_down, m_l1_norm_g, m_l1_b_w_group, m_l1_b_scale, m_l1_ffn_norm_g, m_l1_ffn_w_up, m_l1_ffn_dw_w, m_l1_ffn_dw_b, m_l1_ffn_w_down, m_l2_norm_g, m_l2_c_w_qkv, m_l2_c_q_norm_g, m_l2_c_k_norm_g, m_l2_c_sinks, m_l2_c_w_o, m_l2_ffn_norm_g, m_l2_ffn_w_up, m_l2_ffn_dw_w, m_l2_ffn_dw_b, m_l2_ffn_w_down, m_l3_norm_g, m_l3_a_w_in, m_l3_a_b_in, m_l3_a_dw_w, m_l3_a_dw_b, m_l3_a_ln_g, m_l3_a_ln_b, m_l3_a_w_out, m_l3_a_b_out, m_l3_ffn_norm_g, m_l3_ffn_w_up, m_l3_ffn_dw_w, m_l3_ffn_dw_b, m_l3_ffn_w_down, v_l0_norm_g, v_l0_a_w_in, v_l0_a_b_in, v_l0_a_dw_w, v_l0_a_dw_b, v_l0_a_ln_g, v_l0_a_ln_b, v_l0_a_w_out, v_l0_a_b_out, v_l0_ffn_norm_g, v_l0_ffn_w_up, v_l0_ffn_dw_w, v_l0_ffn_dw_b, v_l0_ffn_w_down, v_l1_norm_g, v_l1_b_w_group, v_l1_b_scale, v_l1_ffn_norm_g, v_l1_ffn_w_up, v_l1_ffn_dw_w, v_l1_ffn_dw_b, v_l1_ffn_w_down, v_l2_norm_g, v_l2_c_w_qkv, v_l2_c_q_norm_g, v_l2_c_k_norm_g, v_l2_c_sinks, v_l2_c_w_o, v_l2_ffn_norm_g, v_l2_ffn_w_up, v_l2_ffn_dw_w, v_l2_ffn_dw_b, v_l2_ffn_w_down, v_l3_norm_g, v_l3_a_w_in, v_l3_a_b_in, v_l3_a_dw_w, v_l3_a_dw_b, v_l3_a_ln_g, v_l3_a_ln_b, v_l3_a_w_out, v_l3_a_b_out, v_l3_ffn_norm_g, v_l3_ffn_w_up, v_l3_ffn_dw_w, v_l3_ffn_dw_b, v_l3_ffn_w_down):
    given = dict(x=x, positions=positions, l0_norm_g=l0_norm_g, l0_a_w_in=l0_a_w_in, l0_a_b_in=l0_a_b_in, l0_a_dw_w=l0_a_dw_w, l0_a_dw_b=l0_a_dw_b, l0_a_ln_g=l0_a_ln_g, l0_a_ln_b=l0_a_ln_b, l0_a_w_out=l0_a_w_out, l0_a_b_out=l0_a_b_out, l0_ffn_norm_g=l0_ffn_norm_g, l0_ffn_w_up=l0_ffn_w_up, l0_ffn_dw_w=l0_ffn_dw_w, l0_ffn_dw_b=l0_ffn_dw_b, l0_ffn_w_down=l0_ffn_w_down, l1_norm_g=l1_norm_g, l1_b_w_group=l1_b_w_group, l1_b_scale=l1_b_scale, l1_ffn_norm_g=l1_ffn_norm_g, l1_ffn_w_up=l1_ffn_w_up, l1_ffn_dw_w=l1_ffn_dw_w, l1_ffn_dw_b=l1_ffn_dw_b, l1_ffn_w_down=l1_ffn_w_down, l2_norm_g=l2_norm_g, l2_c_w_qkv=l2_c_w_qkv, l2_c_q_norm_g=l2_c_q_norm_g, l2_c_k_norm_g=l2_c_k_norm_g, l2_c_sinks=l2_c_sinks, l2_c_w_o=l2_c_w_o, l2_ffn_norm_g=l2_ffn_norm_g, l2_ffn_w_up=l2_ffn_w_up, l2_ffn_dw_w=l2_ffn_dw_w, l2_ffn_dw_b=l2_ffn_dw_b, l2_ffn_w_down=l2_ffn_w_down, l3_norm_g=l3_norm_g, l3_a_w_in=l3_a_w_in, l3_a_b_in=l3_a_b_in, l3_a_dw_w=l3_a_dw_w, l3_a_dw_b=l3_a_dw_b, l3_a_ln_g=l3_a_ln_g, l3_a_ln_b=l3_a_ln_b, l3_a_w_out=l3_a_w_out, l3_a_b_out=l3_a_b_out, l3_ffn_norm_g=l3_ffn_norm_g, l3_ffn_w_up=l3_ffn_w_up, l3_ffn_dw_w=l3_ffn_dw_w, l3_ffn_dw_b=l3_ffn_dw_b, l3_ffn_w_down=l3_ffn_w_down, loss_target=loss_target, m_l0_norm_g=m_l0_norm_g, m_l0_a_w_in=m_l0_a_w_in, m_l0_a_b_in=m_l0_a_b_in, m_l0_a_dw_w=m_l0_a_dw_w, m_l0_a_dw_b=m_l0_a_dw_b, m_l0_a_ln_g=m_l0_a_ln_g, m_l0_a_ln_b=m_l0_a_ln_b, m_l0_a_w_out=m_l0_a_w_out, m_l0_a_b_out=m_l0_a_b_out, m_l0_ffn_norm_g=m_l0_ffn_norm_g, m_l0_ffn_w_up=m_l0_ffn_w_up, m_l0_ffn_dw_w=m_l0_ffn_dw_w, m_l0_ffn_dw_b=m_l0_ffn_dw_b, m_l0_ffn_w_down=m_l0_ffn_w_down, m_l1_norm_g=m_l1_norm_g, m_l1_b_w_group=m_l1_b_w_group, m_l1_b_scale=m_l1_b_scale, m_l1_ffn_norm_g=m_l1_ffn_norm_g, m_l1_ffn_w_up=m_l1_ffn_w_up, m_l1_ffn_dw_w=m_l1_ffn_dw_w, m_l1_ffn_dw_b=m_l1_ffn_dw_b, m_l1_ffn_w_down=m_l1_ffn_w_down, m_l2_norm_g=m_l2_norm_g, m_l2_c_w_qkv=m_l2_c_w_qkv, m_l2_c_q_norm_g=m_l2_c_q_norm_g, m_l2_c_k_norm_g=m_l2_c_k_norm_g, m_l2_c_sinks=m_l2_c_sinks, m_l2_c_w_o=m_l2_c_w_o, m_l2_ffn_norm_g=m_l2_ffn_norm_g, m_l2_ffn_w_up=m_l2_ffn_w_up, m_l2_ffn_dw_w=m_l2_ffn_dw_w, m_l2_ffn_dw_b=m_l2_ffn_dw_b, m_l2_ffn_w_down=m_l2_ffn_w_down, m_l3_norm_g=m_l3_norm_g, m_l3_a_w_in=m_l3_a_w_in, m_l3_a_b_in=m_l3_a_b_in, m_l3_a_dw_w=m_l3_a_dw_w, m_l3_a_dw_b=m_l3_a_dw_b, m_l3_a_ln_g=m_l3_a_ln_g, m_l3_a_ln_b=m_l3_a_ln_b, m_l3_a_w_out=m_l3_a_w_out, m_l3_a_b_out=m_l3_a_b_out, m_l3_ffn_norm_g=m_l3_ffn_norm_g, m_l3_ffn_w_up=m_l3_ffn_w_up, m_l3_ffn_dw_w=m_l3_ffn_dw_w, m_l3_ffn_dw_b=m_l3_ffn_dw_b, m_l3_ffn_w_down=m_l3_ffn_w_down, v_l0_norm_g=v_l0_norm_g, v_l0_a_w_in=v_l0_a_w_in, v_l0_a_b_in=v_l0_a_b_in, v_l0_a_dw_w=v_l0_a_dw_w, v_l0_a_dw_b=v_l0_a_dw_b, v_l0_a_ln_g=v_l0_a_ln_g, v_l0_a_ln_b=v_l0_a_ln_b, v_l0_a_w_out=v_l0_a_w_out, v_l0_a_b_out=v_l0_a_b_out, v_l0_ffn_norm_g=v_l0_ffn_norm_g, v_l0_ffn_w_up=v_l0_ffn_w_up, v_l0_ffn_dw_w=v_l0_ffn_dw_w, v_l0_ffn_dw_b=v_l0_ffn_dw_b, v_l0_ffn_w_down=v_l0_ffn_w_down, v_l1_norm_g=v_l1_norm_g, v_l1_b_w_group=v_l1_b_w_group, v_l1_b_scale=v_l1_b_scale, v_l1_ffn_norm_g=v_l1_ffn_norm_g, v_l1_ffn_w_up=v_l1_ffn_w_up, v_l1_ffn_dw_w=v_l1_ffn_dw_w, v_l1_ffn_dw_b=v_l1_ffn_dw_b, v_l1_ffn_w_down=v_l1_ffn_w_down, v_l2_norm_g=v_l2_norm_g, v_l2_c_w_qkv=v_l2_c_w_qkv, v_l2_c_q_norm_g=v_l2_c_q_norm_g, v_l2_c_k_norm_g=v_l2_c_k_norm_g, v_l2_c_sinks=v_l2_c_sinks, v_l2_c_w_o=v_l2_c_w_o, v_l2_ffn_norm_g=v_l2_ffn_norm_g, v_l2_ffn_w_up=v_l2_ffn_w_up, v_l2_ffn_dw_w=v_l2_ffn_dw_w, v_l2_ffn_dw_b=v_l2_ffn_dw_b, v_l2_ffn_w_down=v_l2_ffn_w_down, v_l3_norm_g=v_l3_norm_g, v_l3_a_w_in=v_l3_a_w_in, v_l3_a_b_in=v_l3_a_b_in, v_l3_a_dw_w=v_l3_a_dw_w, v_l3_a_dw_b=v_l3_a_dw_b, v_l3_a_ln_g=v_l3_a_ln_g, v_l3_a_ln_b=v_l3_a_ln_b, v_l3_a_w_out=v_l3_a_w_out, v_l3_a_b_out=v_l3_a_b_out, v_l3_ffn_norm_g=v_l3_ffn_norm_g, v_l3_ffn_w_up=v_l3_ffn_w_up, v_l3_ffn_dw_w=v_l3_ffn_dw_w, v_l3_ffn_dw_b=v_l3_ffn_dw_b, v_l3_ffn_w_down=v_l3_ffn_w_down)
    weights = {n: given[n] for n in TWIN_WEIGHTS}
    shared = {n: given[n] for n in SHARED_INPUTS}
    per_example = {n: given[n] for n in ['x', 'positions']}
    grad_fn = _jax.value_and_grad(_loss, argnums=(0, 1))

    def one_microbatch(ex, loss_target):
        ex = dict(ex)
        diff = ex.pop(TWIN_DIFF_INPUT)
        return grad_fn(weights, diff, {**shared, **ex}, loss_target)

    if N_MICROBATCH == 1:
        loss, (grad_w, grad_x) = one_microbatch(per_example, given["loss_target"])
    else:
        def body(carry, xs):
            loss_sum, grad_sum = carry
            l_k, (gw_k, gx_k) = one_microbatch(xs[0], xs[1])
            with _jax.named_scope("update"):
                return (loss_sum + l_k, _jax.tree.map(_jnp.add, grad_sum, gw_k)), gx_k

        init = (_jnp.zeros((), _jnp.float32), _jax.tree.map(_jnp.zeros_like, weights))
        (loss, grad_w), grad_x = _jax.lax.scan(body, init, (per_example, given["loss_target"]))
    with _jax.named_scope("update"):
        delta_w, new_m, new_v = {}, {}, {}
        for n in TWIN_WEIGHTS:
            delta_w[n], new_m[n], new_v[n] = _adamw(weights[n], grad_w[n], given["m_" + n], given["v_" + n])
    return (loss, grad_x, *[grad_w[n] for n in TWIN_WEIGHTS], *[delta_w[n] for n in TWIN_WEIGHTS],
            *[new_m[n] for n in TWIN_WEIGHTS], *[new_v[n] for n in TWIN_WEIGHTS])
```

```python
import functools

import jax
import jax.numpy as jnp
from jax import lax
from jax.experimental import pallas as pl
from jax.experimental.pallas import tpu as pltpu

F32 = jnp.float32
BF16 = jnp.bfloat16
EPS = 1e-6
HEAD_DIM = 64
KV_GROUP = 8
ATT_BLOCK = 128
ROT_DIM = 16
ROPE_THETA = 500000.0
POOL_GROUPS = 4
CONF_TAPS = 31
FFN_TAPS = 3
N_CHIPS = 4
N_DEV = 8
ADAM_LR, ADAM_B1, ADAM_B2, ADAM_EPS, ADAM_WD, ADAM_STEP = 0.001, 0.9, 0.999, 1e-08, 0.01, 10
VMEM_LIMIT_BYTES = 56 * 1024 * 1024
MESH = pl.DeviceIdType.MESH

CONF_NAMES = ["norm_g", "a_w_in", "a_b_in", "a_dw_w", "a_dw_b", "a_ln_g", "a_ln_b", "a_w_out", "a_b_out"]
FFN_NAMES = ["ffn_norm_g", "ffn_w_up", "ffn_dw_w", "ffn_dw_b", "ffn_w_down"]
POOL_NAMES = ["norm_g", "b_w_group", "b_scale"]
ATT_NAMES = ["norm_g", "c_w_qkv", "c_q_norm_g", "c_k_norm_g", "c_sinks", "c_w_o"]
WEIGHT_NAMES = ([f"l0_{n}" for n in CONF_NAMES + FFN_NAMES] + [f"l1_{n}" for n in POOL_NAMES + FFN_NAMES]
                + [f"l2_{n}" for n in ATT_NAMES + FFN_NAMES] + [f"l3_{n}" for n in CONF_NAMES + FFN_NAMES])
COL_SHARDED = ("a_w_in", "ffn_w_up", "c_w_qkv")
ROW_SHARDED = ("a_w_out", "ffn_w_down", "c_w_o")
TAP_SHARDED = ("a_dw_w", "ffn_dw_w")


def _kind(name):
    base = name[3:]
    if base in COL_SHARDED:
        return "col"
    if base in ROW_SHARDED:
        return "row"
    if base in TAP_SHARDED:
        return "tap"
    if base == "b_w_group":
        return "grp"
    return "rep"


def _pick(n, prefs):
    for p in prefs:
        if p <= n and n % p == 0:
            return p
    return n


def _params(*sem):
    return pltpu.CompilerParams(dimension_semantics=sem, vmem_limit_bytes=VMEM_LIMIT_BYTES)


def _sigmoid(x):
    return 1.0 / (1.0 + jnp.exp(-x))


NN = (((1,), (0,)), ((), ()))
NT = (((1,), (1,)), ((), ()))
TN = (((0,), (0,)), ((), ()))


def _mm(a, b, *, dims, grid, a_spec, b_spec, o_spec, out_shape, acc_shape, name,
        bias=None, scale=None, resid=None, vec_spec=None, raw_shape=None):
    nk = grid[2]
    has_bias, has_scale, has_resid, want_raw = bias is not None, scale is not None, resid is not None, raw_shape is not None

    def body(*refs):
        a_ref, b_ref = refs[0], refs[1]
        pos = 2
        bias_ref = scale_ref = resid_ref = raw_ref = None
        if has_bias:
            bias_ref = refs[pos]; pos += 1
        if has_scale:
            scale_ref = refs[pos]; pos += 1
        if has_resid:
            resid_ref = refs[pos]; pos += 1
        o_ref = refs[pos]; pos += 1
        if want_raw:
            raw_ref = refs[pos]; pos += 1
        acc_ref = refs[pos]
        k = pl.program_id(2)
        part = lax.dot_general(a_ref[...].astype(BF16), b_ref[...].astype(BF16), dims, preferred_element_type=F32)

        @pl.when(k == 0)
        def _():
            acc_ref[...] = part

        @pl.when(k > 0)
        def _():
            acc_ref[...] += part

        @pl.when(k == nk - 1)
        def _():
            r = acc_ref[...]
            if want_raw:
                raw_ref[...] = r.astype(raw_ref.dtype)
            if has_bias:
                r = r + bias_ref[...]
            if has_scale:
                r = r * scale_ref[...]
            if has_resid:
                r = r + resid_ref[...]
            o_ref[...] = r.astype(o_ref.dtype)

    operands, in_specs = [a, b], [a_spec, b_spec]
    for v in (bias, scale):
        if v is not None:
            operands.append(v); in_specs.append(vec_spec)
    if has_resid:
        operands.append(resid); in_specs.append(o_spec)
    out_shapes, out_specs = out_shape, o_spec
    if want_raw:
        out_shapes, out_specs = (out_shape, raw_shape), (o_spec, o_spec)
    return pl.pallas_call(
        body, grid=grid, in_specs=in_specs, out_specs=out_specs, out_shape=out_shapes,
        scratch_shapes=[pltpu.VMEM(acc_shape, F32)], name=name,
        compiler_params=_params("parallel", "parallel", "arbitrary"))(*operands)


def _mm_tiles(m, n, k):
    return _pick(m, (1024, 512, 256, 128)), _pick(n, (512, 256, 640, 128)), _pick(k, (2048, 1024, 512, 256, 128))


def mm_nn_cols(a, g, *, split, out_dtype, bias=None, name):
    t, k = a.shape
    ns = g.shape[2]
    tm, tn, tk = _mm_tiles(t, ns, k)
    nj = ns // tn
    if split:
        o_spec = pl.BlockSpec((None, tm, tn), lambda i, j, kk: (j // (2 * nj), i, j % (2 * nj)))
        out_shape = jax.ShapeDtypeStruct((2, t, 2 * ns), out_dtype)
        vec_spec = pl.BlockSpec((None, 1, tn), lambda i, j, kk: (j // (2 * nj), 0, j % (2 * nj)))
        if bias is not None:
            bias = bias.reshape(2, 1, 2 * ns)
    else:
        o_spec = pl.BlockSpec((tm, tn), lambda i, j, kk: (i, j))
        out_shape = jax.ShapeDtypeStruct((t, 4 * ns), out_dtype)
        vec_spec = pl.BlockSpec((1, tn), lambda i, j, kk: (0, j))
        if bias is not None:
            bias = bias.reshape(1, 4 * ns)
    return _mm(a, g, dims=NN, grid=(t // tm, 4 * nj, k // tk),
               a_spec=pl.BlockSpec((tm, tk), lambda i, j, kk: (i, kk)),
               b_spec=pl.BlockSpec((None, tk, tn), lambda i, j, kk: (j // nj, kk, j % nj)),
               o_spec=o_spec, out_shape=out_shape, acc_shape=(tm, tn), name=name, bias=bias, vec_spec=vec_spec)


def mm_nn(a, w, *, out_dtype, bias=None, scale=None, resid=None, raw_dtype=None, name):
    t, k = a.shape
    n = w.shape[1]
    tm, tn, tk = _mm_tiles(t, n, k)
    raw_shape = None if raw_dtype is None else jax.ShapeDtypeStruct((t, n), raw_dtype)
    return _mm(a, w, dims=NN, grid=(t // tm, n // tn, k // tk),
               a_spec=pl.BlockSpec((tm, tk), lambda i, j, kk: (i, kk)),
               b_spec=pl.BlockSpec((tk, tn), lambda i, j, kk: (kk, j)),
               o_spec=pl.BlockSpec((tm, tn), lambda i, j, kk: (i, j)),
               out_shape=jax.ShapeDtypeStruct((t, n), out_dtype), acc_shape=(tm, tn), name=name,
               bias=None if bias is None else bias.reshape(1, n), scale=None if scale is None else scale.reshape(1, n),
               resid=resid, vec_spec=pl.BlockSpec((1, tn), lambda i, j, kk: (0, j)), raw_shape=raw_shape)


def mm_nt(dy, w, *, out_dtype, name):
    t, n = dy.shape
    kdim = w.shape[0]
    tm, tn, tk = _mm_tiles(t, kdim, n)
    return _mm(dy, w, dims=NT, grid=(t // tm, kdim // tn, n // tk),
               a_spec=pl.BlockSpec((tm, tk), lambda i, j, kk: (i, kk)),
               b_spec=pl.BlockSpec((tn, tk), lambda i, j, kk: (j, kk)),
               o_spec=pl.BlockSpec((tm, tn), lambda i, j, kk: (i, j)),
               out_shape=jax.ShapeDtypeStruct((t, kdim), out_dtype), acc_shape=(tm, tn), name=name)


def mm_nt_cols(du, g, *, split, out_dtype, name):
    kdim, ns = g.shape[1], g.shape[2]
    t = du.shape[1] if split else du.shape[0]
    tm, tn, _ = _mm_tiles(t, kdim, ns)
    tk = _pick(ns, (1408, 1024, 640, 512, 256, 128))
    nkb = ns // tk
    if split:
        a_spec = pl.BlockSpec((None, tm, tk), lambda i, j, kk: (kk // (2 * nkb), i, kk % (2 * nkb)))
    else:
        a_spec = pl.BlockSpec((tm, tk), lambda i, j, kk: (i, kk))
    return _mm(du, g, dims=NT, grid=(t // tm, kdim // tn, 4 * nkb), a_spec=a_spec,
               b_spec=pl.BlockSpec((None, tn, tk), lambda i, j, kk: (kk // nkb, j, kk % nkb)),
               o_spec=pl.BlockSpec((tm, tn), lambda i, j, kk: (i, j)),
               out_shape=jax.ShapeDtypeStruct((t, kdim), out_dtype), acc_shape=(tm, tn), name=name)


def mm_tn(a, dy, *, out_dtype, name):
    t, m = a.shape
    n = dy.shape[1]
    tm, tn, tk = _mm_tiles(m, n, t)
    tk = _pick(t, (1024, 512, 256, 128))
    return _mm(a, dy, dims=TN, grid=(m // tm, n // tn, t // tk),
               a_spec=pl.BlockSpec((tk, tm), lambda i, j, kk: (kk, i)),
               b_spec=pl.BlockSpec((tk, tn), lambda i, j, kk: (kk, j)),
               o_spec=pl.BlockSpec((tm, tn), lambda i, j, kk: (i, j)),
               out_shape=jax.ShapeDtypeStruct((m, n), out_dtype), acc_shape=(tm, tn), name=name)


def mm_tn_cols(h, du, *, split, out_dtype, name):
    t, kdim = h.shape
    ns = (du.shape[2] // 2) if split else (du.shape[1] // 4)
    tm, tn, _ = _mm_tiles(kdim, ns, t)
    tk = _pick(t, (1024, 512, 256, 128))
    nj = ns // tn
    if split:
        b_spec = pl.BlockSpec((None, tk, tn), lambda i, j, kk: (j // (2 * nj), kk, j % (2 * nj)))
    else:
        b_spec = pl.BlockSpec((tk, tn), lambda i, j, kk: (kk, j))
    return _mm(h, du, dims=TN, grid=(kdim // tm, 4 * nj, t // tk),
               a_spec=pl.BlockSpec((tk, tm), lambda i, j, kk: (kk, i)), b_spec=b_spec,
               o_spec=pl.BlockSpec((None, tm, tn), lambda i, j, kk: (j // nj, i, j % nj)),
               out_shape=jax.ShapeDtypeStruct((4, kdim, ns), out_dtype), acc_shape=(tm, tn), name=name)


def _row_tile(t):
    return _pick(t, (256, 128))


def _acc_rows(ref, part, i):
    @pl.when(i == 0)
    def _():
        ref[...] = part

    @pl.when(i > 0)
    def _():
        ref[...] += part


def rms_fwd(x, g, out_dtype):
    t, d = x.shape
    tr = _row_tile(t)

    def body(x_ref, g_ref, o_ref):
        xf = x_ref[...]
        r = lax.rsqrt(jnp.mean(xf * xf, axis=-1, keepdims=True) + EPS)
        o_ref[...] = (xf * r * g_ref[...]).astype(o_ref.dtype)

    row = pl.BlockSpec((tr, d), lambda i: (i, 0))
    return pl.pallas_call(body, grid=(t // tr,), in_specs=[row, pl.BlockSpec((1, d), lambda i: (0, 0))], out_specs=row,
                          out_shape=jax.ShapeDtypeStruct((t, d), out_dtype), name="rms_fwd",
                          compiler_params=_params("parallel"))(x, g.reshape(1, d))


def rms_bwd(x, g, dh, dres):
    t, d = x.shape
    tr = _row_tile(t)

    def body(x_ref, g_ref, dh_ref, dres_ref, dx_ref, dg_ref):
        i = pl.program_id(0)
        xf = x_ref[...]
        r = lax.rsqrt(jnp.mean(xf * xf, axis=-1, keepdims=True) + EPS)
        xhat = xf * r
        dhf = dh_ref[...].astype(F32)
        dxh = dhf * g_ref[...]
        m = jnp.mean(dxh * xhat, axis=-1, keepdims=True)
        dx_ref[...] = dres_ref[...] + r * (dxh - xhat * m)
        _acc_rows(dg_ref, jnp.sum(dhf * xhat, axis=0, keepdims=True), i)

    row = pl.BlockSpec((tr, d), lambda i: (i, 0))
    vec = pl.BlockSpec((1, d), lambda i: (0, 0))
    return pl.pallas_call(body, grid=(t // tr,), in_specs=[row, vec, row, row], out_specs=(row, vec),
                          out_shape=(jax.ShapeDtypeStruct((t, d), F32), jax.ShapeDtypeStruct((1, d), F32)),
                          name="rms_bwd", compiler_params=_params("arbitrary"))(x, g.reshape(1, d), dh, dres)


def ln_silu_fwd(c, g, b):
    t, d = c.shape
    tr = _row_tile(t)

    def body(c_ref, g_ref, b_ref, o_ref):
        xf = c_ref[...]
        mu = jnp.mean(xf, axis=-1, keepdims=True)
        xc = xf - mu
        var = jnp.mean(xc * xc, axis=-1, keepdims=True)
        n = xc * lax.rsqrt(var + EPS) * g_ref[...] + b_ref[...]
        o_ref[...] = (n * _sigmoid(n)).astype(o_ref.dtype)

    row = pl.BlockSpec((tr, d), lambda i: (i, 0))
    vec = pl.BlockSpec((1, d), lambda i: (0, 0))
    return pl.pallas_call(body, grid=(t // tr,), in_specs=[row, vec, vec], out_specs=row,
                          out_shape=jax.ShapeDtypeStruct((t, d), BF16), name="ln_silu_fwd",
                          compiler_params=_params("parallel"))(c, g.reshape(1, d), b.reshape(1, d))


def ln_silu_bwd(c, g, b, ds):
    t, d = c.shape
    tr = _row_tile(t)

    def body(c_ref, g_ref, b_ref, ds_ref, dc_ref, dg_ref, db_ref):
        i = pl.program_id(0)
        xf = c_ref[...]
        mu = jnp.mean(xf, axis=-1, keepdims=True)
        xc = xf - mu
        var = jnp.mean(xc * xc, axis=-1, keepdims=True)
        rstd = lax.rsqrt(var + EPS)
        xhat = xc * rstd
        n = xhat * g_ref[...] + b_ref[...]
        sg = _sigmoid(n)
        dn = ds_ref[...].astype(F32) * (sg * (1.0 + n * (1.0 - sg)))
        dxh = dn * g_ref[...]
        m1 = jnp.mean(dxh, axis=-1, keepdims=True)
        m2 = jnp.mean(dxh * xhat, axis=-1, keepdims=True)
        dc_ref[...] = rstd * (dxh - m1 - xhat * m2)
        _acc_rows(dg_ref, jnp.sum(dn * xhat, axis=0, keepdims=True), i)
        _acc_rows(db_ref, jnp.sum(dn, axis=0, keepdims=True), i)

    row = pl.BlockSpec((tr, d), lambda i: (i, 0))
    vec = pl.BlockSpec((1, d), lambda i: (0, 0))
    vshape = jax.ShapeDtypeStruct((1, d), F32)
    return pl.pallas_call(body, grid=(t // tr,), in_specs=[row, vec, vec, row], out_specs=(row, vec, vec),
                          out_shape=(jax.ShapeDtypeStruct((t, d), F32), vshape, vshape), name="ln_silu_bwd",
                          compiler_params=_params("arbitrary"))(c, g.reshape(1, d), b.reshape(1, d), ds)


def loss_grad(y, tgt):
    t, d = y.shape
    tr = _row_tile(t)

    def body(y_ref, t_ref, dy_ref, sq_ref):
        i = pl.program_id(0)
        err = y_ref[...] - t_ref[...]
        dy_ref[...] = err * (1.0 / d)
        _acc_rows(sq_ref, jnp.sum(err * err, axis=0, keepdims=True), i)

    row = pl.BlockSpec((tr, d), lambda i: (i, 0))
    vec = pl.BlockSpec((1, d), lambda i: (0, 0))
    return pl.pallas_call(body, grid=(t // tr,), in_specs=[row, row], out_specs=(row, vec),
                          out_shape=(jax.ShapeDtypeStruct((t, d), F32), jax.ShapeDtypeStruct((1, d), F32)),
                          name="loss_grad", compiler_params=_params("arbitrary"))(y, tgt)


def col_sum(a):
    t, d = a.shape
    tr = _row_tile(t)

    def body(a_ref, o_ref):
        _acc_rows(o_ref, jnp.sum(a_ref[...].astype(F32), axis=0, keepdims=True), pl.program_id(0))

    return pl.pallas_call(body, grid=(t // tr,), in_specs=[pl.BlockSpec((tr, d), lambda i: (i, 0))],
                          out_specs=pl.BlockSpec((1, d), lambda i: (0, 0)), out_shape=jax.ShapeDtypeStruct((1, d), F32),
                          name="col_sum", compiler_params=_params("arbitrary"))(a)


def scale_bwd(dx, ypre, scale):
    t, d = dx.shape
    tr = _row_tile(t)

    def body(dx_ref, y_ref, s_ref, dy_ref, ds_ref):
        dxf = dx_ref[...]
        dy_ref[...] = (dxf * s_ref[...]).astype(dy_ref.dtype)
        _acc_rows(ds_ref, jnp.sum(dxf * y_ref[...], axis=0, keepdims=True), pl.program_id(0))

    row = pl.BlockSpec((tr, d), lambda i: (i, 0))
    vec = pl.BlockSpec((1, d), lambda i: (0, 0))
    return pl.pallas_call(body, grid=(t // tr,), in_specs=[row, row, vec], out_specs=(row, vec),
                          out_shape=(jax.ShapeDtypeStruct((t, d), BF16), jax.ShapeDtypeStruct((1, d), F32)),
                          name="scale_bwd", compiler_params=_params("arbitrary"))(dx, ypre, scale.reshape(1, d))


def _chunk_rows(t):
    return _pick(t, (256, 128))


def _load_halo(ref, lead, base, rows, t, first, last, pre, post):
    idx = (lambda s, n: (pl.ds(s, n), slice(None))) if lead is None else (lambda s, n: (lead, pl.ds(s, n), slice(None)))
    parts = []
    if pre:
        start = pl.multiple_of(jnp.maximum(base - pre, 0), pre)
        parts.append(ref[idx(start, pre)].astype(F32) * jnp.where(first, 0.0, 1.0))
    parts.append(ref[idx(base, rows)].astype(F32))
    if post:
        start = pl.multiple_of(jnp.minimum(base + rows, t - post), post)
        parts.append(ref[idx(start, post)].astype(F32) * jnp.where(last, 0.0, 1.0))
    return parts[0] if len(parts) == 1 else jnp.concatenate(parts, axis=0)


def _fold8(x):
    r, c = x.shape
    return x.reshape(r // 8, 8, c).sum(axis=0)


def ffn_gate_fwd(u2, w3, b2):
    _, t, f = u2.shape
    tc = _pick(f, (256, 128))
    rows = _chunk_rows(t)
    nch = t // rows
    halo = 16

    def body(u_ref, w_ref, b_ref, a_ref):
        def conv(p, base, first):
            xs = _load_halo(u_ref, p, base, rows, t, first, False, halo, 0)
            wp = w_ref[p]
            return (wp[0:1] * pltpu.roll(xs, 2, 0)[halo:] + wp[1:2] * pltpu.roll(xs, 1, 0)[halo:]
                    + wp[2:3] * xs[halo:] + b_ref[p])

        def chunk(i, carry):
            base = pl.multiple_of(i * rows, rows)
            gate, val = conv(0, base, i == 0), conv(1, base, i == 0)
            a_ref[pl.ds(base, rows), :] = (gate * _sigmoid(gate) * val).astype(a_ref.dtype)
            return carry

        lax.fori_loop(0, nch, chunk, 0)

    return pl.pallas_call(
        body, grid=(f // tc,),
        in_specs=[pl.BlockSpec((2, t, tc), lambda j: (0, 0, j)), pl.BlockSpec((2, 3, tc), lambda j: (0, 0, j)),
                  pl.BlockSpec((2, 1, tc), lambda j: (0, 0, j))],
        out_specs=pl.BlockSpec((t, tc), lambda j: (0, j)), out_shape=jax.ShapeDtypeStruct((t, f), BF16),
        name="ffn_gate_fwd", compiler_params=_params("parallel"))(u2, w3, b2)


def ffn_gate_bwd(u2, da, w3, b2):
    _, t, f = u2.shape
    tc = _pick(f, (256, 128))
    rows = _chunk_rows(t)
    nch = t // rows
    halo = 16
    n = rows + 2 * halo

    def body(u_ref, da_ref, w_ref, b_ref, du_ref, dw_ref, db_ref, acc_ref):
        acc_ref[...] = jnp.zeros_like(acc_ref)

        def chunk(i, carry):
            base = pl.multiple_of(i * rows, rows)
            first, last = i == 0, i == nch - 1
            daf = jnp.concatenate(
                [jnp.zeros((halo, tc), F32), _load_halo(da_ref, None, base, rows, t, first, last, 0, halo)], axis=0)
            pre, shifted = [], []
            for p in range(2):
                xs = _load_halo(u_ref, p, base, rows, t, first, last, halo, halo)
                x1, x2 = pltpu.roll(xs, 1, 0), pltpu.roll(xs, 2, 0)
                wp = w_ref[p]
                pre.append(wp[0:1] * x2 + wp[1:2] * x1 + wp[2:3] * xs + b_ref[p])
                shifted.append((x2, x1, xs))
            gate, val = pre
            sg = _sigmoid(gate)
            d_pre = (daf * val * (sg * (1.0 + gate * (1.0 - sg))), daf * gate * sg)
            for p in range(2):
                dp = d_pre[p]
                wp = w_ref[p]
                du = wp[2:3] * dp + wp[1:2] * pltpu.roll(dp, n - 1, 0) + wp[0:1] * pltpu.roll(dp, n - 2, 0)
                du_ref[p, pl.ds(base, rows), :] = du[halo:halo + rows].astype(du_ref.dtype)
                own = dp[halo:halo + rows]
                for k in range(3):
                    acc_ref[p, k] += _fold8(own * shifted[p][k][halo:halo + rows])
                acc_ref[p, 3] += _fold8(own)
            return carry

        lax.fori_loop(0, nch, chunk, 0)
        for p in range(2):
            for k in range(3):
                dw_ref[p, k:k + 1, :] = jnp.sum(acc_ref[p, k], axis=0, keepdims=True)
            db_ref[p] = jnp.sum(acc_ref[p, 3], axis=0, keepdims=True)

    blk = pl.BlockSpec((2, t, tc), lambda j: (0, 0, j))
    wspec = pl.BlockSpec((2, 3, tc), lambda j: (0, 0, j))
    bspec = pl.BlockSpec((2, 1, tc), lambda j: (0, 0, j))
    return pl.pallas_call(
        body, grid=(f // tc,), in_specs=[blk, pl.BlockSpec((t, tc), lambda j: (0, j)), wspec, bspec],
        out_specs=(blk, wspec, bspec),
        out_shape=(jax.ShapeDtypeStruct((2, t, f), BF16), jax.ShapeDtypeStruct((2, 3, f), F32),
                   jax.ShapeDtypeStruct((2, 1, f), F32)),
        scratch_shapes=[pltpu.VMEM((2, 4, 8, tc), F32)], name="ffn_gate_bwd",
        compiler_params=_params("parallel"))(u2, da, w3, b2)


def glu_conv_fwd(u2, w, b):
    _, t, d = u2.shape
    taps = w.shape[0]
    tc = 128
    rows = _chunk_rows(t)
    nch = t // rows
    halo = 32

    def body(u_ref, w_ref, b_ref, c_ref):
        def chunk(i, carry):
            base = pl.multiple_of(i * rows, rows)
            a = _load_halo(u_ref, 0, base, rows, t, i == 0, False, halo, 0)
            g = _load_halo(u_ref, 1, base, rows, t, i == 0, False, halo, 0)
            xs = a * _sigmoid(g)
            acc = w_ref[taps - 1:taps, :] * xs[halo:] + b_ref[...]
            for j in range(taps - 1):
                acc = acc + w_ref[j:j + 1, :] * pltpu.roll(xs, taps - 1 - j, 0)[halo:]
            c_ref[pl.ds(base, rows), :] = acc
            return carry

        lax.fori_loop(0, nch, chunk, 0)

    return pl.pallas_call(
        body, grid=(d // tc,),
        in_specs=[pl.BlockSpec((2, t, tc), lambda j: (0, 0, j)), pl.BlockSpec((taps, tc), lambda j: (0, j)),
                  pl.BlockSpec((1, tc), lambda j: (0, j))],
        out_specs=pl.BlockSpec((t, tc), lambda j: (0, j)), out_shape=jax.ShapeDtypeStruct((t, d), F32),
        name="glu_conv_fwd", compiler_params=_params("parallel"))(u2, w, b)


def glu_conv_bwd(u2, dc, w):
    _, t, d = u2.shape
    taps = w.shape[0]
    tc = 128
    rows = _chunk_rows(t)
    nch = t // rows
    halo = 32
    n = rows + halo

    def body(u_ref, dc_ref, w_ref, du_ref, dw_ref, dwb_ref, dbin_ref, acc_ref, bacc_ref):
        acc_ref[...] = jnp.zeros_like(acc_ref)
        bacc_ref[...] = jnp.zeros_like(bacc_ref)

        def chunk(i, carry):
            base = pl.multiple_of(i * rows, rows)
            first, last = i == 0, i == nch - 1
            a = _load_halo(u_ref, 0, base, rows, t, first, False, halo, 0)
            g = _load_halo(u_ref, 1, base, rows, t, first, False, halo, 0)
            sg = _sigmoid(g)
            xs = a * sg
            dcs = _load_halo(dc_ref, None, base, rows, t, first, last, 0, halo)
            own = dcs[:rows]
            dglu = w_ref[taps - 1:taps, :] * own
            acc_ref[taps - 1] += _fold8(own * xs[halo:])
            for j in range(taps - 1):
                s = taps - 1 - j
                dglu = dglu + w_ref[j:j + 1, :] * pltpu.roll(dcs, n - s, 0)[:rows]
                acc_ref[j] += _fold8(own * pltpu.roll(xs, s, 0)[halo:])
            a_c, sg_c = a[halo:], sg[halo:]
            da = dglu * sg_c
            dg = dglu * a_c * sg_c * (1.0 - sg_c)
            du_ref[0, pl.ds(base, rows), :] = da.astype(du_ref.dtype)
            du_ref[1, pl.ds(base, rows), :] = dg.astype(du_ref.dtype)
            bacc_ref[0] += _fold8(own)
            bacc_ref[1] += _fold8(da)
            bacc_ref[2] += _fold8(dg)
            return carry

        lax.fori_loop(0, nch, chunk, 0)
        for j in range(taps):
            dw_ref[j:j + 1, :] = jnp.sum(acc_ref[j], axis=0, keepdims=True)
        dwb_ref[...] = jnp.sum(bacc_ref[0], axis=0, keepdims=True)
        dbin_ref[0] = jnp.sum(bacc_ref[1], axis=0, keepdims=True)
        dbin_ref[1] = jnp.sum(bacc_ref[2], axis=0, keepdims=True)

    blk = pl.BlockSpec((2, t, tc), lambda j: (0, 0, j))
    col = pl.BlockSpec((t, tc), lambda j: (0, j))
    return pl.pallas_call(
        body, grid=(d // tc,), in_specs=[blk, col, pl.BlockSpec((taps, tc), lambda j: (0, j))],
        out_specs=(blk, pl.BlockSpec((taps, tc), lambda j: (0, j)), pl.BlockSpec((1, tc), lambda j: (0, j)),
                   pl.BlockSpec((2, 1, tc), lambda j: (0, 0, j))),
        out_shape=(jax.ShapeDtypeStruct((2, t, d), BF16), jax.ShapeDtypeStruct((taps, d), F32),
                   jax.ShapeDtypeStruct((1, d), F32), jax.ShapeDtypeStruct((2, 1, d), F32)),
        scratch_shapes=[pltpu.VMEM((taps, 8, tc), F32), pltpu.VMEM((3, 8, tc), F32)], name="glu_conv_bwd",
        compiler_params=_params("parallel"))(u2, dc, w)


def _pool_select(grp, levels):
    out = levels[3]
    for k in (2, 1, 0):
        out = jnp.where(grp == k, levels[k], out)
    return out


def _pool_count(base, rows, tc, grp):
    tpos = (base + lax.broadcasted_iota(jnp.int32, (rows, tc), 0) + 1).astype(F32)
    window = jnp.left_shift(2, grp).astype(F32)
    return jnp.minimum(tpos, window)


def pool_fwd(h):
    t, d = h.shape
    pg = d // POOL_GROUPS
    tc = _pick(pg, (256, 128))
    rows = _chunk_rows(t)
    nch = t // rows
    halo = 16

    def body(h_ref, o_ref):
        grp = (pl.program_id(0) * tc) // pg

        def chunk(i, carry):
            base = pl.multiple_of(i * rows, rows)
            xs = _load_halo(h_ref, None, base, rows, t, i == 0, False, halo, 0)
            levels, cur = [], xs
            for k in range(4):
                cur = cur + pltpu.roll(cur, 1 << k, 0)
                levels.append(cur[halo:])
            pooled = _pool_select(grp, levels) / _pool_count(base, rows, tc, grp)
            o_ref[pl.ds(base, rows), :] = (pooled - xs[halo:]).astype(o_ref.dtype)
            return carry

        lax.fori_loop(0, nch, chunk, 0)

    col = pl.BlockSpec((t, tc), lambda j: (0, j))
    return pl.pallas_call(body, grid=(d // tc,), in_specs=[col], out_specs=col,
                          out_shape=jax.ShapeDtypeStruct((t, d), BF16), name="pool_fwd",
                          compiler_params=_params("parallel"))(h)


def pool_bwd(dmix):
    t, d = dmix.shape
    pg = d // POOL_GROUPS
    tc = _pick(pg, (256, 128))
    rows = _chunk_rows(t)
    nch = t // rows
    halo = 16
    n = rows + halo

    def body(d_ref, o_ref):
        grp = (pl.program_id(0) * tc) // pg

        def chunk(i, carry):
            base = pl.multiple_of(i * rows, rows)
            ds = _load_halo(d_ref, None, base, rows, t, i == 0, i == nch - 1, 0, halo)
            levels, cur = [], ds / _pool_count(base, n, tc, grp)
            for k in range(4):
                cur = cur + pltpu.roll(cur, n - (1 << k), 0)
                levels.append(cur[:rows])
            o_ref[pl.ds(base, rows), :] = _pool_select(grp, levels) - ds[:rows]
            return carry

        lax.fori_loop(0, nch, chunk, 0)

    col = pl.BlockSpec((t, tc), lambda j: (0, j))
    return pl.pallas_call(body, grid=(d // tc,), in_specs=[col], out_specs=col,
                          out_shape=jax.ShapeDtypeStruct((t, d), F32), name="pool_bwd",
                          compiler_params=_params("parallel"))(dmix)


def mm_groups(a, wg, *, mode, out_dtype, scale=None, resid=None, raw_dtype=None, name):
    t, d = a.shape
    pg = wg.shape[1]
    tm = _pick(t, (1024, 512, 256, 128))
    tn = _pick(pg, (256, 128))
    nj = pg // tn
    if mode == "nn":
        dims, b_spec = NN, pl.BlockSpec((None, pg, tn), lambda i, j, kk: (j // nj, 0, j % nj))
    else:
        dims, b_spec = NT, pl.BlockSpec((None, tn, pg), lambda i, j, kk: (j // nj, j % nj, 0))
    raw_shape = None if raw_dtype is None else jax.ShapeDtypeStruct((t, d), raw_dtype)
    return _mm(a, wg, dims=dims, grid=(t // tm, d // tn, 1),
               a_spec=pl.BlockSpec((tm, pg), lambda i, j, kk: (i, j // nj)), b_spec=b_spec,
               o_spec=pl.BlockSpec((tm, tn), lambda i, j, kk: (i, j)),
               out_shape=jax.ShapeDtypeStruct((t, d), out_dtype), acc_shape=(tm, tn), name=name,
               scale=None if scale is None else scale.reshape(1, d), resid=resid,
               vec_spec=pl.BlockSpec((1, tn), lambda i, j, kk: (0, j)), raw_shape=raw_shape)


def mm_groups_tn(a, dy, groups, *, out_dtype, name):
    t, d = a.shape
    pg = d // groups
    tk = _pick(t, (1024, 512, 256, 128))
    blk = pl.BlockSpec((tk, pg), lambda i, j, kk: (kk, i))
    return _mm(a, dy, dims=TN, grid=(groups, 1, t // tk), a_spec=blk, b_spec=blk,
               o_spec=pl.BlockSpec((None, pg, pg), lambda i, j, kk: (i, 0, 0)),
               out_shape=jax.ShapeDtypeStruct((groups, pg, pg), out_dtype), acc_shape=(pg, pg), name=name)


def _split_dot(y, p):
    hi = y.astype(BF16)
    r1 = y - hi.astype(F32)
    mid = r1.astype(BF16)
    lo = (r1 - mid.astype(F32)).astype(BF16)
    pb = p.astype(BF16)
    dot = lambda v: jnp.dot(v, pb, preferred_element_type=F32)
    return (dot(hi) + dot(mid)) + dot(lo)


def rope_tables(positions):
    half = ROT_DIM // 2
    inv_freq = ROPE_THETA ** (-jnp.arange(0, ROT_DIM, 2, dtype=F32) / ROT_DIM)
    ang = positions.astype(F32)[:, None] * inv_freq
    t = positions.shape[0]
    cos, sin = jnp.cos(ang), jnp.sin(ang)
    rest = HEAD_DIM - ROT_DIM
    cosf = jnp.concatenate([cos, cos, jnp.ones((t, rest), F32)], axis=1)
    sinf = jnp.concatenate([-sin, sin, jnp.zeros((t, rest), F32)], axis=1)
    idx = jnp.arange(HEAD_DIM)
    partner = jnp.where(idx < half, idx + half, jnp.where(idx < ROT_DIM, idx - half, idx))
    pmat = (idx[:, None] == partner[None, :]).astype(F32)
    return cosf, sinf, pmat


def qk_rope_fwd(x, g, cosf, sinf, pmat, out_scale):
    hn, t, hd = x.shape
    tq = _pick(t, (512, 256, 128))

    def body(x_ref, g_ref, c_ref, s_ref, p_ref, o_ref):
        xf = x_ref[...]
        r = lax.rsqrt(jnp.mean(xf * xf, axis=-1, keepdims=True) + EPS)
        y = xf * r * g_ref[...]
        rot = y * c_ref[...] + _split_dot(y, p_ref[...]) * s_ref[...]
        o_ref[...] = (rot * out_scale).astype(o_ref.dtype)

    blk = pl.BlockSpec((None, tq, hd), lambda h, i: (h, i, 0))
    tab = pl.BlockSpec((tq, hd), lambda h, i: (i, 0))
    return pl.pallas_call(
        body, grid=(hn, t // tq),
        in_specs=[blk, pl.BlockSpec((1, hd), lambda h, i: (0, 0)), tab, tab, pl.BlockSpec((hd, hd), lambda h, i: (0, 0))],
        out_specs=blk, out_shape=jax.ShapeDtypeStruct((hn, t, hd), BF16), name="qk_rope_fwd",
        compiler_params=_params("parallel", "parallel"))(x, g.reshape(1, hd), cosf, sinf, pmat)


def qk_rope_bwd(dy, x, g, cosf, sinf, pmat_t, in_scale):
    hn, t, hd = x.shape
    tq = _pick(t, (512, 256, 128))

    def body(dy_ref, x_ref, g_ref, c_ref, s_ref, p_ref, dx_ref, dg_ref):
        step = pl.program_id(0) * pl.num_programs(1) + pl.program_id(1)
        dr = dy_ref[...] * in_scale
        dyn = dr * c_ref[...] + _split_dot(dr * s_ref[...], p_ref[...])
        xf = x_ref[...]
        r = lax.rsqrt(jnp.mean(xf * xf, axis=-1, keepdims=True) + EPS)
        xhat = xf * r
        dxh = dyn * g_ref[...]
        m = jnp.mean(dxh * xhat, axis=-1, keepdims=True)
        dx_ref[...] = r * (dxh - xhat * m)
        _acc_rows(dg_ref, jnp.sum(dyn * xhat, axis=0, keepdims=True), step)

    blk = pl.BlockSpec((None, tq, hd), lambda h, i: (h, i, 0))
    tab = pl.BlockSpec((tq, hd), lambda h, i: (i, 0))
    vec = pl.BlockSpec((1, hd), lambda h, i: (0, 0))
    return pl.pallas_call(
        body, grid=(hn, t // tq),
        in_specs=[blk, blk, vec, tab, tab, pl.BlockSpec((hd, hd), lambda h, i: (0, 0))],
        out_specs=(blk, vec), out_shape=(jax.ShapeDtypeStruct((hn, t, hd), F32), jax.ShapeDtypeStruct((1, hd), F32)),
        name="qk_rope_bwd", compiler_params=_params("arbitrary", "arbitrary"))(dy, x, g.reshape(1, hd), cosf, sinf, pmat_t)


NEG_BIG = -1e30


def _att_masks(i):
    qi = lax.broadcasted_iota(jnp.int32, (ATT_BLOCK, ATT_BLOCK), 0)
    kj = lax.broadcasted_iota(jnp.int32, (ATT_BLOCK, ATT_BLOCK), 1)
    return kj <= qi, jnp.logical_and(kj > qi, i > 0)


def _att_probs(q, kc, kp, mask_c, mask_p, sink):
    s_c = jnp.where(mask_c, lax.dot_general(q, kc, NT, preferred_element_type=F32), NEG_BIG)
    s_p = jnp.where(mask_p, lax.dot_general(q, kp, NT, preferred_element_type=F32), NEG_BIG)
    m = jnp.maximum(jnp.maximum(jnp.max(s_c, axis=-1, keepdims=True), jnp.max(s_p, axis=-1, keepdims=True)), sink)
    p_c, p_p = jnp.exp(s_c - m), jnp.exp(s_p - m)
    p_s = jnp.exp(sink - m)
    denom = jnp.sum(p_c, axis=-1, keepdims=True) + jnp.sum(p_p, axis=-1, keepdims=True) + p_s
    return p_c, p_p, p_s, denom


def _att_specs(t):
    nb = t // ATT_BLOCK
    qblk = pl.BlockSpec((KV_GROUP, ATT_BLOCK, HEAD_DIM), lambda kv, i: (kv, i, 0))
    cur = pl.BlockSpec((None, ATT_BLOCK, HEAD_DIM), lambda kv, i: (kv, i, 0))
    prev = pl.BlockSpec((None, ATT_BLOCK, HEAD_DIM), lambda kv, i: (kv, jnp.maximum(i - 1, 0), 0))
    return nb, qblk, cur, prev, pl.BlockSpec(memory_space=pltpu.SMEM)


def attn_fwd(q, k, v, sinks):
    h, t, hd = q.shape
    nb, qblk, cur, prev, smem = _att_specs(t)

    def body(q_ref, kc_ref, kp_ref, vc_ref, vp_ref, sink_ref, o_ref):
        kv, i = pl.program_id(0), pl.program_id(1)
        mask_c, mask_p = _att_masks(i)
        kc, kp, vc, vp = kc_ref[...], kp_ref[...], vc_ref[...], vp_ref[...]
        for g in range(KV_GROUP):
            p_c, p_p, _, denom = _att_probs(q_ref[g], kc, kp, mask_c, mask_p, sink_ref[kv * KV_GROUP + g])
            o = (jnp.dot(p_c.astype(BF16), vc, preferred_element_type=F32)
                 + jnp.dot(p_p.astype(BF16), vp, preferred_element_type=F32)) / denom
            o_ref[g] = o.astype(o_ref.dtype)

    return pl.pallas_call(
        body, grid=(h // KV_GROUP, nb), in_specs=[qblk, cur, prev, cur, prev, smem], out_specs=qblk,
        out_shape=jax.ShapeDtypeStruct((h, t, hd), BF16), name="attn_fwd",
        compiler_params=_params("parallel", "parallel"))(q, k, k, v, v, sinks)


def attn_bwd(q, k, v, do, sinks):
    h, t, hd = q.shape
    kvh = h // KV_GROUP
    nb, qblk, cur, prev, smem = _att_specs(t)

    def body(q_ref, kc_ref, kp_ref, vc_ref, vp_ref, do_ref, sink_ref, dq_ref, dk_ref, dv_ref, dsk_ref):
        kv, i = pl.program_id(0), pl.program_id(1)

        @pl.when(i == 0)
        def _():
            dk_ref[...] = jnp.zeros_like(dk_ref)
            dv_ref[...] = jnp.zeros_like(dv_ref)
            dsk_ref[...] = jnp.zeros_like(dsk_ref)

        mask_c, mask_p = _att_masks(i)
        kc, kp, vc, vp = kc_ref[...], kp_ref[...], vc_ref[...], vp_ref[...]
        zero = jnp.zeros((ATT_BLOCK, hd), F32)
        dkc, dkp, dvc, dvp = zero, zero, zero, zero
        dsink_rows = []
        for g in range(KV_GROUP):
            q = q_ref[g]
            p_c, p_p, p_s, denom = _att_probs(q, kc, kp, mask_c, mask_p, sink_ref[kv * KV_GROUP + g])
            inv = 1.0 / denom
            pn_c, pn_p = p_c * inv, p_p * inv
            dob = do_ref[g].astype(BF16)
            dp_c = lax.dot_general(dob, vc, NT, preferred_element_type=F32)
            dp_p = lax.dot_general(dob, vp, NT, preferred_element_type=F32)
            dsum = jnp.sum(pn_c * dp_c, axis=-1, keepdims=True) + jnp.sum(pn_p * dp_p, axis=-1, keepdims=True)
            ds_c = (pn_c * (dp_c - dsum)).astype(BF16)
            ds_p = (pn_p * (dp_p - dsum)).astype(BF16)
            dq_ref[g] = (jnp.dot(ds_c, kc, preferred_element_type=F32) + jnp.dot(ds_p, kp, preferred_element_type=F32))
            dkc = dkc + lax.dot_general(ds_c, q, TN, preferred_element_type=F32)
            dkp = dkp + lax.dot_general(ds_p, q, TN, preferred_element_type=F32)
            dvc = dvc + lax.dot_general(pn_c.astype(BF16), dob, TN, preferred_element_type=F32)
            dvp = dvp + lax.dot_general(pn_p.astype(BF16), dob, TN, preferred_element_type=F32)
            dsink = -jnp.sum(p_s * inv * dsum, axis=0, keepdims=True)
            dsink_rows.append(jnp.broadcast_to(dsink, (1, 128)))
        here = pl.ds(pl.multiple_of(i * ATT_BLOCK, ATT_BLOCK), ATT_BLOCK)
        before = pl.ds(pl.multiple_of(jnp.maximum(i - 1, 0) * ATT_BLOCK, ATT_BLOCK), ATT_BLOCK)
        dk_ref[here, :] += dkc
        dv_ref[here, :] += dvc
        dk_ref[before, :] += dkp
        dv_ref[before, :] += dvp
        dsk_ref[...] += jnp.concatenate(dsink_rows, axis=0)

    whole = pl.BlockSpec((None, t, hd), lambda kv, i: (kv, 0, 0))
    return pl.pallas_call(
        body, grid=(kvh, nb), in_specs=[qblk, cur, prev, cur, prev, qblk, smem],
        out_specs=(qblk, whole, whole, pl.BlockSpec((None, KV_GROUP, 128), lambda kv, i: (kv, 0, 0))),
        out_shape=(jax.ShapeDtypeStruct((h, t, hd), F32), jax.ShapeDtypeStruct((kvh, t, hd), F32),
                   jax.ShapeDtypeStruct((kvh, t, hd), F32), jax.ShapeDtypeStruct((kvh, KV_GROUP, 128), F32)),
        name="attn_bwd", compiler_params=_params("parallel", "arbitrary"))(q, k, k, v, v, do, sinks)


def _ffn_taps(w, b):
    f2 = w.shape[1]
    return w.reshape(FFN_TAPS, 2, f2 // 2).transpose(1, 0, 2), b.reshape(2, 1, f2 // 2)


def ffn_fwd(x, W, p):
    h = rms_fwd(x, W[p + "ffn_norm_g"], BF16)
    u2 = mm_nn_cols(h, W[p + "ffn_w_up"], split=True, out_dtype=BF16, name="ffn_up")
    w3, b2 = _ffn_taps(W[p + "ffn_dw_w"], W[p + "ffn_dw_b"])
    a = ffn_gate_fwd(u2, w3, b2)
    y = mm_nn(a, W[p + "ffn_w_down"], out_dtype=F32, resid=x, name="ffn_down")
    return y, (x, h, u2, a)


def ffn_bwd(saved, W, p, dy):
    x, h, u2, a = saved
    w3, b2 = _ffn_taps(W[p + "ffn_dw_w"], W[p + "ffn_dw_b"])
    grads = {p + "ffn_w_down": mm_tn(a, dy, out_dtype=BF16, name="ffn_down_dw")}
    da = mm_nt(dy, W[p + "ffn_w_down"], out_dtype=BF16, name="ffn_down_dx")
    du2, dw3, db2 = ffn_gate_bwd(u2, da, w3, b2)
    grads[p + "ffn_dw_w"] = dw3.transpose(1, 0, 2).reshape(FFN_TAPS, -1)
    grads[p + "ffn_dw_b"] = db2.reshape(-1)
    grads[p + "ffn_w_up"] = mm_tn_cols(h, du2, split=True, out_dtype=BF16, name="ffn_up_dw")
    dh = mm_nt_cols(du2, W[p + "ffn_w_up"], split=True, out_dtype=F32, name="ffn_up_dx")
    dx, dg = rms_bwd(x, W[p + "ffn_norm_g"], dh, dy)
    grads[p + "ffn_norm_g"] = dg.reshape(-1)
    return dx, grads


def conf_fwd(x, W, p):
    d = x.shape[1]
    h = rms_fwd(x, W[p + "norm_g"], BF16)
    u2 = mm_nn_cols(h, W[p + "a_w_in"], split=True, out_dtype=BF16, bias=W[p + "a_b_in"], name="conf_in")
    c = glu_conv_fwd(u2, W[p + "a_dw_w"], W[p + "a_dw_b"].reshape(1, d))
    s = ln_silu_fwd(c, W[p + "a_ln_g"], W[p + "a_ln_b"])
    y = mm_nn(s, W[p + "a_w_out"], out_dtype=F32, bias=W[p + "a_b_out"], resid=x, name="conf_out")
    return y, (x, h, u2, c, s)


def conf_bwd(saved, W, p, dy):
    x, h, u2, c, s = saved
    grads = {p + "a_w_out": mm_tn(s, dy, out_dtype=BF16, name="conf_out_dw"), p + "a_b_out": col_sum(dy).reshape(-1)}
    ds = mm_nt(dy, W[p + "a_w_out"], out_dtype=BF16, name="conf_out_dx")
    dc, dlg, dlb = ln_silu_bwd(c, W[p + "a_ln_g"], W[p + "a_ln_b"], ds)
    grads[p + "a_ln_g"], grads[p + "a_ln_b"] = dlg.reshape(-1), dlb.reshape(-1)
    du2, ddw, ddwb, dbin = glu_conv_bwd(u2, dc, W[p + "a_dw_w"])
    grads[p + "a_dw_w"], grads[p + "a_dw_b"], grads[p + "a_b_in"] = ddw, ddwb.reshape(-1), dbin.reshape(-1)
    grads[p + "a_w_in"] = mm_tn_cols(h, du2, split=True, out_dtype=BF16, name="conf_in_dw")
    dh = mm_nt_cols(du2, W[p + "a_w_in"], split=True, out_dtype=F32, name="conf_in_dx")
    dx, dg = rms_bwd(x, W[p + "norm_g"], dh, dy)
    grads[p + "norm_g"] = dg.reshape(-1)
    return dx, grads


def pool_layer_fwd(x, W, p):
    h = rms_fwd(x, W[p + "norm_g"], F32)
    mixed = pool_fwd(h)
    y, ypre = mm_groups(mixed, W[p + "b_w_group"], mode="nn", out_dtype=F32, scale=W[p + "b_scale"], resid=x,
                        raw_dtype=F32, name="pool_mix")
    return y, (x, mixed, ypre)


def pool_layer_bwd(saved, W, p, dy):
    x, mixed, ypre = saved
    dyp, dscale = scale_bwd(dy, ypre, W[p + "b_scale"])
    grads = {p + "b_scale": dscale.reshape(-1),
             p + "b_w_group": mm_groups_tn(mixed, dyp, POOL_GROUPS, out_dtype=BF16, name="pool_mix_dw")}
    dmix = mm_groups(dyp, W[p + "b_w_group"], mode="nt", out_dtype=F32, name="pool_mix_dx")
    dh = pool_bwd(dmix)
    dx, dg = rms_bwd(x, W[p + "norm_g"], dh, dy)
    grads[p + "norm_g"] = dg.reshape(-1)
    return dx, grads


def _heads(a, n):
    t = a.shape[0]
    return a.reshape(t, n, HEAD_DIM).transpose(1, 0, 2)


def _unheads(a):
    n, t, _ = a.shape
    return a.transpose(1, 0, 2).reshape(t, n * HEAD_DIM)


def attn_layer_fwd(x, W, p, tables):
    d = x.shape[1]
    nh = d // HEAD_DIM
    nkv = nh // KV_GROUP
    cosf, sinf, pmat = tables
    h = rms_fwd(x, W[p + "norm_g"], BF16)
    qkv = mm_nn_cols(h, W[p + "c_w_qkv"], split=False, out_dtype=F32, name="att_qkv")
    q = _heads(qkv[:, :d], nh)
    k = _heads(qkv[:, d:d + nkv * HEAD_DIM], nkv)
    v = _heads(qkv[:, d + nkv * HEAD_DIM:], nkv).astype(BF16)
    qr = qk_rope_fwd(q, W[p + "c_q_norm_g"], cosf, sinf, pmat, HEAD_DIM ** -0.5)
    kr = qk_rope_fwd(k, W[p + "c_k_norm_g"], cosf, sinf, pmat, 1.0)
    o = attn_fwd(qr, kr, v, W[p + "c_sinks"])
    o2 = _unheads(o)
    y = mm_nn(o2, W[p + "c_w_o"], out_dtype=F32, resid=x, name="att_out")
    return y, (x, h, q, k, v, qr, kr, o2)


def attn_layer_bwd(saved, W, p, dy, tables):
    x, h, q, k, v, qr, kr, o2 = saved
    cosf, sinf, pmat = tables
    nh = q.shape[0]
    grads = {p + "c_w_o": mm_tn(o2, dy, out_dtype=BF16, name="att_out_dw")}
    do = _heads(mm_nt(dy, W[p + "c_w_o"], out_dtype=BF16, name="att_out_dx"), nh)
    dqr, dkr, dv, dsk = attn_bwd(qr, kr, v, do, W[p + "c_sinks"])
    grads[p + "c_sinks"] = dsk[:, :, 0].reshape(-1)
    dq, dqg = qk_rope_bwd(dqr, q, W[p + "c_q_norm_g"], cosf, sinf, pmat.T, HEAD_DIM ** -0.5)
    dk, dkg = qk_rope_bwd(dkr, k, W[p + "c_k_norm_g"], cosf, sinf, pmat.T, 1.0)
    grads[p + "c_q_norm_g"], grads[p + "c_k_norm_g"] = dqg.reshape(-1), dkg.reshape(-1)
    dqkv = jnp.concatenate([_unheads(dq), _unheads(dk), _unheads(dv)], axis=1).astype(BF16)
    grads[p + "c_w_qkv"] = mm_tn_cols(h, dqkv, split=False, out_dtype=BF16, name="att_qkv_dw")
    dh = mm_nt_cols(dqkv, W[p + "c_w_qkv"], split=False, out_dtype=F32, name="att_qkv_dx")
    dx, dg = rms_bwd(x, W[p + "norm_g"], dh, dy)
    grads[p + "norm_g"] = dg.reshape(-1)
    return dx, grads


def local_step(x, positions, tgt, W):
    tables = rope_tables(positions)
    mixers = [(conf_fwd, conf_bwd), (pool_layer_fwd, pool_layer_bwd), (attn_layer_fwd, attn_layer_bwd), (conf_fwd, conf_bwd)]
    saved = []
    for i, (fwd, _) in enumerate(mixers):
        p = f"l{i}_"
        x, sm = fwd(x, W, p, tables) if i == 2 else fwd(x, W, p)
        x, sf = ffn_fwd(x, W, p)
        saved.append((sm, sf))
    dy, sq = loss_grad(x, tgt)
    loss = 0.5 * jnp.sum(sq) / x.shape[1]
    grads = {}
    for i in reversed(range(len(mixers))):
        p = f"l{i}_"
        sm, sf = saved[i]
        dy, g = ffn_bwd(sf, W, p, dy)
        grads.update(g)
        bwd = mixers[i][1]
        dy, g = bwd(sm, W, p, dy, tables) if i == 2 else bwd(sm, W, p, dy)
        grads.update(g)
    return loss, dy, grads


ANY = pl.BlockSpec(memory_space=pl.ANY)


def _place():
    x, y, c = lax.axis_index("x"), lax.axis_index("y"), lax.axis_index("c")
    chips = [(1 - x, y), (x, 1 - y), (1 - x, 1 - y)]
    return x, y, c, 2 * x + y, (x, y, 1 - c), chips


def _half(rows, which):
    return pl.ds(which * (rows // 2), rows // 2)


def all_gather_chips(shards):
    n = len(shards)

    def body(*refs):
        ins, outs = refs[:n], refs[n:2 * n]
        ici_send, ici_recv, d2d_send, d2d_recv, loc_sem = refs[2 * n:]
        x, y, c, k, sibling, chips = _place()

        def rdma(src, dst, send, recv, dev):
            return pltpu.make_async_remote_copy(src_ref=src, dst_ref=dst, send_sem=send, recv_sem=recv,
                                                device_id=dev, device_id_type=MESH)

        local = [pltpu.make_async_copy(ins[t], outs[t].at[k], loc_sem.at[t]) for t in range(n)]
        for cp in local:
            cp.start()
        sends = []
        for t in range(n):
            rows = shards[t].shape[0]
            for j, (px, py) in enumerate(chips):
                sends.append(rdma(ins[t].at[_half(rows, c)], outs[t].at[k, _half(rows, c)],
                                  ici_send.at[t, j], ici_recv.at[t, j], (px, py, c)))
        for cp in sends:
            cp.start()
        for t in range(n):
            rows = shards[t].shape[0]
            for j, (px, py) in enumerate(chips):
                landed = outs[t].at[2 * px + py, _half(rows, c)]
                rdma(landed, landed, ici_send.at[t, j], ici_recv.at[t, j], sibling).wait_recv()
                fwd = rdma(landed, landed, d2d_send.at[t, j], d2d_recv.at[t, j], sibling)
                fwd.start()
                sends.append(fwd)
        for t in range(n):
            rows = shards[t].shape[0]
            for j, (px, py) in enumerate(chips):
                other = outs[t].at[2 * px + py, _half(rows, 1 - c)]
                rdma(other, other, d2d_send.at[t, j], d2d_recv.at[t, j], sibling).wait_recv()
        for cp in sends:
            cp.wait_send()
        for cp in local:
            cp.wait()

    return pl.pallas_call(
        body, in_specs=[ANY] * n, out_specs=[ANY] * n,
        out_shape=[jax.ShapeDtypeStruct((N_CHIPS,) + s.shape, s.dtype) for s in shards],
        scratch_shapes=[pltpu.SemaphoreType.DMA((n, 3))] * 4 + [pltpu.SemaphoreType.DMA((n,))],
        name="all_gather_chips", compiler_params=pltpu.CompilerParams())(*shards)


def exchange_sibling_halves(gs):
    n = len(gs)

    def body(*refs):
        ins, outs = refs[:n], refs[n:2 * n]
        send, recv = refs[2 * n:]
        x, y, c, k, sibling, chips = _place()
        cps = []
        for t in range(n):
            rows = gs[t].shape[1]
            cps.append(pltpu.make_async_remote_copy(
                src_ref=ins[t].at[:, _half(rows, 1 - c), :], dst_ref=outs[t], send_sem=send.at[t], recv_sem=recv.at[t],
                device_id=sibling, device_id_type=MESH))
        for cp in cps:
            cp.start()
        for cp in cps:
            cp.wait()

    return pl.pallas_call(
        body, in_specs=[ANY] * n, out_specs=[ANY] * n,
        out_shape=[jax.ShapeDtypeStruct((g.shape[0], g.shape[1] // 2, g.shape[2]), g.dtype) for g in gs],
        scratch_shapes=[pltpu.SemaphoreType.DMA((n,))] * 2, name="exchange_sibling_halves",
        compiler_params=pltpu.CompilerParams())(*gs)


def exchange_chip_blocks(ps):
    n = len(ps)

    def body(*refs):
        ins, outs = refs[:n], refs[n:2 * n]
        send, recv, loc_sem = refs[2 * n:]
        x, y, c, k, sibling, chips = _place()
        local = [pltpu.make_async_copy(ins[t].at[k], outs[t].at[k], loc_sem.at[t]) for t in range(n)]
        for cp in local:
            cp.start()
        cps = []
        for t in range(n):
            for j, (px, py) in enumerate(chips):
                cps.append(pltpu.make_async_remote_copy(
                    src_ref=ins[t].at[2 * px + py], dst_ref=outs[t].at[k], send_sem=send.at[t, j], recv_sem=recv.at[t, j],
                    device_id=(px, py, c), device_id_type=MESH))
        for cp in cps:
            cp.start()
        for t in range(n):
            for j, (px, py) in enumerate(chips):
                got = outs[t].at[2 * px + py]
                pltpu.make_async_remote_copy(src_ref=got, dst_ref=got, send_sem=send.at[t, j], recv_sem=recv.at[t, j],
                                             device_id=(px, py, c), device_id_type=MESH).wait_recv()
        for cp in cps:
            cp.wait_send()
        for cp in local:
            cp.wait()

    return pl.pallas_call(
        body, in_specs=[ANY] * n, out_specs=[ANY] * n,
        out_shape=[jax.ShapeDtypeStruct(p.shape, p.dtype) for p in ps],
        scratch_shapes=[pltpu.SemaphoreType.DMA((n, 3))] * 2 + [pltpu.SemaphoreType.DMA((n,))],
        name="exchange_chip_blocks", compiler_params=pltpu.CompilerParams())(*ps)


def gather_sibling_halves(ss):
    n = len(ss)

    def body(*refs):
        ins, outs = refs[:n], refs[n:2 * n]
        send, recv, loc_sem = refs[2 * n:]
        x, y, c, k, sibling, chips = _place()
        local, cps = [], []
        for t in range(n):
            rows = 2 * ss[t].shape[0]
            local.append(pltpu.make_async_copy(ins[t], outs[t].at[_half(rows, c)], loc_sem.at[t]))
            cps.append(pltpu.make_async_remote_copy(
                src_ref=ins[t], dst_ref=outs[t].at[_half(rows, c)], send_sem=send.at[t], recv_sem=recv.at[t],
                device_id=sibling, device_id_type=MESH))
        for cp in local + cps:
            cp.start()
        for t in range(n):
            rows = 2 * ss[t].shape[0]
            got = outs[t].at[_half(rows, 1 - c)]
            pltpu.make_async_remote_copy(src_ref=got, dst_ref=got, send_sem=send.at[t], recv_sem=recv.at[t],
                                         device_id=sibling, device_id_type=MESH).wait_recv()
        for cp in cps:
            cp.wait_send()
        for cp in local:
            cp.wait()

    return pl.pallas_call(
        body, in_specs=[ANY] * n, out_specs=[ANY] * n,
        out_shape=[jax.ShapeDtypeStruct((2 * s.shape[0], s.shape[1]), s.dtype) for s in ss],
        scratch_shapes=[pltpu.SemaphoreType.DMA((n,))] * 3, name="gather_sibling_halves",
        compiler_params=pltpu.CompilerParams())(*ss)


def _sum_rows_tile(rows):
    return _pick(rows, (128, 64, 32, 16))


def add_sibling_half(g, land, core):
    nb, half, cols = land.shape
    tr = _sum_rows_tile(half)
    nrb = half // tr

    def body(c_ref, g_ref, l_ref, o_ref):
        o_ref[...] = (g_ref[...].astype(F32) + l_ref[...].astype(F32)).astype(o_ref.dtype)

    spec = pl.BlockSpec((None, tr, cols), lambda b, i, c_ref: (b, i, 0))
    grid_spec = pltpu.PrefetchScalarGridSpec(
        num_scalar_prefetch=1, grid=(nb, nrb),
        in_specs=[pl.BlockSpec((None, tr, cols), lambda b, i, c_ref: (b, c_ref[0] * nrb + i, 0)), spec], out_specs=spec)
    return pl.pallas_call(body, grid_spec=grid_spec, out_shape=jax.ShapeDtypeStruct(land.shape, BF16),
                          name="add_sibling_half", compiler_params=_params("parallel", "parallel"))(core, g, land)


def sum_chip_blocks(l2):
    nb, half, cols = l2.shape
    tr = _sum_rows_tile(half)

    def body(l_ref, o_ref):
        acc = l_ref[0].astype(F32)
        for b in range(1, nb):
            acc = acc + l_ref[b].astype(F32)
        o_ref[...] = acc

    return pl.pallas_call(body, grid=(half // tr,), in_specs=[pl.BlockSpec((nb, tr, cols), lambda i: (0, i, 0))],
                          out_specs=pl.BlockSpec((tr, cols), lambda i: (i, 0)),
                          out_shape=jax.ShapeDtypeStruct((half, cols), F32), name="sum_chip_blocks",
                          compiler_params=_params("parallel"))(l2)


def reduce_scatter(gs, core):
    lands = exchange_sibling_halves(gs)
    ps = [add_sibling_half(g, l, core) for g, l in zip(gs, lands)]
    l2s = exchange_chip_blocks(ps)
    ss = [sum_chip_blocks(l2) for l2 in l2s]
    return gather_sibling_halves(ss)


SMALL_CHUNK_ROWS = 256


def all_reduce_small(v):
    rows = v.shape[0]
    nchunk = rows // SMALL_CHUNK_ROWS

    def body(v_ref, o_ref, buf, send, recv):
        x, y, c = lax.axis_index("x"), lax.axis_index("y"), lax.axis_index("c")
        me = 4 * x + 2 * y + c
        buf[me] = v_ref[...]
        cps = []
        for d in range(1, N_DEV):
            peer = (x ^ ((d >> 2) & 1), y ^ ((d >> 1) & 1), c ^ (d & 1))
            cps.append(pltpu.make_async_remote_copy(src_ref=v_ref, dst_ref=buf.at[me], send_sem=send.at[d - 1],
                                                    recv_sem=recv.at[d - 1], device_id=peer, device_id_type=MESH))
        for cp in cps:
            cp.start()
        for d in range(1, N_DEV):
            got = buf.at[me ^ d]
            pltpu.make_async_remote_copy(src_ref=got, dst_ref=got, send_sem=send.at[d - 1], recv_sem=recv.at[d - 1],
                                         device_id=(x, y, c), device_id_type=MESH).wait_recv()
        for cp in cps:
            cp.wait_send()

        def chunk(i, carry):
            sl = pl.ds(pl.multiple_of(i * SMALL_CHUNK_ROWS, SMALL_CHUNK_ROWS), SMALL_CHUNK_ROWS)
            acc = buf[0, sl, :]
            for s in range(1, N_DEV):
                acc = acc + buf[s, sl, :]
            o_ref[sl, :] = acc
            return carry

        lax.fori_loop(0, nchunk, chunk, 0)

    vmem = pl.BlockSpec(memory_space=pltpu.VMEM)
    return pl.pallas_call(
        body, in_specs=[vmem], out_specs=vmem, out_shape=jax.ShapeDtypeStruct(v.shape, F32),
        scratch_shapes=[pltpu.VMEM((N_DEV,) + v.shape, F32), pltpu.SemaphoreType.DMA((N_DEV - 1,)),
                        pltpu.SemaphoreType.DMA((N_DEV - 1,))],
        name="all_reduce_small",
        compiler_params=pltpu.CompilerParams(vmem_limit_bytes=VMEM_LIMIT_BYTES))(v)


def adamw(w, g, m, v):
    rows, cols = w.shape
    tr = rows
    for cand in (512, 256, 128, 64, 32, 16, 8):
        if rows % cand == 0 and cand * cols * 4 <= (1 << 20):
            tr = cand
            break
    c1 = 1.0 - ADAM_B1 ** ADAM_STEP
    c2 = 1.0 - ADAM_B2 ** ADAM_STEP

    def body(w_ref, g_ref, m_ref, v_ref, d_ref, nm_ref, nv_ref):
        gf = g_ref[...]
        nm = ADAM_B1 * m_ref[...] + (1.0 - ADAM_B1) * gf
        nv = ADAM_B2 * v_ref[...] + (1.0 - ADAM_B2) * (gf * gf)
        d_ref[...] = -ADAM_LR * ((nm / c1) / (jnp.sqrt(nv / c2) + ADAM_EPS) + ADAM_WD * w_ref[...])
        nm_ref[...] = nm
        nv_ref[...] = nv

    spec = pl.BlockSpec((tr, cols), lambda i: (i, 0))
    shape = jax.ShapeDtypeStruct((rows, cols), F32)
    return pl.pallas_call(body, grid=(rows // tr,), in_specs=[spec] * 4, out_specs=(spec,) * 3, out_shape=(shape,) * 3,
                          name="adamw", compiler_params=_params("parallel"))(w, g, m, v)


TAP_ROWS_ALIGN = 16
FLAT_ALIGN = 128 * SMALL_CHUNK_ROWS


def _pad_to(a, n):
    return jnp.pad(a, (0, n - a.shape[0]))


def _round_up(n, m):
    return (n + m - 1) // m * m


def train_step(a):
    x, positions, tgt = a["x"][0], a["positions"][0], a["loss_target"][0]
    mats = [n for n in WEIGHT_NAMES if _kind(n) in ("col", "row", "grp")]
    taps = [n for n in WEIGHT_NAMES if _kind(n) == "tap"]
    reps = [n for n in WEIGHT_NAMES if _kind(n) == "rep"]
    chip = 2 * lax.axis_index("x") + lax.axis_index("y")
    core = lax.axis_index("c").astype(jnp.int32).reshape(1)

    shards = [a[n].reshape(-1, a[n].shape[-1]).astype(BF16) for n in mats]
    shards += [jnp.pad(a[n], ((0, _round_up(a[n].shape[0], TAP_ROWS_ALIGN) - a[n].shape[0]), (0, 0))) for n in taps]
    gathered = all_gather_chips(shards)
    W = {n: a[n] for n in reps}
    for n, g in zip(mats + taps, gathered):
        kind = _kind(n)
        if kind == "col":
            W[n] = g
        elif kind == "row":
            W[n] = g.reshape(-1, g.shape[-1])
        elif kind == "grp":
            grp, r, pg = a[n].shape
            W[n] = g.reshape(N_CHIPS, grp, r, pg).transpose(1, 0, 2, 3).reshape(grp, N_CHIPS * r, pg)
        else:
            nt = a[n].shape[0]
            W[n] = g[:, :nt].transpose(1, 0, 2).reshape(nt, -1)

    loss, dx, grads = local_step(x, positions, tgt, W)
    loss = lax.psum(loss, ("x", "y", "c"))

    gl = []
    for n in mats:
        g, kind = grads[n], _kind(n)
        if kind == "row":
            g = g.reshape(N_CHIPS, -1, g.shape[-1])
        elif kind == "grp":
            grp, r, pg = a[n].shape
            g = g.reshape(grp, N_CHIPS, r, pg).transpose(1, 0, 2, 3).reshape(N_CHIPS, grp * r, pg)
        gl.append(g)
    reduced = dict(zip(mats, reduce_scatter(gl, core)))

    n_rep = _round_up(sum(a[n].size for n in reps), FLAT_ALIGN)
    flat_rep = _pad_to(jnp.concatenate([grads[n].reshape(-1) for n in reps]), n_rep)
    flat_tap = jnp.concatenate([grads[n].reshape(-1) for n in taps])
    flat = jnp.concatenate([flat_rep, _pad_to(flat_tap, _round_up(flat_tap.shape[0], FLAT_ALIGN))])
    summed = all_reduce_small(flat.reshape(-1, 128))
    rep_rows = n_rep // 128
    tap_flat = summed[rep_rows:].reshape(-1)

    out = {}
    pack = lambda pre: _pad_to(jnp.concatenate([a[pre + n].reshape(-1) for n in reps]), n_rep).reshape(-1, 128)
    g_rep = summed[:rep_rows]
    d_rep, m_rep, v_rep = adamw(pack(""), g_rep, pack("m_"), pack("v_"))
    off = 0
    for n in reps:
        size, shape = a[n].size, a[n].shape
        out[n] = tuple(f.reshape(-1)[off:off + size].reshape(shape) for f in (g_rep, d_rep, m_rep, v_rep))
        off += size
    off = 0
    for n in taps:
        nt, cs = a[n].shape
        full = tap_flat[off:off + nt * cs * N_CHIPS].reshape(nt, cs * N_CHIPS)
        off += nt * cs * N_CHIPS
        g = lax.dynamic_slice(full, (0, chip * cs), (nt, cs))
        out[n] = (g,) + tuple(adamw(a[n], g, a["m_" + n], a["v_" + n]))
    for n in mats:
        shape = a[n].shape
        two_d = lambda t: t.reshape(-1, shape[-1])
        g = reduced[n]
        out[n] = (g.reshape(shape),) + tuple(t.reshape(shape) for t in adamw(two_d(a[n]), g, two_d(a["m_" + n]), two_d(a["v_" + n])))

    res = [loss, dx[None]]
    for part in range(4):
        res += [out[n][part] for n in WEIGHT_NAMES]
    return tuple(res)


def kernel(x, positions, l0_norm_g, l0_a_w_in, l0_a_b_in, l0_a_dw_w, l0_a_dw_b, l0_a_ln_g, l0_a_ln_b, l0_a_w_out, l0_a_b_out, l0_ffn_norm_g, l0_ffn_w_up, l0_ffn_dw_w, l0_ffn_dw_b, l0_ffn_w_down, l1_norm_g, l1_b_w_group, l1_b_scale, l1_ffn_norm_g, l1_ffn_w_up, l1_ffn_dw_w, l1_ffn_dw_b, l1_ffn_w_down, l2_norm_g, l2_c_w_qkv, l2_c_q_norm_g, l2_c_k_norm_g, l2_c_sinks, l2_c_w_o, l2_ffn_norm_g, l2_ffn_w_up, l2_ffn_dw_w, l2_ffn_dw_b, l2_ffn_w_down, l3_norm_g, l3_a_w_in, l3_a_b_in, l3_a_dw_w, l3_a_dw_b, l3_a_ln_g, l3_a_ln_b, l3_a_w_out, l3_a_b_out, l3_ffn_norm_g, l3_ffn_w_up, l3_ffn_dw_w, l3_ffn_dw_b, l3_ffn_w_down, loss_target, m_l0_norm_g, m_l0_a_w_in, m_l0_a_b_in, m_l0_a_dw_w, m_l0_a_dw_b, m_l0_a_ln_g, m_l0_a_ln_b, m_l0_a_w_out, m_l0_a_b_out, m_l0_ffn_norm_g, m_l0_ffn_w_up, m_l0_ffn_dw_w, m_l0_ffn_dw_b, m_l0_ffn_w_down, m_l1_norm_g, m_l1_b_w_group, m_l1_b_scale, m_l1_ffn_norm_g, m_l1_ffn_w_up, m_l1_ffn_dw_w, m_l1_ffn_dw_b, m_l1_ffn_w_down, m_l2_norm_g, m_l2_c_w_qkv, m_l2_c_q_norm_g, m_l2_c_k_norm_g, m_l2_c_sinks, m_l2_c_w_o, m_l2_ffn_norm_g, m_l2_ffn_w_up, m_l2_ffn_dw_w, m_l2_ffn_dw_b, m_l2_ffn_w_down, m_l3_norm_g, m_l3_a_w_in, m_l3_a_b_in, m_l3_a_dw_w, m_l3_a_dw_b, m_l3_a_ln_g, m_l3_a_ln_b, m_l3_a_w_out, m_l3_a_b_out, m_l3_ffn_norm_g, m_l3_ffn_w_up, m_l3_ffn_dw_w, m_l3_ffn_dw_b, m_l3_ffn_w_down, v_l0_norm_g, v_l0_a_w_in, v_l0_a_b_in, v_l0_a_dw_w, v_l0_a_dw_b, v_l0_a_ln_g, v_l0_a_ln_b, v_l0_a_w_out, v_l0_a_b_out, v_l0_ffn_norm_g, v_l0_ffn_w_up, v_l0_ffn_dw_w, v_l0_ffn_dw_b, v_l0_ffn_w_down, v_l1_norm_g, v_l1_b_w_group, v_l1_b_scale, v_l1_ffn_norm_g, v_l1_ffn_w_up, v_l1_ffn_dw_w, v_l1_ffn_dw_b, v_l1_ffn_w_down, v_l2_norm_g, v_l2_c_w_qkv, v_l2_c_q_norm_g, v_l2_c_k_norm_g, v_l2_c_sinks, v_l2_c_w_o, v_l2_ffn_norm_g, v_l2_ffn_w_up, v_l2_ffn_dw_w, v_l2_ffn_dw_b, v_l2_ffn_w_down, v_l3_norm_g, v_l3_a_w_in, v_l3_a_b_in, v_l3_a_dw_w, v_l3_a_dw_b, v_l3_a_ln_g, v_l3_a_ln_b, v_l3_a_w_out, v_l3_a_b_out, v_l3_ffn_norm_g, v_l3_ffn_w_up, v_l3_ffn_dw_w, v_l3_ffn_dw_b, v_l3_ffn_w_down):
    return train_step(dict(locals()))
```

```python
import functools

import jax
import jax.numpy as jnp
from jax import lax
from jax.experimental import pallas as pl
from jax.experimental.pallas import tpu as pltpu

F32 = jnp.float32
BF16 = jnp.bfloat16
EPS = 1e-6
HEAD_DIM = 64
KV_GROUP = 8
ATT_BLOCK = 128
ROT_DIM = 16
ROPE_THETA = 500000.0
POOL_GROUPS = 4
CONF_TAPS = 31
FFN_TAPS = 3
N_CHIPS = 4
N_DEV = 8
ADAM_LR, ADAM_B1, ADAM_B2, ADAM_EPS, ADAM_WD, ADAM_STEP = 0.001, 0.9, 0.999, 1e-08, 0.01, 10
VMEM_LIMIT_BYTES = 56 * 1024 * 1024
MESH = pl.DeviceIdType.MESH

CONF_NAMES = ["norm_g", "a_w_in", "a_b_in", "a_dw_w", "a_dw_b", "a_ln_g", "a_ln_b", "a_w_out", "a_b_out"]
FFN_NAMES = ["ffn_norm_g", "ffn_w_up", "ffn_dw_w", "ffn_dw_b", "ffn_w_down"]
POOL_NAMES = ["norm_g", "b_w_group", "b_scale"]
ATT_NAMES = ["norm_g", "c_w_qkv", "c_q_norm_g", "c_k_norm_g", "c_sinks", "c_w_o"]
WEIGHT_NAMES = ([f"l0_{n}" for n in CONF_NAMES + FFN_NAMES] + [f"l1_{n}" for n in POOL_NAMES + FFN_NAMES]
                + [f"l2_{n}" for n in ATT_NAMES + FFN_NAMES] + [f"l3_{n}" for n in CONF_NAMES + FFN_NAMES])
COL_SHARDED = ("a_w_in", "ffn_w_up", "c_w_qkv")
ROW_SHARDED = ("a_w_out", "ffn_w_down", "c_w_o")
TAP_SHARDED = ("a_dw_w", "ffn_dw_w")


def _kind(name):
    base = name[3:]
    if base in COL_SHARDED:
        return "col"
    if base in ROW_SHARDED:
        return "row"
    if base in TAP_SHARDED:
        return "tap"
    if base == "b_w_group":
        return "grp"
    return "rep"


def _pick(n, prefs):
    for p in prefs:
        if p <= n and n % p == 0:
            return p
    return n


def _params(*sem):
    return pltpu.CompilerParams(dimension_semantics=sem, vmem_limit_bytes=VMEM_LIMIT_BYTES)


def _sigmoid(x):
    return 1.0 / (1.0 + jnp.exp(-x))


NN = (((1,), (0,)), ((), ()))
NT = (((1,), (1,)), ((), ()))
TN = (((0,), (0,)), ((), ()))


MM_VMEM_BUDGET = 44 * 1024 * 1024
MM_STEP_SECONDS = 0.35e-6
MM_FLOPS, MM_HBM_BYTES = 9.0e14, 3.0e12
TILE_SIZES = (4096, 2816, 2048, 1408, 1024, 704, 640, 512, 256, 128)


def _tile_options(n, lane):
    opts = [c for c in TILE_SIZES if c <= n and n % c == 0 and (not lane or c % 128 == 0)]
    return opts or [n]


def _mm_plan(m, n, k, *, n_unit=None, k_unit=None, a_bytes=2, b_bytes=2, o_bytes=2, extra_bytes=0):
    best = None
    for tm in _tile_options(m, False):
        for tn in _tile_options(n_unit or n, True):
            for tk in _tile_options(k_unit or k, True) + ([k] if not k_unit else []):
                nk = k // tk
                vmem = 2 * (tm * tk * a_bytes + tk * tn * b_bytes + tm * tn * (o_bytes + extra_bytes)) + tm * tn * 4 * (2 if nk > 1 else 1)
                if vmem > MM_VMEM_BUDGET:
                    continue
                ni, nj = m // tm, n // tn
                a_all, b_all, o_all = m * k * a_bytes, k * n * b_bytes, m * n * (o_bytes + extra_bytes)
                for i_inner in (False, True):
                    if nk > 1:
                        traffic = a_all * nj + b_all * ni + o_all
                    elif i_inner:
                        traffic = a_all * nj + b_all + o_all
                    else:
                        traffic = a_all + b_all * ni + o_all
                    cost = ni * nj * nk * MM_STEP_SECONDS + max(2.0 * m * n * k / MM_FLOPS, traffic / MM_HBM_BYTES)
                    if best is None or cost < best[0]:
                        best = (cost, tm, tn, tk, i_inner)
    assert best is not None, (m, n, k)
    return best[1:]


def _mm(a, b, *, dims, sizes, plan, a_blk, a_idx, b_blk, b_idx, o_blk, o_idx, out_shape, name,
        bias=None, scale=None, vec_blk=None, vec_idx=None, resid=None, raw_shape=None):
    m, n, k = sizes
    tm, tn, tk, i_inner = plan
    ni, nj, nk = m // tm, n // tn, k // tk
    has_bias, has_scale, has_resid, want_raw = bias is not None, scale is not None, resid is not None, raw_shape is not None

    def body(*refs):
        a_ref, b_ref = refs[0], refs[1]
        pos = 2
        bias_ref = scale_ref = resid_ref = raw_ref = None
        if has_bias:
            bias_ref = refs[pos]; pos += 1
        if has_scale:
            scale_ref = refs[pos]; pos += 1
        if has_resid:
            resid_ref = refs[pos]; pos += 1
        o_ref = refs[pos]; pos += 1
        if want_raw:
            raw_ref = refs[pos]; pos += 1
        part = lax.dot_general(a_ref[...].astype(BF16), b_ref[...].astype(BF16), dims, preferred_element_type=F32)

        def finish(r):
            if want_raw:
                raw_ref[...] = r.astype(raw_ref.dtype)
            if has_bias:
                r = r + bias_ref[...]
            if has_scale:
                r = r * scale_ref[...]
            if has_resid:
                r = r + resid_ref[...]
            o_ref[...] = r.astype(o_ref.dtype)

        if nk == 1:
            finish(part)
        else:
            acc_ref = refs[pos]
            kk = pl.program_id(2)

            @pl.when(kk == 0)
            def _():
                acc_ref[...] = part

            @pl.when(kk > 0)
            def _():
                acc_ref[...] += part

            @pl.when(kk == nk - 1)
            def _():
                finish(acc_ref[...])

    order = (lambda f: (lambda j, i, kk: f(i, j, kk))) if i_inner else (lambda f: f)
    spec = lambda blk, idx: pl.BlockSpec(blk, order(idx))
    operands, in_specs = [a, b], [spec(a_blk, a_idx), spec(b_blk, b_idx)]
    for v in (bias, scale):
        if v is not None:
            operands.append(v); in_specs.append(spec(vec_blk, vec_idx))
    if has_resid:
        operands.append(resid); in_specs.append(spec(o_blk, o_idx))
    out_shapes, out_specs = out_shape, spec(o_blk, o_idx)
    if want_raw:
        out_shapes, out_specs = (out_shape, raw_shape), (spec(o_blk, o_idx), spec(o_blk, o_idx))
    return pl.pallas_call(
        body, grid=(nj, ni, nk) if i_inner else (ni, nj, nk), in_specs=in_specs, out_specs=out_specs,
        out_shape=out_shapes, scratch_shapes=[pltpu.VMEM((tm, tn), F32)] if nk > 1 else [], name=name,
        compiler_params=_params("parallel", "parallel", "arbitrary"))(*operands)


def mm_nn_cols(a, g, *, split, out_dtype, bias=None, name):
    t, k = a.shape
    ns = g.shape[2]
    n = N_CHIPS * ns
    plan = _mm_plan(t, n, k, n_unit=ns, o_bytes=jnp.dtype(out_dtype).itemsize)
    tm, tn, tk, _ = plan
    nj = ns // tn
    if split:
        o_blk, o_idx = (None, tm, tn), (lambda i, j, kk: (j // (2 * nj), i, j % (2 * nj)))
        out_shape = jax.ShapeDtypeStruct((2, t, 2 * ns), out_dtype)
        vec_blk, vec_idx = (None, 1, tn), (lambda i, j, kk: (j // (2 * nj), 0, j % (2 * nj)))
        if bias is not None:
            bias = bias.reshape(2, 1, 2 * ns)
    else:
        o_blk, o_idx = (tm, tn), (lambda i, j, kk: (i, j))
        out_shape = jax.ShapeDtypeStruct((t, n), out_dtype)
        vec_blk, vec_idx = (1, tn), (lambda i, j, kk: (0, j))
        if bias is not None:
            bias = bias.reshape(1, n)
    return _mm(a, g, dims=NN, sizes=(t, n, k), plan=plan, a_blk=(tm, tk), a_idx=lambda i, j, kk: (i, kk),
               b_blk=(None, tk, tn), b_idx=lambda i, j, kk: (j // nj, kk, j % nj), o_blk=o_blk, o_idx=o_idx,
               out_shape=out_shape, name=name, bias=bias, vec_blk=vec_blk, vec_idx=vec_idx)


def mm_nn(a, w, *, out_dtype, bias=None, scale=None, resid=None, raw_dtype=None, name):
    t, k = a.shape
    n = w.shape[1]
    extra = (4 if resid is not None else 0) + (0 if raw_dtype is None else jnp.dtype(raw_dtype).itemsize)
    plan = _mm_plan(t, n, k, a_bytes=a.dtype.itemsize, o_bytes=jnp.dtype(out_dtype).itemsize, extra_bytes=extra)
    tm, tn, tk, _ = plan
    raw_shape = None if raw_dtype is None else jax.ShapeDtypeStruct((t, n), raw_dtype)
    return _mm(a, w, dims=NN, sizes=(t, n, k), plan=plan, a_blk=(tm, tk), a_idx=lambda i, j, kk: (i, kk),
               b_blk=(tk, tn), b_idx=lambda i, j, kk: (kk, j), o_blk=(tm, tn), o_idx=lambda i, j, kk: (i, j),
               out_shape=jax.ShapeDtypeStruct((t, n), out_dtype), name=name,
               bias=None if bias is None else bias.reshape(1, n), scale=None if scale is None else scale.reshape(1, n),
               vec_blk=(1, tn), vec_idx=lambda i, j, kk: (0, j), resid=resid, raw_shape=raw_shape)


def mm_nt(dy, w, *, out_dtype, name):
    t, n = dy.shape
    kdim = w.shape[0]
    plan = _mm_plan(t, kdim, n, a_bytes=dy.dtype.itemsize, o_bytes=jnp.dtype(out_dtype).itemsize)
    tm, tn, tk, _ = plan
    return _mm(dy, w, dims=NT, sizes=(t, kdim, n), plan=plan, a_blk=(tm, tk), a_idx=lambda i, j, kk: (i, kk),
               b_blk=(tn, tk), b_idx=lambda i, j, kk: (j, kk), o_blk=(tm, tn), o_idx=lambda i, j, kk: (i, j),
               out_shape=jax.ShapeDtypeStruct((t, kdim), out_dtype), name=name)


def mm_nt_cols(du, g, *, split, out_dtype, name):
    kdim, ns = g.shape[1], g.shape[2]
    t = du.shape[1] if split else du.shape[0]
    plan = _mm_plan(t, kdim, N_CHIPS * ns, k_unit=ns, o_bytes=jnp.dtype(out_dtype).itemsize)
    tm, tn, tk, _ = plan
    nkb = ns // tk
    if split:
        a_blk, a_idx = (None, tm, tk), (lambda i, j, kk: (kk // (2 * nkb), i, kk % (2 * nkb)))
    else:
        a_blk, a_idx = (tm, tk), (lambda i, j, kk: (i, kk))
    return _mm(du, g, dims=NT, sizes=(t, kdim, N_CHIPS * ns), plan=plan, a_blk=a_blk, a_idx=a_idx,
               b_blk=(None, tn, tk), b_idx=lambda i, j, kk: (kk // nkb, j, kk % nkb),
               o_blk=(tm, tn), o_idx=lambda i, j, kk: (i, j),
               out_shape=jax.ShapeDtypeStruct((t, kdim), out_dtype), name=name)


def mm_wgrad(at, dy, *, out_dtype, name):
    return mm_nn(at, dy, out_dtype=out_dtype, name=name)


def mm_wgrad_cols(ht, du, *, split, out_dtype, name):
    kdim, t = ht.shape
    ns = (du.shape[2] // 2) if split else (du.shape[1] // N_CHIPS)
    plan = _mm_plan(kdim, N_CHIPS * ns, t, n_unit=ns, o_bytes=jnp.dtype(out_dtype).itemsize)
    tm, tn, tk, _ = plan
    nj = ns // tn
    if split:
        b_blk, b_idx = (None, tk, tn), (lambda i, j, kk: (j // (2 * nj), kk, j % (2 * nj)))
    else:
        b_blk, b_idx = (tk, tn), (lambda i, j, kk: (kk, j))
    return _mm(ht, du, dims=NN, sizes=(kdim, N_CHIPS * ns, t), plan=plan, a_blk=(tm, tk), a_idx=lambda i, j, kk: (i, kk),
               b_blk=b_blk, b_idx=b_idx, o_blk=(None, tm, tn), o_idx=lambda i, j, kk: (j // nj, i, j % nj),
               out_shape=jax.ShapeDtypeStruct((N_CHIPS, kdim, ns), out_dtype), name=name)


def _row_tile(t):
    return _pick(t, (256, 128))


def _acc_rows(ref, part, i):
    @pl.when(i == 0)
    def _():
        ref[...] = part

    @pl.when(i > 0)
    def _():
        ref[...] += part


def rms_fwd(x, g, out_dtype):
    t, d = x.shape
    tr = _row_tile(t)

    def body(x_ref, g_ref, o_ref):
        xf = x_ref[...]
        r = lax.rsqrt(jnp.mean(xf * xf, axis=-1, keepdims=True) + EPS)
        o_ref[...] = (xf * r * g_ref[...]).astype(o_ref.dtype)

    row = pl.BlockSpec((tr, d), lambda i: (i, 0))
    return pl.pallas_call(body, grid=(t // tr,), in_specs=[row, pl.BlockSpec((1, d), lambda i: (0, 0))], out_specs=row,
                          out_shape=jax.ShapeDtypeStruct((t, d), out_dtype), name="rms_fwd",
                          compiler_params=_params("parallel"))(x, g.reshape(1, d))


def rms_bwd(x, g, dh, dres):
    t, d = x.shape
    tr = _row_tile(t)

    def body(x_ref, g_ref, dh_ref, dres_ref, dx_ref, dx16_ref, dg_ref):
        i = pl.program_id(0)
        xf = x_ref[...]
        r = lax.rsqrt(jnp.mean(xf * xf, axis=-1, keepdims=True) + EPS)
        xhat = xf * r
        dhf = dh_ref[...].astype(F32)
        dxh = dhf * g_ref[...]
        m = jnp.mean(dxh * xhat, axis=-1, keepdims=True)
        dx = dres_ref[...] + r * (dxh - xhat * m)
        dx_ref[...] = dx
        dx16_ref[...] = dx.astype(BF16)
        _acc_rows(dg_ref, jnp.sum(dhf * xhat, axis=0, keepdims=True), i)

    row = pl.BlockSpec((tr, d), lambda i: (i, 0))
    vec = pl.BlockSpec((1, d), lambda i: (0, 0))
    return pl.pallas_call(body, grid=(t // tr,), in_specs=[row, vec, row, row], out_specs=(row, row, vec),
                          out_shape=(jax.ShapeDtypeStruct((t, d), F32), jax.ShapeDtypeStruct((t, d), BF16),
                                     jax.ShapeDtypeStruct((1, d), F32)),
                          name="rms_bwd", compiler_params=_params("arbitrary"))(x, g.reshape(1, d), dh, dres)


def ln_silu_fwd(c, g, b):
    t, d = c.shape
    tr = _row_tile(t)

    def body(c_ref, g_ref, b_ref, o_ref):
        xf = c_ref[...]
        mu = jnp.mean(xf, axis=-1, keepdims=True)
        xc = xf - mu
        var = jnp.mean(xc * xc, axis=-1, keepdims=True)
        n = xc * lax.rsqrt(var + EPS) * g_ref[...] + b_ref[...]
        o_ref[...] = (n * _sigmoid(n)).astype(o_ref.dtype)

    row = pl.BlockSpec((tr, d), lambda i: (i, 0))
    vec = pl.BlockSpec((1, d), lambda i: (0, 0))
    return pl.pallas_call(body, grid=(t // tr,), in_specs=[row, vec, vec], out_specs=row,
                          out_shape=jax.ShapeDtypeStruct((t, d), BF16), name="ln_silu_fwd",
                          compiler_params=_params("parallel"))(c, g.reshape(1, d), b.reshape(1, d))


def ln_silu_bwd(c, g, b, ds):
    t, d = c.shape
    tr = _row_tile(t)

    def body(c_ref, g_ref, b_ref, ds_ref, dc_ref, dg_ref, db_ref):
        i = pl.program_id(0)
        xf = c_ref[...]
        mu = jnp.mean(xf, axis=-1, keepdims=True)
        xc = xf - mu
        var = jnp.mean(xc * xc, axis=-1, keepdims=True)
        rstd = lax.rsqrt(var + EPS)
        xhat = xc * rstd
        n = xhat * g_ref[...] + b_ref[...]
        sg = _sigmoid(n)
        dn = ds_ref[...].astype(F32) * (sg * (1.0 + n * (1.0 - sg)))
        dxh = dn * g_ref[...]
        m1 = jnp.mean(dxh, axis=-1, keepdims=True)
        m2 = jnp.mean(dxh * xhat, axis=-1, keepdims=True)
        dc_ref[...] = rstd * (dxh - m1 - xhat * m2)
        _acc_rows(dg_ref, jnp.sum(dn * xhat, axis=0, keepdims=True), i)
        _acc_rows(db_ref, jnp.sum(dn, axis=0, keepdims=True), i)

    row = pl.BlockSpec((tr, d), lambda i: (i, 0))
    vec = pl.BlockSpec((1, d), lambda i: (0, 0))
    vshape = jax.ShapeDtypeStruct((1, d), F32)
    return pl.pallas_call(body, grid=(t // tr,), in_specs=[row, vec, vec, row], out_specs=(row, vec, vec),
                          out_shape=(jax.ShapeDtypeStruct((t, d), F32), vshape, vshape), name="ln_silu_bwd",
                          compiler_params=_params("arbitrary"))(c, g.reshape(1, d), b.reshape(1, d), ds)


def loss_grad(y, tgt):
    t, d = y.shape
    tr = _row_tile(t)

    def body(y_ref, t_ref, dy_ref, dy16_ref, sq_ref):
        i = pl.program_id(0)
        err = y_ref[...] - t_ref[...]
        dy = err * (1.0 / d)
        dy_ref[...] = dy
        dy16_ref[...] = dy.astype(BF16)
        _acc_rows(sq_ref, jnp.sum(err * err, axis=0, keepdims=True), i)

    row = pl.BlockSpec((tr, d), lambda i: (i, 0))
    vec = pl.BlockSpec((1, d), lambda i: (0, 0))
    return pl.pallas_call(body, grid=(t // tr,), in_specs=[row, row], out_specs=(row, row, vec),
                          out_shape=(jax.ShapeDtypeStruct((t, d), F32), jax.ShapeDtypeStruct((t, d), BF16),
                                     jax.ShapeDtypeStruct((1, d), F32)),
                          name="loss_grad", compiler_params=_params("arbitrary"))(y, tgt)


def col_sum(a):
    t, d = a.shape
    tr = _row_tile(t)

    def body(a_ref, o_ref):
        _acc_rows(o_ref, jnp.sum(a_ref[...].astype(F32), axis=0, keepdims=True), pl.program_id(0))

    return pl.pallas_call(body, grid=(t // tr,), in_specs=[pl.BlockSpec((tr, d), lambda i: (i, 0))],
                          out_specs=pl.BlockSpec((1, d), lambda i: (0, 0)), out_shape=jax.ShapeDtypeStruct((1, d), F32),
                          name="col_sum", compiler_params=_params("arbitrary"))(a)


def scale_bwd(dx, ypre, scale):
    t, d = dx.shape
    tr = _row_tile(t)

    def body(dx_ref, y_ref, s_ref, dy_ref, ds_ref):
        dxf = dx_ref[...]
        dy_ref[...] = (dxf * s_ref[...]).astype(dy_ref.dtype)
        _acc_rows(ds_ref, jnp.sum(dxf * y_ref[...], axis=0, keepdims=True), pl.program_id(0))

    row = pl.BlockSpec((tr, d), lambda i: (i, 0))
    vec = pl.BlockSpec((1, d), lambda i: (0, 0))
    return pl.pallas_call(body, grid=(t // tr,), in_specs=[row, row, vec], out_specs=(row, vec),
                          out_shape=(jax.ShapeDtypeStruct((t, d), BF16), jax.ShapeDtypeStruct((1, d), F32)),
                          name="scale_bwd", compiler_params=_params("arbitrary"))(dx, ypre, scale.reshape(1, d))


def _chunk_rows(t):
    return _pick(t, (256, 128))


def _load_halo(ref, lead, base, rows, t, first, last, pre, post):
    idx = (lambda s, n: (pl.ds(s, n), slice(None))) if lead is None else (lambda s, n: (lead, pl.ds(s, n), slice(None)))
    parts = []
    if pre:
        start = pl.multiple_of(jnp.maximum(base - pre, 0), pre)
        parts.append(ref[idx(start, pre)].astype(F32) * jnp.where(first, 0.0, 1.0))
    parts.append(ref[idx(base, rows)].astype(F32))
    if post:
        start = pl.multiple_of(jnp.minimum(base + rows, t - post), post)
        parts.append(ref[idx(start, post)].astype(F32) * jnp.where(last, 0.0, 1.0))
    return parts[0] if len(parts) == 1 else jnp.concatenate(parts, axis=0)


def _fold8(x):
    r, c = x.shape
    return x.reshape(r // 8, 8, c).sum(axis=0)


def ffn_gate_fwd(u2, w3, b2):
    _, t, f = u2.shape
    tc = _pick(f, (256, 128))
    rows = _chunk_rows(t)
    nch = t // rows
    halo = 16

    def body(u_ref, w_ref, b_ref, a_ref):
        def conv(p, base, first):
            xs = _load_halo(u_ref, p, base, rows, t, first, False, halo, 0)
            wp = w_ref[p]
            return (wp[0:1] * pltpu.roll(xs, 2, 0)[halo:] + wp[1:2] * pltpu.roll(xs, 1, 0)[halo:]
                    + wp[2:3] * xs[halo:] + b_ref[p])

        def chunk(i, carry):
            base = pl.multiple_of(i * rows, rows)
            gate, val = conv(0, base, i == 0), conv(1, base, i == 0)
            a_ref[pl.ds(base, rows), :] = (gate * _sigmoid(gate) * val).astype(a_ref.dtype)
            return carry

        lax.fori_loop(0, nch, chunk, 0)

    return pl.pallas_call(
        body, grid=(f // tc,),
        in_specs=[pl.BlockSpec((2, t, tc), lambda j: (0, 0, j)), pl.BlockSpec((2, 3, tc), lambda j: (0, 0, j)),
                  pl.BlockSpec((2, 1, tc), lambda j: (0, 0, j))],
        out_specs=pl.BlockSpec((t, tc), lambda j: (0, j)), out_shape=jax.ShapeDtypeStruct((t, f), BF16),
        name="ffn_gate_fwd", compiler_params=_params("parallel"))(u2, w3, b2)


def ffn_gate_bwd(u2, da, w3, b2):
    _, t, f = u2.shape
    tc = _pick(f, (256, 128))
    rows = _chunk_rows(t)
    nch = t // rows
    halo = 16
    n = rows + 2 * halo

    def body(u_ref, da_ref, w_ref, b_ref, du_ref, dw_ref, db_ref, acc_ref):
        acc_ref[...] = jnp.zeros_like(acc_ref)

        def chunk(i, carry):
            base = pl.multiple_of(i * rows, rows)
            first, last = i == 0, i == nch - 1
            daf = jnp.concatenate(
                [jnp.zeros((halo, tc), F32), _load_halo(da_ref, None, base, rows, t, first, last, 0, halo)], axis=0)
            pre, shifted = [], []
            for p in range(2):
                xs = _load_halo(u_ref, p, base, rows, t, first, last, halo, halo)
                x1, x2 = pltpu.roll(xs, 1, 0), pltpu.roll(xs, 2, 0)
                wp = w_ref[p]
                pre.append(wp[0:1] * x2 + wp[1:2] * x1 + wp[2:3] * xs + b_ref[p])
                shifted.append((x2, x1, xs))
            gate, val = pre
            sg = _sigmoid(gate)
            d_pre = (daf * val * (sg * (1.0 + gate * (1.0 - sg))), daf * gate * sg)
            for p in range(2):
                dp = d_pre[p]
                wp = w_ref[p]
                du = wp[2:3] * dp + wp[1:2] * pltpu.roll(dp, n - 1, 0) + wp[0:1] * pltpu.roll(dp, n - 2, 0)
                du_ref[p, pl.ds(base, rows), :] = du[halo:halo + rows].astype(du_ref.dtype)
                own = dp[halo:halo + rows]
                for k in range(3):
                    acc_ref[p, k] += _fold8(own * shifted[p][k][halo:halo + rows])
                acc_ref[p, 3] += _fold8(own)
            return carry

        lax.fori_loop(0, nch, chunk, 0)
        for p in range(2):
            for k in range(3):
                dw_ref[p, k:k + 1, :] = jnp.sum(acc_ref[p, k], axis=0, keepdims=True)
            db_ref[p] = jnp.sum(acc_ref[p, 3], axis=0, keepdims=True)

    blk = pl.BlockSpec((2, t, tc), lambda j: (0, 0, j))
    wspec = pl.BlockSpec((2, 3, tc), lambda j: (0, 0, j))
    bspec = pl.BlockSpec((2, 1, tc), lambda j: (0, 0, j))
    return pl.pallas_call(
        body, grid=(f // tc,), in_specs=[blk, pl.BlockSpec((t, tc), lambda j: (0, j)), wspec, bspec],
        out_specs=(blk, wspec, bspec),
        out_shape=(jax.ShapeDtypeStruct((2, t, f), BF16), jax.ShapeDtypeStruct((2, 3, f), F32),
                   jax.ShapeDtypeStruct((2, 1, f), F32)),
        scratch_shapes=[pltpu.VMEM((2, 4, 8, tc), F32)], name="ffn_gate_bwd",
        compiler_params=_params("parallel"))(u2, da, w3, b2)


def glu_conv_fwd(u2, w, b):
    _, t, d = u2.shape
    taps = w.shape[0]
    tc = 128
    rows = _chunk_rows(t)
    nch = t // rows
    halo = 32

    def body(u_ref, w_ref, b_ref, c_ref):
        def chunk(i, carry):
            base = pl.multiple_of(i * rows, rows)
            a = _load_halo(u_ref, 0, base, rows, t, i == 0, False, halo, 0)
            g = _load_halo(u_ref, 1, base, rows, t, i == 0, False, halo, 0)
            xs = a * _sigmoid(g)
            acc = w_ref[taps - 1:taps, :] * xs[halo:] + b_ref[...]
            for j in range(taps - 1):
                acc = acc + w_ref[j:j + 1, :] * pltpu.roll(xs, taps - 1 - j, 0)[halo:]
            c_ref[pl.ds(base, rows), :] = acc
            return carry

        lax.fori_loop(0, nch, chunk, 0)

    return pl.pallas_call(
        body, grid=(d // tc,),
        in_specs=[pl.BlockSpec((2, t, tc), lambda j: (0, 0, j)), pl.BlockSpec((taps, tc), lambda j: (0, j)),
                  pl.BlockSpec((1, tc), lambda j: (0, j))],
        out_specs=pl.BlockSpec((t, tc), lambda j: (0, j)), out_shape=jax.ShapeDtypeStruct((t, d), F32),
        name="glu_conv_fwd", compiler_params=_params("parallel"))(u2, w, b)


def glu_conv_bwd(u2, dc, w):
    _, t, d = u2.shape
    taps = w.shape[0]
    tc = 128
    rows = _chunk_rows(t)
    nch = t // rows
    halo = 32
    n = rows + halo

    def body(u_ref, dc_ref, w_ref, du_ref, dw_ref, dwb_ref, dbin_ref, acc_ref, bacc_ref):
        acc_ref[...] = jnp.zeros_like(acc_ref)
        bacc_ref[...] = jnp.zeros_like(bacc_ref)

        def chunk(i, carry):
            base = pl.multiple_of(i * rows, rows)
            first, last = i == 0, i == nch - 1
            a = _load_halo(u_ref, 0, base, rows, t, first, False, halo, 0)
            g = _load_halo(u_ref, 1, base, rows, t, first, False, halo, 0)
            sg = _sigmoid(g)
            xs = a * sg
            dcs = _load_halo(dc_ref, None, base, rows, t, first, last, 0, halo)
            own = dcs[:rows]
            dglu = w_ref[taps - 1:taps, :] * own
            acc_ref[taps - 1] += _fold8(own * xs[halo:])
            for j in range(taps - 1):
                s = taps - 1 - j
                dglu = dglu + w_ref[j:j + 1, :] * pltpu.roll(dcs, n - s, 0)[:rows]
                acc_ref[j] += _fold8(own * pltpu.roll(xs, s, 0)[halo:])
            a_c, sg_c = a[halo:], sg[halo:]
            da = dglu * sg_c
            dg = dglu * a_c * sg_c * (1.0 - sg_c)
            du_ref[0, pl.ds(base, rows), :] = da.astype(du_ref.dtype)
            du_ref[1, pl.ds(base, rows), :] = dg.astype(du_ref.dtype)
            bacc_ref[0] += _fold8(own)
            bacc_ref[1] += _fold8(da)
            bacc_ref[2] += _fold8(dg)
            return carry

        lax.fori_loop(0, nch, chunk, 0)
        for j in range(taps):
            dw_ref[j:j + 1, :] = jnp.sum(acc_ref[j], axis=0, keepdims=True)
        dwb_ref[...] = jnp.sum(bacc_ref[0], axis=0, keepdims=True)
        dbin_ref[0] = jnp.sum(bacc_ref[1], axis=0, keepdims=True)
        dbin_ref[1] = jnp.sum(bacc_ref[2], axis=0, keepdims=True)

    blk = pl.BlockSpec((2, t, tc), lambda j: (0, 0, j))
    col = pl.BlockSpec((t, tc), lambda j: (0, j))
    return pl.pallas_call(
        body, grid=(d // tc,), in_specs=[blk, col, pl.BlockSpec((taps, tc), lambda j: (0, j))],
        out_specs=(blk, pl.BlockSpec((taps, tc), lambda j: (0, j)), pl.BlockSpec((1, tc), lambda j: (0, j)),
                   pl.BlockSpec((2, 1, tc), lambda j: (0, 0, j))),
        out_shape=(jax.ShapeDtypeStruct((2, t, d), BF16), jax.ShapeDtypeStruct((taps, d), F32),
                   jax.ShapeDtypeStruct((1, d), F32), jax.ShapeDtypeStruct((2, 1, d), F32)),
        scratch_shapes=[pltpu.VMEM((taps, 8, tc), F32), pltpu.VMEM((3, 8, tc), F32)], name="glu_conv_bwd",
        compiler_params=_params("parallel"))(u2, dc, w)


def _pool_select(grp, levels):
    out = levels[3]
    for k in (2, 1, 0):
        out = jnp.where(grp == k, levels[k], out)
    return out


def _pool_count(base, rows, tc, grp):
    tpos = (base + lax.broadcasted_iota(jnp.int32, (rows, tc), 0) + 1).astype(F32)
    window = jnp.left_shift(2, grp).astype(F32)
    return jnp.minimum(tpos, window)


def pool_fwd(h):
    t, d = h.shape
    pg = d // POOL_GROUPS
    tc = _pick(pg, (256, 128))
    rows = _chunk_rows(t)
    nch = t // rows
    halo = 16

    def body(h_ref, o_ref):
        grp = (pl.program_id(0) * tc) // pg

        def chunk(i, carry):
            base = pl.multiple_of(i * rows, rows)
            xs = _load_halo(h_ref, None, base, rows, t, i == 0, False, halo, 0)
            levels, cur = [], xs
            for k in range(4):
                cur = cur + pltpu.roll(cur, 1 << k, 0)
                levels.append(cur[halo:])
            pooled = _pool_select(grp, levels) / _pool_count(base, rows, tc, grp)
            o_ref[pl.ds(base, rows), :] = (pooled - xs[halo:]).astype(o_ref.dtype)
            return carry

        lax.fori_loop(0, nch, chunk, 0)

    col = pl.BlockSpec((t, tc), lambda j: (0, j))
    return pl.pallas_call(body, grid=(d // tc,), in_specs=[col], out_specs=col,
                          out_shape=jax.ShapeDtypeStruct((t, d), BF16), name="pool_fwd",
                          compiler_params=_params("parallel"))(h)


def pool_bwd(dmix):
    t, d = dmix.shape
    pg = d // POOL_GROUPS
    tc = _pick(pg, (256, 128))
    rows = _chunk_rows(t)
    nch = t // rows
    halo = 16
    n = rows + halo

    def body(d_ref, o_ref):
        grp = (pl.program_id(0) * tc) // pg

        def chunk(i, carry):
            base = pl.multiple_of(i * rows, rows)
            ds = _load_halo(d_ref, None, base, rows, t, i == 0, i == nch - 1, 0, halo)
            levels, cur = [], ds / _pool_count(base, n, tc, grp)
            for k in range(4):
                cur = cur + pltpu.roll(cur, n - (1 << k), 0)
                levels.append(cur[:rows])
            o_ref[pl.ds(base, rows), :] = _pool_select(grp, levels) - ds[:rows]
            return carry

        lax.fori_loop(0, nch, chunk, 0)

    col = pl.BlockSpec((t, tc), lambda j: (0, j))
    return pl.pallas_call(body, grid=(d // tc,), in_specs=[col], out_specs=col,
                          out_shape=jax.ShapeDtypeStruct((t, d), F32), name="pool_bwd",
                          compiler_params=_params("parallel"))(dmix)


def mm_groups(a, wg, *, mode, out_dtype, scale=None, resid=None, raw_dtype=None, name):
    t, d = a.shape
    pg = wg.shape[1]
    tm = _pick(t, (1024, 512, 256, 128))
    tn = pg
    if mode == "nn":
        dims, b_blk, b_idx = NN, (None, pg, tn), (lambda i, j, kk: (j, 0, 0))
    else:
        dims, b_blk, b_idx = NT, (None, tn, pg), (lambda i, j, kk: (j, 0, 0))
    raw_shape = None if raw_dtype is None else jax.ShapeDtypeStruct((t, d), raw_dtype)
    return _mm(a, wg, dims=dims, sizes=(t, d, pg), plan=(tm, tn, pg, False), a_blk=(tm, pg), a_idx=lambda i, j, kk: (i, j),
               b_blk=b_blk, b_idx=b_idx, o_blk=(tm, tn), o_idx=lambda i, j, kk: (i, j),
               out_shape=jax.ShapeDtypeStruct((t, d), out_dtype), name=name,
               scale=None if scale is None else scale.reshape(1, d), vec_blk=(1, tn), vec_idx=lambda i, j, kk: (0, j),
               resid=resid, raw_shape=raw_shape)


def mm_groups_wgrad(at, dy, groups, *, out_dtype, name):
    d, t = at.shape
    pg = d // groups
    tk = _pick(t, (2048, 1024, 512, 256, 128))
    return _mm(at, dy, dims=NN, sizes=(d, pg, t), plan=(pg, pg, tk, False), a_blk=(pg, tk), a_idx=lambda i, j, kk: (i, kk),
               b_blk=(tk, pg), b_idx=lambda i, j, kk: (kk, i), o_blk=(None, pg, pg), o_idx=lambda i, j, kk: (i, 0, 0),
               out_shape=jax.ShapeDtypeStruct((groups, pg, pg), out_dtype), name=name)


def _split_dot(y, p):
    hi = y.astype(BF16)
    r1 = y - hi.astype(F32)
    mid = r1.astype(BF16)
    lo = (r1 - mid.astype(F32)).astype(BF16)
    pb = p.astype(BF16)
    dot = lambda v: jnp.dot(v, pb, preferred_element_type=F32)
    return (dot(hi) + dot(mid)) + dot(lo)


def rope_tables(positions):
    half = ROT_DIM // 2
    inv_freq = ROPE_THETA ** (-jnp.arange(0, ROT_DIM, 2, dtype=F32) / ROT_DIM)
    ang = positions.astype(F32)[:, None] * inv_freq
    t = positions.shape[0]
    cos, sin = jnp.cos(ang), jnp.sin(ang)
    rest = HEAD_DIM - ROT_DIM
    cosf = jnp.concatenate([cos, cos, jnp.ones((t, rest), F32)], axis=1)
    sinf = jnp.concatenate([-sin, sin, jnp.zeros((t, rest), F32)], axis=1)
    idx = jnp.arange(HEAD_DIM)
    partner = jnp.where(idx < half, idx + half, jnp.where(idx < ROT_DIM, idx - half, idx))
    pmat = (idx[:, None] == partner[None, :]).astype(F32)
    return cosf, sinf, pmat


def qk_rope_fwd(x, g, cosf, sinf, pmat, out_scale):
    hn, t, hd = x.shape
    tq = _pick(t, (512, 256, 128))

    def body(x_ref, g_ref, c_ref, s_ref, p_ref, o_ref):
        xf = x_ref[...]
        r = lax.rsqrt(jnp.mean(xf * xf, axis=-1, keepdims=True) + EPS)
        y = xf * r * g_ref[...]
        rot = y * c_ref[...] + _split_dot(y, p_ref[...]) * s_ref[...]
        o_ref[...] = (rot * out_scale).astype(o_ref.dtype)

    blk = pl.BlockSpec((None, tq, hd), lambda h, i: (h, i, 0))
    tab = pl.BlockSpec((tq, hd), lambda h, i: (i, 0))
    return pl.pallas_call(
        body, grid=(hn, t // tq),
        in_specs=[blk, pl.BlockSpec((1, hd), lambda h, i: (0, 0)), tab, tab, pl.BlockSpec((hd, hd), lambda h, i: (0, 0))],
        out_specs=blk, out_shape=jax.ShapeDtypeStruct((hn, t, hd), BF16), name="qk_rope_fwd",
        compiler_params=_params("parallel", "parallel"))(x, g.reshape(1, hd), cosf, sinf, pmat)


def qk_rope_bwd(dy, x, g, cosf, sinf, pmat_t, in_scale):
    hn, t, hd = x.shape
    tq = _pick(t, (512, 256, 128))

    def body(dy_ref, x_ref, g_ref, c_ref, s_ref, p_ref, dx_ref, dg_ref):
        step = pl.program_id(0) * pl.num_programs(1) + pl.program_id(1)
        dr = dy_ref[...] * in_scale
        dyn = dr * c_ref[...] + _split_dot(dr * s_ref[...], p_ref[...])
        xf = x_ref[...]
        r = lax.rsqrt(jnp.mean(xf * xf, axis=-1, keepdims=True) + EPS)
        xhat = xf * r
        dxh = dyn * g_ref[...]
        m = jnp.mean(dxh * xhat, axis=-1, keepdims=True)
        dx_ref[...] = r * (dxh - xhat * m)
        _acc_rows(dg_ref, jnp.sum(dyn * xhat, axis=0, keepdims=True), step)

    blk = pl.BlockSpec((None, tq, hd), lambda h, i: (h, i, 0))
    tab = pl.BlockSpec((tq, hd), lambda h, i: (i, 0))
    vec = pl.BlockSpec((1, hd), lambda h, i: (0, 0))
    return pl.pallas_call(
        body, grid=(hn, t // tq),
        in_specs=[blk, blk, vec, tab, tab, pl.BlockSpec((hd, hd), lambda h, i: (0, 0))],
        out_specs=(blk, vec), out_shape=(jax.ShapeDtypeStruct((hn, t, hd), F32), jax.ShapeDtypeStruct((1, hd), F32)),
        name="qk_rope_bwd", compiler_params=_params("arbitrary", "arbitrary"))(dy, x, g.reshape(1, hd), cosf, sinf, pmat_t)


NEG_BIG = -1e30


def _att_masks(i):
    qi = lax.broadcasted_iota(jnp.int32, (ATT_BLOCK, ATT_BLOCK), 0)
    kj = lax.broadcasted_iota(jnp.int32, (ATT_BLOCK, ATT_BLOCK), 1)
    return kj <= qi, jnp.logical_and(kj > qi, i > 0)


def _att_probs(q, kc, kp, mask_c, mask_p, sink):
    s_c = jnp.where(mask_c, lax.dot_general(q, kc, NT, preferred_element_type=F32), NEG_BIG)
    s_p = jnp.where(mask_p, lax.dot_general(q, kp, NT, preferred_element_type=F32), NEG_BIG)
    m = jnp.maximum(jnp.maximum(jnp.max(s_c, axis=-1, keepdims=True), jnp.max(s_p, axis=-1, keepdims=True)), sink)
    p_c, p_p = jnp.exp(s_c - m), jnp.exp(s_p - m)
    p_s = jnp.exp(sink - m)
    denom = jnp.sum(p_c, axis=-1, keepdims=True) + jnp.sum(p_p, axis=-1, keepdims=True) + p_s
    return p_c, p_p, p_s, denom


def _att_specs(t):
    nb = t // ATT_BLOCK
    qblk = pl.BlockSpec((KV_GROUP, ATT_BLOCK, HEAD_DIM), lambda kv, i: (kv, i, 0))
    cur = pl.BlockSpec((None, ATT_BLOCK, HEAD_DIM), lambda kv, i: (kv, i, 0))
    prev = pl.BlockSpec((None, ATT_BLOCK, HEAD_DIM), lambda kv, i: (kv, jnp.maximum(i - 1, 0), 0))
    return nb, qblk, cur, prev, pl.BlockSpec(memory_space=pltpu.SMEM)


def attn_fwd(q, k, v, sinks):
    h, t, hd = q.shape
    nb, qblk, cur, prev, smem = _att_specs(t)

    def body(q_ref, kc_ref, kp_ref, vc_ref, vp_ref, sink_ref, o_ref):
        kv, i = pl.program_id(0), pl.program_id(1)
        mask_c, mask_p = _att_masks(i)
        kc, kp, vc, vp = kc_ref[...], kp_ref[...], vc_ref[...], vp_ref[...]
        for g in range(KV_GROUP):
            p_c, p_p, _, denom = _att_probs(q_ref[g], kc, kp, mask_c, mask_p, sink_ref[kv * KV_GROUP + g])
            o = (jnp.dot(p_c.astype(BF16), vc, preferred_element_type=F32)
                 + jnp.dot(p_p.astype(BF16), vp, preferred_element_type=F32)) / denom
            o_ref[g] = o.astype(o_ref.dtype)

    return pl.pallas_call(
        body, grid=(h // KV_GROUP, nb), in_specs=[qblk, cur, prev, cur, prev, smem], out_specs=qblk,
        out_shape=jax.ShapeDtypeStruct((h, t, hd), BF16), name="attn_fwd",
        compiler_params=_params("parallel", "parallel"))(q, k, k, v, v, sinks)


def attn_bwd(q, k, v, do, sinks):
    h, t, hd = q.shape
    kvh = h // KV_GROUP
    nb, qblk, cur, prev, smem = _att_specs(t)

    def body(q_ref, kc_ref, kp_ref, vc_ref, vp_ref, do_ref, sink_ref, dq_ref, dk_ref, dv_ref, dsk_ref):
        kv, i = pl.program_id(0), pl.program_id(1)

        @pl.when(i == 0)
        def _():
            dk_ref[...] = jnp.zeros_like(dk_ref)
            dv_ref[...] = jnp.zeros_like(dv_ref)
            dsk_ref[...] = jnp.zeros_like(dsk_ref)

        mask_c, mask_p = _att_masks(i)
        kc, kp, vc, vp = kc_ref[...], kp_ref[...], vc_ref[...], vp_ref[...]
        zero = jnp.zeros((ATT_BLOCK, hd), F32)
        dkc, dkp, dvc, dvp = zero, zero, zero, zero
        dsink_rows = []
        for g in range(KV_GROUP):
            q = q_ref[g]
            p_c, p_p, p_s, denom = _att_probs(q, kc, kp, mask_c, mask_p, sink_ref[kv * KV_GROUP + g])
            inv = 1.0 / denom
            pn_c, pn_p = p_c * inv, p_p * inv
            dob = do_ref[g].astype(BF16)
            dp_c = lax.dot_general(dob, vc, NT, preferred_element_type=F32)
            dp_p = lax.dot_general(dob, vp, NT, preferred_element_type=F32)
            dsum = jnp.sum(pn_c * dp_c, axis=-1, keepdims=True) + jnp.sum(pn_p * dp_p, axis=-1, keepdims=True)
            ds_c = (pn_c * (dp_c - dsum)).astype(BF16)
            ds_p = (pn_p * (dp_p - dsum)).astype(BF16)
            dq_ref[g] = (jnp.dot(ds_c, kc, preferred_element_type=F32) + jnp.dot(ds_p, kp, preferred_element_type=F32))
            dkc = dkc + lax.dot_general(ds_c, q, TN, preferred_element_type=F32)
            dkp = dkp + lax.dot_general(ds_p, q, TN, preferred_element_type=F32)
            dvc = dvc + lax.dot_general(pn_c.astype(BF16), dob, TN, preferred_element_type=F32)
            dvp = dvp + lax.dot_general(pn_p.astype(BF16), dob, TN, preferred_element_type=F32)
            dsink = -jnp.sum(p_s * inv * dsum, axis=0, keepdims=True)
            dsink_rows.append(jnp.broadcast_to(dsink, (1, 128)))
        here = pl.ds(pl.multiple_of(i * ATT_BLOCK, ATT_BLOCK), ATT_BLOCK)
        before = pl.ds(pl.multiple_of(jnp.maximum(i - 1, 0) * ATT_BLOCK, ATT_BLOCK), ATT_BLOCK)
        dk_ref[here, :] += dkc
        dv_ref[here, :] += dvc
        dk_ref[before, :] += dkp
        dv_ref[before, :] += dvp
        dsk_ref[...] += jnp.concatenate(dsink_rows, axis=0)

    whole = pl.BlockSpec((None, t, hd), lambda kv, i: (kv, 0, 0))
    return pl.pallas_call(
        body, grid=(kvh, nb), in_specs=[qblk, cur, prev, cur, prev, qblk, smem],
        out_specs=(qblk, whole, whole, pl.BlockSpec((None, KV_GROUP, 128), lambda kv, i: (kv, 0, 0))),
        out_shape=(jax.ShapeDtypeStruct((h, t, hd), F32), jax.ShapeDtypeStruct((kvh, t, hd), F32),
                   jax.ShapeDtypeStruct((kvh, t, hd), F32), jax.ShapeDtypeStruct((kvh, KV_GROUP, 128), F32)),
        name="attn_bwd", compiler_params=_params("parallel", "arbitrary"))(q, k, k, v, v, do, sinks)


def _ffn_taps(w, b):
    f2 = w.shape[1]
    return w.reshape(FFN_TAPS, 2, f2 // 2).transpose(1, 0, 2), b.reshape(2, 1, f2 // 2)


def ffn_fwd(x, W, p):
    h = rms_fwd(x, W[p + "ffn_norm_g"], BF16)
    u2 = mm_nn_cols(h, W[p + "ffn_w_up"], split=True, out_dtype=BF16, name="ffn_up")
    w3, b2 = _ffn_taps(W[p + "ffn_dw_w"], W[p + "ffn_dw_b"])
    a = ffn_gate_fwd(u2, w3, b2)
    y = mm_nn(a, W[p + "ffn_w_down"], out_dtype=F32, resid=x, name="ffn_down")
    return y, (x, h, u2, a)


def ffn_bwd(saved, W, p, dy):
    x, h, u2, a = saved
    dyf, dyb = dy
    w3, b2 = _ffn_taps(W[p + "ffn_dw_w"], W[p + "ffn_dw_b"])
    grads = {p + "ffn_w_down": mm_wgrad(a.T, dyb, out_dtype=BF16, name="ffn_down_dw")}
    da = mm_nt(dyb, W[p + "ffn_w_down"], out_dtype=BF16, name="ffn_down_dx")
    du2, dw3, db2 = ffn_gate_bwd(u2, da, w3, b2)
    grads[p + "ffn_dw_w"] = dw3.transpose(1, 0, 2).reshape(FFN_TAPS, -1)
    grads[p + "ffn_dw_b"] = db2.reshape(-1)
    grads[p + "ffn_w_up"] = mm_wgrad_cols(h.T, du2, split=True, out_dtype=BF16, name="ffn_up_dw")
    dh = mm_nt_cols(du2, W[p + "ffn_w_up"], split=True, out_dtype=F32, name="ffn_up_dx")
    dx, dx16, dg = rms_bwd(x, W[p + "ffn_norm_g"], dh, dyf)
    grads[p + "ffn_norm_g"] = dg.reshape(-1)
    return (dx, dx16), grads


def conf_fwd(x, W, p):
    d = x.shape[1]
    h = rms_fwd(x, W[p + "norm_g"], BF16)
    u2 = mm_nn_cols(h, W[p + "a_w_in"], split=True, out_dtype=BF16, bias=W[p + "a_b_in"], name="conf_in")
    c = glu_conv_fwd(u2, W[p + "a_dw_w"], W[p + "a_dw_b"].reshape(1, d))
    s = ln_silu_fwd(c, W[p + "a_ln_g"], W[p + "a_ln_b"])
    y = mm_nn(s, W[p + "a_w_out"], out_dtype=F32, bias=W[p + "a_b_out"], resid=x, name="conf_out")
    return y, (x, h, u2, c, s)


def conf_bwd(saved, W, p, dy):
    x, h, u2, c, s = saved
    dyf, dyb = dy
    grads = {p + "a_w_out": mm_wgrad(s.T, dyb, out_dtype=BF16, name="conf_out_dw"), p + "a_b_out": col_sum(dyf).reshape(-1)}
    ds = mm_nt(dyb, W[p + "a_w_out"], out_dtype=BF16, name="conf_out_dx")
    dc, dlg, dlb = ln_silu_bwd(c, W[p + "a_ln_g"], W[p + "a_ln_b"], ds)
    grads[p + "a_ln_g"], grads[p + "a_ln_b"] = dlg.reshape(-1), dlb.reshape(-1)
    du2, ddw, ddwb, dbin = glu_conv_bwd(u2, dc, W[p + "a_dw_w"])
    grads[p + "a_dw_w"], grads[p + "a_dw_b"], grads[p + "a_b_in"] = ddw, ddwb.reshape(-1), dbin.reshape(-1)
    grads[p + "a_w_in"] = mm_wgrad_cols(h.T, du2, split=True, out_dtype=BF16, name="conf_in_dw")
    dh = mm_nt_cols(du2, W[p + "a_w_in"], split=True, out_dtype=F32, name="conf_in_dx")
    dx, dx16, dg = rms_bwd(x, W[p + "norm_g"], dh, dyf)
    grads[p + "norm_g"] = dg.reshape(-1)
    return (dx, dx16), grads


def pool_layer_fwd(x, W, p):
    h = rms_fwd(x, W[p + "norm_g"], F32)
    mixed = pool_fwd(h)
    y, ypre = mm_groups(mixed, W[p + "b_w_group"], mode="nn", out_dtype=F32, scale=W[p + "b_scale"], resid=x,
                        raw_dtype=F32, name="pool_mix")
    return y, (x, mixed, ypre)


def pool_layer_bwd(saved, W, p, dy):
    x, mixed, ypre = saved
    dyf, dyb = dy
    dyp, dscale = scale_bwd(dyf, ypre, W[p + "b_scale"])
    grads = {p + "b_scale": dscale.reshape(-1),
             p + "b_w_group": mm_groups_wgrad(mixed.T, dyp, POOL_GROUPS, out_dtype=BF16, name="pool_mix_dw")}
    dmix = mm_groups(dyp, W[p + "b_w_group"], mode="nt", out_dtype=F32, name="pool_mix_dx")
    dh = pool_bwd(dmix)
    dx, dx16, dg = rms_bwd(x, W[p + "norm_g"], dh, dyf)
    grads[p + "norm_g"] = dg.reshape(-1)
    return (dx, dx16), grads


def _heads(a, n):
    t = a.shape[0]
    return a.reshape(t, n, HEAD_DIM).transpose(1, 0, 2)


def _unheads(a):
    n, t, _ = a.shape
    return a.transpose(1, 0, 2).reshape(t, n * HEAD_DIM)


def attn_layer_fwd(x, W, p, tables):
    d = x.shape[1]
    nh = d // HEAD_DIM
    nkv = nh // KV_GROUP
    cosf, sinf, pmat = tables
    h = rms_fwd(x, W[p + "norm_g"], BF16)
    qkv = mm_nn_cols(h, W[p + "c_w_qkv"], split=False, out_dtype=F32, name="att_qkv")
    q = _heads(qkv[:, :d], nh)
    k = _heads(qkv[:, d:d + nkv * HEAD_DIM], nkv)
    v = _heads(qkv[:, d + nkv * HEAD_DIM:], nkv).astype(BF16)
    qr = qk_rope_fwd(q, W[p + "c_q_norm_g"], cosf, sinf, pmat, HEAD_DIM ** -0.5)
    kr = qk_rope_fwd(k, W[p + "c_k_norm_g"], cosf, sinf, pmat, 1.0)
    o = attn_fwd(qr, kr, v, W[p + "c_sinks"])
    o2 = _unheads(o)
    y = mm_nn(o2, W[p + "c_w_o"], out_dtype=F32, resid=x, name="att_out")
    return y, (x, h, q, k, v, qr, kr, o2)


def attn_layer_bwd(saved, W, p, dy, tables):
    x, h, q, k, v, qr, kr, o2 = saved
    dyf, dyb = dy
    cosf, sinf, pmat = tables
    nh = q.shape[0]
    grads = {p + "c_w_o": mm_wgrad(o2.T, dyb, out_dtype=BF16, name="att_out_dw")}
    do = _heads(mm_nt(dyb, W[p + "c_w_o"], out_dtype=BF16, name="att_out_dx"), nh)
    dqr, dkr, dv, dsk = attn_bwd(qr, kr, v, do, W[p + "c_sinks"])
    grads[p + "c_sinks"] = dsk[:, :, 0].reshape(-1)
    dq, dqg = qk_rope_bwd(dqr, q, W[p + "c_q_norm_g"], cosf, sinf, pmat.T, HEAD_DIM ** -0.5)
    dk, dkg = qk_rope_bwd(dkr, k, W[p + "c_k_norm_g"], cosf, sinf, pmat.T, 1.0)
    grads[p + "c_q_norm_g"], grads[p + "c_k_norm_g"] = dqg.reshape(-1), dkg.reshape(-1)
    dqkv = jnp.concatenate([_unheads(dq), _unheads(dk), _unheads(dv)], axis=1).astype(BF16)
    grads[p + "c_w_qkv"] = mm_wgrad_cols(h.T, dqkv, split=False, out_dtype=BF16, name="att_qkv_dw")
    dh = mm_nt_cols(dqkv, W[p + "c_w_qkv"], split=False, out_dtype=F32, name="att_qkv_dx")
    dx, dx16, dg = rms_bwd(x, W[p + "norm_g"], dh, dyf)
    grads[p + "norm_g"] = dg.reshape(-1)
    return (dx, dx16), grads


def local_step(x, positions, tgt, W):
    tables = rope_tables(positions)
    mixers = [(conf_fwd, conf_bwd), (pool_layer_fwd, pool_layer_bwd), (attn_layer_fwd, attn_layer_bwd), (conf_fwd, conf_bwd)]
    saved = []
    for i, (fwd, _) in enumerate(mixers):
        p = f"l{i}_"
        x, sm = fwd(x, W, p, tables) if i == 2 else fwd(x, W, p)
        x, sf = ffn_fwd(x, W, p)
        saved.append((sm, sf))
    dyf, dyb, sq = loss_grad(x, tgt)
    dy = (dyf, dyb)
    loss = 0.5 * jnp.sum(sq) / x.shape[1]
    grads = {}
    for i in reversed(range(len(mixers))):
        p = f"l{i}_"
        sm, sf = saved[i]
        dy, g = ffn_bwd(sf, W, p, dy)
        grads.update(g)
        bwd = mixers[i][1]
        dy, g = bwd(sm, W, p, dy, tables) if i == 2 else bwd(sm, W, p, dy)
        grads.update(g)
    return loss, dy[0], grads


ANY = pl.BlockSpec(memory_space=pl.ANY)


def _place():
    x, y, c = lax.axis_index("x"), lax.axis_index("y"), lax.axis_index("c")
    chips = [(1 - x, y), (x, 1 - y), (1 - x, 1 - y)]
    return x, y, c, 2 * x + y, (x, y, 1 - c), chips


def _half(rows, which):
    return pl.ds(which * (rows // 2), rows // 2)


def place_block(shard, chip_core, out_dtype):
    rows, cols = shard.shape
    tr = rows
    for cand in (512, 256, 128, 64, 32, 16):
        if rows % cand == 0 and cand * cols * 4 <= (2 << 20):
            tr = cand
            break

    def body(pos_ref, s_ref, o_ref):
        o_ref[...] = s_ref[...].astype(o_ref.dtype)

    grid_spec = pltpu.PrefetchScalarGridSpec(
        num_scalar_prefetch=1, grid=(rows // tr,), in_specs=[pl.BlockSpec((tr, cols), lambda i, pos: (i, 0))],
        out_specs=pl.BlockSpec((None, tr, cols), lambda i, pos: (pos[0], i, 0)))
    return pl.pallas_call(body, grid_spec=grid_spec, out_shape=jax.ShapeDtypeStruct((N_CHIPS, rows, cols), out_dtype),
                          name="place_block", compiler_params=_params("parallel"))(chip_core, shard)


def all_gather_chips(bufs):
    n = len(bufs)

    def body(*refs):
        outs = refs[n:2 * n]
        ici_send, ici_recv, d2d_send, d2d_recv = refs[2 * n:]
        x, y, c, k, sibling, chips = _place()

        def rdma(src, dst, send, recv, dev):
            return pltpu.make_async_remote_copy(src_ref=src, dst_ref=dst, send_sem=send, recv_sem=recv,
                                                device_id=dev, device_id_type=MESH)

        sends = []
        for t in range(n):
            rows = bufs[t].shape[1]
            for j, (px, py) in enumerate(chips):
                mine = outs[t].at[k, _half(rows, c)]
                sends.append(rdma(mine, mine, ici_send.at[t, j], ici_recv.at[t, j], (px, py, c)))
        for cp in sends:
            cp.start()
        for t in range(n):
            rows = bufs[t].shape[1]
            for j, (px, py) in enumerate(chips):
                landed = outs[t].at[2 * px + py, _half(rows, c)]
                rdma(landed, landed, ici_send.at[t, j], ici_recv.at[t, j], sibling).wait_recv()
                fwd = rdma(landed, landed, d2d_send.at[t, j], d2d_recv.at[t, j], sibling)
                fwd.start()
                sends.append(fwd)
        for t in range(n):
            rows = bufs[t].shape[1]
            for j, (px, py) in enumerate(chips):
                other = outs[t].at[2 * px + py, _half(rows, 1 - c)]
                rdma(other, other, d2d_send.at[t, j], d2d_recv.at[t, j], sibling).wait_recv()
        for cp in sends:
            cp.wait_send()

    return pl.pallas_call(
        body, in_specs=[ANY] * n, out_specs=[ANY] * n,
        out_shape=[jax.ShapeDtypeStruct(b.shape, b.dtype) for b in bufs],
        input_output_aliases={t: t for t in range(n)},
        scratch_shapes=[pltpu.SemaphoreType.DMA((n, 3))] * 4,
        name="all_gather_chips", compiler_params=pltpu.CompilerParams())(*bufs)


def exchange_sibling_halves(gs):
    n = len(gs)

    def body(*refs):
        ins, outs = refs[:n], refs[n:2 * n]
        send, recv = refs[2 * n:]
        x, y, c, k, sibling, chips = _place()
        cps = []
        for t in range(n):
            rows = gs[t].shape[1]
            cps.append(pltpu.make_async_remote_copy(
                src_ref=ins[t].at[:, _half(rows, 1 - c), :], dst_ref=outs[t], send_sem=send.at[t], recv_sem=recv.at[t],
                device_id=sibling, device_id_type=MESH))
        for cp in cps:
            cp.start()
        for cp in cps:
            cp.wait()

    return pl.pallas_call(
        body, in_specs=[ANY] * n, out_specs=[ANY] * n,
        out_shape=[jax.ShapeDtypeStruct((g.shape[0], g.shape[1] // 2, g.shape[2]), g.dtype) for g in gs],
        scratch_shapes=[pltpu.SemaphoreType.DMA((n,))] * 2, name="exchange_sibling_halves",
        compiler_params=pltpu.CompilerParams())(*gs)


def exchange_chip_blocks(ps):
    n = len(ps)

    def body(*refs):
        ins, outs = refs[:n], refs[n:2 * n]
        send, recv = refs[2 * n:]
        x, y, c, k, sibling, chips = _place()
        cps = []
        for t in range(n):
            for j, (px, py) in enumerate(chips):
                cps.append(pltpu.make_async_remote_copy(
                    src_ref=ins[t].at[2 * px + py], dst_ref=outs[t].at[j], send_sem=send.at[t, j], recv_sem=recv.at[t, j],
                    device_id=(px, py, c), device_id_type=MESH))
        for cp in cps:
            cp.start()
        for cp in cps:
            cp.wait()

    return pl.pallas_call(
        body, in_specs=[ANY] * n, out_specs=[ANY] * n,
        out_shape=[jax.ShapeDtypeStruct((3,) + p.shape[1:], p.dtype) for p in ps],
        scratch_shapes=[pltpu.SemaphoreType.DMA((n, 3))] * 2,
        name="exchange_chip_blocks", compiler_params=pltpu.CompilerParams())(*ps)


def gather_sibling_halves(ss):
    n = len(ss)

    def body(*refs):
        outs = refs[n:2 * n]
        send, recv = refs[2 * n:]
        x, y, c, k, sibling, chips = _place()
        cps = []
        for t in range(n):
            rows = ss[t].shape[0]
            mine = outs[t].at[_half(rows, c)]
            cps.append(pltpu.make_async_remote_copy(src_ref=mine, dst_ref=mine, send_sem=send.at[t], recv_sem=recv.at[t],
                                                    device_id=sibling, device_id_type=MESH))
        for cp in cps:
            cp.start()
        for t in range(n):
            rows = ss[t].shape[0]
            got = outs[t].at[_half(rows, 1 - c)]
            pltpu.make_async_remote_copy(src_ref=got, dst_ref=got, send_sem=send.at[t], recv_sem=recv.at[t],
                                         device_id=sibling, device_id_type=MESH).wait_recv()
        for cp in cps:
            cp.wait_send()

    return pl.pallas_call(
        body, in_specs=[ANY] * n, out_specs=[ANY] * n,
        out_shape=[jax.ShapeDtypeStruct(s.shape, s.dtype) for s in ss],
        input_output_aliases={t: t for t in range(n)},
        scratch_shapes=[pltpu.SemaphoreType.DMA((n,))] * 2, name="gather_sibling_halves",
        compiler_params=pltpu.CompilerParams())(*ss)


def _sum_rows_tile(rows):
    return _pick(rows, (128, 64, 32, 16))


def add_sibling_half(g, land, core):
    nb, half, cols = land.shape
    tr = _sum_rows_tile(half)
    nrb = half // tr

    def body(c_ref, g_ref, l_ref, o_ref):
        o_ref[...] = (g_ref[...].astype(F32) + l_ref[...].astype(F32)).astype(o_ref.dtype)

    spec = pl.BlockSpec((None, tr, cols), lambda b, i, c_ref: (b, i, 0))
    grid_spec = pltpu.PrefetchScalarGridSpec(
        num_scalar_prefetch=1, grid=(nb, nrb),
        in_specs=[pl.BlockSpec((None, tr, cols), lambda b, i, c_ref: (b, c_ref[1] * nrb + i, 0)), spec], out_specs=spec)
    return pl.pallas_call(body, grid_spec=grid_spec, out_shape=jax.ShapeDtypeStruct(land.shape, BF16),
                          name="add_sibling_half", compiler_params=_params("parallel", "parallel"))(core, g, land)


def sum_chip_blocks(p, l2, chip_core):
    nb, half, cols = l2.shape
    tr = _sum_rows_tile(half)
    nrb = half // tr

    def body(pos_ref, p_ref, l_ref, o_ref):
        acc = p_ref[...].astype(F32)
        for b in range(nb):
            acc = acc + l_ref[b].astype(F32)
        o_ref[...] = acc

    grid_spec = pltpu.PrefetchScalarGridSpec(
        num_scalar_prefetch=1, grid=(nrb,),
        in_specs=[pl.BlockSpec((None, tr, cols), lambda i, pos: (pos[0], i, 0)),
                  pl.BlockSpec((nb, tr, cols), lambda i, pos: (0, i, 0))],
        out_specs=pl.BlockSpec((tr, cols), lambda i, pos: (pos[1] * nrb + i, 0)))
    return pl.pallas_call(body, grid_spec=grid_spec, out_shape=jax.ShapeDtypeStruct((2 * half, cols), F32),
                          name="sum_chip_blocks", compiler_params=_params("parallel"))(chip_core, p, l2)


def reduce_scatter(gs, chip_core):
    lands = exchange_sibling_halves(gs)
    ps = [add_sibling_half(g, l, chip_core) for g, l in zip(gs, lands)]
    l2s = exchange_chip_blocks(ps)
    ss = [sum_chip_blocks(p, l2, chip_core) for p, l2 in zip(ps, l2s)]
    return gather_sibling_halves(ss)


SMALL_CHUNK_ROWS = 256


def all_reduce_small(v):
    rows = v.shape[0]
    nchunk = rows // SMALL_CHUNK_ROWS

    def body(v_ref, o_ref, buf, send, recv):
        x, y, c = lax.axis_index("x"), lax.axis_index("y"), lax.axis_index("c")
        me = 4 * x + 2 * y + c
        buf[me] = v_ref[...]
        cps = []
        for d in range(1, N_DEV):
            peer = (x ^ ((d >> 2) & 1), y ^ ((d >> 1) & 1), c ^ (d & 1))
            cps.append(pltpu.make_async_remote_copy(src_ref=v_ref, dst_ref=buf.at[me], send_sem=send.at[d - 1],
                                                    recv_sem=recv.at[d - 1], device_id=peer, device_id_type=MESH))
        for cp in cps:
            cp.start()
        for d in range(1, N_DEV):
            got = buf.at[me ^ d]
            pltpu.make_async_remote_copy(src_ref=got, dst_ref=got, send_sem=send.at[d - 1], recv_sem=recv.at[d - 1],
                                         device_id=(x, y, c), device_id_type=MESH).wait_recv()
        for cp in cps:
            cp.wait_send()

        def chunk(i, carry):
            sl = pl.ds(pl.multiple_of(i * SMALL_CHUNK_ROWS, SMALL_CHUNK_ROWS), SMALL_CHUNK_ROWS)
            acc = buf[0, sl, :]
            for s in range(1, N_DEV):
                acc = acc + buf[s, sl, :]
            o_ref[sl, :] = acc
            return carry

        lax.fori_loop(0, nchunk, chunk, 0)

    vmem = pl.BlockSpec(memory_space=pltpu.VMEM)
    return pl.pallas_call(
        body, in_specs=[vmem], out_specs=vmem, out_shape=jax.ShapeDtypeStruct(v.shape, F32),
        scratch_shapes=[pltpu.VMEM((N_DEV,) + v.shape, F32), pltpu.SemaphoreType.DMA((N_DEV - 1,)),
                        pltpu.SemaphoreType.DMA((N_DEV - 1,))],
        name="all_reduce_small",
        compiler_params=pltpu.CompilerParams(vmem_limit_bytes=VMEM_LIMIT_BYTES))(v)


def adamw(w, g, m, v):
    rows, cols = w.shape
    tr = rows
    for cand in (512, 256, 128, 64, 32, 16, 8):
        if rows % cand == 0 and cand * cols * 4 <= (1 << 20):
            tr = cand
            break
    c1 = 1.0 - ADAM_B1 ** ADAM_STEP
    c2 = 1.0 - ADAM_B2 ** ADAM_STEP

    def body(w_ref, g_ref, m_ref, v_ref, d_ref, nm_ref, nv_ref):
        gf = g_ref[...]
        nm = ADAM_B1 * m_ref[...] + (1.0 - ADAM_B1) * gf
        nv = ADAM_B2 * v_ref[...] + (1.0 - ADAM_B2) * (gf * gf)
        d_ref[...] = -ADAM_LR * ((nm / c1) / (jnp.sqrt(nv / c2) + ADAM_EPS) + ADAM_WD * w_ref[...])
        nm_ref[...] = nm
        nv_ref[...] = nv

    spec = pl.BlockSpec((tr, cols), lambda i: (i, 0))
    shape = jax.ShapeDtypeStruct((rows, cols), F32)
    return pl.pallas_call(body, grid=(rows // tr,), in_specs=[spec] * 4, out_specs=(spec,) * 3, out_shape=(shape,) * 3,
                          name="adamw", compiler_params=_params("parallel"))(w, g, m, v)


TAP_ROWS_ALIGN = 16
FLAT_ALIGN = 128 * SMALL_CHUNK_ROWS


def _pad_to(a, n):
    return jnp.pad(a, (0, n - a.shape[0]))


def _round_up(n, m):
    return (n + m - 1) // m * m


def train_step(a):
    x, positions, tgt = a["x"][0], a["positions"][0], a["loss_target"][0]
    mats = [n for n in WEIGHT_NAMES if _kind(n) in ("col", "row", "grp")]
    taps = [n for n in WEIGHT_NAMES if _kind(n) == "tap"]
    reps = [n for n in WEIGHT_NAMES if _kind(n) == "rep"]
    chip = 2 * lax.axis_index("x") + lax.axis_index("y")
    chip_core = jnp.stack([chip, lax.axis_index("c")]).astype(jnp.int32)

    bufs = [place_block(a[n].reshape(-1, a[n].shape[-1]), chip_core, BF16) for n in mats]
    for n in taps:
        padded = jnp.pad(a[n], ((0, _round_up(a[n].shape[0], TAP_ROWS_ALIGN) - a[n].shape[0]), (0, 0)))
        bufs.append(place_block(padded, chip_core, F32))
    gathered = all_gather_chips(bufs)
    W = {n: a[n] for n in reps}
    for n, g in zip(mats + taps, gathered):
        kind = _kind(n)
        if kind == "col":
            W[n] = g
        elif kind == "row":
            W[n] = g.reshape(-1, g.shape[-1])
        elif kind == "grp":
            grp, r, pg = a[n].shape
            W[n] = g.reshape(N_CHIPS, grp, r, pg).transpose(1, 0, 2, 3).reshape(grp, N_CHIPS * r, pg)
        else:
            nt = a[n].shape[0]
            W[n] = g[:, :nt].transpose(1, 0, 2).reshape(nt, -1)

    loss, dx, grads = local_step(x, positions, tgt, W)
    loss = lax.psum(loss, ("x", "y", "c"))

    gl = []
    for n in mats:
        g, kind = grads[n], _kind(n)
        if kind == "row":
            g = g.reshape(N_CHIPS, -1, g.shape[-1])
        elif kind == "grp":
            grp, r, pg = a[n].shape
            g = g.reshape(grp, N_CHIPS, r, pg).transpose(1, 0, 2, 3).reshape(N_CHIPS, grp * r, pg)
        gl.append(g)
    reduced = dict(zip(mats, reduce_scatter(gl, chip_core)))

    n_rep = _round_up(sum(a[n].size for n in reps), FLAT_ALIGN)
    flat_rep = _pad_to(jnp.concatenate([grads[n].reshape(-1) for n in reps]), n_rep)
    flat_tap = jnp.concatenate([grads[n].reshape(-1) for n in taps])
    flat = jnp.concatenate([flat_rep, _pad_to(flat_tap, _round_up(flat_tap.shape[0], FLAT_ALIGN))])
    summed = all_reduce_small(flat.reshape(-1, 128))
    rep_rows = n_rep // 128
    tap_flat = summed[rep_rows:].reshape(-1)

    out = {}
    pack = lambda pre: _pad_to(jnp.concatenate([a[pre + n].reshape(-1) for n in reps]), n_rep).reshape(-1, 128)
    g_rep = summed[:rep_rows]
    d_rep, m_rep, v_rep = adamw(pack(""), g_rep, pack("m_"), pack("v_"))
    off = 0
    for n in reps:
        size, shape = a[n].size, a[n].shape
        out[n] = tuple(f.reshape(-1)[off:off + size].reshape(shape) for f in (g_rep, d_rep, m_rep, v_rep))
        off += size
    off = 0
    for n in taps:
        nt, cs = a[n].shape
        full = tap_flat[off:off + nt * cs * N_CHIPS].reshape(nt, cs * N_CHIPS)
        off += nt * cs * N_CHIPS
        g = lax.dynamic_slice(full, (0, chip * cs), (nt, cs))
        out[n] = (g,) + tuple(adamw(a[n], g, a["m_" + n], a["v_" + n]))
    for n in mats:
        shape = a[n].shape
        two_d = lambda t: t.reshape(-1, shape[-1])
        g = reduced[n]
        out[n] = (g.reshape(shape),) + tuple(t.reshape(shape) for t in adamw(two_d(a[n]), g, two_d(a["m_" + n]), two_d(a["v_" + n])))

    res = [loss, dx[None]]
    for part in range(4):
        res += [out[n][part] for n in WEIGHT_NAMES]
    return tuple(res)


def kernel(x, positions, l0_norm_g, l0_a_w_in, l0_a_b_in, l0_a_dw_w, l0_a_dw_b, l0_a_ln_g, l0_a_ln_b, l0_a_w_out, l0_a_b_out, l0_ffn_norm_g, l0_ffn_w_up, l0_ffn_dw_w, l0_ffn_dw_b, l0_ffn_w_down, l1_norm_g, l1_b_w_group, l1_b_scale, l1_ffn_norm_g, l1_ffn_w_up, l1_ffn_dw_w, l1_ffn_dw_b, l1_ffn_w_down, l2_norm_g, l2_c_w_qkv, l2_c_q_norm_g, l2_c_k_norm_g, l2_c_sinks, l2_c_w_o, l2_ffn_norm_g, l2_ffn_w_up, l2_ffn_dw_w, l2_ffn_dw_b, l2_ffn_w_down, l3_norm_g, l3_a_w_in, l3_a_b_in, l3_a_dw_w, l3_a_dw_b, l3_a_ln_g, l3_a_ln_b, l3_a_w_out, l3_a_b_out, l3_ffn_norm_g, l3_ffn_w_up, l3_ffn_dw_w, l3_ffn_dw_b, l3_ffn_w_down, loss_target, m_l0_norm_g, m_l0_a_w_in, m_l0_a_b_in, m_l0_a_dw_w, m_l0_a_dw_b, m_l0_a_ln_g, m_l0_a_ln_b, m_l0_a_w_out, m_l0_a_b_out, m_l0_ffn_norm_g, m_l0_ffn_w_up, m_l0_ffn_dw_w, m_l0_ffn_dw_b, m_l0_ffn_w_down, m_l1_norm_g, m_l1_b_w_group, m_l1_b_scale, m_l1_ffn_norm_g, m_l1_ffn_w_up, m_l1_ffn_dw_w, m_l1_ffn_dw_b, m_l1_ffn_w_down, m_l2_norm_g, m_l2_c_w_qkv, m_l2_c_q_norm_g, m_l2_c_k_norm_g, m_l2_c_sinks, m_l2_c_w_o, m_l2_ffn_norm_g, m_l2_ffn_w_up, m_l2_ffn_dw_w, m_l2_ffn_dw_b, m_l2_ffn_w_down, m_l3_norm_g, m_l3_a_w_in, m_l3_a_b_in, m_l3_a_dw_w, m_l3_a_dw_b, m_l3_a_ln_g, m_l3_a_ln_b, m_l3_a_w_out, m_l3_a_b_out, m_l3_ffn_norm_g, m_l3_ffn_w_up, m_l3_ffn_dw_w, m_l3_ffn_dw_b, m_l3_ffn_w_down, v_l0_norm_g, v_l0_a_w_in, v_l0_a_b_in, v_l0_a_dw_w, v_l0_a_dw_b, v_l0_a_ln_g, v_l0_a_ln_b, v_l0_a_w_out, v_l0_a_b_out, v_l0_ffn_norm_g, v_l0_ffn_w_up, v_l0_ffn_dw_w, v_l0_ffn_dw_b, v_l0_ffn_w_down, v_l1_norm_g, v_l1_b_w_group, v_l1_b_scale, v_l1_ffn_norm_g, v_l1_ffn_w_up, v_l1_ffn_dw_w, v_l1_ffn_dw_b, v_l1_ffn_w_down, v_l2_norm_g, v_l2_c_w_qkv, v_l2_c_q_norm_g, v_l2_c_k_norm_g, v_l2_c_sinks, v_l2_c_w_o, v_l2_ffn_norm_g, v_l2_ffn_w_up, v_l2_ffn_dw_w, v_l2_ffn_dw_b, v_l2_ffn_w_down, v_l3_norm_g, v_l3_a_w_in, v_l3_a_b_in, v_l3_a_dw_w, v_l3_a_dw_b, v_l3_a_ln_g, v_l3_a_ln_b, v_l3_a_w_out, v_l3_a_b_out, v_l3_ffn_norm_g, v_l3_ffn_w_up, v_l3_ffn_dw_w, v_l3_ffn_dw_b, v_l3_ffn_w_down):
    return train_step(dict(locals()))
```

```python
import functools

import jax
import jax.numpy as jnp
from jax import lax
from jax.experimental import pallas as pl
from jax.experimental.pallas import tpu as pltpu

F32 = jnp.float32
BF16 = jnp.bfloat16
EPS = 1e-6
HEAD_DIM = 64
KV_GROUP = 8
ATT_BLOCK = 128
ROT_DIM = 16
ROPE_THETA = 500000.0
POOL_GROUPS = 4
CONF_TAPS = 31
FFN_TAPS = 3
N_CHIPS = 4
N_DEV = 8
ADAM_LR, ADAM_B1, ADAM_B2, ADAM_EPS, ADAM_WD, ADAM_STEP = 0.001, 0.9, 0.999, 1e-08, 0.01, 10
VMEM_LIMIT_BYTES = 56 * 1024 * 1024
MESH = pl.DeviceIdType.MESH

CONF_NAMES = ["norm_g", "a_w_in", "a_b_in", "a_dw_w", "a_dw_b", "a_ln_g", "a_ln_b", "a_w_out", "a_b_out"]
FFN_NAMES = ["ffn_norm_g", "ffn_w_up", "ffn_dw_w", "ffn_dw_b", "ffn_w_down"]
POOL_NAMES = ["norm_g", "b_w_group", "b_scale"]
ATT_NAMES = ["norm_g", "c_w_qkv", "c_q_norm_g", "c_k_norm_g", "c_sinks", "c_w_o"]
WEIGHT_NAMES = ([f"l0_{n}" for n in CONF_NAMES + FFN_NAMES] + [f"l1_{n}" for n in POOL_NAMES + FFN_NAMES]
                + [f"l2_{n}" for n in ATT_NAMES + FFN_NAMES] + [f"l3_{n}" for n in CONF_NAMES + FFN_NAMES])
COL_SHARDED = ("a_w_in", "ffn_w_up", "c_w_qkv")
ROW_SHARDED = ("a_w_out", "ffn_w_down", "c_w_o")
TAP_SHARDED = ("a_dw_w", "ffn_dw_w")


def _kind(name):
    base = name[3:]
    if base in COL_SHARDED:
        return "col"
    if base in ROW_SHARDED:
        return "row"
    if base in TAP_SHARDED:
        return "tap"
    if base == "b_w_group":
        return "grp"
    return "rep"


def _pick(n, prefs):
    for p in prefs:
        if p <= n and n % p == 0:
            return p
    return n


def _params(*sem):
    return pltpu.CompilerParams(dimension_semantics=sem, vmem_limit_bytes=VMEM_LIMIT_BYTES)


def _sigmoid(x):
    return 1.0 / (1.0 + jnp.exp(-x))


NN = (((1,), (0,)), ((), ()))
NT = (((1,), (1,)), ((), ()))
TN = (((0,), (0,)), ((), ()))


MM_VMEM_BUDGET = 44 * 1024 * 1024
MM_STEP_SECONDS = 0.35e-6
MM_FLOPS, MM_HBM_BYTES = 9.0e14, 3.0e12
TILE_SIZES = (4096, 2816, 2048, 1408, 1024, 704, 640, 512, 256, 128)


def _tile_options(n, lane):
    opts = [c for c in TILE_SIZES if c <= n and n % c == 0 and (not lane or c % 128 == 0)]
    return opts or [n]


def _mm_plan(m, n, k, *, n_unit=None, k_unit=None, a_bytes=2, b_bytes=2, o_bytes=2, extra_bytes=0):
    best = None
    for tm in _tile_options(m, False):
        for tn in _tile_options(n_unit or n, True):
            for tk in _tile_options(k_unit or k, True) + ([k] if not k_unit else []):
                nk = k // tk
                vmem = 2 * (tm * tk * a_bytes + tk * tn * b_bytes + tm * tn * (o_bytes + extra_bytes)) + tm * tn * 4 * (2 if nk > 1 else 1)
                if vmem > MM_VMEM_BUDGET:
                    continue
                ni, nj = m // tm, n // tn
                a_all, b_all, o_all = m * k * a_bytes, k * n * b_bytes, m * n * (o_bytes + extra_bytes)
                for i_inner in (False, True):
                    if nk > 1:
                        traffic = a_all * nj + b_all * ni + o_all
                    elif i_inner:
                        traffic = a_all * nj + b_all + o_all
                    else:
                        traffic = a_all + b_all * ni + o_all
                    cost = ni * nj * nk * MM_STEP_SECONDS + max(2.0 * m * n * k / MM_FLOPS, traffic / MM_HBM_BYTES)
                    if best is None or cost < best[0]:
                        best = (cost, tm, tn, tk, i_inner)
    assert best is not None, (m, n, k)
    return best[1:]


def _mm(a, b, *, dims, sizes, plan, a_blk, a_idx, b_blk, b_idx, o_blk, o_idx, out_shape, name,
        bias=None, scale=None, vec_blk=None, vec_idx=None, resid=None, raw_shape=None):
    m, n, k = sizes
    tm, tn, tk, i_inner = plan
    ni, nj, nk = m // tm, n // tn, k // tk
    has_bias, has_scale, has_resid, want_raw = bias is not None, scale is not None, resid is not None, raw_shape is not None

    def body(*refs):
        a_ref, b_ref = refs[0], refs[1]
        pos = 2
        bias_ref = scale_ref = resid_ref = raw_ref = None
        if has_bias:
            bias_ref = refs[pos]; pos += 1
        if has_scale:
            scale_ref = refs[pos]; pos += 1
        if has_resid:
            resid_ref = refs[pos]; pos += 1
        o_ref = refs[pos]; pos += 1
        if want_raw:
            raw_ref = refs[pos]; pos += 1
        part = lax.dot_general(a_ref[...].astype(BF16), b_ref[...].astype(BF16), dims, preferred_element_type=F32)

        def finish(r):
            if want_raw:
                raw_ref[...] = r.astype(raw_ref.dtype)
            if has_bias:
                r = r + bias_ref[...]
            if has_scale:
                r = r * scale_ref[...]
            if has_resid:
                r = r + resid_ref[...]
            o_ref[...] = r.astype(o_ref.dtype)

        if nk == 1:
            finish(part)
        else:
            acc_ref = refs[pos]
            kk = pl.program_id(2)

            @pl.when(kk == 0)
            def _():
                acc_ref[...] = part

            @pl.when(kk > 0)
            def _():
                acc_ref[...] += part

            @pl.when(kk == nk - 1)
            def _():
                finish(acc_ref[...])

    order = (lambda f: (lambda j, i, kk: f(i, j, kk))) if i_inner else (lambda f: f)
    spec = lambda blk, idx: pl.BlockSpec(blk, order(idx))
    operands, in_specs = [a, b], [spec(a_blk, a_idx), spec(b_blk, b_idx)]
    for v in (bias, scale):
        if v is not None:
            operands.append(v); in_specs.append(spec(vec_blk, vec_idx))
    if has_resid:
        operands.append(resid); in_specs.append(spec(o_blk, o_idx))
    out_shapes, out_specs = out_shape, spec(o_blk, o_idx)
    if want_raw:
        out_shapes, out_specs = (out_shape, raw_shape), (spec(o_blk, o_idx), spec(o_blk, o_idx))
    return pl.pallas_call(
        body, grid=(nj, ni, nk) if i_inner else (ni, nj, nk), in_specs=in_specs, out_specs=out_specs,
        out_shape=out_shapes, scratch_shapes=[pltpu.VMEM((tm, tn), F32)] if nk > 1 else [], name=name,
        compiler_params=_params("parallel", "parallel", "arbitrary"))(*operands)


def mm_nn_cols(a, g, *, split, out_dtype, bias=None, name):
    t, k = a.shape
    ns = g.shape[2]
    n = N_CHIPS * ns
    plan = _mm_plan(t, n, k, n_unit=ns, o_bytes=jnp.dtype(out_dtype).itemsize)
    tm, tn, tk, _ = plan
    nj = ns // tn
    if split:
        o_blk, o_idx = (None, tm, tn), (lambda i, j, kk: (j // (2 * nj), i, j % (2 * nj)))
        out_shape = jax.ShapeDtypeStruct((2, t, 2 * ns), out_dtype)
        vec_blk, vec_idx = (None, 1, tn), (lambda i, j, kk: (j // (2 * nj), 0, j % (2 * nj)))
        if bias is not None:
            bias = bias.reshape(2, 1, 2 * ns)
    else:
        o_blk, o_idx = (tm, tn), (lambda i, j, kk: (i, j))
        out_shape = jax.ShapeDtypeStruct((t, n), out_dtype)
        vec_blk, vec_idx = (1, tn), (lambda i, j, kk: (0, j))
        if bias is not None:
            bias = bias.reshape(1, n)
    return _mm(a, g, dims=NN, sizes=(t, n, k), plan=plan, a_blk=(tm, tk), a_idx=lambda i, j, kk: (i, kk),
               b_blk=(None, tk, tn), b_idx=lambda i, j, kk: (j // nj, kk, j % nj), o_blk=o_blk, o_idx=o_idx,
               out_shape=out_shape, name=name, bias=bias, vec_blk=vec_blk, vec_idx=vec_idx)


def mm_nn(a, w, *, out_dtype, bias=None, scale=None, resid=None, raw_dtype=None, name):
    t, k = a.shape
    n = w.shape[1]
    extra = (4 if resid is not None else 0) + (0 if raw_dtype is None else jnp.dtype(raw_dtype).itemsize)
    plan = _mm_plan(t, n, k, a_bytes=a.dtype.itemsize, o_bytes=jnp.dtype(out_dtype).itemsize, extra_bytes=extra)
    tm, tn, tk, _ = plan
    raw_shape = None if raw_dtype is None else jax.ShapeDtypeStruct((t, n), raw_dtype)
    return _mm(a, w, dims=NN, sizes=(t, n, k), plan=plan, a_blk=(tm, tk), a_idx=lambda i, j, kk: (i, kk),
               b_blk=(tk, tn), b_idx=lambda i, j, kk: (kk, j), o_blk=(tm, tn), o_idx=lambda i, j, kk: (i, j),
               out_shape=jax.ShapeDtypeStruct((t, n), out_dtype), name=name,
               bias=None if bias is None else bias.reshape(1, n), scale=None if scale is None else scale.reshape(1, n),
               vec_blk=(1, tn), vec_idx=lambda i, j, kk: (0, j), resid=resid, raw_shape=raw_shape)


def mm_nt(dy, w, *, out_dtype, name):
    t, n = dy.shape
    kdim = w.shape[0]
    plan = _mm_plan(t, kdim, n, a_bytes=dy.dtype.itemsize, o_bytes=jnp.dtype(out_dtype).itemsize)
    tm, tn, tk, _ = plan
    return _mm(dy, w, dims=NT, sizes=(t, kdim, n), plan=plan, a_blk=(tm, tk), a_idx=lambda i, j, kk: (i, kk),
               b_blk=(tn, tk), b_idx=lambda i, j, kk: (j, kk), o_blk=(tm, tn), o_idx=lambda i, j, kk: (i, j),
               out_shape=jax.ShapeDtypeStruct((t, kdim), out_dtype), name=name)


def mm_nt_cols(du, g, *, split, out_dtype, name):
    kdim, ns = g.shape[1], g.shape[2]
    t = du.shape[1] if split else du.shape[0]
    plan = _mm_plan(t, kdim, N_CHIPS * ns, k_unit=ns, o_bytes=jnp.dtype(out_dtype).itemsize)
    tm, tn, tk, _ = plan
    nkb = ns // tk
    if split:
        a_blk, a_idx = (None, tm, tk), (lambda i, j, kk: (kk // (2 * nkb), i, kk % (2 * nkb)))
    else:
        a_blk, a_idx = (tm, tk), (lambda i, j, kk: (i, kk))
    return _mm(du, g, dims=NT, sizes=(t, kdim, N_CHIPS * ns), plan=plan, a_blk=a_blk, a_idx=a_idx,
               b_blk=(None, tn, tk), b_idx=lambda i, j, kk: (kk // nkb, j, kk % nkb),
               o_blk=(tm, tn), o_idx=lambda i, j, kk: (i, j),
               out_shape=jax.ShapeDtypeStruct((t, kdim), out_dtype), name=name)


def mm_wgrad(at, dy, *, out_dtype, name):
    return mm_nn(at, dy, out_dtype=out_dtype, name=name)


def mm_wgrad_cols(ht, du, *, split, out_dtype, name):
    kdim, t = ht.shape
    ns = (du.shape[2] // 2) if split else (du.shape[1] // N_CHIPS)
    plan = _mm_plan(kdim, N_CHIPS * ns, t, n_unit=ns, o_bytes=jnp.dtype(out_dtype).itemsize)
    tm, tn, tk, _ = plan
    nj = ns // tn
    if split:
        b_blk, b_idx = (None, tk, tn), (lambda i, j, kk: (j // (2 * nj), kk, j % (2 * nj)))
    else:
        b_blk, b_idx = (tk, tn), (lambda i, j, kk: (kk, j))
    return _mm(ht, du, dims=NN, sizes=(kdim, N_CHIPS * ns, t), plan=plan, a_blk=(tm, tk), a_idx=lambda i, j, kk: (i, kk),
               b_blk=b_blk, b_idx=b_idx, o_blk=(None, tm, tn), o_idx=lambda i, j, kk: (j // nj, i, j % nj),
               out_shape=jax.ShapeDtypeStruct((N_CHIPS, kdim, ns), out_dtype), name=name)


def _row_tile(t):
    return _pick(t, (256, 128))


def _acc_rows(ref, part, i):
    @pl.when(i == 0)
    def _():
        ref[...] = part

    @pl.when(i > 0)
    def _():
        ref[...] += part


TOKEN_SHAPE = (8, 128)


def _token_operand(token):
    if token is None:
        return [], []
    return [token], [pl.BlockSpec(TOKEN_SHAPE, lambda i: (0, 0))]


def rms_fwd(x, g, out_dtype, token=None):
    t, d = x.shape
    tr = _row_tile(t)

    def body(x_ref, g_ref, *rest):
        o_ref = rest[-1]
        xf = x_ref[...]
        r = lax.rsqrt(jnp.mean(xf * xf, axis=-1, keepdims=True) + EPS)
        o_ref[...] = (xf * r * g_ref[...]).astype(o_ref.dtype)

    row = pl.BlockSpec((tr, d), lambda i: (i, 0))
    tok, tok_spec = _token_operand(token)
    return pl.pallas_call(body, grid=(t // tr,), in_specs=[row, pl.BlockSpec((1, d), lambda i: (0, 0))] + tok_spec,
                          out_specs=row, out_shape=jax.ShapeDtypeStruct((t, d), out_dtype), name="rms_fwd",
                          compiler_params=_params("parallel"))(x, g.reshape(1, d), *tok)


def rms_bwd(x, g, dh, dres, token=None):
    t, d = x.shape
    tr = _row_tile(t)

    def body(x_ref, g_ref, dh_ref, dres_ref, *rest):
        dx_ref, dx16_ref, dg_ref = rest[-3:]
        i = pl.program_id(0)
        xf = x_ref[...]
        r = lax.rsqrt(jnp.mean(xf * xf, axis=-1, keepdims=True) + EPS)
        xhat = xf * r
        dhf = dh_ref[...].astype(F32)
        dxh = dhf * g_ref[...]
        m = jnp.mean(dxh * xhat, axis=-1, keepdims=True)
        dx = dres_ref[...] + r * (dxh - xhat * m)
        dx_ref[...] = dx
        dx16_ref[...] = dx.astype(BF16)
        _acc_rows(dg_ref, jnp.sum(dhf * xhat, axis=0, keepdims=True), i)

    row = pl.BlockSpec((tr, d), lambda i: (i, 0))
    vec = pl.BlockSpec((1, d), lambda i: (0, 0))
    tok, tok_spec = _token_operand(token)
    return pl.pallas_call(body, grid=(t // tr,), in_specs=[row, vec, row, row] + tok_spec, out_specs=(row, row, vec),
                          out_shape=(jax.ShapeDtypeStruct((t, d), F32), jax.ShapeDtypeStruct((t, d), BF16),
                                     jax.ShapeDtypeStruct((1, d), F32)),
                          name="rms_bwd", compiler_params=_params("arbitrary"))(x, g.reshape(1, d), dh, dres, *tok)


def ln_silu_fwd(c, g, b):
    t, d = c.shape
    tr = _row_tile(t)

    def body(c_ref, g_ref, b_ref, o_ref):
        xf = c_ref[...]
        mu = jnp.mean(xf, axis=-1, keepdims=True)
        xc = xf - mu
        var = jnp.mean(xc * xc, axis=-1, keepdims=True)
        n = xc * lax.rsqrt(var + EPS) * g_ref[...] + b_ref[...]
        o_ref[...] = (n * _sigmoid(n)).astype(o_ref.dtype)

    row = pl.BlockSpec((tr, d), lambda i: (i, 0))
    vec = pl.BlockSpec((1, d), lambda i: (0, 0))
    return pl.pallas_call(body, grid=(t // tr,), in_specs=[row, vec, vec], out_specs=row,
                          out_shape=jax.ShapeDtypeStruct((t, d), BF16), name="ln_silu_fwd",
                          compiler_params=_params("parallel"))(c, g.reshape(1, d), b.reshape(1, d))


def ln_silu_bwd(c, g, b, ds):
    t, d = c.shape
    tr = _row_tile(t)

    def body(c_ref, g_ref, b_ref, ds_ref, dc_ref, dg_ref, db_ref):
        i = pl.program_id(0)
        xf = c_ref[...]
        mu = jnp.mean(xf, axis=-1, keepdims=True)
        xc = xf - mu
        var = jnp.mean(xc * xc, axis=-1, keepdims=True)
        rstd = lax.rsqrt(var + EPS)
        xhat = xc * rstd
        n = xhat * g_ref[...] + b_ref[...]
        sg = _sigmoid(n)
        dn = ds_ref[...].astype(F32) * (sg * (1.0 + n * (1.0 - sg)))
        dxh = dn * g_ref[...]
        m1 = jnp.mean(dxh, axis=-1, keepdims=True)
        m2 = jnp.mean(dxh * xhat, axis=-1, keepdims=True)
        dc_ref[...] = rstd * (dxh - m1 - xhat * m2)
        _acc_rows(dg_ref, jnp.sum(dn * xhat, axis=0, keepdims=True), i)
        _acc_rows(db_ref, jnp.sum(dn, axis=0, keepdims=True), i)

    row = pl.BlockSpec((tr, d), lambda i: (i, 0))
    vec = pl.BlockSpec((1, d), lambda i: (0, 0))
    vshape = jax.ShapeDtypeStruct((1, d), F32)
    return pl.pallas_call(body, grid=(t // tr,), in_specs=[row, vec, vec, row], out_specs=(row, vec, vec),
                          out_shape=(jax.ShapeDtypeStruct((t, d), F32), vshape, vshape), name="ln_silu_bwd",
                          compiler_params=_params("arbitrary"))(c, g.reshape(1, d), b.reshape(1, d), ds)


def loss_grad(y, tgt):
    t, d = y.shape
    tr = _row_tile(t)

    def body(y_ref, t_ref, dy_ref, dy16_ref, sq_ref):
        i = pl.program_id(0)
        err = y_ref[...] - t_ref[...]
        dy = err * (1.0 / d)
        dy_ref[...] = dy
        dy16_ref[...] = dy.astype(BF16)
        _acc_rows(sq_ref, jnp.sum(err * err, axis=0, keepdims=True), i)

    row = pl.BlockSpec((tr, d), lambda i: (i, 0))
    vec = pl.BlockSpec((1, d), lambda i: (0, 0))
    return pl.pallas_call(body, grid=(t // tr,), in_specs=[row, row], out_specs=(row, row, vec),
                          out_shape=(jax.ShapeDtypeStruct((t, d), F32), jax.ShapeDtypeStruct((t, d), BF16),
                                     jax.ShapeDtypeStruct((1, d), F32)),
                          name="loss_grad", compiler_params=_params("arbitrary"))(y, tgt)


def col_sum(a):
    t, d = a.shape
    tr = _row_tile(t)

    def body(a_ref, o_ref):
        _acc_rows(o_ref, jnp.sum(a_ref[...].astype(F32), axis=0, keepdims=True), pl.program_id(0))

    return pl.pallas_call(body, grid=(t // tr,), in_specs=[pl.BlockSpec((tr, d), lambda i: (i, 0))],
                          out_specs=pl.BlockSpec((1, d), lambda i: (0, 0)), out_shape=jax.ShapeDtypeStruct((1, d), F32),
                          name="col_sum", compiler_params=_params("arbitrary"))(a)


def scale_bwd(dx, ypre, scale):
    t, d = dx.shape
    tr = _row_tile(t)

    def body(dx_ref, y_ref, s_ref, dy_ref, ds_ref):
        dxf = dx_ref[...]
        dy_ref[...] = (dxf * s_ref[...]).astype(dy_ref.dtype)
        _acc_rows(ds_ref, jnp.sum(dxf * y_ref[...], axis=0, keepdims=True), pl.program_id(0))

    row = pl.BlockSpec((tr, d), lambda i: (i, 0))
    vec = pl.BlockSpec((1, d), lambda i: (0, 0))
    return pl.pallas_call(body, grid=(t // tr,), in_specs=[row, row, vec], out_specs=(row, vec),
                          out_shape=(jax.ShapeDtypeStruct((t, d), BF16), jax.ShapeDtypeStruct((1, d), F32)),
                          name="scale_bwd", compiler_params=_params("arbitrary"))(dx, ypre, scale.reshape(1, d))


def _chunk_rows(t):
    return _pick(t, (256, 128))


def _load_halo(ref, lead, base, rows, t, first, last, pre, post):
    idx = (lambda s, n: (pl.ds(s, n), slice(None))) if lead is None else (lambda s, n: (lead, pl.ds(s, n), slice(None)))
    parts = []
    if pre:
        start = pl.multiple_of(jnp.maximum(base - pre, 0), pre)
        parts.append(ref[idx(start, pre)].astype(F32) * jnp.where(first, 0.0, 1.0))
    parts.append(ref[idx(base, rows)].astype(F32))
    if post:
        start = pl.multiple_of(jnp.minimum(base + rows, t - post), post)
        parts.append(ref[idx(start, post)].astype(F32) * jnp.where(last, 0.0, 1.0))
    return parts[0] if len(parts) == 1 else jnp.concatenate(parts, axis=0)


def _fold8(x):
    r, c = x.shape
    return x.reshape(r // 8, 8, c).sum(axis=0)


def ffn_gate_fwd(u2, w3, b2):
    _, t, f = u2.shape
    tc = _pick(f, (256, 128))
    rows = _chunk_rows(t)
    nch = t // rows
    halo = 16

    def body(u_ref, w_ref, b_ref, a_ref):
        def conv(p, base, first):
            xs = _load_halo(u_ref, p, base, rows, t, first, False, halo, 0)
            wp = w_ref[p]
            return (wp[0:1] * pltpu.roll(xs, 2, 0)[halo:] + wp[1:2] * pltpu.roll(xs, 1, 0)[halo:]
                    + wp[2:3] * xs[halo:] + b_ref[p])

        def chunk(i, carry):
            base = pl.multiple_of(i * rows, rows)
            gate, val = conv(0, base, i == 0), conv(1, base, i == 0)
            a_ref[pl.ds(base, rows), :] = (gate * _sigmoid(gate) * val).astype(a_ref.dtype)
            return carry

        lax.fori_loop(0, nch, chunk, 0)

    return pl.pallas_call(
        body, grid=(f // tc,),
        in_specs=[pl.BlockSpec((2, t, tc), lambda j: (0, 0, j)), pl.BlockSpec((2, 3, tc), lambda j: (0, 0, j)),
                  pl.BlockSpec((2, 1, tc), lambda j: (0, 0, j))],
        out_specs=pl.BlockSpec((t, tc), lambda j: (0, j)), out_shape=jax.ShapeDtypeStruct((t, f), BF16),
        name="ffn_gate_fwd", compiler_params=_params("parallel"))(u2, w3, b2)


def ffn_gate_bwd(u2, da, w3, b2):
    _, t, f = u2.shape
    tc = _pick(f, (256, 128))
    rows = _chunk_rows(t)
    nch = t // rows
    halo = 16
    n = rows + 2 * halo

    def body(u_ref, da_ref, w_ref, b_ref, du_ref, dw_ref, db_ref, acc_ref):
        acc_ref[...] = jnp.zeros_like(acc_ref)

        def chunk(i, carry):
            base = pl.multiple_of(i * rows, rows)
            first, last = i == 0, i == nch - 1
            daf = jnp.concatenate(
                [jnp.zeros((halo, tc), F32), _load_halo(da_ref, None, base, rows, t, first, last, 0, halo)], axis=0)
            pre, shifted = [], []
            for p in range(2):
                xs = _load_halo(u_ref, p, base, rows, t, first, last, halo, halo)
                x1, x2 = pltpu.roll(xs, 1, 0), pltpu.roll(xs, 2, 0)
                wp = w_ref[p]
                pre.append(wp[0:1] * x2 + wp[1:2] * x1 + wp[2:3] * xs + b_ref[p])
                shifted.append((x2, x1, xs))
            gate, val = pre
            sg = _sigmoid(gate)
            d_pre = (daf * val * (sg * (1.0 + gate * (1.0 - sg))), daf * gate * sg)
            for p in range(2):
                dp = d_pre[p]
                wp = w_ref[p]
                du = wp[2:3] * dp + wp[1:2] * pltpu.roll(dp, n - 1, 0) + wp[0:1] * pltpu.roll(dp, n - 2, 0)
                du_ref[p, pl.ds(base, rows), :] = du[halo:halo + rows].astype(du_ref.dtype)
                own = dp[halo:halo + rows]
                for k in range(3):
                    acc_ref[p, k] += _fold8(own * shifted[p][k][halo:halo + rows])
                acc_ref[p, 3] += _fold8(own)
            return carry

        lax.fori_loop(0, nch, chunk, 0)
        for p in range(2):
            for k in range(3):
                dw_ref[p, k:k + 1, :] = jnp.sum(acc_ref[p, k], axis=0, keepdims=True)
            db_ref[p] = jnp.sum(acc_ref[p, 3], axis=0, keepdims=True)

    blk = pl.BlockSpec((2, t, tc), lambda j: (0, 0, j))
    wspec = pl.BlockSpec((2, 3, tc), lambda j: (0, 0, j))
    bspec = pl.BlockSpec((2, 1, tc), lambda j: (0, 0, j))
    return pl.pallas_call(
        body, grid=(f // tc,), in_specs=[blk, pl.BlockSpec((t, tc), lambda j: (0, j)), wspec, bspec],
        out_specs=(blk, wspec, bspec),
        out_shape=(jax.ShapeDtypeStruct((2, t, f), BF16), jax.ShapeDtypeStruct((2, 3, f), F32),
                   jax.ShapeDtypeStruct((2, 1, f), F32)),
        scratch_shapes=[pltpu.VMEM((2, 4, 8, tc), F32)], name="ffn_gate_bwd",
        compiler_params=_params("parallel"))(u2, da, w3, b2)


def glu_conv_fwd(u2, w, b):
    _, t, d = u2.shape
    taps = w.shape[0]
    tc = 128
    rows = _chunk_rows(t)
    nch = t // rows
    halo = 32

    def body(u_ref, w_ref, b_ref, c_ref):
        def chunk(i, carry):
            base = pl.multiple_of(i * rows, rows)
            a = _load_halo(u_ref, 0, base, rows, t, i == 0, False, halo, 0)
            g = _load_halo(u_ref, 1, base, rows, t, i == 0, False, halo, 0)
            xs = a * _sigmoid(g)
            acc = w_ref[taps - 1:taps, :] * xs[halo:] + b_ref[...]
            for j in range(taps - 1):
                acc = acc + w_ref[j:j + 1, :] * pltpu.roll(xs, taps - 1 - j, 0)[halo:]
            c_ref[pl.ds(base, rows), :] = acc
            return carry

        lax.fori_loop(0, nch, chunk, 0)

    return pl.pallas_call(
        body, grid=(d // tc,),
        in_specs=[pl.BlockSpec((2, t, tc), lambda j: (0, 0, j)), pl.BlockSpec((taps, tc), lambda j: (0, j)),
                  pl.BlockSpec((1, tc), lambda j: (0, j))],
        out_specs=pl.BlockSpec((t, tc), lambda j: (0, j)), out_shape=jax.ShapeDtypeStruct((t, d), F32),
        name="glu_conv_fwd", compiler_params=_params("parallel"))(u2, w, b)


def glu_conv_bwd(u2, dc, w):
    _, t, d = u2.shape
    taps = w.shape[0]
    tc = 128
    rows = _chunk_rows(t)
    nch = t // rows
    halo = 32
    n = rows + halo

    def body(u_ref, dc_ref, w_ref, du_ref, dw_ref, dwb_ref, dbin_ref, acc_ref, bacc_ref):
        acc_ref[...] = jnp.zeros_like(acc_ref)
        bacc_ref[...] = jnp.zeros_like(bacc_ref)

        def chunk(i, carry):
            base = pl.multiple_of(i * rows, rows)
            first, last = i == 0, i == nch - 1
            a = _load_halo(u_ref, 0, base, rows, t, first, False, halo, 0)
            g = _load_halo(u_ref, 1, base, rows, t, first, False, halo, 0)
            sg = _sigmoid(g)
            xs = a * sg
            dcs = _load_halo(dc_ref, None, base, rows, t, first, last, 0, halo)
            own = dcs[:rows]
            dglu = w_ref[taps - 1:taps, :] * own
            acc_ref[taps - 1] += _fold8(own * xs[halo:])
            for j in range(taps - 1):
                s = taps - 1 - j
                dglu = dglu + w_ref[j:j + 1, :] * pltpu.roll(dcs, n - s, 0)[:rows]
                acc_ref[j] += _fold8(own * pltpu.roll(xs, s, 0)[halo:])
            a_c, sg_c = a[halo:], sg[halo:]
            da = dglu * sg_c
            dg = dglu * a_c * sg_c * (1.0 - sg_c)
            du_ref[0, pl.ds(base, rows), :] = da.astype(du_ref.dtype)
            du_ref[1, pl.ds(base, rows), :] = dg.astype(du_ref.dtype)
            bacc_ref[0] += _fold8(own)
            bacc_ref[1] += _fold8(da)
            bacc_ref[2] += _fold8(dg)
            return carry

        lax.fori_loop(0, nch, chunk, 0)
        for j in range(taps):
            dw_ref[j:j + 1, :] = jnp.sum(acc_ref[j], axis=0, keepdims=True)
        dwb_ref[...] = jnp.sum(bacc_ref[0], axis=0, keepdims=True)
        dbin_ref[0] = jnp.sum(bacc_ref[1], axis=0, keepdims=True)
        dbin_ref[1] = jnp.sum(bacc_ref[2], axis=0, keepdims=True)

    blk = pl.BlockSpec((2, t, tc), lambda j: (0, 0, j))
    col = pl.BlockSpec((t, tc), lambda j: (0, j))
    return pl.pallas_call(
        body, grid=(d // tc,), in_specs=[blk, col, pl.BlockSpec((taps, tc), lambda j: (0, j))],
        out_specs=(blk, pl.BlockSpec((taps, tc), lambda j: (0, j)), pl.BlockSpec((1, tc), lambda j: (0, j)),
                   pl.BlockSpec((2, 1, tc), lambda j: (0, 0, j))),
        out_shape=(jax.ShapeDtypeStruct((2, t, d), BF16), jax.ShapeDtypeStruct((taps, d), F32),
                   jax.ShapeDtypeStruct((1, d), F32), jax.ShapeDtypeStruct((2, 1, d), F32)),
        scratch_shapes=[pltpu.VMEM((taps, 8, tc), F32), pltpu.VMEM((3, 8, tc), F32)], name="glu_conv_bwd",
        compiler_params=_params("parallel"))(u2, dc, w)


def _pool_select(grp, levels):
    out = levels[3]
    for k in (2, 1, 0):
        out = jnp.where(grp == k, levels[k], out)
    return out


def _pool_count(base, rows, tc, grp):
    tpos = (base + lax.broadcasted_iota(jnp.int32, (rows, tc), 0) + 1).astype(F32)
    window = jnp.left_shift(2, grp).astype(F32)
    return jnp.minimum(tpos, window)


def pool_fwd(h):
    t, d = h.shape
    pg = d // POOL_GROUPS
    tc = _pick(pg, (256, 128))
    rows = _chunk_rows(t)
    nch = t // rows
    halo = 16

    def body(h_ref, o_ref):
        grp = (pl.program_id(0) * tc) // pg

        def chunk(i, carry):
            base = pl.multiple_of(i * rows, rows)
            xs = _load_halo(h_ref, None, base, rows, t, i == 0, False, halo, 0)
            levels, cur = [], xs
            for k in range(4):
                cur = cur + pltpu.roll(cur, 1 << k, 0)
                levels.append(cur[halo:])
            pooled = _pool_select(grp, levels) / _pool_count(base, rows, tc, grp)
            o_ref[pl.ds(base, rows), :] = (pooled - xs[halo:]).astype(o_ref.dtype)
            return carry

        lax.fori_loop(0, nch, chunk, 0)

    col = pl.BlockSpec((t, tc), lambda j: (0, j))
    return pl.pallas_call(body, grid=(d // tc,), in_specs=[col], out_specs=col,
                          out_shape=jax.ShapeDtypeStruct((t, d), BF16), name="pool_fwd",
                          compiler_params=_params("parallel"))(h)


def pool_bwd(dmix):
    t, d = dmix.shape
    pg = d // POOL_GROUPS
    tc = _pick(pg, (256, 128))
    rows = _chunk_rows(t)
    nch = t // rows
    halo = 16
    n = rows + halo

    def body(d_ref, o_ref):
        grp = (pl.program_id(0) * tc) // pg

        def chunk(i, carry):
            base = pl.multiple_of(i * rows, rows)
            ds = _load_halo(d_ref, None, base, rows, t, i == 0, i == nch - 1, 0, halo)
            levels, cur = [], ds / _pool_count(base, n, tc, grp)
            for k in range(4):
                cur = cur + pltpu.roll(cur, n - (1 << k), 0)
                levels.append(cur[:rows])
            o_ref[pl.ds(base, rows), :] = _pool_select(grp, levels) - ds[:rows]
            return carry

        lax.fori_loop(0, nch, chunk, 0)

    col = pl.BlockSpec((t, tc), lambda j: (0, j))
    return pl.pallas_call(body, grid=(d // tc,), in_specs=[col], out_specs=col,
                          out_shape=jax.ShapeDtypeStruct((t, d), F32), name="pool_bwd",
                          compiler_params=_params("parallel"))(dmix)


def mm_groups(a, wg, *, mode, out_dtype, scale=None, resid=None, raw_dtype=None, name):
    t, d = a.shape
    pg = wg.shape[1]
    tm = _pick(t, (1024, 512, 256, 128))
    tn = pg
    if mode == "nn":
        dims, b_blk, b_idx = NN, (None, pg, tn), (lambda i, j, kk: (j, 0, 0))
    else:
        dims, b_blk, b_idx = NT, (None, tn, pg), (lambda i, j, kk: (j, 0, 0))
    raw_shape = None if raw_dtype is None else jax.ShapeDtypeStruct((t, d), raw_dtype)
    return _mm(a, wg, dims=dims, sizes=(t, d, pg), plan=(tm, tn, pg, False), a_blk=(tm, pg), a_idx=lambda i, j, kk: (i, j),
               b_blk=b_blk, b_idx=b_idx, o_blk=(tm, tn), o_idx=lambda i, j, kk: (i, j),
               out_shape=jax.ShapeDtypeStruct((t, d), out_dtype), name=name,
               scale=None if scale is None else scale.reshape(1, d), vec_blk=(1, tn), vec_idx=lambda i, j, kk: (0, j),
               resid=resid, raw_shape=raw_shape)


def mm_groups_wgrad(at, dy, groups, *, out_dtype, name):
    d, t = at.shape
    pg = d // groups
    tk = _pick(t, (2048, 1024, 512, 256, 128))
    return _mm(at, dy, dims=NN, sizes=(d, pg, t), plan=(pg, pg, tk, False), a_blk=(pg, tk), a_idx=lambda i, j, kk: (i, kk),
               b_blk=(tk, pg), b_idx=lambda i, j, kk: (kk, i), o_blk=(None, pg, pg), o_idx=lambda i, j, kk: (i, 0, 0),
               out_shape=jax.ShapeDtypeStruct((groups, pg, pg), out_dtype), name=name)


def _split_dot(y, p):
    hi = y.astype(BF16)
    r1 = y - hi.astype(F32)
    mid = r1.astype(BF16)
    lo = (r1 - mid.astype(F32)).astype(BF16)
    pb = p.astype(BF16)
    dot = lambda v: jnp.dot(v, pb, preferred_element_type=F32)
    return (dot(hi) + dot(mid)) + dot(lo)


def rope_tables(positions):
    half = ROT_DIM // 2
    inv_freq = ROPE_THETA ** (-jnp.arange(0, ROT_DIM, 2, dtype=F32) / ROT_DIM)
    ang = positions.astype(F32)[:, None] * inv_freq
    t = positions.shape[0]
    cos, sin = jnp.cos(ang), jnp.sin(ang)
    rest = HEAD_DIM - ROT_DIM
    cosf = jnp.concatenate([cos, cos, jnp.ones((t, rest), F32)], axis=1)
    sinf = jnp.concatenate([-sin, sin, jnp.zeros((t, rest), F32)], axis=1)
    idx = jnp.arange(HEAD_DIM)
    partner = jnp.where(idx < half, idx + half, jnp.where(idx < ROT_DIM, idx - half, idx))
    pmat = (idx[:, None] == partner[None, :]).astype(F32)
    return cosf, sinf, pmat


def qk_rope_fwd(x, g, cosf, sinf, pmat, out_scale):
    hn, t, hd = x.shape
    tq = _pick(t, (512, 256, 128))

    def body(x_ref, g_ref, c_ref, s_ref, p_ref, o_ref):
        xf = x_ref[...]
        r = lax.rsqrt(jnp.mean(xf * xf, axis=-1, keepdims=True) + EPS)
        y = xf * r * g_ref[...]
        rot = y * c_ref[...] + _split_dot(y, p_ref[...]) * s_ref[...]
        o_ref[...] = (rot * out_scale).astype(o_ref.dtype)

    blk = pl.BlockSpec((None, tq, hd), lambda h, i: (h, i, 0))
    tab = pl.BlockSpec((tq, hd), lambda h, i: (i, 0))
    return pl.pallas_call(
        body, grid=(hn, t // tq),
        in_specs=[blk, pl.BlockSpec((1, hd), lambda h, i: (0, 0)), tab, tab, pl.BlockSpec((hd, hd), lambda h, i: (0, 0))],
        out_specs=blk, out_shape=jax.ShapeDtypeStruct((hn, t, hd), BF16), name="qk_rope_fwd",
        compiler_params=_params("parallel", "parallel"))(x, g.reshape(1, hd), cosf, sinf, pmat)


def qk_rope_bwd(dy, x, g, cosf, sinf, pmat_t, in_scale):
    hn, t, hd = x.shape
    tq = _pick(t, (512, 256, 128))

    def body(dy_ref, x_ref, g_ref, c_ref, s_ref, p_ref, dx_ref, dg_ref):
        step = pl.program_id(0) * pl.num_programs(1) + pl.program_id(1)
        dr = dy_ref[...] * in_scale
        dyn = dr * c_ref[...] + _split_dot(dr * s_ref[...], p_ref[...])
        xf = x_ref[...]
        r = lax.rsqrt(jnp.mean(xf * xf, axis=-1, keepdims=True) + EPS)
        xhat = xf * r
        dxh = dyn * g_ref[...]
        m = jnp.mean(dxh * xhat, axis=-1, keepdims=True)
        dx_ref[...] = r * (dxh - xhat * m)
        _acc_rows(dg_ref, jnp.sum(dyn * xhat, axis=0, keepdims=True), step)

    blk = pl.BlockSpec((None, tq, hd), lambda h, i: (h, i, 0))
    tab = pl.BlockSpec((tq, hd), lambda h, i: (i, 0))
    vec = pl.BlockSpec((1, hd), lambda h, i: (0, 0))
    return pl.pallas_call(
        body, grid=(hn, t // tq),
        in_specs=[blk, blk, vec, tab, tab, pl.BlockSpec((hd, hd), lambda h, i: (0, 0))],
        out_specs=(blk, vec), out_shape=(jax.ShapeDtypeStruct((hn, t, hd), F32), jax.ShapeDtypeStruct((1, hd), F32)),
        name="qk_rope_bwd", compiler_params=_params("arbitrary", "arbitrary"))(dy, x, g.reshape(1, hd), cosf, sinf, pmat_t)


NEG_BIG = -1e30


def _att_masks(i):
    qi = lax.broadcasted_iota(jnp.int32, (ATT_BLOCK, ATT_BLOCK), 0)
    kj = lax.broadcasted_iota(jnp.int32, (ATT_BLOCK, ATT_BLOCK), 1)
    return kj <= qi, jnp.logical_and(kj > qi, i > 0)


def _att_probs(q, kc, kp, mask_c, mask_p, sink):
    s_c = jnp.where(mask_c, lax.dot_general(q, kc, NT, preferred_element_type=F32), NEG_BIG)
    s_p = jnp.where(mask_p, lax.dot_general(q, kp, NT, preferred_element_type=F32), NEG_BIG)
    m = jnp.maximum(jnp.maximum(jnp.max(s_c, axis=-1, keepdims=True), jnp.max(s_p, axis=-1, keepdims=True)), sink)
    p_c, p_p = jnp.exp(s_c - m), jnp.exp(s_p - m)
    p_s = jnp.exp(sink - m)
    denom = jnp.sum(p_c, axis=-1, keepdims=True) + jnp.sum(p_p, axis=-1, keepdims=True) + p_s
    return p_c, p_p, p_s, denom


def _att_specs(t):
    nb = t // ATT_BLOCK
    qblk = pl.BlockSpec((KV_GROUP, ATT_BLOCK, HEAD_DIM), lambda kv, i: (kv, i, 0))
    cur = pl.BlockSpec((None, ATT_BLOCK, HEAD_DIM), lambda kv, i: (kv, i, 0))
    prev = pl.BlockSpec((None, ATT_BLOCK, HEAD_DIM), lambda kv, i: (kv, jnp.maximum(i - 1, 0), 0))
    return nb, qblk, cur, prev, pl.BlockSpec(memory_space=pltpu.SMEM)


def attn_fwd(q, k, v, sinks):
    h, t, hd = q.shape
    nb, qblk, cur, prev, smem = _att_specs(t)

    def body(q_ref, kc_ref, kp_ref, vc_ref, vp_ref, sink_ref, o_ref):
        kv, i = pl.program_id(0), pl.program_id(1)
        mask_c, mask_p = _att_masks(i)
        kc, kp, vc, vp = kc_ref[...], kp_ref[...], vc_ref[...], vp_ref[...]
        for g in range(KV_GROUP):
            p_c, p_p, _, denom = _att_probs(q_ref[g], kc, kp, mask_c, mask_p, sink_ref[kv * KV_GROUP + g])
            o = (jnp.dot(p_c.astype(BF16), vc, preferred_element_type=F32)
                 + jnp.dot(p_p.astype(BF16), vp, preferred_element_type=F32)) / denom
            o_ref[g] = o.astype(o_ref.dtype)

    return pl.pallas_call(
        body, grid=(h // KV_GROUP, nb), in_specs=[qblk, cur, prev, cur, prev, smem], out_specs=qblk,
        out_shape=jax.ShapeDtypeStruct((h, t, hd), BF16), name="attn_fwd",
        compiler_params=_params("parallel", "parallel"))(q, k, k, v, v, sinks)


def attn_bwd(q, k, v, do, sinks):
    h, t, hd = q.shape
    kvh = h // KV_GROUP
    nb, qblk, cur, prev, smem = _att_specs(t)

    def body(q_ref, kc_ref, kp_ref, vc_ref, vp_ref, do_ref, sink_ref, dq_ref, dk_ref, dv_ref, dsk_ref):
        kv, i = pl.program_id(0), pl.program_id(1)

        @pl.when(i == 0)
        def _():
            dk_ref[...] = jnp.zeros_like(dk_ref)
            dv_ref[...] = jnp.zeros_like(dv_ref)
            dsk_ref[...] = jnp.zeros_like(dsk_ref)

        mask_c, mask_p = _att_masks(i)
        kc, kp, vc, vp = kc_ref[...], kp_ref[...], vc_ref[...], vp_ref[...]
        zero = jnp.zeros((ATT_BLOCK, hd), F32)
        dkc, dkp, dvc, dvp = zero, zero, zero, zero
        dsink_rows = []
        for g in range(KV_GROUP):
            q = q_ref[g]
            p_c, p_p, p_s, denom = _att_probs(q, kc, kp, mask_c, mask_p, sink_ref[kv * KV_GROUP + g])
            inv = 1.0 / denom
            pn_c, pn_p = p_c * inv, p_p * inv
            dob = do_ref[g].astype(BF16)
            dp_c = lax.dot_general(dob, vc, NT, preferred_element_type=F32)
            dp_p = lax.dot_general(dob, vp, NT, preferred_element_type=F32)
            dsum = jnp.sum(pn_c * dp_c, axis=-1, keepdims=True) + jnp.sum(pn_p * dp_p, axis=-1, keepdims=True)
            ds_c = (pn_c * (dp_c - dsum)).astype(BF16)
            ds_p = (pn_p * (dp_p - dsum)).astype(BF16)
            dq_ref[g] = (jnp.dot(ds_c, kc, preferred_element_type=F32) + jnp.dot(ds_p, kp, preferred_element_type=F32))
            dkc = dkc + lax.dot_general(ds_c, q, TN, preferred_element_type=F32)
            dkp = dkp + lax.dot_general(ds_p, q, TN, preferred_element_type=F32)
            dvc = dvc + lax.dot_general(pn_c.astype(BF16), dob, TN, preferred_element_type=F32)
            dvp = dvp + lax.dot_general(pn_p.astype(BF16), dob, TN, preferred_element_type=F32)
            dsink = -jnp.sum(p_s * inv * dsum, axis=0, keepdims=True)
            dsink_rows.append(jnp.broadcast_to(dsink, (1, 128)))
        here = pl.ds(pl.multiple_of(i * ATT_BLOCK, ATT_BLOCK), ATT_BLOCK)
        before = pl.ds(pl.multiple_of(jnp.maximum(i - 1, 0) * ATT_BLOCK, ATT_BLOCK), ATT_BLOCK)
        dk_ref[here, :] += dkc
        dv_ref[here, :] += dvc
        dk_ref[before, :] += dkp
        dv_ref[before, :] += dvp
        dsk_ref[...] += jnp.concatenate(dsink_rows, axis=0)

    whole = pl.BlockSpec((None, t, hd), lambda kv, i: (kv, 0, 0))
    return pl.pallas_call(
        body, grid=(kvh, nb), in_specs=[qblk, cur, prev, cur, prev, qblk, smem],
        out_specs=(qblk, whole, whole, pl.BlockSpec((None, KV_GROUP, 128), lambda kv, i: (kv, 0, 0))),
        out_shape=(jax.ShapeDtypeStruct((h, t, hd), F32), jax.ShapeDtypeStruct((kvh, t, hd), F32),
                   jax.ShapeDtypeStruct((kvh, t, hd), F32), jax.ShapeDtypeStruct((kvh, KV_GROUP, 128), F32)),
        name="attn_bwd", compiler_params=_params("parallel", "arbitrary"))(q, k, k, v, v, do, sinks)


def _ffn_taps(w, b):
    f2 = w.shape[1]
    return w.reshape(FFN_TAPS, 2, f2 // 2).transpose(1, 0, 2), b.reshape(2, 1, f2 // 2)


def ffn_fwd(x, W, p, tables=None, token=None):
    h = rms_fwd(x, W[p + "ffn_norm_g"], BF16, token)
    u2 = mm_nn_cols(h, W[p + "ffn_w_up"], split=True, out_dtype=BF16, name="ffn_up")
    w3, b2 = _ffn_taps(W[p + "ffn_dw_w"], W[p + "ffn_dw_b"])
    a = ffn_gate_fwd(u2, w3, b2)
    y = mm_nn(a, W[p + "ffn_w_down"], out_dtype=F32, resid=x, name="ffn_down")
    return y, (x, h, u2, a)


def ffn_bwd(saved, W, p, dy, tables=None):
    x, h, u2, a = saved
    dyf, dyb = dy
    w3, b2 = _ffn_taps(W[p + "ffn_dw_w"], W[p + "ffn_dw_b"])
    grads = {p + "ffn_w_down": mm_wgrad(a.T, dyb, out_dtype=BF16, name="ffn_down_dw")}
    da = mm_nt(dyb, W[p + "ffn_w_down"], out_dtype=BF16, name="ffn_down_dx")
    du2, dw3, db2 = ffn_gate_bwd(u2, da, w3, b2)
    grads[p + "ffn_dw_w"] = dw3.transpose(1, 0, 2).reshape(FFN_TAPS, -1)
    grads[p + "ffn_dw_b"] = db2.reshape(-1)
    grads[p + "ffn_w_up"] = mm_wgrad_cols(h.T, du2, split=True, out_dtype=BF16, name="ffn_up_dw")
    dh = mm_nt_cols(du2, W[p + "ffn_w_up"], split=True, out_dtype=F32, name="ffn_up_dx")
    return (x, p + "ffn_norm_g", dh), grads


def conf_fwd(x, W, p, tables=None, token=None):
    d = x.shape[1]
    h = rms_fwd(x, W[p + "norm_g"], BF16, token)
    u2 = mm_nn_cols(h, W[p + "a_w_in"], split=True, out_dtype=BF16, bias=W[p + "a_b_in"], name="conf_in")
    c = glu_conv_fwd(u2, W[p + "a_dw_w"], W[p + "a_dw_b"].reshape(1, d))
    s = ln_silu_fwd(c, W[p + "a_ln_g"], W[p + "a_ln_b"])
    y = mm_nn(s, W[p + "a_w_out"], out_dtype=F32, bias=W[p + "a_b_out"], resid=x, name="conf_out")
    return y, (x, h, u2, c, s)


def conf_bwd(saved, W, p, dy, tables=None):
    x, h, u2, c, s = saved
    dyf, dyb = dy
    grads = {p + "a_w_out": mm_wgrad(s.T, dyb, out_dtype=BF16, name="conf_out_dw"), p + "a_b_out": col_sum(dyf).reshape(-1)}
    ds = mm_nt(dyb, W[p + "a_w_out"], out_dtype=BF16, name="conf_out_dx")
    dc, dlg, dlb = ln_silu_bwd(c, W[p + "a_ln_g"], W[p + "a_ln_b"], ds)
    grads[p + "a_ln_g"], grads[p + "a_ln_b"] = dlg.reshape(-1), dlb.reshape(-1)
    du2, ddw, ddwb, dbin = glu_conv_bwd(u2, dc, W[p + "a_dw_w"])
    grads[p + "a_dw_w"], grads[p + "a_dw_b"], grads[p + "a_b_in"] = ddw, ddwb.reshape(-1), dbin.reshape(-1)
    grads[p + "a_w_in"] = mm_wgrad_cols(h.T, du2, split=True, out_dtype=BF16, name="conf_in_dw")
    dh = mm_nt_cols(du2, W[p + "a_w_in"], split=True, out_dtype=F32, name="conf_in_dx")
    return (x, p + "norm_g", dh), grads


def pool_layer_fwd(x, W, p, tables=None, token=None):
    h = rms_fwd(x, W[p + "norm_g"], F32, token)
    mixed = pool_fwd(h)
    y, ypre = mm_groups(mixed, W[p + "b_w_group"], mode="nn", out_dtype=F32, scale=W[p + "b_scale"], resid=x,
                        raw_dtype=F32, name="pool_mix")
    return y, (x, mixed, ypre)


def pool_layer_bwd(saved, W, p, dy, tables=None):
    x, mixed, ypre = saved
    dyf, dyb = dy
    dyp, dscale = scale_bwd(dyf, ypre, W[p + "b_scale"])
    grads = {p + "b_scale": dscale.reshape(-1),
             p + "b_w_group": mm_groups_wgrad(mixed.T, dyp, POOL_GROUPS, out_dtype=BF16, name="pool_mix_dw")}
    dmix = mm_groups(dyp, W[p + "b_w_group"], mode="nt", out_dtype=F32, name="pool_mix_dx")
    dh = pool_bwd(dmix)
    return (x, p + "norm_g", dh), grads


def _heads(a, n):
    t = a.shape[0]
    return a.reshape(t, n, HEAD_DIM).transpose(1, 0, 2)


def _unheads(a):
    n, t, _ = a.shape
    return a.transpose(1, 0, 2).reshape(t, n * HEAD_DIM)


def attn_layer_fwd(x, W, p, tables, token=None):
    d = x.shape[1]
    nh = d // HEAD_DIM
    nkv = nh // KV_GROUP
    cosf, sinf, pmat = tables
    h = rms_fwd(x, W[p + "norm_g"], BF16, token)
    qkv = mm_nn_cols(h, W[p + "c_w_qkv"], split=False, out_dtype=F32, name="att_qkv")
    q = _heads(qkv[:, :d], nh)
    k = _heads(qkv[:, d:d + nkv * HEAD_DIM], nkv)
    v = _heads(qkv[:, d + nkv * HEAD_DIM:], nkv).astype(BF16)
    qr = qk_rope_fwd(q, W[p + "c_q_norm_g"], cosf, sinf, pmat, HEAD_DIM ** -0.5)
    kr = qk_rope_fwd(k, W[p + "c_k_norm_g"], cosf, sinf, pmat, 1.0)
    o = attn_fwd(qr, kr, v, W[p + "c_sinks"])
    o2 = _unheads(o)
    y = mm_nn(o2, W[p + "c_w_o"], out_dtype=F32, resid=x, name="att_out")
    return y, (x, h, q, k, v, qr, kr, o2)


def attn_layer_bwd(saved, W, p, dy, tables):
    x, h, q, k, v, qr, kr, o2 = saved
    dyf, dyb = dy
    cosf, sinf, pmat = tables
    nh = q.shape[0]
    grads = {p + "c_w_o": mm_wgrad(o2.T, dyb, out_dtype=BF16, name="att_out_dw")}
    do = _heads(mm_nt(dyb, W[p + "c_w_o"], out_dtype=BF16, name="att_out_dx"), nh)
    dqr, dkr, dv, dsk = attn_bwd(qr, kr, v, do, W[p + "c_sinks"])
    grads[p + "c_sinks"] = dsk[:, :, 0].reshape(-1)
    dq, dqg = qk_rope_bwd(dqr, q, W[p + "c_q_norm_g"], cosf, sinf, pmat.T, HEAD_DIM ** -0.5)
    dk, dkg = qk_rope_bwd(dkr, k, W[p + "c_k_norm_g"], cosf, sinf, pmat.T, 1.0)
    grads[p + "c_q_norm_g"], grads[p + "c_k_norm_g"] = dqg.reshape(-1), dkg.reshape(-1)
    dqkv = jnp.concatenate([_unheads(dq), _unheads(dk), _unheads(dv)], axis=1).astype(BF16)
    grads[p + "c_w_qkv"] = mm_wgrad_cols(h.T, dqkv, split=False, out_dtype=BF16, name="att_qkv_dw")
    dh = mm_nt_cols(dqkv, W[p + "c_w_qkv"], split=False, out_dtype=F32, name="att_qkv_dx")
    return (x, p + "norm_g", dh), grads


def local_step(x, positions, tgt, W, comm=None):
    tables = rope_tables(positions)
    saved = []
    for g, (fwd, _, p) in enumerate(SUBLAYERS):
        token = comm.forward_begins(g, W) if comm else None
        x, s = fwd(x, W, p, tables, token)
        saved.append(s)
        if comm:
            comm.forward_ends(g, x, W)
    dyf, dyb, sq = loss_grad(x, tgt)
    loss = 0.5 * jnp.sum(sq) / x.shape[1]
    grads = {}
    for g in reversed(range(len(SUBLAYERS))):
        _, bwd, p = SUBLAYERS[g]
        (xin, gain, dh), gr = bwd(saved[g], W, p, (dyf, dyb), tables)
        token = comm.gradients_ready(g, gr) if comm else None
        dyf, dyb, dg = rms_bwd(xin, W[gain], dh, dyf, token)
        gr[gain] = dg.reshape(-1)
        grads.update(gr)
    return loss, dyf, grads


SUBLAYERS = [(conf_fwd, conf_bwd, "l0_"), (ffn_fwd, ffn_bwd, "l0_"), (pool_layer_fwd, pool_layer_bwd, "l1_"),
             (ffn_fwd, ffn_bwd, "l1_"), (attn_layer_fwd, attn_layer_bwd, "l2_"), (ffn_fwd, ffn_bwd, "l2_"),
             (conf_fwd, conf_bwd, "l3_"), (ffn_fwd, ffn_bwd, "l3_")]
SUBLAYER_WEIGHTS = {conf_fwd: ("a_w_in", "a_w_out", "a_dw_w"), ffn_fwd: ("ffn_w_up", "ffn_w_down", "ffn_dw_w"),
                    pool_layer_fwd: ("b_w_group",), attn_layer_fwd: ("c_w_qkv", "c_w_o")}


def sublayer_weight_names(g):
    fwd, _, p = SUBLAYERS[g]
    return [p + n for n in SUBLAYER_WEIGHTS[fwd]]


ANY = pl.BlockSpec(memory_space=pl.ANY)


def _place():
    x, y, c = lax.axis_index("x"), lax.axis_index("y"), lax.axis_index("c")
    chips = [(1 - x, y), (x, 1 - y), (1 - x, 1 - y)]
    return x, y, c, 2 * x + y, (x, y, 1 - c), chips


def _half(rows, which):
    return pl.ds(which * (rows // 2), rows // 2)


def place_block(shard, chip_core, out_dtype):
    rows, cols = shard.shape
    tr = rows
    for cand in (512, 256, 128, 64, 32, 16):
        if rows % cand == 0 and cand * cols * 4 <= (2 << 20):
            tr = cand
            break

    def body(pos_ref, s_ref, o_ref):
        o_ref[...] = s_ref[...].astype(o_ref.dtype)

    grid_spec = pltpu.PrefetchScalarGridSpec(
        num_scalar_prefetch=1, grid=(rows // tr,), in_specs=[pl.BlockSpec((tr, cols), lambda i, pos: (i, 0))],
        out_specs=pl.BlockSpec((None, tr, cols), lambda i, pos: (pos[0], i, 0)))
    return pl.pallas_call(body, grid_spec=grid_spec, out_shape=jax.ShapeDtypeStruct((N_CHIPS, rows, cols), out_dtype),
                          name="place_block", compiler_params=_params("parallel"))(chip_core, shard)


def all_gather_chips(bufs):
    n = len(bufs)

    def body(*refs):
        outs = refs[n:2 * n]
        ici_send, ici_recv, d2d_send, d2d_recv = refs[2 * n:]
        x, y, c, k, sibling, chips = _place()

        def rdma(src, dst, send, recv, dev):
            return pltpu.make_async_remote_copy(src_ref=src, dst_ref=dst, send_sem=send, recv_sem=recv,
                                                device_id=dev, device_id_type=MESH)

        sends = []
        for t in range(n):
            rows = bufs[t].shape[1]
            for j, (px, py) in enumerate(chips):
                mine = outs[t].at[k, _half(rows, c)]
                sends.append(rdma(mine, mine, ici_send.at[t, j], ici_recv.at[t, j], (px, py, c)))
        for cp in sends:
            cp.start()
        for t in range(n):
            rows = bufs[t].shape[1]
            for j, (px, py) in enumerate(chips):
                landed = outs[t].at[2 * px + py, _half(rows, c)]
                rdma(landed, landed, ici_send.at[t, j], ici_recv.at[t, j], sibling).wait_recv()
                fwd = rdma(landed, landed, d2d_send.at[t, j], d2d_recv.at[t, j], sibling)
                fwd.start()
                sends.append(fwd)
        for t in range(n):
            rows = bufs[t].shape[1]
            for j, (px, py) in enumerate(chips):
                other = outs[t].at[2 * px + py, _half(rows, 1 - c)]
                rdma(other, other, d2d_send.at[t, j], d2d_recv.at[t, j], sibling).wait_recv()
        for cp in sends:
            cp.wait_send()

    return pl.pallas_call(
        body, in_specs=[ANY] * n, out_specs=[ANY] * n,
        out_shape=[jax.ShapeDtypeStruct(b.shape, b.dtype) for b in bufs],
        input_output_aliases={t: t for t in range(n)},
        scratch_shapes=[pltpu.SemaphoreType.DMA((n, 3))] * 4,
        name="all_gather_chips", compiler_params=pltpu.CompilerParams())(*bufs)


def exchange_sibling_halves(gs):
    n = len(gs)

    def body(*refs):
        ins, outs = refs[:n], refs[n:2 * n]
        send, recv = refs[2 * n:]
        x, y, c, k, sibling, chips = _place()
        cps = []
        for t in range(n):
            rows = gs[t].shape[1]
            cps.append(pltpu.make_async_remote_copy(
                src_ref=ins[t].at[:, _half(rows, 1 - c), :], dst_ref=outs[t], send_sem=send.at[t], recv_sem=recv.at[t],
                device_id=sibling, device_id_type=MESH))
        for cp in cps:
            cp.start()
        for cp in cps:
            cp.wait()

    return pl.pallas_call(
        body, in_specs=[ANY] * n, out_specs=[ANY] * n,
        out_shape=[jax.ShapeDtypeStruct((g.shape[0], g.shape[1] // 2, g.shape[2]), g.dtype) for g in gs],
        scratch_shapes=[pltpu.SemaphoreType.DMA((n,))] * 2, name="exchange_sibling_halves",
        compiler_params=pltpu.CompilerParams())(*gs)


def exchange_chip_blocks(ps):
    n = len(ps)

    def body(*refs):
        ins, outs = refs[:n], refs[n:2 * n]
        send, recv = refs[2 * n:]
        x, y, c, k, sibling, chips = _place()
        cps = []
        for t in range(n):
            for j, (px, py) in enumerate(chips):
                cps.append(pltpu.make_async_remote_copy(
                    src_ref=ins[t].at[2 * px + py], dst_ref=outs[t].at[j], send_sem=send.at[t, j], recv_sem=recv.at[t, j],
                    device_id=(px, py, c), device_id_type=MESH))
        for cp in cps:
            cp.start()
        for cp in cps:
            cp.wait()

    return pl.pallas_call(
        body, in_specs=[ANY] * n, out_specs=[ANY] * n,
        out_shape=[jax.ShapeDtypeStruct((3,) + p.shape[1:], p.dtype) for p in ps],
        scratch_shapes=[pltpu.SemaphoreType.DMA((n, 3))] * 2,
        name="exchange_chip_blocks", compiler_params=pltpu.CompilerParams())(*ps)


def gather_sibling_halves(ss):
    n = len(ss)

    def body(*refs):
        outs = refs[n:2 * n]
        send, recv = refs[2 * n:]
        x, y, c, k, sibling, chips = _place()
        cps = []
        for t in range(n):
            rows = ss[t].shape[0]
            mine = outs[t].at[_half(rows, c)]
            cps.append(pltpu.make_async_remote_copy(src_ref=mine, dst_ref=mine, send_sem=send.at[t], recv_sem=recv.at[t],
                                                    device_id=sibling, device_id_type=MESH))
        for cp in cps:
            cp.start()
        for t in range(n):
            rows = ss[t].shape[0]
            got = outs[t].at[_half(rows, 1 - c)]
            pltpu.make_async_remote_copy(src_ref=got, dst_ref=got, send_sem=send.at[t], recv_sem=recv.at[t],
                                         device_id=sibling, device_id_type=MESH).wait_recv()
        for cp in cps:
            cp.wait_send()

    return pl.pallas_call(
        body, in_specs=[ANY] * n, out_specs=[ANY] * n,
        out_shape=[jax.ShapeDtypeStruct(s.shape, s.dtype) for s in ss],
        input_output_aliases={t: t for t in range(n)},
        scratch_shapes=[pltpu.SemaphoreType.DMA((n,))] * 2, name="gather_sibling_halves",
        compiler_params=pltpu.CompilerParams())(*ss)


def _sum_rows_tile(rows):
    return _pick(rows, (128, 64, 32, 16))


def add_sibling_half(g, land, core):
    nb, half, cols = land.shape
    tr = _sum_rows_tile(half)
    nrb = half // tr

    def body(c_ref, g_ref, l_ref, o_ref):
        o_ref[...] = (g_ref[...].astype(F32) + l_ref[...].astype(F32)).astype(o_ref.dtype)

    spec = pl.BlockSpec((None, tr, cols), lambda b, i, c_ref: (b, i, 0))
    grid_spec = pltpu.PrefetchScalarGridSpec(
        num_scalar_prefetch=1, grid=(nb, nrb),
        in_specs=[pl.BlockSpec((None, tr, cols), lambda b, i, c_ref: (b, c_ref[1] * nrb + i, 0)), spec], out_specs=spec)
    return pl.pallas_call(body, grid_spec=grid_spec, out_shape=jax.ShapeDtypeStruct(land.shape, BF16),
                          name="add_sibling_half", compiler_params=_params("parallel", "parallel"))(core, g, land)


def sum_chip_blocks(p, l2, chip_core):
    nb, half, cols = l2.shape
    tr = _sum_rows_tile(half)
    nrb = half // tr

    def body(pos_ref, p_ref, l_ref, o_ref):
        acc = p_ref[...].astype(F32)
        for b in range(nb):
            acc = acc + l_ref[b].astype(F32)
        o_ref[...] = acc

    grid_spec = pltpu.PrefetchScalarGridSpec(
        num_scalar_prefetch=1, grid=(nrb,),
        in_specs=[pl.BlockSpec((None, tr, cols), lambda i, pos: (pos[0], i, 0)),
                  pl.BlockSpec((nb, tr, cols), lambda i, pos: (0, i, 0))],
        out_specs=pl.BlockSpec((tr, cols), lambda i, pos: (pos[1] * nrb + i, 0)))
    return pl.pallas_call(body, grid_spec=grid_spec, out_shape=jax.ShapeDtypeStruct((2 * half, cols), F32),
                          name="sum_chip_blocks", compiler_params=_params("parallel"))(chip_core, p, l2)


def reduce_scatter(gs, chip_core):
    lands = exchange_sibling_halves(gs)
    ps = [add_sibling_half(g, l, chip_core) for g, l in zip(gs, lands)]
    l2s = exchange_chip_blocks(ps)
    ss = [sum_chip_blocks(p, l2, chip_core) for p, l2 in zip(ps, l2s)]
    return gather_sibling_halves(ss)


HBM_SPEC = pl.BlockSpec(memory_space=pltpu.HBM)
SEM_SPEC = pl.BlockSpec(memory_space=pltpu.SEMAPHORE)
SPLIT_EFFECT = pltpu.SideEffectType.DATAFLOW_SIDE_EFFECTING


def _in_hbm(v):
    return pltpu.with_memory_space_constraint(v, pltpu.HBM)


def _gather_ici_copies(bufs, refs, send, recv):
    x, y, c, k, sibling, chips = _place()
    cps = []
    for t in range(len(bufs)):
        rows = bufs[t].shape[1]
        for j, (px, py) in enumerate(chips):
            cps.append(pltpu.make_async_remote_copy(
                src_ref=refs[t].at[k, _half(rows, c)], dst_ref=refs[t].at[k, _half(rows, c)],
                send_sem=send.at[3 * t + j], recv_sem=recv.at[3 * t + j], device_id=(px, py, c), device_id_type=MESH))
    return cps


def gather_ici_start(bufs, after, name):
    n = len(bufs)

    def body(*refs):
        send, recv, token = refs[n + 1], refs[n + 2], refs[-1]
        for cp in _gather_ici_copies(bufs, refs[:n], send, recv):
            cp.start()
        token[...] = jnp.zeros_like(token)

    outs = pl.pallas_call(
        body, name=name, in_specs=[HBM_SPEC] * n + [ANY],
        out_shape=(pltpu.SemaphoreType.DMA((3 * n,)), pltpu.SemaphoreType.DMA((3 * n,)),
                   *[pltpu.HBM(b.shape, b.dtype) for b in bufs], jax.ShapeDtypeStruct(TOKEN_SHAPE, F32)),
        out_specs=(SEM_SPEC, SEM_SPEC, *[HBM_SPEC] * n, pl.BlockSpec(memory_space=pltpu.VMEM)),
        input_output_aliases={t: 2 + t for t in range(n)},
        compiler_params=pltpu.CompilerParams(has_side_effects=SPLIT_EFFECT))(*[_in_hbm(b) for b in bufs], after)
    return outs[0], outs[1], list(outs[2:2 + n]), outs[-1]


def gather_ici_wait(send, recv, bufs, after, name):
    n = len(bufs)

    def body(*refs):
        x, y, c, k, sibling, chips = _place()
        for t in range(n):
            rows = bufs[t].shape[1]
            for j, (px, py) in enumerate(chips):
                cp = pltpu.make_async_remote_copy(
                    src_ref=refs[t].at[k, _half(rows, c)], dst_ref=refs[t].at[2 * px + py, _half(rows, c)],
                    send_sem=refs[n].at[3 * t + j], recv_sem=refs[n + 1].at[3 * t + j], device_id=(px, py, c),
                    device_id_type=MESH)
                cp.wait_send()
                cp.wait_recv()

    return list(pl.pallas_call(
        body, name=name, in_specs=[HBM_SPEC] * n + [SEM_SPEC, SEM_SPEC, ANY],
        out_shape=tuple(pltpu.HBM(b.shape, b.dtype) for b in bufs), out_specs=tuple([HBM_SPEC] * n),
        input_output_aliases={t: t for t in range(n)},
        compiler_params=pltpu.CompilerParams(has_side_effects=SPLIT_EFFECT))(*bufs, send, recv, after))


def gather_forward_sibling(bufs):
    n = len(bufs)

    def body(*refs):
        outs = refs[n:2 * n]
        send, recv = refs[2 * n:]
        x, y, c, k, sibling, chips = _place()
        cps = []
        for t in range(n):
            rows = bufs[t].shape[1]
            for j, (px, py) in enumerate(chips):
                landed = outs[t].at[2 * px + py, _half(rows, c)]
                cps.append(pltpu.make_async_remote_copy(src_ref=landed, dst_ref=landed, send_sem=send.at[t, j],
                                                        recv_sem=recv.at[t, j], device_id=sibling, device_id_type=MESH))
        for cp in cps:
            cp.start()
        for t in range(n):
            rows = bufs[t].shape[1]
            for j, (px, py) in enumerate(chips):
                other = outs[t].at[2 * px + py, _half(rows, 1 - c)]
                pltpu.make_async_remote_copy(src_ref=other, dst_ref=other, send_sem=send.at[t, j], recv_sem=recv.at[t, j],
                                             device_id=sibling, device_id_type=MESH).wait_recv()
        for cp in cps:
            cp.wait_send()

    return pl.pallas_call(
        body, in_specs=[ANY] * n, out_specs=[ANY] * n, out_shape=[jax.ShapeDtypeStruct(b.shape, b.dtype) for b in bufs],
        input_output_aliases={t: t for t in range(n)}, scratch_shapes=[pltpu.SemaphoreType.DMA((n, 3))] * 2,
        name="gather_forward_sibling", compiler_params=pltpu.CompilerParams())(*bufs)


def _reduce_ici_copies(ps, src_refs, dst_refs, send, recv):
    x, y, c, k, sibling, chips = _place()
    cps = []
    for t in range(len(ps)):
        for j, (px, py) in enumerate(chips):
            cps.append(pltpu.make_async_remote_copy(
                src_ref=src_refs[t].at[2 * px + py], dst_ref=dst_refs[t].at[j], send_sem=send.at[3 * t + j],
                recv_sem=recv.at[3 * t + j],
                device_id=(px, py, c), device_id_type=MESH))
    return cps


def reduce_ici_start(ps, name):
    n = len(ps)
    lands = [lax.empty((3,) + p.shape[1:], p.dtype) for p in ps]

    def body(*refs):
        send, recv, token = refs[2 * n], refs[2 * n + 1], refs[-1]
        for cp in _reduce_ici_copies(ps, refs[:n], refs[n:2 * n], send, recv):
            cp.start()
        token[...] = jnp.zeros_like(token)

    outs = pl.pallas_call(
        body, name=name, in_specs=[HBM_SPEC] * (2 * n),
        out_shape=(pltpu.SemaphoreType.DMA((3 * n,)), pltpu.SemaphoreType.DMA((3 * n,)),
                   *[pltpu.HBM(v.shape, v.dtype) for v in ps + lands], jax.ShapeDtypeStruct(TOKEN_SHAPE, F32)),
        out_specs=(SEM_SPEC, SEM_SPEC, *[HBM_SPEC] * (2 * n), pl.BlockSpec(memory_space=pltpu.VMEM)),
        input_output_aliases={t: 2 + t for t in range(2 * n)},
        compiler_params=pltpu.CompilerParams(has_side_effects=SPLIT_EFFECT))(*[_in_hbm(v) for v in ps + lands])
    return outs[0], outs[1], list(outs[2:2 + n]), list(outs[2 + n:2 + 2 * n]), outs[-1]


def reduce_ici_wait(send, recv, ps, lands, after, name):
    n = len(ps)

    def body(*refs):
        for cp in _reduce_ici_copies(ps, refs[:n], refs[n:2 * n], refs[2 * n], refs[2 * n + 1]):
            cp.wait_send()
            cp.wait_recv()

    outs = pl.pallas_call(
        body, name=name, in_specs=[HBM_SPEC] * (2 * n) + [SEM_SPEC, SEM_SPEC, ANY],
        out_shape=tuple(pltpu.HBM(v.shape, v.dtype) for v in ps + lands), out_specs=tuple([HBM_SPEC] * (2 * n)),
        input_output_aliases={t: t for t in range(2 * n)},
        compiler_params=pltpu.CompilerParams(has_side_effects=SPLIT_EFFECT))(*ps, *lands, send, recv, after)
    return list(outs[:n]), list(outs[n:])


SMALL_CHUNK_ROWS = 256


def all_reduce_small(v):
    rows = v.shape[0]
    nchunk = rows // SMALL_CHUNK_ROWS

    def body(v_ref, o_ref, buf, send, recv):
        x, y, c = lax.axis_index("x"), lax.axis_index("y"), lax.axis_index("c")
        me = 4 * x + 2 * y + c
        buf[me] = v_ref[...]
        cps = []
        for d in range(1, N_DEV):
            peer = (x ^ ((d >> 2) & 1), y ^ ((d >> 1) & 1), c ^ (d & 1))
            cps.append(pltpu.make_async_remote_copy(src_ref=v_ref, dst_ref=buf.at[me], send_sem=send.at[d - 1],
                                                    recv_sem=recv.at[d - 1], device_id=peer, device_id_type=MESH))
        for cp in cps:
            cp.start()
        for d in range(1, N_DEV):
            got = buf.at[me ^ d]
            pltpu.make_async_remote_copy(src_ref=got, dst_ref=got, send_sem=send.at[d - 1], recv_sem=recv.at[d - 1],
                                         device_id=(x, y, c), device_id_type=MESH).wait_recv()
        for cp in cps:
            cp.wait_send()

        def chunk(i, carry):
            sl = pl.ds(pl.multiple_of(i * SMALL_CHUNK_ROWS, SMALL_CHUNK_ROWS), SMALL_CHUNK_ROWS)
            acc = buf[0, sl, :]
            for s in range(1, N_DEV):
                acc = acc + buf[s, sl, :]
            o_ref[sl, :] = acc
            return carry

        lax.fori_loop(0, nchunk, chunk, 0)

    vmem = pl.BlockSpec(memory_space=pltpu.VMEM)
    return pl.pallas_call(
        body, in_specs=[vmem], out_specs=vmem, out_shape=jax.ShapeDtypeStruct(v.shape, F32),
        scratch_shapes=[pltpu.VMEM((N_DEV,) + v.shape, F32), pltpu.SemaphoreType.DMA((N_DEV - 1,)),
                        pltpu.SemaphoreType.DMA((N_DEV - 1,))],
        name="all_reduce_small",
        compiler_params=pltpu.CompilerParams(vmem_limit_bytes=VMEM_LIMIT_BYTES))(v)


def adamw(w, g, m, v):
    rows, cols = w.shape
    tr = rows
    for cand in (512, 256, 128, 64, 32, 16, 8):
        if rows % cand == 0 and cand * cols * 4 <= (1 << 20):
            tr = cand
            break
    c1 = 1.0 - ADAM_B1 ** ADAM_STEP
    c2 = 1.0 - ADAM_B2 ** ADAM_STEP

    def body(w_ref, g_ref, m_ref, v_ref, d_ref, nm_ref, nv_ref):
        gf = g_ref[...]
        nm = ADAM_B1 * m_ref[...] + (1.0 - ADAM_B1) * gf
        nv = ADAM_B2 * v_ref[...] + (1.0 - ADAM_B2) * (gf * gf)
        d_ref[...] = -ADAM_LR * ((nm / c1) / (jnp.sqrt(nv / c2) + ADAM_EPS) + ADAM_WD * w_ref[...])
        nm_ref[...] = nm
        nv_ref[...] = nv

    spec = pl.BlockSpec((tr, cols), lambda i: (i, 0))
    shape = jax.ShapeDtypeStruct((rows, cols), F32)
    return pl.pallas_call(body, grid=(rows // tr,), in_specs=[spec] * 4, out_specs=(spec,) * 3, out_shape=(shape,) * 3,
                          name="adamw", compiler_params=_params("parallel"))(w, g, m, v)


TAP_ROWS_ALIGN = 16
FLAT_ALIGN = 128 * SMALL_CHUNK_ROWS


def _pad_to(a, n):
    return jnp.pad(a, (0, n - a.shape[0]))


def _round_up(n, m):
    return (n + m - 1) // m * m


class Exchanges:
    def __init__(self, a, chip_core):
        self.a, self.chip_core = a, chip_core
        self.bufs = []
        for g in range(len(SUBLAYERS)):
            row = []
            for n in sublayer_weight_names(g):
                w = a[n].reshape(-1, a[n].shape[-1])
                if _kind(n) == "tap":
                    w = jnp.pad(w, ((0, _round_up(w.shape[0], TAP_ROWS_ALIGN) - w.shape[0]), (0, 0)))
                row.append(place_block(w, chip_core, F32 if _kind(n) == "tap" else BF16))
            self.bufs.append(row)
        self.started = None
        self.after = chip_core
        self.pending = []

    def _unpack(self, g, gathered, W):
        for n, v in zip(sublayer_weight_names(g), gathered):
            kind = _kind(n)
            if kind == "col":
                W[n] = v
            elif kind == "row":
                W[n] = v.reshape(-1, v.shape[-1])
            elif kind == "grp":
                grp, r, pg = self.a[n].shape
                W[n] = v.reshape(N_CHIPS, grp, r, pg).transpose(1, 0, 2, 3).reshape(grp, N_CHIPS * r, pg)
            else:
                nt = self.a[n].shape[0]
                W[n] = v[:, :nt].transpose(1, 0, 2).reshape(nt, -1)

    def gather_first(self, W):
        gathered = all_gather_chips(self.bufs[0])
        self._unpack(0, gathered, W)
        self.after = gathered[0]

    def forward_begins(self, g, W):
        if g + 1 == len(SUBLAYERS):
            return None
        send, recv, bufs, token = gather_ici_start(self.bufs[g + 1], self.after, f"gather_start_{g + 1}")
        self.started = (send, recv, bufs)
        return token

    def forward_ends(self, g, x, W):
        if g + 1 == len(SUBLAYERS):
            return
        send, recv, bufs = self.started
        gathered = gather_forward_sibling(gather_ici_wait(send, recv, bufs, x, f"gather_wait_{g + 1}"))
        self._unpack(g + 1, gathered, W)
        self.after = gathered[0]

    def gradients_ready(self, g, grads):
        names = [n for n in sublayer_weight_names(g) if _kind(n) != "tap"]
        gl = []
        for n in names:
            v, kind = grads.pop(n), _kind(n)
            if kind == "row":
                v = v.reshape(N_CHIPS, -1, v.shape[-1])
            elif kind == "grp":
                grp, r, pg = self.a[n].shape
                v = v.reshape(grp, N_CHIPS, r, pg).transpose(1, 0, 2, 3).reshape(N_CHIPS, grp * r, pg)
            gl.append(v)
        lands = exchange_sibling_halves(gl)
        ps = [add_sibling_half(v, l, self.chip_core) for v, l in zip(gl, lands)]
        send, recv, ps, l2s, token = reduce_ici_start(ps, f"reduce_start_{g}")
        self.pending.append((g, names, send, recv, ps, l2s))
        return token

    def finish_reductions(self, after):
        names_all, sums = [], []
        for g, names, send, recv, ps, l2s in self.pending:
            ps, l2s = reduce_ici_wait(send, recv, ps, l2s, after, f"reduce_wait_{g}")
            sums += [sum_chip_blocks(p, l2, self.chip_core) for p, l2 in zip(ps, l2s)]
            names_all += names
        return dict(zip(names_all, gather_sibling_halves(sums)))


def train_step(a):
    x, positions, tgt = a["x"][0], a["positions"][0], a["loss_target"][0]
    mats = [n for n in WEIGHT_NAMES if _kind(n) in ("col", "row", "grp")]
    taps = [n for n in WEIGHT_NAMES if _kind(n) == "tap"]
    reps = [n for n in WEIGHT_NAMES if _kind(n) == "rep"]
    chip = 2 * lax.axis_index("x") + lax.axis_index("y")
    chip_core = jnp.stack([chip, lax.axis_index("c")]).astype(jnp.int32)

    comm = Exchanges(a, chip_core)
    W = {n: a[n] for n in reps}
    comm.gather_first(W)
    loss, dx, grads = local_step(x, positions, tgt, W, comm)
    loss = lax.psum(loss, ("x", "y", "c"))
    reduced = comm.finish_reductions(dx)

    n_rep = _round_up(sum(a[n].size for n in reps), FLAT_ALIGN)
    flat_rep = _pad_to(jnp.concatenate([grads[n].reshape(-1) for n in reps]), n_rep)
    flat_tap = jnp.concatenate([grads[n].reshape(-1) for n in taps])
    flat = jnp.concatenate([flat_rep, _pad_to(flat_tap, _round_up(flat_tap.shape[0], FLAT_ALIGN))])
    summed = all_reduce_small(flat.reshape(-1, 128))
    rep_rows = n_rep // 128
    tap_flat = summed[rep_rows:].reshape(-1)

    out = {}
    pack = lambda pre: _pad_to(jnp.concatenate([a[pre + n].reshape(-1) for n in reps]), n_rep).reshape(-1, 128)
    g_rep = summed[:rep_rows]
    d_rep, m_rep, v_rep = adamw(pack(""), g_rep, pack("m_"), pack("v_"))
    off = 0
    for n in reps:
        size, shape = a[n].size, a[n].shape
        out[n] = tuple(f.reshape(-1)[off:off + size].reshape(shape) for f in (g_rep, d_rep, m_rep, v_rep))
        off += size
    off = 0
    for n in taps:
        nt, cs = a[n].shape
        full = tap_flat[off:off + nt * cs * N_CHIPS].reshape(nt, cs * N_CHIPS)
        off += nt * cs * N_CHIPS
        g = lax.dynamic_slice(full, (0, chip * cs), (nt, cs))
        out[n] = (g,) + tuple(adamw(a[n], g, a["m_" + n], a["v_" + n]))
    for n in mats:
        shape = a[n].shape
        two_d = lambda t: t.reshape(-1, shape[-1])
        g = reduced[n]
        out[n] = (g.reshape(shape),) + tuple(t.reshape(shape) for t in adamw(two_d(a[n]), g, two_d(a["m_" + n]), two_d(a["v_" + n])))

    res = [loss, dx[None]]
    for part in range(4):
        res += [out[n][part] for n in WEIGHT_NAMES]
    return tuple(res)


def kernel(x, positions, l0_norm_g, l0_a_w_in, l0_a_b_in, l0_a_dw_w, l0_a_dw_b, l0_a_ln_g, l0_a_ln_b, l0_a_w_out, l0_a_b_out, l0_ffn_norm_g, l0_ffn_w_up, l0_ffn_dw_w, l0_ffn_dw_b, l0_ffn_w_down, l1_norm_g, l1_b_w_group, l1_b_scale, l1_ffn_norm_g, l1_ffn_w_up, l1_ffn_dw_w, l1_ffn_dw_b, l1_ffn_w_down, l2_norm_g, l2_c_w_qkv, l2_c_q_norm_g, l2_c_k_norm_g, l2_c_sinks, l2_c_w_o, l2_ffn_norm_g, l2_ffn_w_up, l2_ffn_dw_w, l2_ffn_dw_b, l2_ffn_w_down, l3_norm_g, l3_a_w_in, l3_a_b_in, l3_a_dw_w, l3_a_dw_b, l3_a_ln_g, l3_a_ln_b, l3_a_w_out, l3_a_b_out, l3_ffn_norm_g, l3_ffn_w_up, l3_ffn_dw_w, l3_ffn_dw_b, l3_ffn_w_down, loss_target, m_l0_norm_g, m_l0_a_w_in, m_l0_a_b_in, m_l0_a_dw_w, m_l0_a_dw_b, m_l0_a_ln_g, m_l0_a_ln_b, m_l0_a_w_out, m_l0_a_b_out, m_l0_ffn_norm_g, m_l0_ffn_w_up, m_l0_ffn_dw_w, m_l0_ffn_dw_b, m_l0_ffn_w_down, m_l1_norm_g, m_l1_b_w_group, m_l1_b_scale, m_l1_ffn_norm_g, m_l1_ffn_w_up, m_l1_ffn_dw_w, m_l1_ffn_dw_b, m_l1_ffn_w_down, m_l2_norm_g, m_l2_c_w_qkv, m_l2_c_q_norm_g, m_l2_c_k_norm_g, m_l2_c_sinks, m_l2_c_w_o, m_l2_ffn_norm_g, m_l2_ffn_w_up, m_l2_ffn_dw_w, m_l2_ffn_dw_b, m_l2_ffn_w_down, m_l3_norm_g, m_l3_a_w_in, m_l3_a_b_in, m_l3_a_dw_w, m_l3_a_dw_b, m_l3_a_ln_g, m_l3_a_ln_b, m_l3_a_w_out, m_l3_a_b_out, m_l3_ffn_norm_g, m_l3_ffn_w_up, m_l3_ffn_dw_w, m_l3_ffn_dw_b, m_l3_ffn_w_down, v_l0_norm_g, v_l0_a_w_in, v_l0_a_b_in, v_l0_a_dw_w, v_l0_a_dw_b, v_l0_a_ln_g, v_l0_a_ln_b, v_l0_a_w_out, v_l0_a_b_out, v_l0_ffn_norm_g, v_l0_ffn_w_up, v_l0_ffn_dw_w, v_l0_ffn_dw_b, v_l0_ffn_w_down, v_l1_norm_g, v_l1_b_w_group, v_l1_b_scale, v_l1_ffn_norm_g, v_l1_ffn_w_up, v_l1_ffn_dw_w, v_l1_ffn_dw_b, v_l1_ffn_w_down, v_l2_norm_g, v_l2_c_w_qkv, v_l2_c_q_norm_g, v_l2_c_k_norm_g, v_l2_c_sinks, v_l2_c_w_o, v_l2_ffn_norm_g, v_l2_ffn_w_up, v_l2_ffn_dw_w, v_l2_ffn_dw_b, v_l2_ffn_w_down, v_l3_norm_g, v_l3_a_w_in, v_l3_a_b_in, v_l3_a_dw_w, v_l3_a_dw_b, v_l3_a_ln_g, v_l3_a_ln_b, v_l3_a_w_out, v_l3_a_b_out, v_l3_ffn_norm_g, v_l3_ffn_w_up, v_l3_ffn_dw_w, v_l3_ffn_dw_b, v_l3_ffn_w_down):
    return train_step(dict(locals()))
```

```python
import functools

import jax
import jax.numpy as jnp
from jax import lax
from jax.experimental import pallas as pl
from jax.experimental.pallas import tpu as pltpu

F32 = jnp.float32
BF16 = jnp.bfloat16
EPS = 1e-6
HEAD_DIM = 64
KV_GROUP = 8
ATT_BLOCK = 128
ROT_DIM = 16
ROPE_THETA = 500000.0
POOL_GROUPS = 4
CONF_TAPS = 31
FFN_TAPS = 3
N_CHIPS = 4
N_DEV = 8
ADAM_LR, ADAM_B1, ADAM_B2, ADAM_EPS, ADAM_WD, ADAM_STEP = 0.001, 0.9, 0.999, 1e-08, 0.01, 10
VMEM_LIMIT_BYTES = 56 * 1024 * 1024
MESH = pl.DeviceIdType.MESH

CONF_NAMES = ["norm_g", "a_w_in", "a_b_in", "a_dw_w", "a_dw_b", "a_ln_g", "a_ln_b", "a_w_out", "a_b_out"]
FFN_NAMES = ["ffn_norm_g", "ffn_w_up", "ffn_dw_w", "ffn_dw_b", "ffn_w_down"]
POOL_NAMES = ["norm_g", "b_w_group", "b_scale"]
ATT_NAMES = ["norm_g", "c_w_qkv", "c_q_norm_g", "c_k_norm_g", "c_sinks", "c_w_o"]
WEIGHT_NAMES = ([f"l0_{n}" for n in CONF_NAMES + FFN_NAMES] + [f"l1_{n}" for n in POOL_NAMES + FFN_NAMES]
                + [f"l2_{n}" for n in ATT_NAMES + FFN_NAMES] + [f"l3_{n}" for n in CONF_NAMES + FFN_NAMES])
COL_SHARDED = ("a_w_in", "ffn_w_up", "c_w_qkv")
ROW_SHARDED = ("a_w_out", "ffn_w_down", "c_w_o")
TAP_SHARDED = ("a_dw_w", "ffn_dw_w")


def _kind(name):
    base = name[3:]
    if base in COL_SHARDED:
        return "col"
    if base in ROW_SHARDED:
        return "row"
    if base in TAP_SHARDED:
        return "tap"
    if base == "b_w_group":
        return "grp"
    return "rep"


def _pick(n, prefs):
    for p in prefs:
        if p <= n and n % p == 0:
            return p
    return n


def _params(*sem):
    return pltpu.CompilerParams(dimension_semantics=sem, vmem_limit_bytes=VMEM_LIMIT_BYTES)


def _sigmoid(x):
    return 1.0 / (1.0 + jnp.exp(-x))


NN = (((1,), (0,)), ((), ()))
NT = (((1,), (1,)), ((), ()))
TN = (((0,), (0,)), ((), ()))


MM_VMEM_BUDGET = 44 * 1024 * 1024
MM_STEP_SECONDS = 0.35e-6
MM_FLOPS, MM_HBM_BYTES = 9.0e14, 3.0e12
TILE_SIZES = (4096, 2816, 2048, 1408, 1024, 704, 640, 512, 256, 128)


def _tile_options(n, lane):
    opts = [c for c in TILE_SIZES if c <= n and n % c == 0 and (not lane or c % 128 == 0)]
    return opts or [n]


def _mm_plan(m, n, k, *, n_unit=None, k_unit=None, a_bytes=2, b_bytes=2, o_bytes=2, extra_bytes=0):
    best = None
    for tm in _tile_options(m, False):
        for tn in _tile_options(n_unit or n, True):
            for tk in _tile_options(k_unit or k, True) + ([k] if not k_unit else []):
                nk = k // tk
                vmem = 2 * (tm * tk * a_bytes + tk * tn * b_bytes + tm * tn * (o_bytes + extra_bytes)) + tm * tn * 4 * (2 if nk > 1 else 1)
                if vmem > MM_VMEM_BUDGET:
                    continue
                ni, nj = m // tm, n // tn
                a_all, b_all, o_all = m * k * a_bytes, k * n * b_bytes, m * n * (o_bytes + extra_bytes)
                for i_inner in (False, True):
                    if nk > 1:
                        traffic = a_all * nj + b_all * ni + o_all
                    elif i_inner:
                        traffic = a_all * nj + b_all + o_all
                    else:
                        traffic = a_all + b_all * ni + o_all
                    cost = ni * nj * nk * MM_STEP_SECONDS + max(2.0 * m * n * k / MM_FLOPS, traffic / MM_HBM_BYTES)
                    if best is None or cost < best[0]:
                        best = (cost, tm, tn, tk, i_inner)
    assert best is not None, (m, n, k)
    return best[1:]


def _mm(a, b, *, dims, sizes, plan, a_blk, a_idx, b_blk, b_idx, o_blk, o_idx, out_shape, name,
        bias=None, scale=None, vec_blk=None, vec_idx=None, resid=None, raw_shape=None):
    m, n, k = sizes
    tm, tn, tk, i_inner = plan
    ni, nj, nk = m // tm, n // tn, k // tk
    has_bias, has_scale, has_resid, want_raw = bias is not None, scale is not None, resid is not None, raw_shape is not None

    def body(*refs):
        a_ref, b_ref = refs[0], refs[1]
        pos = 2
        bias_ref = scale_ref = resid_ref = raw_ref = None
        if has_bias:
            bias_ref = refs[pos]; pos += 1
        if has_scale:
            scale_ref = refs[pos]; pos += 1
        if has_resid:
            resid_ref = refs[pos]; pos += 1
        o_ref = refs[pos]; pos += 1
        if want_raw:
            raw_ref = refs[pos]; pos += 1
        part = lax.dot_general(a_ref[...].astype(BF16), b_ref[...].astype(BF16), dims, preferred_element_type=F32)

        def finish(r):
            if want_raw:
                raw_ref[...] = r.astype(raw_ref.dtype)
            if has_bias:
                r = r + bias_ref[...]
            if has_scale:
                r = r * scale_ref[...]
            if has_resid:
                r = r + resid_ref[...]
            o_ref[...] = r.astype(o_ref.dtype)

        if nk == 1:
            finish(part)
        else:
            acc_ref = refs[pos]
            kk = pl.program_id(2)

            @pl.when(kk == 0)
            def _():
                acc_ref[...] = part

            @pl.when(kk > 0)
            def _():
                acc_ref[...] += part

            @pl.when(kk == nk - 1)
            def _():
                finish(acc_ref[...])

    order = (lambda f: (lambda j, i, kk: f(i, j, kk))) if i_inner else (lambda f: f)
    spec = lambda blk, idx: pl.BlockSpec(blk, order(idx))
    operands, in_specs = [a, b], [spec(a_blk, a_idx), spec(b_blk, b_idx)]
    for v in (bias, scale):
        if v is not None:
            operands.append(v); in_specs.append(spec(vec_blk, vec_idx))
    if has_resid:
        operands.append(resid); in_specs.append(spec(o_blk, o_idx))
    out_shapes, out_specs = out_shape, spec(o_blk, o_idx)
    if want_raw:
        out_shapes, out_specs = (out_shape, raw_shape), (spec(o_blk, o_idx), spec(o_blk, o_idx))
    return pl.pallas_call(
        body, grid=(nj, ni, nk) if i_inner else (ni, nj, nk), in_specs=in_specs, out_specs=out_specs,
        out_shape=out_shapes, scratch_shapes=[pltpu.VMEM((tm, tn), F32)] if nk > 1 else [], name=name,
        compiler_params=_params("parallel", "parallel", "arbitrary"))(*operands)


def mm_nn_cols(a, g, *, split, out_dtype, bias=None, name):
    t, k = a.shape
    ns = g.shape[2]
    n = N_CHIPS * ns
    plan = _mm_plan(t, n, k, n_unit=ns, o_bytes=jnp.dtype(out_dtype).itemsize)
    tm, tn, tk, _ = plan
    nj = ns // tn
    if split:
        o_blk, o_idx = (None, tm, tn), (lambda i, j, kk: (j // (2 * nj), i, j % (2 * nj)))
        out_shape = jax.ShapeDtypeStruct((2, t, 2 * ns), out_dtype)
        vec_blk, vec_idx = (None, 1, tn), (lambda i, j, kk: (j // (2 * nj), 0, j % (2 * nj)))
        if bias is not None:
            bias = bias.reshape(2, 1, 2 * ns)
    else:
        o_blk, o_idx = (tm, tn), (lambda i, j, kk: (i, j))
        out_shape = jax.ShapeDtypeStruct((t, n), out_dtype)
        vec_blk, vec_idx = (1, tn), (lambda i, j, kk: (0, j))
        if bias is not None:
            bias = bias.reshape(1, n)
    return _mm(a, g, dims=NN, sizes=(t, n, k), plan=plan, a_blk=(tm, tk), a_idx=lambda i, j, kk: (i, kk),
               b_blk=(None, tk, tn), b_idx=lambda i, j, kk: (j // nj, kk, j % nj), o_blk=o_blk, o_idx=o_idx,
               out_shape=out_shape, name=name, bias=bias, vec_blk=vec_blk, vec_idx=vec_idx)


def mm_nn(a, w, *, out_dtype, bias=None, scale=None, resid=None, raw_dtype=None, name):
    t, k = a.shape
    n = w.shape[1]
    extra = (4 if resid is not None else 0) + (0 if raw_dtype is None else jnp.dtype(raw_dtype).itemsize)
    plan = _mm_plan(t, n, k, a_bytes=a.dtype.itemsize, o_bytes=jnp.dtype(out_dtype).itemsize, extra_bytes=extra)
    tm, tn, tk, _ = plan
    raw_shape = None if raw_dtype is None else jax.ShapeDtypeStruct((t, n), raw_dtype)
    return _mm(a, w, dims=NN, sizes=(t, n, k), plan=plan, a_blk=(tm, tk), a_idx=lambda i, j, kk: (i, kk),
               b_blk=(tk, tn), b_idx=lambda i, j, kk: (kk, j), o_blk=(tm, tn), o_idx=lambda i, j, kk: (i, j),
               out_shape=jax.ShapeDtypeStruct((t, n), out_dtype), name=name,
               bias=None if bias is None else bias.reshape(1, n), scale=None if scale is None else scale.reshape(1, n),
               vec_blk=(1, tn), vec_idx=lambda i, j, kk: (0, j), resid=resid, raw_shape=raw_shape)


def mm_nt(dy, w, *, out_dtype, name):
    t, n = dy.shape
    kdim = w.shape[0]
    plan = _mm_plan(t, kdim, n, a_bytes=dy.dtype.itemsize, o_bytes=jnp.dtype(out_dtype).itemsize)
    tm, tn, tk, _ = plan
    return _mm(dy, w, dims=NT, sizes=(t, kdim, n), plan=plan, a_blk=(tm, tk), a_idx=lambda i, j, kk: (i, kk),
               b_blk=(tn, tk), b_idx=lambda i, j, kk: (j, kk), o_blk=(tm, tn), o_idx=lambda i, j, kk: (i, j),
               out_shape=jax.ShapeDtypeStruct((t, kdim), out_dtype), name=name)


def mm_nt_cols(du, g, *, split, out_dtype, name):
    kdim, ns = g.shape[1], g.shape[2]
    t = du.shape[1] if split else du.shape[0]
    plan = _mm_plan(t, kdim, N_CHIPS * ns, k_unit=ns, o_bytes=jnp.dtype(out_dtype).itemsize)
    tm, tn, tk, _ = plan
    nkb = ns // tk
    if split:
        a_blk, a_idx = (None, tm, tk), (lambda i, j, kk: (kk // (2 * nkb), i, kk % (2 * nkb)))
    else:
        a_blk, a_idx = (tm, tk), (lambda i, j, kk: (i, kk))
    return _mm(du, g, dims=NT, sizes=(t, kdim, N_CHIPS * ns), plan=plan, a_blk=a_blk, a_idx=a_idx,
               b_blk=(None, tn, tk), b_idx=lambda i, j, kk: (kk // nkb, j, kk % nkb),
               o_blk=(tm, tn), o_idx=lambda i, j, kk: (i, j),
               out_shape=jax.ShapeDtypeStruct((t, kdim), out_dtype), name=name)


def mm_wgrad(at, dy, *, out_dtype, name):
    return mm_nn(at, dy, out_dtype=out_dtype, name=name)


def mm_wgrad_cols(ht, du, *, split, out_dtype, name):
    kdim, t = ht.shape
    ns = (du.shape[2] // 2) if split else (du.shape[1] // N_CHIPS)
    plan = _mm_plan(kdim, N_CHIPS * ns, t, n_unit=ns, o_bytes=jnp.dtype(out_dtype).itemsize)
    tm, tn, tk, _ = plan
    nj = ns // tn
    if split:
        b_blk, b_idx = (None, tk, tn), (lambda i, j, kk: (j // (2 * nj), kk, j % (2 * nj)))
    else:
        b_blk, b_idx = (tk, tn), (lambda i, j, kk: (kk, j))
    return _mm(ht, du, dims=NN, sizes=(kdim, N_CHIPS * ns, t), plan=plan, a_blk=(tm, tk), a_idx=lambda i, j, kk: (i, kk),
               b_blk=b_blk, b_idx=b_idx, o_blk=(None, tm, tn), o_idx=lambda i, j, kk: (j // nj, i, j % nj),
               out_shape=jax.ShapeDtypeStruct((N_CHIPS, kdim, ns), out_dtype), name=name)


def _row_tile(t):
    return _pick(t, (256, 128))


def _acc_rows(ref, part, i):
    @pl.when(i == 0)
    def _():
        ref[...] = part

    @pl.when(i > 0)
    def _():
        ref[...] += part


TOKEN_SHAPE = (8, 128)


def _token_operand(token):
    if token is None:
        return [], []
    return [token], [pl.BlockSpec(TOKEN_SHAPE, lambda i: (0, 0))]


def rms_fwd(x, g, out_dtype, token=None, transposed=False):
    t, d = x.shape
    tr = _row_tile(t)

    def body(x_ref, g_ref, *rest):
        outs = rest[-2:] if transposed else rest[-1:]
        xf = x_ref[...]
        r = lax.rsqrt(jnp.mean(xf * xf, axis=-1, keepdims=True) + EPS)
        y = xf * r * g_ref[...]
        outs[0][...] = y.astype(outs[0].dtype)
        if transposed:
            outs[1][...] = y.T.astype(BF16)

    row = pl.BlockSpec((tr, d), lambda i: (i, 0))
    tok, tok_spec = _token_operand(token)
    out_specs, out_shape = row, jax.ShapeDtypeStruct((t, d), out_dtype)
    if transposed:
        out_specs = (row, pl.BlockSpec((d, tr), lambda i: (0, i)))
        out_shape = (out_shape, jax.ShapeDtypeStruct((d, t), BF16))
    return pl.pallas_call(body, grid=(t // tr,), in_specs=[row, pl.BlockSpec((1, d), lambda i: (0, 0))] + tok_spec,
                          out_specs=out_specs, out_shape=out_shape, name="rms_fwd",
                          compiler_params=_params("parallel"))(x, g.reshape(1, d), *tok)


def rms_bwd(x, g, dh, dres, token=None):
    t, d = x.shape
    tr = _row_tile(t)

    def body(x_ref, g_ref, dh_ref, dres_ref, *rest):
        dx_ref, dx16_ref, dg_ref = rest[-3:]
        i = pl.program_id(0)
        xf = x_ref[...]
        r = lax.rsqrt(jnp.mean(xf * xf, axis=-1, keepdims=True) + EPS)
        xhat = xf * r
        dhf = dh_ref[...].astype(F32)
        dxh = dhf * g_ref[...]
        m = jnp.mean(dxh * xhat, axis=-1, keepdims=True)
        dx = dres_ref[...] + r * (dxh - xhat * m)
        dx_ref[...] = dx
        dx16_ref[...] = dx.astype(BF16)
        _acc_rows(dg_ref, jnp.sum(dhf * xhat, axis=0, keepdims=True), i)

    row = pl.BlockSpec((tr, d), lambda i: (i, 0))
    vec = pl.BlockSpec((1, d), lambda i: (0, 0))
    tok, tok_spec = _token_operand(token)
    return pl.pallas_call(body, grid=(t // tr,), in_specs=[row, vec, row, row] + tok_spec, out_specs=(row, row, vec),
                          out_shape=(jax.ShapeDtypeStruct((t, d), F32), jax.ShapeDtypeStruct((t, d), BF16),
                                     jax.ShapeDtypeStruct((1, d), F32)),
                          name="rms_bwd", compiler_params=_params("arbitrary"))(x, g.reshape(1, d), dh, dres, *tok)


def ln_silu_fwd(c, g, b):
    t, d = c.shape
    tr = _row_tile(t)

    def body(c_ref, g_ref, b_ref, o_ref, ot_ref):
        xf = c_ref[...]
        mu = jnp.mean(xf, axis=-1, keepdims=True)
        xc = xf - mu
        var = jnp.mean(xc * xc, axis=-1, keepdims=True)
        n = xc * lax.rsqrt(var + EPS) * g_ref[...] + b_ref[...]
        s = n * _sigmoid(n)
        o_ref[...] = s.astype(o_ref.dtype)
        ot_ref[...] = s.T.astype(ot_ref.dtype)

    row = pl.BlockSpec((tr, d), lambda i: (i, 0))
    vec = pl.BlockSpec((1, d), lambda i: (0, 0))
    return pl.pallas_call(body, grid=(t // tr,), in_specs=[row, vec, vec],
                          out_specs=(row, pl.BlockSpec((d, tr), lambda i: (0, i))),
                          out_shape=(jax.ShapeDtypeStruct((t, d), BF16), jax.ShapeDtypeStruct((d, t), BF16)),
                          name="ln_silu_fwd", compiler_params=_params("parallel"))(c, g.reshape(1, d), b.reshape(1, d))


def ln_silu_bwd(c, g, b, ds):
    t, d = c.shape
    tr = _row_tile(t)

    def body(c_ref, g_ref, b_ref, ds_ref, dc_ref, dg_ref, db_ref):
        i = pl.program_id(0)
        xf = c_ref[...]
        mu = jnp.mean(xf, axis=-1, keepdims=True)
        xc = xf - mu
        var = jnp.mean(xc * xc, axis=-1, keepdims=True)
        rstd = lax.rsqrt(var + EPS)
        xhat = xc * rstd
        n = xhat * g_ref[...] + b_ref[...]
        sg = _sigmoid(n)
        dn = ds_ref[...].astype(F32) * (sg * (1.0 + n * (1.0 - sg)))
        dxh = dn * g_ref[...]
        m1 = jnp.mean(dxh, axis=-1, keepdims=True)
        m2 = jnp.mean(dxh * xhat, axis=-1, keepdims=True)
        dc_ref[...] = rstd * (dxh - m1 - xhat * m2)
        _acc_rows(dg_ref, jnp.sum(dn * xhat, axis=0, keepdims=True), i)
        _acc_rows(db_ref, jnp.sum(dn, axis=0, keepdims=True), i)

    row = pl.BlockSpec((tr, d), lambda i: (i, 0))
    vec = pl.BlockSpec((1, d), lambda i: (0, 0))
    vshape = jax.ShapeDtypeStruct((1, d), F32)
    return pl.pallas_call(body, grid=(t // tr,), in_specs=[row, vec, vec, row], out_specs=(row, vec, vec),
                          out_shape=(jax.ShapeDtypeStruct((t, d), F32), vshape, vshape), name="ln_silu_bwd",
                          compiler_params=_params("arbitrary"))(c, g.reshape(1, d), b.reshape(1, d), ds)


def loss_grad(y, tgt):
    t, d = y.shape
    tr = _row_tile(t)

    def body(y_ref, t_ref, dy_ref, dy16_ref, sq_ref):
        i = pl.program_id(0)
        err = y_ref[...] - t_ref[...]
        dy = err * (1.0 / d)
        dy_ref[...] = dy
        dy16_ref[...] = dy.astype(BF16)
        _acc_rows(sq_ref, jnp.sum(err * err, axis=0, keepdims=True), i)

    row = pl.BlockSpec((tr, d), lambda i: (i, 0))
    vec = pl.BlockSpec((1, d), lambda i: (0, 0))
    return pl.pallas_call(body, grid=(t // tr,), in_specs=[row, row], out_specs=(row, row, vec),
                          out_shape=(jax.ShapeDtypeStruct((t, d), F32), jax.ShapeDtypeStruct((t, d), BF16),
                                     jax.ShapeDtypeStruct((1, d), F32)),
                          name="loss_grad", compiler_params=_params("arbitrary"))(y, tgt)


def col_sum(a):
    t, d = a.shape
    tr = _row_tile(t)

    def body(a_ref, o_ref):
        _acc_rows(o_ref, jnp.sum(a_ref[...].astype(F32), axis=0, keepdims=True), pl.program_id(0))

    return pl.pallas_call(body, grid=(t // tr,), in_specs=[pl.BlockSpec((tr, d), lambda i: (i, 0))],
                          out_specs=pl.BlockSpec((1, d), lambda i: (0, 0)), out_shape=jax.ShapeDtypeStruct((1, d), F32),
                          name="col_sum", compiler_params=_params("arbitrary"))(a)


def scale_bwd(dx, ypre, scale):
    t, d = dx.shape
    tr = _row_tile(t)

    def body(dx_ref, y_ref, s_ref, dy_ref, ds_ref):
        dxf = dx_ref[...]
        dy_ref[...] = (dxf * s_ref[...]).astype(dy_ref.dtype)
        _acc_rows(ds_ref, jnp.sum(dxf * y_ref[...], axis=0, keepdims=True), pl.program_id(0))

    row = pl.BlockSpec((tr, d), lambda i: (i, 0))
    vec = pl.BlockSpec((1, d), lambda i: (0, 0))
    return pl.pallas_call(body, grid=(t // tr,), in_specs=[row, row, vec], out_specs=(row, vec),
                          out_shape=(jax.ShapeDtypeStruct((t, d), BF16), jax.ShapeDtypeStruct((1, d), F32)),
                          name="scale_bwd", compiler_params=_params("arbitrary"))(dx, ypre, scale.reshape(1, d))


def _chunk_rows(t):
    return _pick(t, (256, 128))


def _load_halo(ref, lead, base, rows, t, first, last, pre, post):
    idx = (lambda s, n: (pl.ds(s, n), slice(None))) if lead is None else (lambda s, n: (lead, pl.ds(s, n), slice(None)))
    parts = []
    if pre:
        start = pl.multiple_of(jnp.maximum(base - pre, 0), pre)
        parts.append(ref[idx(start, pre)].astype(F32) * jnp.where(first, 0.0, 1.0))
    parts.append(ref[idx(base, rows)].astype(F32))
    if post:
        start = pl.multiple_of(jnp.minimum(base + rows, t - post), post)
        parts.append(ref[idx(start, post)].astype(F32) * jnp.where(last, 0.0, 1.0))
    return parts[0] if len(parts) == 1 else jnp.concatenate(parts, axis=0)


def _fold8(x):
    r, c = x.shape
    return x.reshape(r // 8, 8, c).sum(axis=0)


def ffn_gate_fwd(u2, w3, b2):
    _, t, f = u2.shape
    tc = _pick(f, (256, 128))
    rows = _chunk_rows(t)
    nch = t // rows
    halo = 16

    def body(u_ref, w_ref, b_ref, a_ref, at_ref):
        def conv(p, base, first):
            xs = _load_halo(u_ref, p, base, rows, t, first, False, halo, 0)
            wp = w_ref[p]
            return (wp[0:1] * pltpu.roll(xs, 2, 0)[halo:] + wp[1:2] * pltpu.roll(xs, 1, 0)[halo:]
                    + wp[2:3] * xs[halo:] + b_ref[p])

        def chunk(i, carry):
            base = pl.multiple_of(i * rows, rows)
            gate, val = conv(0, base, i == 0), conv(1, base, i == 0)
            act = gate * _sigmoid(gate) * val
            a_ref[pl.ds(base, rows), :] = act.astype(a_ref.dtype)
            at_ref[:, pl.ds(base, rows)] = act.T.astype(at_ref.dtype)
            return carry

        lax.fori_loop(0, nch, chunk, 0)

    return pl.pallas_call(
        body, grid=(f // tc,),
        in_specs=[pl.BlockSpec((2, t, tc), lambda j: (0, 0, j)), pl.BlockSpec((2, 3, tc), lambda j: (0, 0, j)),
                  pl.BlockSpec((2, 1, tc), lambda j: (0, 0, j))],
        out_specs=(pl.BlockSpec((t, tc), lambda j: (0, j)), pl.BlockSpec((tc, t), lambda j: (j, 0))),
        out_shape=(jax.ShapeDtypeStruct((t, f), BF16), jax.ShapeDtypeStruct((f, t), BF16)),
        name="ffn_gate_fwd", compiler_params=_params("parallel"))(u2, w3, b2)


def ffn_gate_bwd(u2, da, w3, b2):
    _, t, f = u2.shape
    tc = _pick(f, (256, 128))
    rows = _chunk_rows(t)
    nch = t // rows
    halo = 16
    n = rows + 2 * halo

    def body(u_ref, da_ref, w_ref, b_ref, du_ref, dw_ref, db_ref, acc_ref):
        acc_ref[...] = jnp.zeros_like(acc_ref)

        def chunk(i, carry):
            base = pl.multiple_of(i * rows, rows)
            first, last = i == 0, i == nch - 1
            daf = jnp.concatenate(
                [jnp.zeros((halo, tc), F32), _load_halo(da_ref, None, base, rows, t, first, last, 0, halo)], axis=0)
            pre, shifted = [], []
            for p in range(2):
                xs = _load_halo(u_ref, p, base, rows, t, first, last, halo, halo)
                x1, x2 = pltpu.roll(xs, 1, 0), pltpu.roll(xs, 2, 0)
                wp = w_ref[p]
                pre.append(wp[0:1] * x2 + wp[1:2] * x1 + wp[2:3] * xs + b_ref[p])
                shifted.append((x2, x1, xs))
            gate, val = pre
            sg = _sigmoid(gate)
            d_pre = (daf * val * (sg * (1.0 + gate * (1.0 - sg))), daf * gate * sg)
            for p in range(2):
                dp = d_pre[p]
                wp = w_ref[p]
                du = wp[2:3] * dp + wp[1:2] * pltpu.roll(dp, n - 1, 0) + wp[0:1] * pltpu.roll(dp, n - 2, 0)
                du_ref[p, pl.ds(base, rows), :] = du[halo:halo + rows].astype(du_ref.dtype)
                own = dp[halo:halo + rows]
                for k in range(3):
                    acc_ref[p, k] += _fold8(own * shifted[p][k][halo:halo + rows])
                acc_ref[p, 3] += _fold8(own)
            return carry

        lax.fori_loop(0, nch, chunk, 0)
        for p in range(2):
            for k in range(3):
                dw_ref[p, k:k + 1, :] = jnp.sum(acc_ref[p, k], axis=0, keepdims=True)
            db_ref[p] = jnp.sum(acc_ref[p, 3], axis=0, keepdims=True)

    blk = pl.BlockSpec((2, t, tc), lambda j: (0, 0, j))
    wspec = pl.BlockSpec((2, 3, tc), lambda j: (0, 0, j))
    bspec = pl.BlockSpec((2, 1, tc), lambda j: (0, 0, j))
    return pl.pallas_call(
        body, grid=(f // tc,), in_specs=[blk, pl.BlockSpec((t, tc), lambda j: (0, j)), wspec, bspec],
        out_specs=(blk, wspec, bspec),
        out_shape=(jax.ShapeDtypeStruct((2, t, f), BF16), jax.ShapeDtypeStruct((2, 3, f), F32),
                   jax.ShapeDtypeStruct((2, 1, f), F32)),
        scratch_shapes=[pltpu.VMEM((2, 4, 8, tc), F32)], name="ffn_gate_bwd",
        compiler_params=_params("parallel"))(u2, da, w3, b2)


def glu_conv_fwd(u2, w, b):
    _, t, d = u2.shape
    taps = w.shape[0]
    tc = 128
    rows = _chunk_rows(t)
    nch = t // rows
    halo = 32

    def body(u_ref, w_ref, b_ref, c_ref):
        def chunk(i, carry):
            base = pl.multiple_of(i * rows, rows)
            a = _load_halo(u_ref, 0, base, rows, t, i == 0, False, halo, 0)
            g = _load_halo(u_ref, 1, base, rows, t, i == 0, False, halo, 0)
            xs = a * _sigmoid(g)
            acc = w_ref[taps - 1:taps, :] * xs[halo:] + b_ref[...]
            for j in range(taps - 1):
                acc = acc + w_ref[j:j + 1, :] * pltpu.roll(xs, taps - 1 - j, 0)[halo:]
            c_ref[pl.ds(base, rows), :] = acc
            return carry

        lax.fori_loop(0, nch, chunk, 0)

    return pl.pallas_call(
        body, grid=(d // tc,),
        in_specs=[pl.BlockSpec((2, t, tc), lambda j: (0, 0, j)), pl.BlockSpec((taps, tc), lambda j: (0, j)),
                  pl.BlockSpec((1, tc), lambda j: (0, j))],
        out_specs=pl.BlockSpec((t, tc), lambda j: (0, j)), out_shape=jax.ShapeDtypeStruct((t, d), F32),
        name="glu_conv_fwd", compiler_params=_params("parallel"))(u2, w, b)


def glu_conv_bwd(u2, dc, w):
    _, t, d = u2.shape
    taps = w.shape[0]
    tc = 128
    rows = _chunk_rows(t)
    nch = t // rows
    halo = 32
    n = rows + halo

    def body(u_ref, dc_ref, w_ref, du_ref, dw_ref, dwb_ref, dbin_ref, acc_ref, bacc_ref):
        acc_ref[...] = jnp.zeros_like(acc_ref)
        bacc_ref[...] = jnp.zeros_like(bacc_ref)

        def chunk(i, carry):
            base = pl.multiple_of(i * rows, rows)
            first, last = i == 0, i == nch - 1
            a = _load_halo(u_ref, 0, base, rows, t, first, False, halo, 0)
            g = _load_halo(u_ref, 1, base, rows, t, first, False, halo, 0)
            sg = _sigmoid(g)
            xs = a * sg
            dcs = _load_halo(dc_ref, None, base, rows, t, first, last, 0, halo)
            own = dcs[:rows]
            dglu = w_ref[taps - 1:taps, :] * own
            acc_ref[taps - 1] += _fold8(own * xs[halo:])
            for j in range(taps - 1):
                s = taps - 1 - j
                dglu = dglu + w_ref[j:j + 1, :] * pltpu.roll(dcs, n - s, 0)[:rows]
                acc_ref[j] += _fold8(own * pltpu.roll(xs, s, 0)[halo:])
            a_c, sg_c = a[halo:], sg[halo:]
            da = dglu * sg_c
            dg = dglu * a_c * sg_c * (1.0 - sg_c)
            du_ref[0, pl.ds(base, rows), :] = da.astype(du_ref.dtype)
            du_ref[1, pl.ds(base, rows), :] = dg.astype(du_ref.dtype)
            bacc_ref[0] += _fold8(own)
            bacc_ref[1] += _fold8(da)
            bacc_ref[2] += _fold8(dg)
            return carry

        lax.fori_loop(0, nch, chunk, 0)
        for j in range(taps):
            dw_ref[j:j + 1, :] = jnp.sum(acc_ref[j], axis=0, keepdims=True)
        dwb_ref[...] = jnp.sum(bacc_ref[0], axis=0, keepdims=True)
        dbin_ref[0] = jnp.sum(bacc_ref[1], axis=0, keepdims=True)
        dbin_ref[1] = jnp.sum(bacc_ref[2], axis=0, keepdims=True)

    blk = pl.BlockSpec((2, t, tc), lambda j: (0, 0, j))
    col = pl.BlockSpec((t, tc), lambda j: (0, j))
    return pl.pallas_call(
        body, grid=(d // tc,), in_specs=[blk, col, pl.BlockSpec((taps, tc), lambda j: (0, j))],
        out_specs=(blk, pl.BlockSpec((taps, tc), lambda j: (0, j)), pl.BlockSpec((1, tc), lambda j: (0, j)),
                   pl.BlockSpec((2, 1, tc), lambda j: (0, 0, j))),
        out_shape=(jax.ShapeDtypeStruct((2, t, d), BF16), jax.ShapeDtypeStruct((taps, d), F32),
                   jax.ShapeDtypeStruct((1, d), F32), jax.ShapeDtypeStruct((2, 1, d), F32)),
        scratch_shapes=[pltpu.VMEM((taps, 8, tc), F32), pltpu.VMEM((3, 8, tc), F32)], name="glu_conv_bwd",
        compiler_params=_params("parallel"))(u2, dc, w)


def _pool_select(grp, levels):
    out = levels[3]
    for k in (2, 1, 0):
        out = jnp.where(grp == k, levels[k], out)
    return out


def _pool_count(base, rows, tc, grp):
    tpos = (base + lax.broadcasted_iota(jnp.int32, (rows, tc), 0) + 1).astype(F32)
    window = jnp.left_shift(2, grp).astype(F32)
    return jnp.minimum(tpos, window)


def pool_fwd(h):
    t, d = h.shape
    pg = d // POOL_GROUPS
    tc = _pick(pg, (256, 128))
    rows = _chunk_rows(t)
    nch = t // rows
    halo = 16

    def body(h_ref, o_ref, ot_ref):
        grp = (pl.program_id(0) * tc) // pg

        def chunk(i, carry):
            base = pl.multiple_of(i * rows, rows)
            xs = _load_halo(h_ref, None, base, rows, t, i == 0, False, halo, 0)
            levels, cur = [], xs
            for k in range(4):
                cur = cur + pltpu.roll(cur, 1 << k, 0)
                levels.append(cur[halo:])
            pooled = _pool_select(grp, levels) / _pool_count(base, rows, tc, grp)
            mixed = pooled - xs[halo:]
            o_ref[pl.ds(base, rows), :] = mixed.astype(o_ref.dtype)
            ot_ref[:, pl.ds(base, rows)] = mixed.T.astype(ot_ref.dtype)
            return carry

        lax.fori_loop(0, nch, chunk, 0)

    col = pl.BlockSpec((t, tc), lambda j: (0, j))
    return pl.pallas_call(body, grid=(d // tc,), in_specs=[col], out_specs=(col, pl.BlockSpec((tc, t), lambda j: (j, 0))),
                          out_shape=(jax.ShapeDtypeStruct((t, d), BF16), jax.ShapeDtypeStruct((d, t), BF16)),
                          name="pool_fwd", compiler_params=_params("parallel"))(h)


def pool_bwd(dmix):
    t, d = dmix.shape
    pg = d // POOL_GROUPS
    tc = _pick(pg, (256, 128))
    rows = _chunk_rows(t)
    nch = t // rows
    halo = 16
    n = rows + halo

    def body(d_ref, o_ref):
        grp = (pl.program_id(0) * tc) // pg

        def chunk(i, carry):
            base = pl.multiple_of(i * rows, rows)
            ds = _load_halo(d_ref, None, base, rows, t, i == 0, i == nch - 1, 0, halo)
            levels, cur = [], ds / _pool_count(base, n, tc, grp)
            for k in range(4):
                cur = cur + pltpu.roll(cur, n - (1 << k), 0)
                levels.append(cur[:rows])
            o_ref[pl.ds(base, rows), :] = _pool_select(grp, levels) - ds[:rows]
            return carry

        lax.fori_loop(0, nch, chunk, 0)

    col = pl.BlockSpec((t, tc), lambda j: (0, j))
    return pl.pallas_call(body, grid=(d // tc,), in_specs=[col], out_specs=col,
                          out_shape=jax.ShapeDtypeStruct((t, d), F32), name="pool_bwd",
                          compiler_params=_params("parallel"))(dmix)


def mm_groups(a, wg, *, mode, out_dtype, scale=None, resid=None, raw_dtype=None, name):
    t, d = a.shape
    pg = wg.shape[1]
    tm = _pick(t, (1024, 512, 256, 128))
    tn = pg
    if mode == "nn":
        dims, b_blk, b_idx = NN, (None, pg, tn), (lambda i, j, kk: (j, 0, 0))
    else:
        dims, b_blk, b_idx = NT, (None, tn, pg), (lambda i, j, kk: (j, 0, 0))
    raw_shape = None if raw_dtype is None else jax.ShapeDtypeStruct((t, d), raw_dtype)
    return _mm(a, wg, dims=dims, sizes=(t, d, pg), plan=(tm, tn, pg, False), a_blk=(tm, pg), a_idx=lambda i, j, kk: (i, j),
               b_blk=b_blk, b_idx=b_idx, o_blk=(tm, tn), o_idx=lambda i, j, kk: (i, j),
               out_shape=jax.ShapeDtypeStruct((t, d), out_dtype), name=name,
               scale=None if scale is None else scale.reshape(1, d), vec_blk=(1, tn), vec_idx=lambda i, j, kk: (0, j),
               resid=resid, raw_shape=raw_shape)


def mm_groups_wgrad(at, dy, groups, *, out_dtype, name):
    d, t = at.shape
    pg = d // groups
    tk = _pick(t, (2048, 1024, 512, 256, 128))
    return _mm(at, dy, dims=NN, sizes=(d, pg, t), plan=(pg, pg, tk, False), a_blk=(pg, tk), a_idx=lambda i, j, kk: (i, kk),
               b_blk=(tk, pg), b_idx=lambda i, j, kk: (kk, i), o_blk=(None, pg, pg), o_idx=lambda i, j, kk: (i, 0, 0),
               out_shape=jax.ShapeDtypeStruct((groups, pg, pg), out_dtype), name=name)


def _split_dot(y, p):
    hi = y.astype(BF16)
    r1 = y - hi.astype(F32)
    mid = r1.astype(BF16)
    lo = (r1 - mid.astype(F32)).astype(BF16)
    pb = p.astype(BF16)
    dot = lambda v: jnp.dot(v, pb, preferred_element_type=F32)
    return (dot(hi) + dot(mid)) + dot(lo)


def rope_tables(positions):
    half = ROT_DIM // 2
    inv_freq = ROPE_THETA ** (-jnp.arange(0, ROT_DIM, 2, dtype=F32) / ROT_DIM)
    ang = positions.astype(F32)[:, None] * inv_freq
    t = positions.shape[0]
    cos, sin = jnp.cos(ang), jnp.sin(ang)
    rest = HEAD_DIM - ROT_DIM
    cosf = jnp.concatenate([cos, cos, jnp.ones((t, rest), F32)], axis=1)
    sinf = jnp.concatenate([-sin, sin, jnp.zeros((t, rest), F32)], axis=1)
    idx = jnp.arange(HEAD_DIM)
    partner = jnp.where(idx < half, idx + half, jnp.where(idx < ROT_DIM, idx - half, idx))
    pmat = (idx[:, None] == partner[None, :]).astype(F32)
    return cosf, sinf, pmat


def qk_rope_fwd(x, g, cosf, sinf, pmat, out_scale):
    hn, t, hd = x.shape
    tq = _pick(t, (512, 256, 128))

    def body(x_ref, g_ref, c_ref, s_ref, p_ref, o_ref):
        xf = x_ref[...]
        r = lax.rsqrt(jnp.mean(xf * xf, axis=-1, keepdims=True) + EPS)
        y = xf * r * g_ref[...]
        rot = y * c_ref[...] + _split_dot(y, p_ref[...]) * s_ref[...]
        o_ref[...] = (rot * out_scale).astype(o_ref.dtype)

    blk = pl.BlockSpec((None, tq, hd), lambda h, i: (h, i, 0))
    tab = pl.BlockSpec((tq, hd), lambda h, i: (i, 0))
    return pl.pallas_call(
        body, grid=(hn, t // tq),
        in_specs=[blk, pl.BlockSpec((1, hd), lambda h, i: (0, 0)), tab, tab, pl.BlockSpec((hd, hd), lambda h, i: (0, 0))],
        out_specs=blk, out_shape=jax.ShapeDtypeStruct((hn, t, hd), BF16), name="qk_rope_fwd",
        compiler_params=_params("parallel", "parallel"))(x, g.reshape(1, hd), cosf, sinf, pmat)


def qk_rope_bwd(dy, x, g, cosf, sinf, pmat_t, in_scale):
    hn, t, hd = x.shape
    tq = _pick(t, (512, 256, 128))

    def body(dy_ref, x_ref, g_ref, c_ref, s_ref, p_ref, dx_ref, dg_ref):
        step = pl.program_id(0) * pl.num_programs(1) + pl.program_id(1)
        dr = dy_ref[...] * in_scale
        dyn = dr * c_ref[...] + _split_dot(dr * s_ref[...], p_ref[...])
        xf = x_ref[...]
        r = lax.rsqrt(jnp.mean(xf * xf, axis=-1, keepdims=True) + EPS)
        xhat = xf * r
        dxh = dyn * g_ref[...]
        m = jnp.mean(dxh * xhat, axis=-1, keepdims=True)
        dx_ref[...] = r * (dxh - xhat * m)
        _acc_rows(dg_ref, jnp.sum(dyn * xhat, axis=0, keepdims=True), step)

    blk = pl.BlockSpec((None, tq, hd), lambda h, i: (h, i, 0))
    tab = pl.BlockSpec((tq, hd), lambda h, i: (i, 0))
    vec = pl.BlockSpec((1, hd), lambda h, i: (0, 0))
    return pl.pallas_call(
        body, grid=(hn, t // tq),
        in_specs=[blk, blk, vec, tab, tab, pl.BlockSpec((hd, hd), lambda h, i: (0, 0))],
        out_specs=(blk, vec), out_shape=(jax.ShapeDtypeStruct((hn, t, hd), F32), jax.ShapeDtypeStruct((1, hd), F32)),
        name="qk_rope_bwd", compiler_params=_params("arbitrary", "arbitrary"))(dy, x, g.reshape(1, hd), cosf, sinf, pmat_t)


NEG_BIG = -1e30


def _att_masks(i):
    qi = lax.broadcasted_iota(jnp.int32, (ATT_BLOCK, ATT_BLOCK), 0)
    kj = lax.broadcasted_iota(jnp.int32, (ATT_BLOCK, ATT_BLOCK), 1)
    return kj <= qi, jnp.logical_and(kj > qi, i > 0)


def _att_probs(q, kc, kp, mask_c, mask_p, sink):
    s_c = jnp.where(mask_c, lax.dot_general(q, kc, NT, preferred_element_type=F32), NEG_BIG)
    s_p = jnp.where(mask_p, lax.dot_general(q, kp, NT, preferred_element_type=F32), NEG_BIG)
    m = jnp.maximum(jnp.maximum(jnp.max(s_c, axis=-1, keepdims=True), jnp.max(s_p, axis=-1, keepdims=True)), sink)
    p_c, p_p = jnp.exp(s_c - m), jnp.exp(s_p - m)
    p_s = jnp.exp(sink - m)
    denom = jnp.sum(p_c, axis=-1, keepdims=True) + jnp.sum(p_p, axis=-1, keepdims=True) + p_s
    return p_c, p_p, p_s, denom


def _att_specs(t):
    nb = t // ATT_BLOCK
    qblk = pl.BlockSpec((KV_GROUP, ATT_BLOCK, HEAD_DIM), lambda kv, i: (kv, i, 0))
    cur = pl.BlockSpec((None, ATT_BLOCK, HEAD_DIM), lambda kv, i: (kv, i, 0))
    prev = pl.BlockSpec((None, ATT_BLOCK, HEAD_DIM), lambda kv, i: (kv, jnp.maximum(i - 1, 0), 0))
    return nb, qblk, cur, prev, pl.BlockSpec(memory_space=pltpu.SMEM)


def attn_fwd(q, k, v, sinks):
    h, t, hd = q.shape
    nb, qblk, cur, prev, smem = _att_specs(t)

    def body(q_ref, kc_ref, kp_ref, vc_ref, vp_ref, sink_ref, o_ref):
        kv, i = pl.program_id(0), pl.program_id(1)
        mask_c, mask_p = _att_masks(i)
        kc, kp, vc, vp = kc_ref[...], kp_ref[...], vc_ref[...], vp_ref[...]
        for g in range(KV_GROUP):
            p_c, p_p, _, denom = _att_probs(q_ref[g], kc, kp, mask_c, mask_p, sink_ref[kv * KV_GROUP + g])
            o = (jnp.dot(p_c.astype(BF16), vc, preferred_element_type=F32)
                 + jnp.dot(p_p.astype(BF16), vp, preferred_element_type=F32)) / denom
            o_ref[g] = o.astype(o_ref.dtype)

    return pl.pallas_call(
        body, grid=(h // KV_GROUP, nb), in_specs=[qblk, cur, prev, cur, prev, smem], out_specs=qblk,
        out_shape=jax.ShapeDtypeStruct((h, t, hd), BF16), name="attn_fwd",
        compiler_params=_params("parallel", "parallel"))(q, k, k, v, v, sinks)


def attn_bwd(q, k, v, do, sinks):
    h, t, hd = q.shape
    kvh = h // KV_GROUP
    nb, qblk, cur, prev, smem = _att_specs(t)

    def body(q_ref, kc_ref, kp_ref, vc_ref, vp_ref, do_ref, sink_ref, dq_ref, dk_ref, dv_ref, dsk_ref):
        kv, i = pl.program_id(0), pl.program_id(1)

        @pl.when(i == 0)
        def _():
            dk_ref[...] = jnp.zeros_like(dk_ref)
            dv_ref[...] = jnp.zeros_like(dv_ref)
            dsk_ref[...] = jnp.zeros_like(dsk_ref)

        mask_c, mask_p = _att_masks(i)
        kc, kp, vc, vp = kc_ref[...], kp_ref[...], vc_ref[...], vp_ref[...]
        zero = jnp.zeros((ATT_BLOCK, hd), F32)
        dkc, dkp, dvc, dvp = zero, zero, zero, zero
        dsink_rows = []
        for g in range(KV_GROUP):
            q = q_ref[g]
            p_c, p_p, p_s, denom = _att_probs(q, kc, kp, mask_c, mask_p, sink_ref[kv * KV_GROUP + g])
            inv = 1.0 / denom
            pn_c, pn_p = p_c * inv, p_p * inv
            dob = do_ref[g].astype(BF16)
            dp_c = lax.dot_general(dob, vc, NT, preferred_element_type=F32)
            dp_p = lax.dot_general(dob, vp, NT, preferred_element_type=F32)
            dsum = jnp.sum(pn_c * dp_c, axis=-1, keepdims=True) + jnp.sum(pn_p * dp_p, axis=-1, keepdims=True)
            ds_c = (pn_c * (dp_c - dsum)).astype(BF16)
            ds_p = (pn_p * (dp_p - dsum)).astype(BF16)
            dq_ref[g] = (jnp.dot(ds_c, kc, preferred_element_type=F32) + jnp.dot(ds_p, kp, preferred_element_type=F32))
            dkc = dkc + lax.dot_general(ds_c, q, TN, preferred_element_type=F32)
            dkp = dkp + lax.dot_general(ds_p, q, TN, preferred_element_type=F32)
            dvc = dvc + lax.dot_general(pn_c.astype(BF16), dob, TN, preferred_element_type=F32)
            dvp = dvp + lax.dot_general(pn_p.astype(BF16), dob, TN, preferred_element_type=F32)
            dsink = -jnp.sum(p_s * inv * dsum, axis=0, keepdims=True)
            dsink_rows.append(jnp.broadcast_to(dsink, (1, 128)))
        here = pl.ds(pl.multiple_of(i * ATT_BLOCK, ATT_BLOCK), ATT_BLOCK)
        before = pl.ds(pl.multiple_of(jnp.maximum(i - 1, 0) * ATT_BLOCK, ATT_BLOCK), ATT_BLOCK)
        dk_ref[here, :] += dkc
        dv_ref[here, :] += dvc
        dk_ref[before, :] += dkp
        dv_ref[before, :] += dvp
        dsk_ref[...] += jnp.concatenate(dsink_rows, axis=0)

    whole = pl.BlockSpec((None, t, hd), lambda kv, i: (kv, 0, 0))
    return pl.pallas_call(
        body, grid=(kvh, nb), in_specs=[qblk, cur, prev, cur, prev, qblk, smem],
        out_specs=(qblk, whole, whole, pl.BlockSpec((None, KV_GROUP, 128), lambda kv, i: (kv, 0, 0))),
        out_shape=(jax.ShapeDtypeStruct((h, t, hd), F32), jax.ShapeDtypeStruct((kvh, t, hd), F32),
                   jax.ShapeDtypeStruct((kvh, t, hd), F32), jax.ShapeDtypeStruct((kvh, KV_GROUP, 128), F32)),
        name="attn_bwd", compiler_params=_params("parallel", "arbitrary"))(q, k, k, v, v, do, sinks)


def _ffn_taps(w, b):
    f2 = w.shape[1]
    return w.reshape(FFN_TAPS, 2, f2 // 2).transpose(1, 0, 2), b.reshape(2, 1, f2 // 2)


def ffn_fwd(x, W, p, tables=None, token=None):
    h, ht = rms_fwd(x, W[p + "ffn_norm_g"], BF16, token, transposed=True)
    u2 = mm_nn_cols(h, W[p + "ffn_w_up"], split=True, out_dtype=BF16, name="ffn_up")
    w3, b2 = _ffn_taps(W[p + "ffn_dw_w"], W[p + "ffn_dw_b"])
    a, at = ffn_gate_fwd(u2, w3, b2)
    y = mm_nn(a, W[p + "ffn_w_down"], out_dtype=F32, resid=x, name="ffn_down")
    return y, (x, ht, u2, at)


def ffn_bwd(saved, W, p, dy, tables=None):
    x, ht, u2, at = saved
    dyf, dyb = dy
    w3, b2 = _ffn_taps(W[p + "ffn_dw_w"], W[p + "ffn_dw_b"])
    grads = {p + "ffn_w_down": mm_wgrad(at, dyb, out_dtype=BF16, name="ffn_down_dw")}
    da = mm_nt(dyb, W[p + "ffn_w_down"], out_dtype=BF16, name="ffn_down_dx")
    du2, dw3, db2 = ffn_gate_bwd(u2, da, w3, b2)
    grads[p + "ffn_dw_w"] = dw3.transpose(1, 0, 2).reshape(FFN_TAPS, -1)
    grads[p + "ffn_dw_b"] = db2.reshape(-1)
    grads[p + "ffn_w_up"] = mm_wgrad_cols(ht, du2, split=True, out_dtype=BF16, name="ffn_up_dw")
    dh = mm_nt_cols(du2, W[p + "ffn_w_up"], split=True, out_dtype=F32, name="ffn_up_dx")
    return (x, p + "ffn_norm_g", dh), grads


def conf_fwd(x, W, p, tables=None, token=None):
    d = x.shape[1]
    h, ht = rms_fwd(x, W[p + "norm_g"], BF16, token, transposed=True)
    u2 = mm_nn_cols(h, W[p + "a_w_in"], split=True, out_dtype=BF16, bias=W[p + "a_b_in"], name="conf_in")
    c = glu_conv_fwd(u2, W[p + "a_dw_w"], W[p + "a_dw_b"].reshape(1, d))
    s, st = ln_silu_fwd(c, W[p + "a_ln_g"], W[p + "a_ln_b"])
    y = mm_nn(s, W[p + "a_w_out"], out_dtype=F32, bias=W[p + "a_b_out"], resid=x, name="conf_out")
    return y, (x, ht, u2, c, st)


def conf_bwd(saved, W, p, dy, tables=None):
    x, ht, u2, c, st = saved
    dyf, dyb = dy
    grads = {p + "a_w_out": mm_wgrad(st, dyb, out_dtype=BF16, name="conf_out_dw"), p + "a_b_out": col_sum(dyf).reshape(-1)}
    ds = mm_nt(dyb, W[p + "a_w_out"], out_dtype=BF16, name="conf_out_dx")
    dc, dlg, dlb = ln_silu_bwd(c, W[p + "a_ln_g"], W[p + "a_ln_b"], ds)
    grads[p + "a_ln_g"], grads[p + "a_ln_b"] = dlg.reshape(-1), dlb.reshape(-1)
    du2, ddw, ddwb, dbin = glu_conv_bwd(u2, dc, W[p + "a_dw_w"])
    grads[p + "a_dw_w"], grads[p + "a_dw_b"], grads[p + "a_b_in"] = ddw, ddwb.reshape(-1), dbin.reshape(-1)
    grads[p + "a_w_in"] = mm_wgrad_cols(ht, du2, split=True, out_dtype=BF16, name="conf_in_dw")
    dh = mm_nt_cols(du2, W[p + "a_w_in"], split=True, out_dtype=F32, name="conf_in_dx")
    return (x, p + "norm_g", dh), grads


def pool_layer_fwd(x, W, p, tables=None, token=None):
    h = rms_fwd(x, W[p + "norm_g"], F32, token)
    mixed, mixed_t = pool_fwd(h)
    y, ypre = mm_groups(mixed, W[p + "b_w_group"], mode="nn", out_dtype=F32, scale=W[p + "b_scale"], resid=x,
                        raw_dtype=F32, name="pool_mix")
    return y, (x, mixed_t, ypre)


def pool_layer_bwd(saved, W, p, dy, tables=None):
    x, mixed_t, ypre = saved
    dyf, dyb = dy
    dyp, dscale = scale_bwd(dyf, ypre, W[p + "b_scale"])
    grads = {p + "b_scale": dscale.reshape(-1),
             p + "b_w_group": mm_groups_wgrad(mixed_t, dyp, POOL_GROUPS, out_dtype=BF16, name="pool_mix_dw")}
    dmix = mm_groups(dyp, W[p + "b_w_group"], mode="nt", out_dtype=F32, name="pool_mix_dx")
    dh = pool_bwd(dmix)
    return (x, p + "norm_g", dh), grads


def _heads(a, n):
    t = a.shape[0]
    return a.reshape(t, n, HEAD_DIM).transpose(1, 0, 2)


def _unheads(a):
    n, t, _ = a.shape
    return a.transpose(1, 0, 2).reshape(t, n * HEAD_DIM)


def attn_layer_fwd(x, W, p, tables, token=None):
    d = x.shape[1]
    nh = d // HEAD_DIM
    nkv = nh // KV_GROUP
    cosf, sinf, pmat = tables
    h, ht = rms_fwd(x, W[p + "norm_g"], BF16, token, transposed=True)
    qkv = mm_nn_cols(h, W[p + "c_w_qkv"], split=False, out_dtype=F32, name="att_qkv")
    q = _heads(qkv[:, :d], nh)
    k = _heads(qkv[:, d:d + nkv * HEAD_DIM], nkv)
    v = _heads(qkv[:, d + nkv * HEAD_DIM:], nkv).astype(BF16)
    qr = qk_rope_fwd(q, W[p + "c_q_norm_g"], cosf, sinf, pmat, HEAD_DIM ** -0.5)
    kr = qk_rope_fwd(k, W[p + "c_k_norm_g"], cosf, sinf, pmat, 1.0)
    o = attn_fwd(qr, kr, v, W[p + "c_sinks"])
    o2 = _unheads(o)
    y = mm_nn(o2, W[p + "c_w_o"], out_dtype=F32, resid=x, name="att_out")
    return y, (x, ht, q, k, v, qr, kr, o2)


def attn_layer_bwd(saved, W, p, dy, tables):
    x, ht, q, k, v, qr, kr, o2 = saved
    dyf, dyb = dy
    cosf, sinf, pmat = tables
    nh = q.shape[0]
    grads = {p + "c_w_o": mm_wgrad(o2.T, dyb, out_dtype=BF16, name="att_out_dw")}
    do = _heads(mm_nt(dyb, W[p + "c_w_o"], out_dtype=BF16, name="att_out_dx"), nh)
    dqr, dkr, dv, dsk = attn_bwd(qr, kr, v, do, W[p + "c_sinks"])
    grads[p + "c_sinks"] = dsk[:, :, 0].reshape(-1)
    dq, dqg = qk_rope_bwd(dqr, q, W[p + "c_q_norm_g"], cosf, sinf, pmat.T, HEAD_DIM ** -0.5)
    dk, dkg = qk_rope_bwd(dkr, k, W[p + "c_k_norm_g"], cosf, sinf, pmat.T, 1.0)
    grads[p + "c_q_norm_g"], grads[p + "c_k_norm_g"] = dqg.reshape(-1), dkg.reshape(-1)
    dqkv = jnp.concatenate([_unheads(dq), _unheads(dk), _unheads(dv)], axis=1).astype(BF16)
    grads[p + "c_w_qkv"] = mm_wgrad_cols(ht, dqkv, split=False, out_dtype=BF16, name="att_qkv_dw")
    dh = mm_nt_cols(dqkv, W[p + "c_w_qkv"], split=False, out_dtype=F32, name="att_qkv_dx")
    return (x, p + "norm_g", dh), grads


def local_step(x, positions, tgt, W, comm=None):
    tables = rope_tables(positions)
    saved = []
    for g, (fwd, _, p) in enumerate(SUBLAYERS):
        token = comm.forward_begins(g, W) if comm else None
        x, s = fwd(x, W, p, tables, token)
        saved.append(s)
        if comm:
            comm.forward_ends(g, x, W)
    dyf, dyb, sq = loss_grad(x, tgt)
    loss = 0.5 * jnp.sum(sq) / x.shape[1]
    grads = {}
    for g in reversed(range(len(SUBLAYERS))):
        _, bwd, p = SUBLAYERS[g]
        (xin, gain, dh), gr = bwd(saved[g], W, p, (dyf, dyb), tables)
        token = comm.gradients_ready(g, gr) if comm else None
        dyf, dyb, dg = rms_bwd(xin, W[gain], dh, dyf, token)
        gr[gain] = dg.reshape(-1)
        grads.update(gr)
    return loss, dyf, grads


SUBLAYERS = [(conf_fwd, conf_bwd, "l0_"), (ffn_fwd, ffn_bwd, "l0_"), (pool_layer_fwd, pool_layer_bwd, "l1_"),
             (ffn_fwd, ffn_bwd, "l1_"), (attn_layer_fwd, attn_layer_bwd, "l2_"), (ffn_fwd, ffn_bwd, "l2_"),
             (conf_fwd, conf_bwd, "l3_"), (ffn_fwd, ffn_bwd, "l3_")]
SUBLAYER_WEIGHTS = {conf_fwd: ("a_w_in", "a_w_out", "a_dw_w"), ffn_fwd: ("ffn_w_up", "ffn_w_down", "ffn_dw_w"),
                    pool_layer_fwd: ("b_w_group",), attn_layer_fwd: ("c_w_qkv", "c_w_o")}


def sublayer_weight_names(g):
    fwd, _, p = SUBLAYERS[g]
    return [p + n for n in SUBLAYER_WEIGHTS[fwd]]


ANY = pl.BlockSpec(memory_space=pl.ANY)


def _place():
    x, y, c = lax.axis_index("x"), lax.axis_index("y"), lax.axis_index("c")
    chips = [(1 - x, y), (x, 1 - y), (1 - x, 1 - y)]
    return x, y, c, 2 * x + y, (x, y, 1 - c), chips


def _half(rows, which):
    return pl.ds(which * (rows // 2), rows // 2)


def place_block(shard, chip_core, out_dtype):
    rows, cols = shard.shape
    tr = rows
    for cand in (512, 256, 128, 64, 32, 16):
        if rows % cand == 0 and cand * cols * 4 <= (2 << 20):
            tr = cand
            break

    def body(pos_ref, s_ref, o_ref):
        o_ref[...] = s_ref[...].astype(o_ref.dtype)

    grid_spec = pltpu.PrefetchScalarGridSpec(
        num_scalar_prefetch=1, grid=(rows // tr,), in_specs=[pl.BlockSpec((tr, cols), lambda i, pos: (i, 0))],
        out_specs=pl.BlockSpec((None, tr, cols), lambda i, pos: (pos[0], i, 0)))
    return pl.pallas_call(body, grid_spec=grid_spec, out_shape=jax.ShapeDtypeStruct((N_CHIPS, rows, cols), out_dtype),
                          name="place_block", compiler_params=_params("parallel"))(chip_core, shard)


def all_gather_chips(bufs):
    n = len(bufs)

    def body(*refs):
        outs = refs[n:2 * n]
        ici_send, ici_recv, d2d_send, d2d_recv = refs[2 * n:]
        x, y, c, k, sibling, chips = _place()

        def rdma(src, dst, send, recv, dev):
            return pltpu.make_async_remote_copy(src_ref=src, dst_ref=dst, send_sem=send, recv_sem=recv,
                                                device_id=dev, device_id_type=MESH)

        sends = []
        for t in range(n):
            rows = bufs[t].shape[1]
            for j, (px, py) in enumerate(chips):
                mine = outs[t].at[k, _half(rows, c)]
                sends.append(rdma(mine, mine, ici_send.at[t, j], ici_recv.at[t, j], (px, py, c)))
        for cp in sends:
            cp.start()
        for t in range(n):
            rows = bufs[t].shape[1]
            for j, (px, py) in enumerate(chips):
                landed = outs[t].at[2 * px + py, _half(rows, c)]
                rdma(landed, landed, ici_send.at[t, j], ici_recv.at[t, j], sibling).wait_recv()
                fwd = rdma(landed, landed, d2d_send.at[t, j], d2d_recv.at[t, j], sibling)
                fwd.start()
                sends.append(fwd)
        for t in range(n):
            rows = bufs[t].shape[1]
            for j, (px, py) in enumerate(chips):
                other = outs[t].at[2 * px + py, _half(rows, 1 - c)]
                rdma(other, other, d2d_send.at[t, j], d2d_recv.at[t, j], sibling).wait_recv()
        for cp in sends:
            cp.wait_send()

    return pl.pallas_call(
        body, in_specs=[ANY] * n, out_specs=[ANY] * n,
        out_shape=[jax.ShapeDtypeStruct(b.shape, b.dtype) for b in bufs],
        input_output_aliases={t: t for t in range(n)},
        scratch_shapes=[pltpu.SemaphoreType.DMA((n, 3))] * 4,
        name="all_gather_chips", compiler_params=pltpu.CompilerParams())(*bufs)


def gather_sibling_halves(ss):
    n = len(ss)

    def body(*refs):
        outs = refs[n:2 * n]
        send, recv = refs[2 * n:]
        x, y, c, k, sibling, chips = _place()
        cps = []
        for t in range(n):
            rows = ss[t].shape[0]
            mine = outs[t].at[_half(rows, c)]
            cps.append(pltpu.make_async_remote_copy(src_ref=mine, dst_ref=mine, send_sem=send.at[t], recv_sem=recv.at[t],
                                                    device_id=sibling, device_id_type=MESH))
        for cp in cps:
            cp.start()
        for t in range(n):
            rows = ss[t].shape[0]
            got = outs[t].at[_half(rows, 1 - c)]
            pltpu.make_async_remote_copy(src_ref=got, dst_ref=got, send_sem=send.at[t], recv_sem=recv.at[t],
                                         device_id=sibling, device_id_type=MESH).wait_recv()
        for cp in cps:
            cp.wait_send()

    return pl.pallas_call(
        body, in_specs=[ANY] * n, out_specs=[ANY] * n,
        out_shape=[jax.ShapeDtypeStruct(s.shape, s.dtype) for s in ss],
        input_output_aliases={t: t for t in range(n)},
        scratch_shapes=[pltpu.SemaphoreType.DMA((n,))] * 2, name="gather_sibling_halves",
        compiler_params=pltpu.CompilerParams())(*ss)


def _sum_rows_tile(rows):
    return _pick(rows, (256, 352, 128, 64, 32, 16))


def add_sibling_half(g, land, core):
    nb, half, cols = land.shape
    tr = _sum_rows_tile(half)
    nrb = half // tr

    def body(c_ref, g_ref, l_ref, o_ref):
        o_ref[...] = (g_ref[...].astype(F32) + l_ref[...].astype(F32)).astype(o_ref.dtype)

    spec = pl.BlockSpec((None, tr, cols), lambda b, i, c_ref: (b, i, 0))
    grid_spec = pltpu.PrefetchScalarGridSpec(
        num_scalar_prefetch=1, grid=(nb, nrb),
        in_specs=[pl.BlockSpec((None, tr, cols), lambda b, i, c_ref: (b, c_ref[1] * nrb + i, 0)), spec], out_specs=spec)
    return pl.pallas_call(body, grid_spec=grid_spec, out_shape=jax.ShapeDtypeStruct(land.shape, BF16),
                          name="add_sibling_half", compiler_params=_params("parallel", "parallel"))(core, g, land)


def sum_chip_blocks(p, l2, chip_core):
    nb, half, cols = l2.shape
    tr = _sum_rows_tile(half)
    nrb = half // tr

    def body(pos_ref, p_ref, l_ref, o_ref):
        acc = p_ref[...].astype(F32)
        for b in range(nb):
            acc = acc + l_ref[b].astype(F32)
        o_ref[...] = acc

    grid_spec = pltpu.PrefetchScalarGridSpec(
        num_scalar_prefetch=1, grid=(nrb,),
        in_specs=[pl.BlockSpec((None, tr, cols), lambda i, pos: (pos[0], i, 0)),
                  pl.BlockSpec((nb, tr, cols), lambda i, pos: (0, i, 0))],
        out_specs=pl.BlockSpec((tr, cols), lambda i, pos: (pos[1] * nrb + i, 0)))
    return pl.pallas_call(body, grid_spec=grid_spec, out_shape=jax.ShapeDtypeStruct((2 * half, cols), F32),
                          name="sum_chip_blocks", compiler_params=_params("parallel"))(chip_core, p, l2)


HBM_SPEC = pl.BlockSpec(memory_space=pltpu.HBM)
SEM_SPEC = pl.BlockSpec(memory_space=pltpu.SEMAPHORE)
SPLIT_EFFECT = pltpu.SideEffectType.DATAFLOW_SIDE_EFFECTING


def _in_hbm(v):
    return pltpu.with_memory_space_constraint(v, pltpu.HBM)


def _gather_ici_copies(bufs, refs, send, recv):
    x, y, c, k, sibling, chips = _place()
    cps = []
    for t in range(len(bufs)):
        rows = bufs[t].shape[1]
        for j, (px, py) in enumerate(chips):
            cps.append(pltpu.make_async_remote_copy(
                src_ref=refs[t].at[k, _half(rows, c)], dst_ref=refs[t].at[k, _half(rows, c)],
                send_sem=send.at[3 * t + j], recv_sem=recv.at[3 * t + j], device_id=(px, py, c), device_id_type=MESH))
    return cps


def gather_ici_start(bufs, after, name):
    n = len(bufs)

    def body(*refs):
        send, recv, token = refs[n + 1], refs[n + 2], refs[-1]
        for cp in _gather_ici_copies(bufs, refs[:n], send, recv):
            cp.start()
        token[...] = jnp.zeros_like(token)

    outs = pl.pallas_call(
        body, name=name, in_specs=[HBM_SPEC] * n + [ANY],
        out_shape=(pltpu.SemaphoreType.DMA((3 * n,)), pltpu.SemaphoreType.DMA((3 * n,)),
                   *[pltpu.HBM(b.shape, b.dtype) for b in bufs], jax.ShapeDtypeStruct(TOKEN_SHAPE, F32)),
        out_specs=(SEM_SPEC, SEM_SPEC, *[HBM_SPEC] * n, pl.BlockSpec(memory_space=pltpu.VMEM)),
        input_output_aliases={t: 2 + t for t in range(n)},
        compiler_params=pltpu.CompilerParams(has_side_effects=SPLIT_EFFECT))(*[_in_hbm(b) for b in bufs], after)
    return outs[0], outs[1], list(outs[2:2 + n]), outs[-1]


def gather_ici_wait(send, recv, bufs, after, name):
    n = len(bufs)

    def body(*refs):
        x, y, c, k, sibling, chips = _place()
        for t in range(n):
            rows = bufs[t].shape[1]
            for j, (px, py) in enumerate(chips):
                cp = pltpu.make_async_remote_copy(
                    src_ref=refs[t].at[k, _half(rows, c)], dst_ref=refs[t].at[2 * px + py, _half(rows, c)],
                    send_sem=refs[n].at[3 * t + j], recv_sem=refs[n + 1].at[3 * t + j], device_id=(px, py, c),
                    device_id_type=MESH)
                cp.wait_send()
                cp.wait_recv()

    return list(pl.pallas_call(
        body, name=name, in_specs=[HBM_SPEC] * n + [SEM_SPEC, SEM_SPEC, ANY],
        out_shape=tuple(pltpu.HBM(b.shape, b.dtype) for b in bufs), out_specs=tuple([HBM_SPEC] * n),
        input_output_aliases={t: t for t in range(n)},
        compiler_params=pltpu.CompilerParams(has_side_effects=SPLIT_EFFECT))(*bufs, send, recv, after))


def gather_forward_sibling(bufs):
    n = len(bufs)

    def body(*refs):
        outs = refs[n:2 * n]
        send, recv = refs[2 * n:]
        x, y, c, k, sibling, chips = _place()
        cps = []
        for t in range(n):
            rows = bufs[t].shape[1]
            for j, (px, py) in enumerate(chips):
                landed = outs[t].at[2 * px + py, _half(rows, c)]
                cps.append(pltpu.make_async_remote_copy(src_ref=landed, dst_ref=landed, send_sem=send.at[t, j],
                                                        recv_sem=recv.at[t, j], device_id=sibling, device_id_type=MESH))
        for cp in cps:
            cp.start()
        for t in range(n):
            rows = bufs[t].shape[1]
            for j, (px, py) in enumerate(chips):
                other = outs[t].at[2 * px + py, _half(rows, 1 - c)]
                pltpu.make_async_remote_copy(src_ref=other, dst_ref=other, send_sem=send.at[t, j], recv_sem=recv.at[t, j],
                                             device_id=sibling, device_id_type=MESH).wait_recv()
        for cp in cps:
            cp.wait_send()

    return pl.pallas_call(
        body, in_specs=[ANY] * n, out_specs=[ANY] * n, out_shape=[jax.ShapeDtypeStruct(b.shape, b.dtype) for b in bufs],
        input_output_aliases={t: t for t in range(n)}, scratch_shapes=[pltpu.SemaphoreType.DMA((n, 3))] * 2,
        name="gather_forward_sibling", compiler_params=pltpu.CompilerParams())(*bufs)


def _sibling_copies(gs, src_refs, dst_refs, send, recv):
    x, y, c, k, sibling, chips = _place()
    return [pltpu.make_async_remote_copy(
        src_ref=src_refs[t].at[:, _half(gs[t].shape[1], 1 - c), :], dst_ref=dst_refs[t], send_sem=send.at[t],
        recv_sem=recv.at[t], device_id=sibling, device_id_type=MESH) for t in range(len(gs))]


def sibling_start(gs, after, name):
    n = len(gs)
    lands = [lax.empty((g.shape[0], g.shape[1] // 2, g.shape[2]), g.dtype) for g in gs]

    def body(*refs):
        send, recv, token = refs[2 * n + 1], refs[2 * n + 2], refs[-1]
        for cp in _sibling_copies(gs, refs[:n], refs[n:2 * n], send, recv):
            cp.start()
        token[...] = jnp.zeros_like(token)

    outs = pl.pallas_call(
        body, name=name, in_specs=[HBM_SPEC] * (2 * n) + [ANY],
        out_shape=(pltpu.SemaphoreType.DMA((n,)), pltpu.SemaphoreType.DMA((n,)),
                   *[pltpu.HBM(v.shape, v.dtype) for v in gs + lands], jax.ShapeDtypeStruct(TOKEN_SHAPE, F32)),
        out_specs=(SEM_SPEC, SEM_SPEC, *[HBM_SPEC] * (2 * n), pl.BlockSpec(memory_space=pltpu.VMEM)),
        input_output_aliases={t: 2 + t for t in range(2 * n)},
        compiler_params=pltpu.CompilerParams(has_side_effects=SPLIT_EFFECT))(*[_in_hbm(v) for v in gs + lands], after)
    return outs[0], outs[1], list(outs[2:2 + n]), list(outs[2 + n:2 + 2 * n]), outs[-1]


def sibling_wait(send, recv, gs, lands, after, name):
    n = len(gs)

    def body(*refs):
        for cp in _sibling_copies(gs, refs[:n], refs[n:2 * n], refs[2 * n], refs[2 * n + 1]):
            cp.wait_send()
            cp.wait_recv()

    outs = pl.pallas_call(
        body, name=name, in_specs=[HBM_SPEC] * (2 * n) + [SEM_SPEC, SEM_SPEC, ANY],
        out_shape=tuple(pltpu.HBM(v.shape, v.dtype) for v in gs + lands), out_specs=tuple([HBM_SPEC] * (2 * n)),
        input_output_aliases={t: t for t in range(2 * n)},
        compiler_params=pltpu.CompilerParams(has_side_effects=SPLIT_EFFECT))(*gs, *lands, send, recv, after)
    return list(outs[:n]), list(outs[n:])


def _reduce_ici_copies(ps, src_refs, dst_refs, send, recv):
    x, y, c, k, sibling, chips = _place()
    cps = []
    for t in range(len(ps)):
        for j, (px, py) in enumerate(chips):
            cps.append(pltpu.make_async_remote_copy(
                src_ref=src_refs[t].at[2 * px + py], dst_ref=dst_refs[t].at[j], send_sem=send.at[3 * t + j],
                recv_sem=recv.at[3 * t + j],
                device_id=(px, py, c), device_id_type=MESH))
    return cps


def reduce_ici_start(ps, name):
    n = len(ps)
    lands = [lax.empty((3,) + p.shape[1:], p.dtype) for p in ps]

    def body(*refs):
        send, recv, token = refs[2 * n], refs[2 * n + 1], refs[-1]
        for cp in _reduce_ici_copies(ps, refs[:n], refs[n:2 * n], send, recv):
            cp.start()
        token[...] = jnp.zeros_like(token)

    outs = pl.pallas_call(
        body, name=name, in_specs=[HBM_SPEC] * (2 * n),
        out_shape=(pltpu.SemaphoreType.DMA((3 * n,)), pltpu.SemaphoreType.DMA((3 * n,)),
                   *[pltpu.HBM(v.shape, v.dtype) for v in ps + lands], jax.ShapeDtypeStruct(TOKEN_SHAPE, F32)),
        out_specs=(SEM_SPEC, SEM_SPEC, *[HBM_SPEC] * (2 * n), pl.BlockSpec(memory_space=pltpu.VMEM)),
        input_output_aliases={t: 2 + t for t in range(2 * n)},
        compiler_params=pltpu.CompilerParams(has_side_effects=SPLIT_EFFECT))(*[_in_hbm(v) for v in ps + lands])
    return outs[0], outs[1], list(outs[2:2 + n]), list(outs[2 + n:2 + 2 * n]), outs[-1]


def reduce_ici_wait(send, recv, ps, lands, after, name):
    n = len(ps)

    def body(*refs):
        for cp in _reduce_ici_copies(ps, refs[:n], refs[n:2 * n], refs[2 * n], refs[2 * n + 1]):
            cp.wait_send()
            cp.wait_recv()

    outs = pl.pallas_call(
        body, name=name, in_specs=[HBM_SPEC] * (2 * n) + [SEM_SPEC, SEM_SPEC, ANY],
        out_shape=tuple(pltpu.HBM(v.shape, v.dtype) for v in ps + lands), out_specs=tuple([HBM_SPEC] * (2 * n)),
        input_output_aliases={t: t for t in range(2 * n)},
        compiler_params=pltpu.CompilerParams(has_side_effects=SPLIT_EFFECT))(*ps, *lands, send, recv, after)
    return list(outs[:n]), list(outs[n:])


SMALL_CHUNK_ROWS = 256


def all_reduce_small(v):
    rows = v.shape[0]
    nchunk = rows // SMALL_CHUNK_ROWS

    def body(v_ref, o_ref, buf, send, recv):
        x, y, c = lax.axis_index("x"), lax.axis_index("y"), lax.axis_index("c")
        me = 4 * x + 2 * y + c
        buf[me] = v_ref[...]
        cps = []
        for d in range(1, N_DEV):
            peer = (x ^ ((d >> 2) & 1), y ^ ((d >> 1) & 1), c ^ (d & 1))
            cps.append(pltpu.make_async_remote_copy(src_ref=v_ref, dst_ref=buf.at[me], send_sem=send.at[d - 1],
                                                    recv_sem=recv.at[d - 1], device_id=peer, device_id_type=MESH))
        for cp in cps:
            cp.start()
        for d in range(1, N_DEV):
            got = buf.at[me ^ d]
            pltpu.make_async_remote_copy(src_ref=got, dst_ref=got, send_sem=send.at[d - 1], recv_sem=recv.at[d - 1],
                                         device_id=(x, y, c), device_id_type=MESH).wait_recv()
        for cp in cps:
            cp.wait_send()

        def chunk(i, carry):
            sl = pl.ds(pl.multiple_of(i * SMALL_CHUNK_ROWS, SMALL_CHUNK_ROWS), SMALL_CHUNK_ROWS)
            acc = buf[0, sl, :]
            for s in range(1, N_DEV):
                acc = acc + buf[s, sl, :]
            o_ref[sl, :] = acc
            return carry

        lax.fori_loop(0, nchunk, chunk, 0)

    vmem = pl.BlockSpec(memory_space=pltpu.VMEM)
    return pl.pallas_call(
        body, in_specs=[vmem], out_specs=vmem, out_shape=jax.ShapeDtypeStruct(v.shape, F32),
        scratch_shapes=[pltpu.VMEM((N_DEV,) + v.shape, F32), pltpu.SemaphoreType.DMA((N_DEV - 1,)),
                        pltpu.SemaphoreType.DMA((N_DEV - 1,))],
        name="all_reduce_small",
        compiler_params=pltpu.CompilerParams(vmem_limit_bytes=VMEM_LIMIT_BYTES))(v)


def adamw(w, g, m, v):
    rows, cols = w.shape
    tr = rows
    for cand in (512, 256, 128, 64, 32, 16, 8):
        if rows % cand == 0 and cand * cols * 4 <= (1 << 20):
            tr = cand
            break
    c1 = 1.0 - ADAM_B1 ** ADAM_STEP
    c2 = 1.0 - ADAM_B2 ** ADAM_STEP

    def body(w_ref, g_ref, m_ref, v_ref, d_ref, nm_ref, nv_ref):
        gf = g_ref[...]
        nm = ADAM_B1 * m_ref[...] + (1.0 - ADAM_B1) * gf
        nv = ADAM_B2 * v_ref[...] + (1.0 - ADAM_B2) * (gf * gf)
        d_ref[...] = -ADAM_LR * ((nm / c1) / (jnp.sqrt(nv / c2) + ADAM_EPS) + ADAM_WD * w_ref[...])
        nm_ref[...] = nm
        nv_ref[...] = nv

    spec = pl.BlockSpec((tr, cols), lambda i: (i, 0))
    shape = jax.ShapeDtypeStruct((rows, cols), F32)
    return pl.pallas_call(body, grid=(rows // tr,), in_specs=[spec] * 4, out_specs=(spec,) * 3, out_shape=(shape,) * 3,
                          name="adamw", compiler_params=_params("parallel"))(w, g, m, v)


TAP_ROWS_ALIGN = 16
FLAT_ALIGN = 128 * SMALL_CHUNK_ROWS


def _pad_to(a, n):
    return jnp.pad(a, (0, n - a.shape[0]))


def _round_up(n, m):
    return (n + m - 1) // m * m


class Exchanges:
    def __init__(self, a, chip_core):
        self.a, self.chip_core = a, chip_core
        self.bufs = []
        for g in range(len(SUBLAYERS)):
            row = []
            for n in sublayer_weight_names(g):
                w = a[n].reshape(-1, a[n].shape[-1])
                if _kind(n) == "tap":
                    w = jnp.pad(w, ((0, _round_up(w.shape[0], TAP_ROWS_ALIGN) - w.shape[0]), (0, 0)))
                row.append(place_block(w, chip_core, F32 if _kind(n) == "tap" else BF16))
            self.bufs.append(row)
        self.started = None
        self.after = chip_core
        self.halves = None
        self.pending = []

    def _unpack(self, g, gathered, W):
        for n, v in zip(sublayer_weight_names(g), gathered):
            kind = _kind(n)
            if kind == "col":
                W[n] = v
            elif kind == "row":
                W[n] = v.reshape(-1, v.shape[-1])
            elif kind == "grp":
                grp, r, pg = self.a[n].shape
                W[n] = v.reshape(N_CHIPS, grp, r, pg).transpose(1, 0, 2, 3).reshape(grp, N_CHIPS * r, pg)
            else:
                nt = self.a[n].shape[0]
                W[n] = v[:, :nt].transpose(1, 0, 2).reshape(nt, -1)

    def gather_first(self, W):
        gathered = all_gather_chips(self.bufs[0])
        self._unpack(0, gathered, W)
        self.after = gathered[0]

    def forward_begins(self, g, W):
        if g + 1 == len(SUBLAYERS):
            return None
        send, recv, bufs, token = gather_ici_start(self.bufs[g + 1], self.after, f"gather_start_{g + 1}")
        self.started = (send, recv, bufs)
        return token

    def forward_ends(self, g, x, W):
        if g + 1 == len(SUBLAYERS):
            return
        send, recv, bufs = self.started
        gathered = gather_forward_sibling(gather_ici_wait(send, recv, bufs, x, f"gather_wait_{g + 1}"))
        self._unpack(g + 1, gathered, W)
        self.after = gathered[0]

    def gradients_ready(self, g, grads):
        names = [n for n in sublayer_weight_names(g) if _kind(n) != "tap"]
        gl = []
        for n in names:
            v, kind = grads.pop(n), _kind(n)
            if kind == "row":
                v = v.reshape(N_CHIPS, -1, v.shape[-1])
            elif kind == "grp":
                grp, r, pg = self.a[n].shape
                v = v.reshape(grp, N_CHIPS, r, pg).transpose(1, 0, 2, 3).reshape(N_CHIPS, grp * r, pg)
            gl.append(v)
        after = self._halves_to_ici(gl[0])
        send, recv, gl, lands, token = sibling_start(gl, gl[0] if after is None else after, f"sibling_start_{g}")
        self.halves = (g, names, send, recv, gl, lands)
        return token

    def _halves_to_ici(self, after):
        if self.halves is None:
            return None
        g, names, send, recv, gl, lands = self.halves
        gl, lands = sibling_wait(send, recv, gl, lands, after, f"sibling_wait_{g}")
        ps = [add_sibling_half(v, l, self.chip_core) for v, l in zip(gl, lands)]
        send, recv, ps, l2s, token = reduce_ici_start(ps, f"reduce_start_{g}")
        self.pending.append((g, names, send, recv, ps, l2s))
        self.halves = None
        return token

    def finish_reductions(self, after):
        self._halves_to_ici(after)
        names_all, sums = [], []
        for g, names, send, recv, ps, l2s in self.pending:
            ps, l2s = reduce_ici_wait(send, recv, ps, l2s, after, f"reduce_wait_{g}")
            sums += [sum_chip_blocks(p, l2, self.chip_core) for p, l2 in zip(ps, l2s)]
            names_all += names
        return dict(zip(names_all, gather_sibling_halves(sums)))


def train_step(a):
    x, positions, tgt = a["x"][0], a["positions"][0], a["loss_target"][0]
    mats = [n for n in WEIGHT_NAMES if _kind(n) in ("col", "row", "grp")]
    taps = [n for n in WEIGHT_NAMES if _kind(n) == "tap"]
    reps = [n for n in WEIGHT_NAMES if _kind(n) == "rep"]
    chip = 2 * lax.axis_index("x") + lax.axis_index("y")
    chip_core = jnp.stack([chip, lax.axis_index("c")]).astype(jnp.int32)

    comm = Exchanges(a, chip_core)
    W = {n: a[n] for n in reps}
    comm.gather_first(W)
    loss, dx, grads = local_step(x, positions, tgt, W, comm)
    loss = lax.psum(loss, ("x", "y", "c"))
    reduced = comm.finish_reductions(dx)

    n_rep = _round_up(sum(a[n].size for n in reps), FLAT_ALIGN)
    flat_rep = _pad_to(jnp.concatenate([grads[n].reshape(-1) for n in reps]), n_rep)
    flat_tap = jnp.concatenate([grads[n].reshape(-1) for n in taps])
    flat = jnp.concatenate([flat_rep, _pad_to(flat_tap, _round_up(flat_tap.shape[0], FLAT_ALIGN))])
    summed = all_reduce_small(flat.reshape(-1, 128))
    rep_rows = n_rep // 128
    tap_flat = summed[rep_rows:].reshape(-1)

    out = {}
    pack = lambda pre: _pad_to(jnp.concatenate([a[pre + n].reshape(-1) for n in reps]), n_rep).reshape(-1, 128)
    g_rep = summed[:rep_rows]
    d_rep, m_rep, v_rep = adamw(pack(""), g_rep, pack("m_"), pack("v_"))
    off = 0
    for n in reps:
        size, shape = a[n].size, a[n].shape
        out[n] = tuple(f.reshape(-1)[off:off + size].reshape(shape) for f in (g_rep, d_rep, m_rep, v_rep))
        off += size
    off = 0
    for n in taps:
        nt, cs = a[n].shape
        full = tap_flat[off:off + nt * cs * N_CHIPS].reshape(nt, cs * N_CHIPS)
        off += nt * cs * N_CHIPS
        g = lax.dynamic_slice(full, (0, chip * cs), (nt, cs))
        out[n] = (g,) + tuple(adamw(a[n], g, a["m_" + n], a["v_" + n]))
    for n in mats:
        shape = a[n].shape
        two_d = lambda t: t.reshape(-1, shape[-1])
        g = reduced[n]
        out[n] = (g.reshape(shape),) + tuple(t.reshape(shape) for t in adamw(two_d(a[n]), g, two_d(a["m_" + n]), two_d(a["v_" + n])))

    res = [loss, dx[None]]
    for part in range(4):
        res += [out[n][part] for n in WEIGHT_NAMES]
    return tuple(res)


def kernel(x, positions, l0_norm_g, l0_a_w_in, l0_a_b_in, l0_a_dw_w, l0_a_dw_b, l0_a_ln_g, l0_a_ln_b, l0_a_w_out, l0_a_b_out, l0_ffn_norm_g, l0_ffn_w_up, l0_ffn_dw_w, l0_ffn_dw_b, l0_ffn_w_down, l1_norm_g, l1_b_w_group, l1_b_scale, l1_ffn_norm_g, l1_ffn_w_up, l1_ffn_dw_w, l1_ffn_dw_b, l1_ffn_w_down, l2_norm_g, l2_c_w_qkv, l2_c_q_norm_g, l2_c_k_norm_g, l2_c_sinks, l2_c_w_o, l2_ffn_norm_g, l2_ffn_w_up, l2_ffn_dw_w, l2_ffn_dw_b, l2_ffn_w_down, l3_norm_g, l3_a_w_in, l3_a_b_in, l3_a_dw_w, l3_a_dw_b, l3_a_ln_g, l3_a_ln_b, l3_a_w_out, l3_a_b_out, l3_ffn_norm_g, l3_ffn_w_up, l3_ffn_dw_w, l3_ffn_dw_b, l3_ffn_w_down, loss_target, m_l0_norm_g, m_l0_a_w_in, m_l0_a_b_in, m_l0_a_dw_w, m_l0_a_dw_b, m_l0_a_ln_g, m_l0_a_ln_b, m_l0_a_w_out, m_l0_a_b_out, m_l0_ffn_norm_g, m_l0_ffn_w_up, m_l0_ffn_dw_w, m_l0_ffn_dw_b, m_l0_ffn_w_down, m_l1_norm_g, m_l1_b_w_group, m_l1_b_scale, m_l1_ffn_norm_g, m_l1_ffn_w_up, m_l1_ffn_dw_w, m_l1_ffn_dw_b, m_l1_ffn_w_down, m_l2_norm_g, m_l2_c_w_qkv, m_l2_c_q_norm_g, m_l2_c_k_norm_g, m_l2_c_sinks, m_l2_c_w_o, m_l2_ffn_norm_g, m_l2_ffn_w_up, m_l2_ffn_dw_w, m_l2_ffn_dw_b, m_l2_ffn_w_down, m_l3_norm_g, m_l3_a_w_in, m_l3_a_b_in, m_l3_a_dw_w, m_l3_a_dw_b, m_l3_a_ln_g, m_l3_a_ln_b, m_l3_a_w_out, m_l3_a_b_out, m_l3_ffn_norm_g, m_l3_ffn_w_up, m_l3_ffn_dw_w, m_l3_ffn_dw_b, m_l3_ffn_w_down, v_l0_norm_g, v_l0_a_w_in, v_l0_a_b_in, v_l0_a_dw_w, v_l0_a_dw_b, v_l0_a_ln_g, v_l0_a_ln_b, v_l0_a_w_out, v_l0_a_b_out, v_l0_ffn_norm_g, v_l0_ffn_w_up, v_l0_ffn_dw_w, v_l0_ffn_dw_b, v_l0_ffn_w_down, v_l1_norm_g, v_l1_b_w_group, v_l1_b_scale, v_l1_ffn_norm_g, v_l1_ffn_w_up, v_l1_ffn_dw_w, v_l1_ffn_dw_b, v_l1_ffn_w_down, v_l2_norm_g, v_l2_c_w_qkv, v_l2_c_q_norm_g, v_l2_c_k_norm_g, v_l2_c_sinks, v_l2_c_w_o, v_l2_ffn_norm_g, v_l2_ffn_w_up, v_l2_ffn_dw_w, v_l2_ffn_dw_b, v_l2_ffn_w_down, v_l3_norm_g, v_l3_a_w_in, v_l3_a_b_in, v_l3_a_dw_w, v_l3_a_dw_b, v_l3_a_ln_g, v_l3_a_ln_b, v_l3_a_w_out, v_l3_a_b_out, v_l3_ffn_norm_g, v_l3_ffn_w_up, v_l3_ffn_dw_w, v_l3_ffn_dw_b, v_l3_ffn_w_down):
    return train_step(dict(locals()))
```

```python
import functools

import jax
import jax.numpy as jnp
from jax import lax
from jax.experimental import pallas as pl
from jax.experimental.pallas import tpu as pltpu

F32 = jnp.float32
BF16 = jnp.bfloat16
EPS = 1e-6
HEAD_DIM = 64
KV_GROUP = 8
ATT_BLOCK = 128
ROT_DIM = 16
ROPE_THETA = 500000.0
POOL_GROUPS = 4
CONF_TAPS = 31
FFN_TAPS = 3
N_CHIPS = 4
N_DEV = 8
ADAM_LR, ADAM_B1, ADAM_B2, ADAM_EPS, ADAM_WD, ADAM_STEP = 0.001, 0.9, 0.999, 1e-08, 0.01, 10
VMEM_LIMIT_BYTES = 56 * 1024 * 1024
MESH = pl.DeviceIdType.MESH

CONF_NAMES = ["norm_g", "a_w_in", "a_b_in", "a_dw_w", "a_dw_b", "a_ln_g", "a_ln_b", "a_w_out", "a_b_out"]
FFN_NAMES = ["ffn_norm_g", "ffn_w_up", "ffn_dw_w", "ffn_dw_b", "ffn_w_down"]
POOL_NAMES = ["norm_g", "b_w_group", "b_scale"]
ATT_NAMES = ["norm_g", "c_w_qkv", "c_q_norm_g", "c_k_norm_g", "c_sinks", "c_w_o"]
WEIGHT_NAMES = ([f"l0_{n}" for n in CONF_NAMES + FFN_NAMES] + [f"l1_{n}" for n in POOL_NAMES + FFN_NAMES]
                + [f"l2_{n}" for n in ATT_NAMES + FFN_NAMES] + [f"l3_{n}" for n in CONF_NAMES + FFN_NAMES])
COL_SHARDED = ("a_w_in", "ffn_w_up", "c_w_qkv")
ROW_SHARDED = ("a_w_out", "ffn_w_down", "c_w_o")
TAP_SHARDED = ("a_dw_w", "ffn_dw_w")


def _kind(name):
    base = name[3:]
    if base in COL_SHARDED:
        return "col"
    if base in ROW_SHARDED:
        return "row"
    if base in TAP_SHARDED:
        return "tap"
    if base == "b_w_group":
        return "grp"
    return "rep"


def _pick(n, prefs):
    for p in prefs:
        if p <= n and n % p == 0:
            return p
    return n


def _params(*sem):
    return pltpu.CompilerParams(dimension_semantics=sem, vmem_limit_bytes=VMEM_LIMIT_BYTES)


def _sigmoid(x):
    return 1.0 / (1.0 + jnp.exp(-x))


NN = (((1,), (0,)), ((), ()))
NT = (((1,), (1,)), ((), ()))
TN = (((0,), (0,)), ((), ()))


MM_VMEM_BUDGET = 44 * 1024 * 1024
MM_STEP_SECONDS = 0.35e-6
MM_FLOPS, MM_HBM_BYTES = 9.0e14, 3.0e12
TILE_SIZES = (4096, 2816, 2048, 1408, 1024, 704, 640, 512, 256, 128)


def _tile_options(n, lane):
    opts = [c for c in TILE_SIZES if c <= n and n % c == 0 and (not lane or c % 128 == 0)]
    return opts or [n]


def _mm_plan(m, n, k, *, n_unit=None, k_unit=None, a_bytes=2, b_bytes=2, o_bytes=2, extra_bytes=0):
    best = None
    for tm in _tile_options(m, False):
        for tn in _tile_options(n_unit or n, True):
            for tk in _tile_options(k_unit or k, True) + ([k] if not k_unit else []):
                nk = k // tk
                vmem = 2 * (tm * tk * a_bytes + tk * tn * b_bytes + tm * tn * (o_bytes + extra_bytes)) + tm * tn * 4 * (2 if nk > 1 else 1)
                if vmem > MM_VMEM_BUDGET:
                    continue
                ni, nj = m // tm, n // tn
                a_all, b_all, o_all = m * k * a_bytes, k * n * b_bytes, m * n * (o_bytes + extra_bytes)
                for i_inner in (False, True):
                    if nk > 1:
                        traffic = a_all * nj + b_all * ni + o_all
                    elif i_inner:
                        traffic = a_all * nj + b_all + o_all
                    else:
                        traffic = a_all + b_all * ni + o_all
                    cost = ni * nj * nk * MM_STEP_SECONDS + max(2.0 * m * n * k / MM_FLOPS, traffic / MM_HBM_BYTES)
                    if best is None or cost < best[0]:
                        best = (cost, tm, tn, tk, i_inner)
    assert best is not None, (m, n, k)
    return best[1:]


def _mm(a, b, *, dims, sizes, plan, a_blk, a_idx, b_blk, b_idx, o_blk, o_idx, out_shape, name,
        bias=None, scale=None, vec_blk=None, vec_idx=None, resid=None, raw_shape=None):
    m, n, k = sizes
    tm, tn, tk, i_inner = plan
    ni, nj, nk = m // tm, n // tn, k // tk
    has_bias, has_scale, has_resid, want_raw = bias is not None, scale is not None, resid is not None, raw_shape is not None

    def body(*refs):
        a_ref, b_ref = refs[0], refs[1]
        pos = 2
        bias_ref = scale_ref = resid_ref = raw_ref = None
        if has_bias:
            bias_ref = refs[pos]; pos += 1
        if has_scale:
            scale_ref = refs[pos]; pos += 1
        if has_resid:
            resid_ref = refs[pos]; pos += 1
        o_ref = refs[pos]; pos += 1
        if want_raw:
            raw_ref = refs[pos]; pos += 1
        part = lax.dot_general(a_ref[...].astype(BF16), b_ref[...].astype(BF16), dims, preferred_element_type=F32)

        def finish(r):
            if want_raw:
                raw_ref[...] = r.astype(raw_ref.dtype)
            if has_bias:
                r = r + bias_ref[...]
            if has_scale:
                r = r * scale_ref[...]
            if has_resid:
                r = r + resid_ref[...]
            o_ref[...] = r.astype(o_ref.dtype)

        if nk == 1:
            finish(part)
        else:
            acc_ref = refs[pos]
            kk = pl.program_id(2)

            @pl.when(kk == 0)
            def _():
                acc_ref[...] = part

            @pl.when(kk > 0)
            def _():
                acc_ref[...] += part

            @pl.when(kk == nk - 1)
            def _():
                finish(acc_ref[...])

    order = (lambda f: (lambda j, i, kk: f(i, j, kk))) if i_inner else (lambda f: f)
    spec = lambda blk, idx: pl.BlockSpec(blk, order(idx))
    operands, in_specs = [a, b], [spec(a_blk, a_idx), spec(b_blk, b_idx)]
    for v in (bias, scale):
        if v is not None:
            operands.append(v); in_specs.append(spec(vec_blk, vec_idx))
    if has_resid:
        operands.append(resid); in_specs.append(spec(o_blk, o_idx))
    out_shapes, out_specs = out_shape, spec(o_blk, o_idx)
    if want_raw:
        out_shapes, out_specs = (out_shape, raw_shape), (spec(o_blk, o_idx), spec(o_blk, o_idx))
    return pl.pallas_call(
        body, grid=(nj, ni, nk) if i_inner else (ni, nj, nk), in_specs=in_specs, out_specs=out_specs,
        out_shape=out_shapes, scratch_shapes=[pltpu.VMEM((tm, tn), F32)] if nk > 1 else [], name=name,
        compiler_params=_params("parallel", "parallel", "arbitrary"))(*operands)


def mm_nn_cols(a, g, *, split, out_dtype, bias=None, name):
    t, k = a.shape
    ns = g.shape[2]
    n = N_CHIPS * ns
    plan = _mm_plan(t, n, k, n_unit=ns, o_bytes=jnp.dtype(out_dtype).itemsize)
    tm, tn, tk, _ = plan
    nj = ns // tn
    if split:
        o_blk, o_idx = (None, tm, tn), (lambda i, j, kk: (j // (2 * nj), i, j % (2 * nj)))
        out_shape = jax.ShapeDtypeStruct((2, t, 2 * ns), out_dtype)
        vec_blk, vec_idx = (None, 1, tn), (lambda i, j, kk: (j // (2 * nj), 0, j % (2 * nj)))
        if bias is not None:
            bias = bias.reshape(2, 1, 2 * ns)
    else:
        o_blk, o_idx = (tm, tn), (lambda i, j, kk: (i, j))
        out_shape = jax.ShapeDtypeStruct((t, n), out_dtype)
        vec_blk, vec_idx = (1, tn), (lambda i, j, kk: (0, j))
        if bias is not None:
            bias = bias.reshape(1, n)
    return _mm(a, g, dims=NN, sizes=(t, n, k), plan=plan, a_blk=(tm, tk), a_idx=lambda i, j, kk: (i, kk),
               b_blk=(None, tk, tn), b_idx=lambda i, j, kk: (j // nj, kk, j % nj), o_blk=o_blk, o_idx=o_idx,
               out_shape=out_shape, name=name, bias=bias, vec_blk=vec_blk, vec_idx=vec_idx)


def mm_nn(a, w, *, out_dtype, bias=None, scale=None, resid=None, raw_dtype=None, name):
    t, k = a.shape
    n = w.shape[1]
    extra = (4 if resid is not None else 0) + (0 if raw_dtype is None else jnp.dtype(raw_dtype).itemsize)
    plan = _mm_plan(t, n, k, a_bytes=a.dtype.itemsize, o_bytes=jnp.dtype(out_dtype).itemsize, extra_bytes=extra)
    tm, tn, tk, _ = plan
    raw_shape = None if raw_dtype is None else jax.ShapeDtypeStruct((t, n), raw_dtype)
    return _mm(a, w, dims=NN, sizes=(t, n, k), plan=plan, a_blk=(tm, tk), a_idx=lambda i, j, kk: (i, kk),
               b_blk=(tk, tn), b_idx=lambda i, j, kk: (kk, j), o_blk=(tm, tn), o_idx=lambda i, j, kk: (i, j),
               out_shape=jax.ShapeDtypeStruct((t, n), out_dtype), name=name,
               bias=None if bias is None else bias.reshape(1, n), scale=None if scale is None else scale.reshape(1, n),
               vec_blk=(1, tn), vec_idx=lambda i, j, kk: (0, j), resid=resid, raw_shape=raw_shape)


def mm_nt(dy, w, *, out_dtype, name):
    t, n = dy.shape
    kdim = w.shape[0]
    plan = _mm_plan(t, kdim, n, a_bytes=dy.dtype.itemsize, o_bytes=jnp.dtype(out_dtype).itemsize)
    tm, tn, tk, _ = plan
    return _mm(dy, w, dims=NT, sizes=(t, kdim, n), plan=plan, a_blk=(tm, tk), a_idx=lambda i, j, kk: (i, kk),
               b_blk=(tn, tk), b_idx=lambda i, j, kk: (j, kk), o_blk=(tm, tn), o_idx=lambda i, j, kk: (i, j),
               out_shape=jax.ShapeDtypeStruct((t, kdim), out_dtype), name=name)


def mm_nt_cols(du, g, *, split, out_dtype, name):
    kdim, ns = g.shape[1], g.shape[2]
    t = du.shape[1] if split else du.shape[0]
    plan = _mm_plan(t, kdim, N_CHIPS * ns, k_unit=ns, o_bytes=jnp.dtype(out_dtype).itemsize)
    tm, tn, tk, _ = plan
    nkb = ns // tk
    if split:
        a_blk, a_idx = (None, tm, tk), (lambda i, j, kk: (kk // (2 * nkb), i, kk % (2 * nkb)))
    else:
        a_blk, a_idx = (tm, tk), (lambda i, j, kk: (i, kk))
    return _mm(du, g, dims=NT, sizes=(t, kdim, N_CHIPS * ns), plan=plan, a_blk=a_blk, a_idx=a_idx,
               b_blk=(None, tn, tk), b_idx=lambda i, j, kk: (kk // nkb, j, kk % nkb),
               o_blk=(tm, tn), o_idx=lambda i, j, kk: (i, j),
               out_shape=jax.ShapeDtypeStruct((t, kdim), out_dtype), name=name)


def mm_wgrad(at, dy, *, out_dtype, name):
    return mm_nn(at, dy, out_dtype=out_dtype, name=name)


def mm_wgrad_cols(ht, du, *, split, out_dtype, name):
    kdim, t = ht.shape
    ns = (du.shape[2] // 2) if split else (du.shape[1] // N_CHIPS)
    plan = _mm_plan(kdim, N_CHIPS * ns, t, n_unit=ns, o_bytes=jnp.dtype(out_dtype).itemsize)
    tm, tn, tk, _ = plan
    nj = ns // tn
    if split:
        b_blk, b_idx = (None, tk, tn), (lambda i, j, kk: (j // (2 * nj), kk, j % (2 * nj)))
    else:
        b_blk, b_idx = (tk, tn), (lambda i, j, kk: (kk, j))
    return _mm(ht, du, dims=NN, sizes=(kdim, N_CHIPS * ns, t), plan=plan, a_blk=(tm, tk), a_idx=lambda i, j, kk: (i, kk),
               b_blk=b_blk, b_idx=b_idx, o_blk=(None, tm, tn), o_idx=lambda i, j, kk: (j // nj, i, j % nj),
               out_shape=jax.ShapeDtypeStruct((N_CHIPS, kdim, ns), out_dtype), name=name)


def _row_tile(t):
    return _pick(t, (256, 128))


def _acc_rows(ref, part, i):
    @pl.when(i == 0)
    def _():
        ref[...] = part

    @pl.when(i > 0)
    def _():
        ref[...] += part


TOKEN_SHAPE = (8, 128)


def _token_operand(token):
    if token is None:
        return [], []
    return [token], [pl.BlockSpec(TOKEN_SHAPE, lambda i: (0, 0))]


def rms_fwd(x, g, out_dtype, token=None, transposed=False):
    t, d = x.shape
    tr = _row_tile(t)

    def body(x_ref, g_ref, *rest):
        outs = rest[-2:] if transposed else rest[-1:]
        xf = x_ref[...]
        r = lax.rsqrt(jnp.mean(xf * xf, axis=-1, keepdims=True) + EPS)
        y = xf * r * g_ref[...]
        outs[0][...] = y.astype(outs[0].dtype)
        if transposed:
            outs[1][...] = y.T.astype(BF16)

    row = pl.BlockSpec((tr, d), lambda i: (i, 0))
    tok, tok_spec = _token_operand(token)
    out_specs, out_shape = row, jax.ShapeDtypeStruct((t, d), out_dtype)
    if transposed:
        out_specs = (row, pl.BlockSpec((d, tr), lambda i: (0, i)))
        out_shape = (out_shape, jax.ShapeDtypeStruct((d, t), BF16))
    return pl.pallas_call(body, grid=(t // tr,), in_specs=[row, pl.BlockSpec((1, d), lambda i: (0, 0))] + tok_spec,
                          out_specs=out_specs, out_shape=out_shape, name="rms_fwd",
                          compiler_params=_params("parallel"))(x, g.reshape(1, d), *tok)


def rms_bwd(x, g, dh, dres, token=None):
    t, d = x.shape
    tr = _row_tile(t)

    def body(x_ref, g_ref, dh_ref, dres_ref, *rest):
        dx_ref, dx16_ref, dg_ref = rest[-3:]
        i = pl.program_id(0)
        xf = x_ref[...]
        r = lax.rsqrt(jnp.mean(xf * xf, axis=-1, keepdims=True) + EPS)
        xhat = xf * r
        dhf = dh_ref[...].astype(F32)
        dxh = dhf * g_ref[...]
        m = jnp.mean(dxh * xhat, axis=-1, keepdims=True)
        dx = dres_ref[...] + r * (dxh - xhat * m)
        dx_ref[...] = dx
        dx16_ref[...] = dx.astype(BF16)
        _acc_rows(dg_ref, jnp.sum(dhf * xhat, axis=0, keepdims=True), i)

    row = pl.BlockSpec((tr, d), lambda i: (i, 0))
    vec = pl.BlockSpec((1, d), lambda i: (0, 0))
    tok, tok_spec = _token_operand(token)
    return pl.pallas_call(body, grid=(t // tr,), in_specs=[row, vec, row, row] + tok_spec, out_specs=(row, row, vec),
                          out_shape=(jax.ShapeDtypeStruct((t, d), F32), jax.ShapeDtypeStruct((t, d), BF16),
                                     jax.ShapeDtypeStruct((1, d), F32)),
                          name="rms_bwd", compiler_params=_params("arbitrary"))(x, g.reshape(1, d), dh, dres, *tok)


def ln_silu_fwd(c, g, b):
    t, d = c.shape
    tr = _row_tile(t)

    def body(c_ref, g_ref, b_ref, o_ref, ot_ref):
        xf = c_ref[...]
        mu = jnp.mean(xf, axis=-1, keepdims=True)
        xc = xf - mu
        var = jnp.mean(xc * xc, axis=-1, keepdims=True)
        n = xc * lax.rsqrt(var + EPS) * g_ref[...] + b_ref[...]
        s = n * _sigmoid(n)
        o_ref[...] = s.astype(o_ref.dtype)
        ot_ref[...] = s.T.astype(ot_ref.dtype)

    row = pl.BlockSpec((tr, d), lambda i: (i, 0))
    vec = pl.BlockSpec((1, d), lambda i: (0, 0))
    return pl.pallas_call(body, grid=(t // tr,), in_specs=[row, vec, vec],
                          out_specs=(row, pl.BlockSpec((d, tr), lambda i: (0, i))),
                          out_shape=(jax.ShapeDtypeStruct((t, d), BF16), jax.ShapeDtypeStruct((d, t), BF16)),
                          name="ln_silu_fwd", compiler_params=_params("parallel"))(c, g.reshape(1, d), b.reshape(1, d))


def ln_silu_bwd(c, g, b, ds):
    t, d = c.shape
    tr = _row_tile(t)

    def body(c_ref, g_ref, b_ref, ds_ref, dc_ref, dg_ref, db_ref):
        i = pl.program_id(0)
        xf = c_ref[...]
        mu = jnp.mean(xf, axis=-1, keepdims=True)
        xc = xf - mu
        var = jnp.mean(xc * xc, axis=-1, keepdims=True)
        rstd = lax.rsqrt(var + EPS)
        xhat = xc * rstd
        n = xhat * g_ref[...] + b_ref[...]
        sg = _sigmoid(n)
        dn = ds_ref[...].astype(F32) * (sg * (1.0 + n * (1.0 - sg)))
        dxh = dn * g_ref[...]
        m1 = jnp.mean(dxh, axis=-1, keepdims=True)
        m2 = jnp.mean(dxh * xhat, axis=-1, keepdims=True)
        dc_ref[...] = rstd * (dxh - m1 - xhat * m2)
        _acc_rows(dg_ref, jnp.sum(dn * xhat, axis=0, keepdims=True), i)
        _acc_rows(db_ref, jnp.sum(dn, axis=0, keepdims=True), i)

    row = pl.BlockSpec((tr, d), lambda i: (i, 0))
    vec = pl.BlockSpec((1, d), lambda i: (0, 0))
    vshape = jax.ShapeDtypeStruct((1, d), F32)
    return pl.pallas_call(body, grid=(t // tr,), in_specs=[row, vec, vec, row], out_specs=(row, vec, vec),
                          out_shape=(jax.ShapeDtypeStruct((t, d), F32), vshape, vshape), name="ln_silu_bwd",
                          compiler_params=_params("arbitrary"))(c, g.reshape(1, d), b.reshape(1, d), ds)


def loss_grad(y, tgt):
    t, d = y.shape
    tr = _row_tile(t)

    def body(y_ref, t_ref, dy_ref, dy16_ref, sq_ref):
        i = pl.program_id(0)
        err = y_ref[...] - t_ref[...]
        dy = err * (1.0 / d)
        dy_ref[...] = dy
        dy16_ref[...] = dy.astype(BF16)
        _acc_rows(sq_ref, jnp.sum(err * err, axis=0, keepdims=True), i)

    row = pl.BlockSpec((tr, d), lambda i: (i, 0))
    vec = pl.BlockSpec((1, d), lambda i: (0, 0))
    return pl.pallas_call(body, grid=(t // tr,), in_specs=[row, row], out_specs=(row, row, vec),
                          out_shape=(jax.ShapeDtypeStruct((t, d), F32), jax.ShapeDtypeStruct((t, d), BF16),
                                     jax.ShapeDtypeStruct((1, d), F32)),
                          name="loss_grad", compiler_params=_params("arbitrary"))(y, tgt)


def col_sum(a):
    t, d = a.shape
    tr = _row_tile(t)

    def body(a_ref, o_ref):
        _acc_rows(o_ref, jnp.sum(a_ref[...].astype(F32), axis=0, keepdims=True), pl.program_id(0))

    return pl.pallas_call(body, grid=(t // tr,), in_specs=[pl.BlockSpec((tr, d), lambda i: (i, 0))],
                          out_specs=pl.BlockSpec((1, d), lambda i: (0, 0)), out_shape=jax.ShapeDtypeStruct((1, d), F32),
                          name="col_sum", compiler_params=_params("arbitrary"))(a)


def scale_bwd(dx, ypre, scale):
    t, d = dx.shape
    tr = _row_tile(t)

    def body(dx_ref, y_ref, s_ref, dy_ref, ds_ref):
        dxf = dx_ref[...]
        dy_ref[...] = (dxf * s_ref[...]).astype(dy_ref.dtype)
        _acc_rows(ds_ref, jnp.sum(dxf * y_ref[...], axis=0, keepdims=True), pl.program_id(0))

    row = pl.BlockSpec((tr, d), lambda i: (i, 0))
    vec = pl.BlockSpec((1, d), lambda i: (0, 0))
    return pl.pallas_call(body, grid=(t // tr,), in_specs=[row, row, vec], out_specs=(row, vec),
                          out_shape=(jax.ShapeDtypeStruct((t, d), BF16), jax.ShapeDtypeStruct((1, d), F32)),
                          name="scale_bwd", compiler_params=_params("arbitrary"))(dx, ypre, scale.reshape(1, d))


def _chunk_rows(t):
    return _pick(t, (256, 128))


def _load_halo(ref, lead, base, rows, t, first, last, pre, post):
    idx = (lambda s, n: (pl.ds(s, n), slice(None))) if lead is None else (lambda s, n: (lead, pl.ds(s, n), slice(None)))
    parts = []
    if pre:
        start = pl.multiple_of(jnp.maximum(base - pre, 0), pre)
        parts.append(ref[idx(start, pre)].astype(F32) * jnp.where(first, 0.0, 1.0))
    parts.append(ref[idx(base, rows)].astype(F32))
    if post:
        start = pl.multiple_of(jnp.minimum(base + rows, t - post), post)
        parts.append(ref[idx(start, post)].astype(F32) * jnp.where(last, 0.0, 1.0))
    return parts[0] if len(parts) == 1 else jnp.concatenate(parts, axis=0)


def _fold8(x):
    r, c = x.shape
    return x.reshape(r // 8, 8, c).sum(axis=0)


def ffn_gate_fwd(u2, w3, b2):
    _, t, f = u2.shape
    tc = _pick(f, (256, 128))
    rows = _chunk_rows(t)
    nch = t // rows
    halo = 16

    def body(u_ref, w_ref, b_ref, a_ref, at_ref):
        def conv(p, base, first):
            xs = _load_halo(u_ref, p, base, rows, t, first, False, halo, 0)
            wp = w_ref[p]
            return (wp[0:1] * pltpu.roll(xs, 2, 0)[halo:] + wp[1:2] * pltpu.roll(xs, 1, 0)[halo:]
                    + wp[2:3] * xs[halo:] + b_ref[p])

        def chunk(i, carry):
            base = pl.multiple_of(i * rows, rows)
            gate, val = conv(0, base, i == 0), conv(1, base, i == 0)
            act = gate * _sigmoid(gate) * val
            act = act.astype(a_ref.dtype)
            a_ref[pl.ds(base, rows), :] = act
            at_ref[:, pl.ds(base, rows)] = act.T
            return carry

        lax.fori_loop(0, nch, chunk, 0)

    return pl.pallas_call(
        body, grid=(f // tc,),
        in_specs=[pl.BlockSpec((2, t, tc), lambda j: (0, 0, j)), pl.BlockSpec((2, 3, tc), lambda j: (0, 0, j)),
                  pl.BlockSpec((2, 1, tc), lambda j: (0, 0, j))],
        out_specs=(pl.BlockSpec((t, tc), lambda j: (0, j)), pl.BlockSpec((tc, t), lambda j: (j, 0))),
        out_shape=(jax.ShapeDtypeStruct((t, f), BF16), jax.ShapeDtypeStruct((f, t), BF16)),
        name="ffn_gate_fwd", compiler_params=_params("parallel"))(u2, w3, b2)


def ffn_gate_bwd(u2, da, w3, b2):
    _, t, f = u2.shape
    tc = _pick(f, (256, 128))
    rows = _chunk_rows(t)
    nch = t // rows
    halo = 16
    n = rows + 2 * halo

    def body(u_ref, da_ref, w_ref, b_ref, du_ref, dw_ref, db_ref, acc_ref):
        acc_ref[...] = jnp.zeros_like(acc_ref)

        def chunk(i, carry):
            base = pl.multiple_of(i * rows, rows)
            first, last = i == 0, i == nch - 1
            daf = jnp.concatenate(
                [jnp.zeros((halo, tc), F32), _load_halo(da_ref, None, base, rows, t, first, last, 0, halo)], axis=0)
            pre, shifted = [], []
            for p in range(2):
                xs = _load_halo(u_ref, p, base, rows, t, first, last, halo, halo)
                x1, x2 = pltpu.roll(xs, 1, 0), pltpu.roll(xs, 2, 0)
                wp = w_ref[p]
                pre.append(wp[0:1] * x2 + wp[1:2] * x1 + wp[2:3] * xs + b_ref[p])
                shifted.append((x2, x1, xs))
            gate, val = pre
            sg = _sigmoid(gate)
            d_pre = (daf * val * (sg * (1.0 + gate * (1.0 - sg))), daf * gate * sg)
            for p in range(2):
                dp = d_pre[p]
                wp = w_ref[p]
                du = wp[2:3] * dp + wp[1:2] * pltpu.roll(dp, n - 1, 0) + wp[0:1] * pltpu.roll(dp, n - 2, 0)
                du_ref[p, pl.ds(base, rows), :] = du[halo:halo + rows].astype(du_ref.dtype)
                own = dp[halo:halo + rows]
                for k in range(3):
                    acc_ref[p, k] += _fold8(own * shifted[p][k][halo:halo + rows])
                acc_ref[p, 3] += _fold8(own)
            return carry

        lax.fori_loop(0, nch, chunk, 0)
        for p in range(2):
            for k in range(3):
                dw_ref[p, k:k + 1, :] = jnp.sum(acc_ref[p, k], axis=0, keepdims=True)
            db_ref[p] = jnp.sum(acc_ref[p, 3], axis=0, keepdims=True)

    blk = pl.BlockSpec((2, t, tc), lambda j: (0, 0, j))
    wspec = pl.BlockSpec((2, 3, tc), lambda j: (0, 0, j))
    bspec = pl.BlockSpec((2, 1, tc), lambda j: (0, 0, j))
    return pl.pallas_call(
        body, grid=(f // tc,), in_specs=[blk, pl.BlockSpec((t, tc), lambda j: (0, j)), wspec, bspec],
        out_specs=(blk, wspec, bspec),
        out_shape=(jax.ShapeDtypeStruct((2, t, f), BF16), jax.ShapeDtypeStruct((2, 3, f), F32),
                   jax.ShapeDtypeStruct((2, 1, f), F32)),
        scratch_shapes=[pltpu.VMEM((2, 4, 8, tc), F32)], name="ffn_gate_bwd",
        compiler_params=_params("parallel"))(u2, da, w3, b2)


def glu_conv_fwd(u2, w, b):
    _, t, d = u2.shape
    taps = w.shape[0]
    tc = 128
    rows = _chunk_rows(t)
    nch = t // rows
    halo = 32

    def body(u_ref, w_ref, b_ref, c_ref):
        def chunk(i, carry):
            base = pl.multiple_of(i * rows, rows)
            a = _load_halo(u_ref, 0, base, rows, t, i == 0, False, halo, 0)
            g = _load_halo(u_ref, 1, base, rows, t, i == 0, False, halo, 0)
            xs = a * _sigmoid(g)
            acc = w_ref[taps - 1:taps, :] * xs[halo:] + b_ref[...]
            for j in range(taps - 1):
                acc = acc + w_ref[j:j + 1, :] * pltpu.roll(xs, taps - 1 - j, 0)[halo:]
            c_ref[pl.ds(base, rows), :] = acc
            return carry

        lax.fori_loop(0, nch, chunk, 0)

    return pl.pallas_call(
        body, grid=(d // tc,),
        in_specs=[pl.BlockSpec((2, t, tc), lambda j: (0, 0, j)), pl.BlockSpec((taps, tc), lambda j: (0, j)),
                  pl.BlockSpec((1, tc), lambda j: (0, j))],
        out_specs=pl.BlockSpec((t, tc), lambda j: (0, j)), out_shape=jax.ShapeDtypeStruct((t, d), F32),
        name="glu_conv_fwd", compiler_params=_params("parallel"))(u2, w, b)


def glu_conv_bwd(u2, dc, w):
    _, t, d = u2.shape
    taps = w.shape[0]
    tc = 128
    rows = _chunk_rows(t)
    nch = t // rows
    halo = 32
    n = rows + halo

    def body(u_ref, dc_ref, w_ref, du_ref, dw_ref, dwb_ref, dbin_ref, acc_ref, bacc_ref):
        acc_ref[...] = jnp.zeros_like(acc_ref)
        bacc_ref[...] = jnp.zeros_like(bacc_ref)

        def chunk(i, carry):
            base = pl.multiple_of(i * rows, rows)
            first, last = i == 0, i == nch - 1
            a = _load_halo(u_ref, 0, base, rows, t, first, False, halo, 0)
            g = _load_halo(u_ref, 1, base, rows, t, first, False, halo, 0)
            sg = _sigmoid(g)
            xs = a * sg
            dcs = _load_halo(dc_ref, None, base, rows, t, first, last, 0, halo)
            own = dcs[:rows]
            dglu = w_ref[taps - 1:taps, :] * own
            acc_ref[taps - 1] += _fold8(own * xs[halo:])
            for j in range(taps - 1):
                s = taps - 1 - j
                dglu = dglu + w_ref[j:j + 1, :] * pltpu.roll(dcs, n - s, 0)[:rows]
                acc_ref[j] += _fold8(own * pltpu.roll(xs, s, 0)[halo:])
            a_c, sg_c = a[halo:], sg[halo:]
            da = dglu * sg_c
            dg = dglu * a_c * sg_c * (1.0 - sg_c)
            du_ref[0, pl.ds(base, rows), :] = da.astype(du_ref.dtype)
            du_ref[1, pl.ds(base, rows), :] = dg.astype(du_ref.dtype)
            bacc_ref[0] += _fold8(own)
            bacc_ref[1] += _fold8(da)
            bacc_ref[2] += _fold8(dg)
            return carry

        lax.fori_loop(0, nch, chunk, 0)
        for j in range(taps):
            dw_ref[j:j + 1, :] = jnp.sum(acc_ref[j], axis=0, keepdims=True)
        dwb_ref[...] = jnp.sum(bacc_ref[0], axis=0, keepdims=True)
        dbin_ref[0] = jnp.sum(bacc_ref[1], axis=0, keepdims=True)
        dbin_ref[1] = jnp.sum(bacc_ref[2], axis=0, keepdims=True)

    blk = pl.BlockSpec((2, t, tc), lambda j: (0, 0, j))
    col = pl.BlockSpec((t, tc), lambda j: (0, j))
    return pl.pallas_call(
        body, grid=(d // tc,), in_specs=[blk, col, pl.BlockSpec((taps, tc), lambda j: (0, j))],
        out_specs=(blk, pl.BlockSpec((taps, tc), lambda j: (0, j)), pl.BlockSpec((1, tc), lambda j: (0, j)),
                   pl.BlockSpec((2, 1, tc), lambda j: (0, 0, j))),
        out_shape=(jax.ShapeDtypeStruct((2, t, d), BF16), jax.ShapeDtypeStruct((taps, d), F32),
                   jax.ShapeDtypeStruct((1, d), F32), jax.ShapeDtypeStruct((2, 1, d), F32)),
        scratch_shapes=[pltpu.VMEM((taps, 8, tc), F32), pltpu.VMEM((3, 8, tc), F32)], name="glu_conv_bwd",
        compiler_params=_params("parallel"))(u2, dc, w)


def _pool_select(grp, levels):
    out = levels[3]
    for k in (2, 1, 0):
        out = jnp.where(grp == k, levels[k], out)
    return out


def _pool_count(base, rows, tc, grp):
    tpos = (base + lax.broadcasted_iota(jnp.int32, (rows, tc), 0) + 1).astype(F32)
    window = jnp.left_shift(2, grp).astype(F32)
    return jnp.minimum(tpos, window)


def pool_fwd(h):
    t, d = h.shape
    pg = d // POOL_GROUPS
    tc = _pick(pg, (256, 128))
    rows = _chunk_rows(t)
    nch = t // rows
    halo = 16

    def body(h_ref, o_ref, ot_ref):
        grp = (pl.program_id(0) * tc) // pg

        def chunk(i, carry):
            base = pl.multiple_of(i * rows, rows)
            xs = _load_halo(h_ref, None, base, rows, t, i == 0, False, halo, 0)
            levels, cur = [], xs
            for k in range(4):
                cur = cur + pltpu.roll(cur, 1 << k, 0)
                levels.append(cur[halo:])
            pooled = _pool_select(grp, levels) / _pool_count(base, rows, tc, grp)
            mixed = pooled - xs[halo:]
            o_ref[pl.ds(base, rows), :] = mixed.astype(o_ref.dtype)
            ot_ref[:, pl.ds(base, rows)] = mixed.T.astype(ot_ref.dtype)
            return carry

        lax.fori_loop(0, nch, chunk, 0)

    col = pl.BlockSpec((t, tc), lambda j: (0, j))
    return pl.pallas_call(body, grid=(d // tc,), in_specs=[col], out_specs=(col, pl.BlockSpec((tc, t), lambda j: (j, 0))),
                          out_shape=(jax.ShapeDtypeStruct((t, d), BF16), jax.ShapeDtypeStruct((d, t), BF16)),
                          name="pool_fwd", compiler_params=_params("parallel"))(h)


def pool_bwd(dmix):
    t, d = dmix.shape
    pg = d // POOL_GROUPS
    tc = _pick(pg, (256, 128))
    rows = _chunk_rows(t)
    nch = t // rows
    halo = 16
    n = rows + halo

    def body(d_ref, o_ref):
        grp = (pl.program_id(0) * tc) // pg

        def chunk(i, carry):
            base = pl.multiple_of(i * rows, rows)
            ds = _load_halo(d_ref, None, base, rows, t, i == 0, i == nch - 1, 0, halo)
            levels, cur = [], ds / _pool_count(base, n, tc, grp)
            for k in range(4):
                cur = cur + pltpu.roll(cur, n - (1 << k), 0)
                levels.append(cur[:rows])
            o_ref[pl.ds(base, rows), :] = _pool_select(grp, levels) - ds[:rows]
            return carry

        lax.fori_loop(0, nch, chunk, 0)

    col = pl.BlockSpec((t, tc), lambda j: (0, j))
    return pl.pallas_call(body, grid=(d // tc,), in_specs=[col], out_specs=col,
                          out_shape=jax.ShapeDtypeStruct((t, d), F32), name="pool_bwd",
                          compiler_params=_params("parallel"))(dmix)


def mm_groups(a, wg, *, mode, out_dtype, scale=None, resid=None, raw_dtype=None, name):
    t, d = a.shape
    pg = wg.shape[1]
    tm = _pick(t, (1024, 512, 256, 128))
    tn = pg
    if mode == "nn":
        dims, b_blk, b_idx = NN, (None, pg, tn), (lambda i, j, kk: (j, 0, 0))
    else:
        dims, b_blk, b_idx = NT, (None, tn, pg), (lambda i, j, kk: (j, 0, 0))
    raw_shape = None if raw_dtype is None else jax.ShapeDtypeStruct((t, d), raw_dtype)
    return _mm(a, wg, dims=dims, sizes=(t, d, pg), plan=(tm, tn, pg, False), a_blk=(tm, pg), a_idx=lambda i, j, kk: (i, j),
               b_blk=b_blk, b_idx=b_idx, o_blk=(tm, tn), o_idx=lambda i, j, kk: (i, j),
               out_shape=jax.ShapeDtypeStruct((t, d), out_dtype), name=name,
               scale=None if scale is None else scale.reshape(1, d), vec_blk=(1, tn), vec_idx=lambda i, j, kk: (0, j),
               resid=resid, raw_shape=raw_shape)


def mm_groups_wgrad(at, dy, groups, *, out_dtype, name):
    d, t = at.shape
    pg = d // groups
    tk = _pick(t, (2048, 1024, 512, 256, 128))
    return _mm(at, dy, dims=NN, sizes=(d, pg, t), plan=(pg, pg, tk, False), a_blk=(pg, tk), a_idx=lambda i, j, kk: (i, kk),
               b_blk=(tk, pg), b_idx=lambda i, j, kk: (kk, i), o_blk=(None, pg, pg), o_idx=lambda i, j, kk: (i, 0, 0),
               out_shape=jax.ShapeDtypeStruct((groups, pg, pg), out_dtype), name=name)


def _split_dot(y, p):
    hi = y.astype(BF16)
    r1 = y - hi.astype(F32)
    mid = r1.astype(BF16)
    lo = (r1 - mid.astype(F32)).astype(BF16)
    pb = p.astype(BF16)
    dot = lambda v: jnp.dot(v, pb, preferred_element_type=F32)
    return (dot(hi) + dot(mid)) + dot(lo)


def rope_tables(positions):
    half = ROT_DIM // 2
    inv_freq = ROPE_THETA ** (-jnp.arange(0, ROT_DIM, 2, dtype=F32) / ROT_DIM)
    ang = positions.astype(F32)[:, None] * inv_freq
    t = positions.shape[0]
    cos, sin = jnp.cos(ang), jnp.sin(ang)
    rest = HEAD_DIM - ROT_DIM
    cosf = jnp.concatenate([cos, cos, jnp.ones((t, rest), F32)], axis=1)
    sinf = jnp.concatenate([-sin, sin, jnp.zeros((t, rest), F32)], axis=1)
    idx = jnp.arange(HEAD_DIM)
    partner = jnp.where(idx < half, idx + half, jnp.where(idx < ROT_DIM, idx - half, idx))
    pmat = (idx[:, None] == partner[None, :]).astype(F32)
    return cosf, sinf, pmat


def qk_rope_fwd(x, g, cosf, sinf, pmat, out_scale):
    hn, t, hd = x.shape
    tq = _pick(t, (512, 256, 128))

    def body(x_ref, g_ref, c_ref, s_ref, p_ref, o_ref):
        xf = x_ref[...]
        r = lax.rsqrt(jnp.mean(xf * xf, axis=-1, keepdims=True) + EPS)
        y = xf * r * g_ref[...]
        rot = y * c_ref[...] + _split_dot(y, p_ref[...]) * s_ref[...]
        o_ref[...] = (rot * out_scale).astype(o_ref.dtype)

    blk = pl.BlockSpec((None, tq, hd), lambda h, i: (h, i, 0))
    tab = pl.BlockSpec((tq, hd), lambda h, i: (i, 0))
    return pl.pallas_call(
        body, grid=(hn, t // tq),
        in_specs=[blk, pl.BlockSpec((1, hd), lambda h, i: (0, 0)), tab, tab, pl.BlockSpec((hd, hd), lambda h, i: (0, 0))],
        out_specs=blk, out_shape=jax.ShapeDtypeStruct((hn, t, hd), BF16), name="qk_rope_fwd",
        compiler_params=_params("parallel", "parallel"))(x, g.reshape(1, hd), cosf, sinf, pmat)


def qk_rope_bwd(dy, x, g, cosf, sinf, pmat_t, in_scale):
    hn, t, hd = x.shape
    tq = _pick(t, (512, 256, 128))

    def body(dy_ref, x_ref, g_ref, c_ref, s_ref, p_ref, dx_ref, dg_ref):
        step = pl.program_id(0) * pl.num_programs(1) + pl.program_id(1)
        dr = dy_ref[...] * in_scale
        dyn = dr * c_ref[...] + _split_dot(dr * s_ref[...], p_ref[...])
        xf = x_ref[...]
        r = lax.rsqrt(jnp.mean(xf * xf, axis=-1, keepdims=True) + EPS)
        xhat = xf * r
        dxh = dyn * g_ref[...]
        m = jnp.mean(dxh * xhat, axis=-1, keepdims=True)
        dx_ref[...] = r * (dxh - xhat * m)
        _acc_rows(dg_ref, jnp.sum(dyn * xhat, axis=0, keepdims=True), step)

    blk = pl.BlockSpec((None, tq, hd), lambda h, i: (h, i, 0))
    tab = pl.BlockSpec((tq, hd), lambda h, i: (i, 0))
    vec = pl.BlockSpec((1, hd), lambda h, i: (0, 0))
    return pl.pallas_call(
        body, grid=(hn, t // tq),
        in_specs=[blk, blk, vec, tab, tab, pl.BlockSpec((hd, hd), lambda h, i: (0, 0))],
        out_specs=(blk, vec), out_shape=(jax.ShapeDtypeStruct((hn, t, hd), F32), jax.ShapeDtypeStruct((1, hd), F32)),
        name="qk_rope_bwd", compiler_params=_params("arbitrary", "arbitrary"))(dy, x, g.reshape(1, hd), cosf, sinf, pmat_t)


NEG_BIG = -1e30


ATT_ROWS = KV_GROUP * ATT_BLOCK


def _att_mask(i):
    shape = (ATT_ROWS, 2 * ATT_BLOCK)
    qi = jnp.bitwise_and(lax.broadcasted_iota(jnp.int32, shape, 0), ATT_BLOCK - 1)
    kj = lax.broadcasted_iota(jnp.int32, shape, 1)
    cur = jnp.logical_and(kj >= ATT_BLOCK, kj - ATT_BLOCK <= qi)
    prev = jnp.logical_and(jnp.logical_and(kj < ATT_BLOCK, kj > qi), i > 0)
    return jnp.logical_or(cur, prev)


def _att_sinks(sink_ref, kv):
    return jnp.concatenate([jnp.full((ATT_BLOCK, 1), sink_ref[kv * KV_GROUP + g], F32) for g in range(KV_GROUP)], axis=0)


def _att_probs(q, k2, mask, sink):
    s = jnp.where(mask, lax.dot_general(q, k2, NT, preferred_element_type=F32), NEG_BIG)
    m = jnp.maximum(jnp.max(s, axis=-1, keepdims=True), sink)
    p = jnp.exp(s - m)
    p_s = jnp.exp(sink - m)
    return p, p_s, jnp.sum(p, axis=-1, keepdims=True) + p_s


def _att_specs(t):
    nb = t // ATT_BLOCK
    qblk = pl.BlockSpec((KV_GROUP, ATT_BLOCK, HEAD_DIM), lambda kv, i: (kv, i, 0))
    cur = pl.BlockSpec((None, ATT_BLOCK, HEAD_DIM), lambda kv, i: (kv, i, 0))
    prev = pl.BlockSpec((None, ATT_BLOCK, HEAD_DIM), lambda kv, i: (kv, jnp.maximum(i - 1, 0), 0))
    return nb, qblk, cur, prev, pl.BlockSpec(memory_space=pltpu.SMEM)


def attn_fwd(q, k, v, sinks):
    h, t, hd = q.shape
    nb, qblk, cur, prev, smem = _att_specs(t)

    def body(q_ref, kc_ref, kp_ref, vc_ref, vp_ref, sink_ref, o_ref):
        kv, i = pl.program_id(0), pl.program_id(1)
        k2 = jnp.concatenate([kp_ref[...], kc_ref[...]], axis=0)
        v2 = jnp.concatenate([vp_ref[...], vc_ref[...]], axis=0)
        p, _, denom = _att_probs(q_ref[...].reshape(ATT_ROWS, hd), k2, _att_mask(i), _att_sinks(sink_ref, kv))
        o = jnp.dot(p.astype(BF16), v2, preferred_element_type=F32) / denom
        o_ref[...] = o.reshape(KV_GROUP, ATT_BLOCK, hd).astype(o_ref.dtype)

    return pl.pallas_call(
        body, grid=(h // KV_GROUP, nb), in_specs=[qblk, cur, prev, cur, prev, smem], out_specs=qblk,
        out_shape=jax.ShapeDtypeStruct((h, t, hd), BF16), name="attn_fwd",
        compiler_params=_params("parallel", "parallel"))(q, k, k, v, v, sinks)


def attn_bwd(q, k, v, do, sinks):
    h, t, hd = q.shape
    kvh = h // KV_GROUP
    nb, qblk, cur, prev, smem = _att_specs(t)

    def body(q_ref, kc_ref, kp_ref, vc_ref, vp_ref, do_ref, sink_ref, dq_ref, dk_ref, dv_ref, dsk_ref):
        kv, i = pl.program_id(0), pl.program_id(1)

        @pl.when(i == 0)
        def _():
            dk_ref[...] = jnp.zeros_like(dk_ref)
            dv_ref[...] = jnp.zeros_like(dv_ref)
            dsk_ref[...] = jnp.zeros_like(dsk_ref)

        k2 = jnp.concatenate([kp_ref[...], kc_ref[...]], axis=0)
        v2 = jnp.concatenate([vp_ref[...], vc_ref[...]], axis=0)
        q = q_ref[...].reshape(ATT_ROWS, hd)
        p, p_s, denom = _att_probs(q, k2, _att_mask(i), _att_sinks(sink_ref, kv))
        inv = 1.0 / denom
        pn = p * inv
        dob = do_ref[...].reshape(ATT_ROWS, hd).astype(BF16)
        dp = lax.dot_general(dob, v2, NT, preferred_element_type=F32)
        dsum = jnp.sum(pn * dp, axis=-1, keepdims=True)
        ds = (pn * (dp - dsum)).astype(BF16)
        dq_ref[...] = jnp.dot(ds, k2, preferred_element_type=F32).reshape(KV_GROUP, ATT_BLOCK, hd)
        dk2 = lax.dot_general(ds, q, TN, preferred_element_type=F32)
        dv2 = lax.dot_general(pn.astype(BF16), dob, TN, preferred_element_type=F32)
        dsink = p_s * inv * dsum
        dsink_rows = [jnp.broadcast_to(-jnp.sum(dsink[g * ATT_BLOCK:(g + 1) * ATT_BLOCK], axis=0, keepdims=True), (1, 128))
                      for g in range(KV_GROUP)]
        here = pl.ds(pl.multiple_of(i * ATT_BLOCK, ATT_BLOCK), ATT_BLOCK)
        before = pl.ds(pl.multiple_of(jnp.maximum(i - 1, 0) * ATT_BLOCK, ATT_BLOCK), ATT_BLOCK)
        dk_ref[before, :] += dk2[:ATT_BLOCK]
        dv_ref[before, :] += dv2[:ATT_BLOCK]
        dk_ref[here, :] += dk2[ATT_BLOCK:]
        dv_ref[here, :] += dv2[ATT_BLOCK:]
        dsk_ref[...] += jnp.concatenate(dsink_rows, axis=0)

    whole = pl.BlockSpec((None, t, hd), lambda kv, i: (kv, 0, 0))
    return pl.pallas_call(
        body, grid=(kvh, nb), in_specs=[qblk, cur, prev, cur, prev, qblk, smem],
        out_specs=(qblk, whole, whole, pl.BlockSpec((None, KV_GROUP, 128), lambda kv, i: (kv, 0, 0))),
        out_shape=(jax.ShapeDtypeStruct((h, t, hd), F32), jax.ShapeDtypeStruct((kvh, t, hd), F32),
                   jax.ShapeDtypeStruct((kvh, t, hd), F32), jax.ShapeDtypeStruct((kvh, KV_GROUP, 128), F32)),
        name="attn_bwd", compiler_params=_params("parallel", "arbitrary"))(q, k, k, v, v, do, sinks)


def _ffn_taps(w, b):
    f2 = w.shape[1]
    return w.reshape(FFN_TAPS, 2, f2 // 2).transpose(1, 0, 2), b.reshape(2, 1, f2 // 2)


def ffn_fwd(x, W, p, tables=None, token=None):
    h, ht = rms_fwd(x, W[p + "ffn_norm_g"], BF16, token, transposed=True)
    u2 = mm_nn_cols(h, W[p + "ffn_w_up"], split=True, out_dtype=BF16, name="ffn_up")
    w3, b2 = _ffn_taps(W[p + "ffn_dw_w"], W[p + "ffn_dw_b"])
    a, at = ffn_gate_fwd(u2, w3, b2)
    y = mm_nn(a, W[p + "ffn_w_down"], out_dtype=F32, resid=x, name="ffn_down")
    return y, (x, ht, u2, at)


def ffn_bwd(saved, W, p, dy, tables=None):
    x, ht, u2, at = saved
    dyf, dyb = dy
    w3, b2 = _ffn_taps(W[p + "ffn_dw_w"], W[p + "ffn_dw_b"])
    grads = {p + "ffn_w_down": mm_wgrad(at, dyb, out_dtype=BF16, name="ffn_down_dw")}
    da = mm_nt(dyb, W[p + "ffn_w_down"], out_dtype=BF16, name="ffn_down_dx")
    du2, dw3, db2 = ffn_gate_bwd(u2, da, w3, b2)
    grads[p + "ffn_dw_w"] = dw3.transpose(1, 0, 2).reshape(FFN_TAPS, -1)
    grads[p + "ffn_dw_b"] = db2.reshape(-1)
    grads[p + "ffn_w_up"] = mm_wgrad_cols(ht, du2, split=True, out_dtype=BF16, name="ffn_up_dw")
    dh = mm_nt_cols(du2, W[p + "ffn_w_up"], split=True, out_dtype=F32, name="ffn_up_dx")
    return (x, p + "ffn_norm_g", dh), grads


def conf_fwd(x, W, p, tables=None, token=None):
    d = x.shape[1]
    h, ht = rms_fwd(x, W[p + "norm_g"], BF16, token, transposed=True)
    u2 = mm_nn_cols(h, W[p + "a_w_in"], split=True, out_dtype=BF16, bias=W[p + "a_b_in"], name="conf_in")
    c = glu_conv_fwd(u2, W[p + "a_dw_w"], W[p + "a_dw_b"].reshape(1, d))
    s, st = ln_silu_fwd(c, W[p + "a_ln_g"], W[p + "a_ln_b"])
    y = mm_nn(s, W[p + "a_w_out"], out_dtype=F32, bias=W[p + "a_b_out"], resid=x, name="conf_out")
    return y, (x, ht, u2, c, st)


def conf_bwd(saved, W, p, dy, tables=None):
    x, ht, u2, c, st = saved
    dyf, dyb = dy
    grads = {p + "a_w_out": mm_wgrad(st, dyb, out_dtype=BF16, name="conf_out_dw"), p + "a_b_out": col_sum(dyf).reshape(-1)}
    ds = mm_nt(dyb, W[p + "a_w_out"], out_dtype=BF16, name="conf_out_dx")
    dc, dlg, dlb = ln_silu_bwd(c, W[p + "a_ln_g"], W[p + "a_ln_b"], ds)
    grads[p + "a_ln_g"], grads[p + "a_ln_b"] = dlg.reshape(-1), dlb.reshape(-1)
    du2, ddw, ddwb, dbin = glu_conv_bwd(u2, dc, W[p + "a_dw_w"])
    grads[p + "a_dw_w"], grads[p + "a_dw_b"], grads[p + "a_b_in"] = ddw, ddwb.reshape(-1), dbin.reshape(-1)
    grads[p + "a_w_in"] = mm_wgrad_cols(ht, du2, split=True, out_dtype=BF16, name="conf_in_dw")
    dh = mm_nt_cols(du2, W[p + "a_w_in"], split=True, out_dtype=F32, name="conf_in_dx")
    return (x, p + "norm_g", dh), grads


def pool_layer_fwd(x, W, p, tables=None, token=None):
    h = rms_fwd(x, W[p + "norm_g"], F32, token)
    mixed, mixed_t = pool_fwd(h)
    y, ypre = mm_groups(mixed, W[p + "b_w_group"], mode="nn", out_dtype=F32, scale=W[p + "b_scale"], resid=x,
                        raw_dtype=F32, name="pool_mix")
    return y, (x, mixed_t, ypre)


def pool_layer_bwd(saved, W, p, dy, tables=None):
    x, mixed_t, ypre = saved
    dyf, dyb = dy
    dyp, dscale = scale_bwd(dyf, ypre, W[p + "b_scale"])
    grads = {p + "b_scale": dscale.reshape(-1),
             p + "b_w_group": mm_groups_wgrad(mixed_t, dyp, POOL_GROUPS, out_dtype=BF16, name="pool_mix_dw")}
    dmix = mm_groups(dyp, W[p + "b_w_group"], mode="nt", out_dtype=F32, name="pool_mix_dx")
    dh = pool_bwd(dmix)
    return (x, p + "norm_g", dh), grads


def _heads(a, n):
    t = a.shape[0]
    return a.reshape(t, n, HEAD_DIM).transpose(1, 0, 2)


def _unheads(a):
    n, t, _ = a.shape
    return a.transpose(1, 0, 2).reshape(t, n * HEAD_DIM)


def attn_layer_fwd(x, W, p, tables, token=None):
    d = x.shape[1]
    nh = d // HEAD_DIM
    nkv = nh // KV_GROUP
    cosf, sinf, pmat = tables
    h, ht = rms_fwd(x, W[p + "norm_g"], BF16, token, transposed=True)
    qkv = mm_nn_cols(h, W[p + "c_w_qkv"], split=False, out_dtype=F32, name="att_qkv")
    q = _heads(qkv[:, :d], nh)
    k = _heads(qkv[:, d:d + nkv * HEAD_DIM], nkv)
    v = _heads(qkv[:, d + nkv * HEAD_DIM:], nkv).astype(BF16)
    qr = qk_rope_fwd(q, W[p + "c_q_norm_g"], cosf, sinf, pmat, HEAD_DIM ** -0.5)
    kr = qk_rope_fwd(k, W[p + "c_k_norm_g"], cosf, sinf, pmat, 1.0)
    o = attn_fwd(qr, kr, v, W[p + "c_sinks"])
    o2 = _unheads(o)
    y = mm_nn(o2, W[p + "c_w_o"], out_dtype=F32, resid=x, name="att_out")
    return y, (x, ht, q, k, v, qr, kr, o2)


def attn_layer_bwd(saved, W, p, dy, tables):
    x, ht, q, k, v, qr, kr, o2 = saved
    dyf, dyb = dy
    cosf, sinf, pmat = tables
    nh = q.shape[0]
    grads = {p + "c_w_o": mm_wgrad(o2.T, dyb, out_dtype=BF16, name="att_out_dw")}
    do = _heads(mm_nt(dyb, W[p + "c_w_o"], out_dtype=BF16, name="att_out_dx"), nh)
    dqr, dkr, dv, dsk = attn_bwd(qr, kr, v, do, W[p + "c_sinks"])
    grads[p + "c_sinks"] = dsk[:, :, 0].reshape(-1)
    dq, dqg = qk_rope_bwd(dqr, q, W[p + "c_q_norm_g"], cosf, sinf, pmat.T, HEAD_DIM ** -0.5)
    dk, dkg = qk_rope_bwd(dkr, k, W[p + "c_k_norm_g"], cosf, sinf, pmat.T, 1.0)
    grads[p + "c_q_norm_g"], grads[p + "c_k_norm_g"] = dqg.reshape(-1), dkg.reshape(-1)
    dqkv = jnp.concatenate([_unheads(dq), _unheads(dk), _unheads(dv)], axis=1).astype(BF16)
    grads[p + "c_w_qkv"] = mm_wgrad_cols(ht, dqkv, split=False, out_dtype=BF16, name="att_qkv_dw")
    dh = mm_nt_cols(dqkv, W[p + "c_w_qkv"], split=False, out_dtype=F32, name="att_qkv_dx")
    return (x, p + "norm_g", dh), grads


def local_step(x, positions, tgt, W, comm=None):
    tables = rope_tables(positions)
    saved = []
    for g, (fwd, _, p) in enumerate(SUBLAYERS):
        token = comm.forward_begins(g, W) if comm else None
        x, s = fwd(x, W, p, tables, token)
        saved.append(s)
        if comm:
            comm.forward_ends(g, x, W)
    dyf, dyb, sq = loss_grad(x, tgt)
    loss = 0.5 * jnp.sum(sq) / x.shape[1]
    grads = {}
    for g in reversed(range(len(SUBLAYERS))):
        _, bwd, p = SUBLAYERS[g]
        (xin, gain, dh), gr = bwd(saved[g], W, p, (dyf, dyb), tables)
        token = comm.gradients_ready(g, gr) if comm else None
        dyf, dyb, dg = rms_bwd(xin, W[gain], dh, dyf, token)
        gr[gain] = dg.reshape(-1)
        grads.update(gr)
    return loss, dyf, grads


SUBLAYERS = [(conf_fwd, conf_bwd, "l0_"), (ffn_fwd, ffn_bwd, "l0_"), (pool_layer_fwd, pool_layer_bwd, "l1_"),
             (ffn_fwd, ffn_bwd, "l1_"), (attn_layer_fwd, attn_layer_bwd, "l2_"), (ffn_fwd, ffn_bwd, "l2_"),
             (conf_fwd, conf_bwd, "l3_"), (ffn_fwd, ffn_bwd, "l3_")]
SUBLAYER_WEIGHTS = {conf_fwd: ("a_w_in", "a_w_out", "a_dw_w"), ffn_fwd: ("ffn_w_up", "ffn_w_down", "ffn_dw_w"),
                    pool_layer_fwd: ("b_w_group",), attn_layer_fwd: ("c_w_qkv", "c_w_o")}


def sublayer_weight_names(g):
    fwd, _, p = SUBLAYERS[g]
    return [p + n for n in SUBLAYER_WEIGHTS[fwd]]


ANY = pl.BlockSpec(memory_space=pl.ANY)


def _place():
    x, y, c = lax.axis_index("x"), lax.axis_index("y"), lax.axis_index("c")
    chips = [(1 - x, y), (x, 1 - y), (1 - x, 1 - y)]
    return x, y, c, 2 * x + y, (x, y, 1 - c), chips


def _half(rows, which):
    return pl.ds(which * (rows // 2), rows // 2)


def place_block(shard, chip_core, out_dtype):
    rows, cols = shard.shape
    tr = rows
    for cand in (512, 256, 128, 64, 32, 16):
        if rows % cand == 0 and cand * cols * 4 <= (2 << 20):
            tr = cand
            break

    def body(pos_ref, s_ref, o_ref):
        o_ref[...] = s_ref[...].astype(o_ref.dtype)

    grid_spec = pltpu.PrefetchScalarGridSpec(
        num_scalar_prefetch=1, grid=(rows // tr,), in_specs=[pl.BlockSpec((tr, cols), lambda i, pos: (i, 0))],
        out_specs=pl.BlockSpec((None, tr, cols), lambda i, pos: (pos[0], i, 0)))
    return pl.pallas_call(body, grid_spec=grid_spec, out_shape=jax.ShapeDtypeStruct((N_CHIPS, rows, cols), out_dtype),
                          name="place_block", compiler_params=_params("parallel"))(chip_core, shard)


def all_gather_chips(bufs):
    n = len(bufs)

    def body(*refs):
        outs = refs[n:2 * n]
        ici_send, ici_recv, d2d_send, d2d_recv = refs[2 * n:]
        x, y, c, k, sibling, chips = _place()

        def rdma(src, dst, send, recv, dev):
            return pltpu.make_async_remote_copy(src_ref=src, dst_ref=dst, send_sem=send, recv_sem=recv,
                                                device_id=dev, device_id_type=MESH)

        sends = []
        for t in range(n):
            rows = bufs[t].shape[1]
            for j, (px, py) in enumerate(chips):
                mine = outs[t].at[k, _half(rows, c)]
                sends.append(rdma(mine, mine, ici_send.at[t, j], ici_recv.at[t, j], (px, py, c)))
        for cp in sends:
            cp.start()
        for t in range(n):
            rows = bufs[t].shape[1]
            for j, (px, py) in enumerate(chips):
                landed = outs[t].at[2 * px + py, _half(rows, c)]
                rdma(landed, landed, ici_send.at[t, j], ici_recv.at[t, j], sibling).wait_recv()
                fwd = rdma(landed, landed, d2d_send.at[t, j], d2d_recv.at[t, j], sibling)
                fwd.start()
                sends.append(fwd)
        for t in range(n):
            rows = bufs[t].shape[1]
            for j, (px, py) in enumerate(chips):
                other = outs[t].at[2 * px + py, _half(rows, 1 - c)]
                rdma(other, other, d2d_send.at[t, j], d2d_recv.at[t, j], sibling).wait_recv()
        for cp in sends:
            cp.wait_send()

    return pl.pallas_call(
        body, in_specs=[ANY] * n, out_specs=[ANY] * n,
        out_shape=[jax.ShapeDtypeStruct(b.shape, b.dtype) for b in bufs],
        input_output_aliases={t: t for t in range(n)},
        scratch_shapes=[pltpu.SemaphoreType.DMA((n, 3))] * 4,
        name="all_gather_chips", compiler_params=pltpu.CompilerParams())(*bufs)


def _sum_rows_tile(rows):
    return _pick(rows, (256, 352, 128, 64, 32, 16))


def add_sibling_half(g, land, core):
    nb, half, cols = land.shape
    tr = _sum_rows_tile(half)
    nrb = half // tr

    def body(c_ref, g_ref, l_ref, o_ref):
        o_ref[...] = (g_ref[...].astype(F32) + l_ref[...].astype(F32)).astype(o_ref.dtype)

    spec = pl.BlockSpec((None, tr, cols), lambda b, i, c_ref: (b, i, 0))
    grid_spec = pltpu.PrefetchScalarGridSpec(
        num_scalar_prefetch=1, grid=(nb, nrb),
        in_specs=[pl.BlockSpec((None, tr, cols), lambda b, i, c_ref: (b, c_ref[1] * nrb + i, 0)), spec], out_specs=spec)
    return pl.pallas_call(body, grid_spec=grid_spec, out_shape=jax.ShapeDtypeStruct(land.shape, BF16),
                          name="add_sibling_half", compiler_params=_params("parallel", "parallel"))(core, g, land)


def sum_chip_blocks(p, l2, chip_core):
    nb, half, cols = l2.shape
    tr = _sum_rows_tile(half)
    nrb = half // tr

    def body(pos_ref, p_ref, l_ref, o_ref):
        acc = p_ref[...].astype(F32)
        for b in range(nb):
            acc = acc + l_ref[b].astype(F32)
        o_ref[...] = acc

    grid_spec = pltpu.PrefetchScalarGridSpec(
        num_scalar_prefetch=1, grid=(nrb,),
        in_specs=[pl.BlockSpec((None, tr, cols), lambda i, pos: (pos[0], i, 0)),
                  pl.BlockSpec((nb, tr, cols), lambda i, pos: (0, i, 0))],
        out_specs=pl.BlockSpec((tr, cols), lambda i, pos: (pos[1] * nrb + i, 0)))
    return pl.pallas_call(body, grid_spec=grid_spec, out_shape=jax.ShapeDtypeStruct((2 * half, cols), F32),
                          name="sum_chip_blocks", compiler_params=_params("parallel"))(chip_core, p, l2)


HBM_SPEC = pl.BlockSpec(memory_space=pltpu.HBM)
SEM_SPEC = pl.BlockSpec(memory_space=pltpu.SEMAPHORE)
SPLIT_EFFECT = pltpu.SideEffectType.DATAFLOW_SIDE_EFFECTING


def _in_hbm(v):
    return pltpu.with_memory_space_constraint(v, pltpu.HBM)


def _gather_ici_copies(bufs, refs, send, recv):
    x, y, c, k, sibling, chips = _place()
    cps = []
    for t in range(len(bufs)):
        rows = bufs[t].shape[1]
        for j, (px, py) in enumerate(chips):
            cps.append(pltpu.make_async_remote_copy(
                src_ref=refs[t].at[k, _half(rows, c)], dst_ref=refs[t].at[k, _half(rows, c)],
                send_sem=send.at[3 * t + j], recv_sem=recv.at[3 * t + j], device_id=(px, py, c), device_id_type=MESH))
    return cps


def gather_ici_start(bufs, after, name):
    n = len(bufs)

    def body(*refs):
        send, recv, token = refs[n + 1], refs[n + 2], refs[-1]
        for cp in _gather_ici_copies(bufs, refs[:n], send, recv):
            cp.start()
        token[...] = jnp.zeros_like(token)

    outs = pl.pallas_call(
        body, name=name, in_specs=[HBM_SPEC] * n + [ANY],
        out_shape=(pltpu.SemaphoreType.DMA((3 * n,)), pltpu.SemaphoreType.DMA((3 * n,)),
                   *[pltpu.HBM(b.shape, b.dtype) for b in bufs], jax.ShapeDtypeStruct(TOKEN_SHAPE, F32)),
        out_specs=(SEM_SPEC, SEM_SPEC, *[HBM_SPEC] * n, pl.BlockSpec(memory_space=pltpu.VMEM)),
        input_output_aliases={t: 2 + t for t in range(n)},
        compiler_params=pltpu.CompilerParams(has_side_effects=SPLIT_EFFECT))(*[_in_hbm(b) for b in bufs], after)
    return outs[0], outs[1], list(outs[2:2 + n]), outs[-1]


def gather_ici_wait(send, recv, bufs, after, name):
    n = len(bufs)

    def body(*refs):
        x, y, c, k, sibling, chips = _place()
        for t in range(n):
            rows = bufs[t].shape[1]
            for j, (px, py) in enumerate(chips):
                cp = pltpu.make_async_remote_copy(
                    src_ref=refs[t].at[k, _half(rows, c)], dst_ref=refs[t].at[2 * px + py, _half(rows, c)],
                    send_sem=refs[n].at[3 * t + j], recv_sem=refs[n + 1].at[3 * t + j], device_id=(px, py, c),
                    device_id_type=MESH)
                cp.wait_send()
                cp.wait_recv()

    return list(pl.pallas_call(
        body, name=name, in_specs=[HBM_SPEC] * n + [SEM_SPEC, SEM_SPEC, ANY],
        out_shape=tuple(pltpu.HBM(b.shape, b.dtype) for b in bufs), out_specs=tuple([HBM_SPEC] * n),
        input_output_aliases={t: t for t in range(n)},
        compiler_params=pltpu.CompilerParams(has_side_effects=SPLIT_EFFECT))(*bufs, send, recv, after))


def gather_forward_sibling(bufs):
    n = len(bufs)

    def body(*refs):
        outs = refs[n:2 * n]
        send, recv = refs[2 * n:]
        x, y, c, k, sibling, chips = _place()
        cps = []
        for t in range(n):
            rows = bufs[t].shape[1]
            for j, (px, py) in enumerate(chips):
                landed = outs[t].at[2 * px + py, _half(rows, c)]
                cps.append(pltpu.make_async_remote_copy(src_ref=landed, dst_ref=landed, send_sem=send.at[t, j],
                                                        recv_sem=recv.at[t, j], device_id=sibling, device_id_type=MESH))
        for cp in cps:
            cp.start()
        for t in range(n):
            rows = bufs[t].shape[1]
            for j, (px, py) in enumerate(chips):
                other = outs[t].at[2 * px + py, _half(rows, 1 - c)]
                pltpu.make_async_remote_copy(src_ref=other, dst_ref=other, send_sem=send.at[t, j], recv_sem=recv.at[t, j],
                                             device_id=sibling, device_id_type=MESH).wait_recv()
        for cp in cps:
            cp.wait_send()

    return pl.pallas_call(
        body, in_specs=[ANY] * n, out_specs=[ANY] * n, out_shape=[jax.ShapeDtypeStruct(b.shape, b.dtype) for b in bufs],
        input_output_aliases={t: t for t in range(n)}, scratch_shapes=[pltpu.SemaphoreType.DMA((n, 3))] * 2,
        name="gather_forward_sibling", compiler_params=pltpu.CompilerParams())(*bufs)


def _sibling_copies(gs, src_refs, dst_refs, send, recv):
    x, y, c, k, sibling, chips = _place()
    return [pltpu.make_async_remote_copy(
        src_ref=src_refs[t].at[:, _half(gs[t].shape[1], 1 - c), :], dst_ref=dst_refs[t], send_sem=send.at[t],
        recv_sem=recv.at[t], device_id=sibling, device_id_type=MESH) for t in range(len(gs))]


def sibling_start(gs, after, name):
    n = len(gs)
    lands = [lax.empty((g.shape[0], g.shape[1] // 2, g.shape[2]), g.dtype) for g in gs]

    def body(*refs):
        send, recv, token = refs[2 * n + 1], refs[2 * n + 2], refs[-1]
        for cp in _sibling_copies(gs, refs[:n], refs[n:2 * n], send, recv):
            cp.start()
        token[...] = jnp.zeros_like(token)

    outs = pl.pallas_call(
        body, name=name, in_specs=[HBM_SPEC] * (2 * n) + [ANY],
        out_shape=(pltpu.SemaphoreType.DMA((n,)), pltpu.SemaphoreType.DMA((n,)),
                   *[pltpu.HBM(v.shape, v.dtype) for v in gs + lands], jax.ShapeDtypeStruct(TOKEN_SHAPE, F32)),
        out_specs=(SEM_SPEC, SEM_SPEC, *[HBM_SPEC] * (2 * n), pl.BlockSpec(memory_space=pltpu.VMEM)),
        input_output_aliases={t: 2 + t for t in range(2 * n)},
        compiler_params=pltpu.CompilerParams(has_side_effects=SPLIT_EFFECT))(*[_in_hbm(v) for v in gs + lands], after)
    return outs[0], outs[1], list(outs[2:2 + n]), list(outs[2 + n:2 + 2 * n]), outs[-1]


def sibling_wait(send, recv, gs, lands, after, name):
    n = len(gs)

    def body(*refs):
        for cp in _sibling_copies(gs, refs[:n], refs[n:2 * n], refs[2 * n], refs[2 * n + 1]):
            cp.wait_send()
            cp.wait_recv()

    outs = pl.pallas_call(
        body, name=name, in_specs=[HBM_SPEC] * (2 * n) + [SEM_SPEC, SEM_SPEC, ANY],
        out_shape=tuple(pltpu.HBM(v.shape, v.dtype) for v in gs + lands), out_specs=tuple([HBM_SPEC] * (2 * n)),
        input_output_aliases={t: t for t in range(2 * n)},
        compiler_params=pltpu.CompilerParams(has_side_effects=SPLIT_EFFECT))(*gs, *lands, send, recv, after)
    return list(outs[:n]), list(outs[n:])


def _reduce_ici_copies(ps, src_refs, dst_refs, send, recv):
    x, y, c, k, sibling, chips = _place()
    cps = []
    for t in range(len(ps)):
        for j, (px, py) in enumerate(chips):
            cps.append(pltpu.make_async_remote_copy(
                src_ref=src_refs[t].at[2 * px + py], dst_ref=dst_refs[t].at[j], send_sem=send.at[3 * t + j],
                recv_sem=recv.at[3 * t + j],
                device_id=(px, py, c), device_id_type=MESH))
    return cps


def reduce_ici_start(ps, after, name):
    n = len(ps)
    lands = [lax.empty((3,) + p.shape[1:], p.dtype) for p in ps]

    def body(*refs):
        send, recv, token = refs[2 * n + 1], refs[2 * n + 2], refs[-1]
        for cp in _reduce_ici_copies(ps, refs[:n], refs[n:2 * n], send, recv):
            cp.start()
        token[...] = jnp.zeros_like(token)

    outs = pl.pallas_call(
        body, name=name, in_specs=[HBM_SPEC] * (2 * n) + [ANY],
        out_shape=(pltpu.SemaphoreType.DMA((3 * n,)), pltpu.SemaphoreType.DMA((3 * n,)),
                   *[pltpu.HBM(v.shape, v.dtype) for v in ps + lands], jax.ShapeDtypeStruct(TOKEN_SHAPE, F32)),
        out_specs=(SEM_SPEC, SEM_SPEC, *[HBM_SPEC] * (2 * n), pl.BlockSpec(memory_space=pltpu.VMEM)),
        input_output_aliases={t: 2 + t for t in range(2 * n)},
        compiler_params=pltpu.CompilerParams(has_side_effects=SPLIT_EFFECT))(*[_in_hbm(v) for v in ps + lands], after)
    return outs[0], outs[1], list(outs[2:2 + n]), list(outs[2 + n:2 + 2 * n]), outs[-1]


def _halves_copies(ss, refs, send, recv):
    x, y, c, k, sibling, chips = _place()
    return [pltpu.make_async_remote_copy(
        src_ref=refs[t].at[_half(ss[t].shape[0], c)], dst_ref=refs[t].at[_half(ss[t].shape[0], c)], send_sem=send.at[t],
        recv_sem=recv.at[t], device_id=sibling, device_id_type=MESH) for t in range(len(ss))]


def halves_start(ss, after, name):
    n = len(ss)

    def body(*refs):
        send, recv, token = refs[n + 1], refs[n + 2], refs[-1]
        for cp in _halves_copies(ss, refs[:n], send, recv):
            cp.start()
        token[...] = jnp.zeros_like(token)

    outs = pl.pallas_call(
        body, name=name, in_specs=[HBM_SPEC] * n + [ANY],
        out_shape=(pltpu.SemaphoreType.DMA((n,)), pltpu.SemaphoreType.DMA((n,)), *[pltpu.HBM(s.shape, s.dtype) for s in ss],
                   jax.ShapeDtypeStruct(TOKEN_SHAPE, F32)),
        out_specs=(SEM_SPEC, SEM_SPEC, *[HBM_SPEC] * n, pl.BlockSpec(memory_space=pltpu.VMEM)),
        input_output_aliases={t: 2 + t for t in range(n)},
        compiler_params=pltpu.CompilerParams(has_side_effects=SPLIT_EFFECT))(*[_in_hbm(s) for s in ss], after)
    return outs[0], outs[1], list(outs[2:2 + n]), outs[-1]


def halves_wait(send, recv, ss, after, name):
    n = len(ss)

    def body(*refs):
        x, y, c, k, sibling, chips = _place()
        for t in range(n):
            rows = ss[t].shape[0]
            cp = pltpu.make_async_remote_copy(
                src_ref=refs[t].at[_half(rows, c)], dst_ref=refs[t].at[_half(rows, 1 - c)], send_sem=refs[n].at[t],
                recv_sem=refs[n + 1].at[t], device_id=sibling, device_id_type=MESH)
            cp.wait_send()
            cp.wait_recv()

    return list(pl.pallas_call(
        body, name=name, in_specs=[HBM_SPEC] * n + [SEM_SPEC, SEM_SPEC, ANY],
        out_shape=tuple(pltpu.HBM(s.shape, s.dtype) for s in ss), out_specs=tuple([HBM_SPEC] * n),
        input_output_aliases={t: t for t in range(n)},
        compiler_params=pltpu.CompilerParams(has_side_effects=SPLIT_EFFECT))(*ss, send, recv, after))


def reduce_ici_wait(send, recv, ps, lands, after, name):
    n = len(ps)

    def body(*refs):
        for cp in _reduce_ici_copies(ps, refs[:n], refs[n:2 * n], refs[2 * n], refs[2 * n + 1]):
            cp.wait_send()
            cp.wait_recv()

    outs = pl.pallas_call(
        body, name=name, in_specs=[HBM_SPEC] * (2 * n) + [SEM_SPEC, SEM_SPEC, ANY],
        out_shape=tuple(pltpu.HBM(v.shape, v.dtype) for v in ps + lands), out_specs=tuple([HBM_SPEC] * (2 * n)),
        input_output_aliases={t: t for t in range(2 * n)},
        compiler_params=pltpu.CompilerParams(has_side_effects=SPLIT_EFFECT))(*ps, *lands, send, recv, after)
    return list(outs[:n]), list(outs[n:])


SMALL_CHUNK_ROWS = 256


def all_reduce_small(v, after):
    rows = v.shape[0]
    nchunk = rows // SMALL_CHUNK_ROWS

    def body(v_ref, after_ref, o_ref, buf, send, recv):
        x, y, c = lax.axis_index("x"), lax.axis_index("y"), lax.axis_index("c")
        me = 4 * x + 2 * y + c
        buf[me] = v_ref[...]
        cps = []
        for d in range(1, N_DEV):
            peer = (x ^ ((d >> 2) & 1), y ^ ((d >> 1) & 1), c ^ (d & 1))
            cps.append(pltpu.make_async_remote_copy(src_ref=v_ref, dst_ref=buf.at[me], send_sem=send.at[d - 1],
                                                    recv_sem=recv.at[d - 1], device_id=peer, device_id_type=MESH))
        for cp in cps:
            cp.start()
        for d in range(1, N_DEV):
            got = buf.at[me ^ d]
            pltpu.make_async_remote_copy(src_ref=got, dst_ref=got, send_sem=send.at[d - 1], recv_sem=recv.at[d - 1],
                                         device_id=(x, y, c), device_id_type=MESH).wait_recv()
        for cp in cps:
            cp.wait_send()

        def chunk(i, carry):
            sl = pl.ds(pl.multiple_of(i * SMALL_CHUNK_ROWS, SMALL_CHUNK_ROWS), SMALL_CHUNK_ROWS)
            acc = buf[0, sl, :]
            for s in range(1, N_DEV):
                acc = acc + buf[s, sl, :]
            o_ref[sl, :] = acc
            return carry

        lax.fori_loop(0, nchunk, chunk, 0)

    vmem = pl.BlockSpec(memory_space=pltpu.VMEM)
    return pl.pallas_call(
        body, in_specs=[vmem, ANY], out_specs=vmem, out_shape=jax.ShapeDtypeStruct(v.shape, F32),
        scratch_shapes=[pltpu.VMEM((N_DEV,) + v.shape, F32), pltpu.SemaphoreType.DMA((N_DEV - 1,)),
                        pltpu.SemaphoreType.DMA((N_DEV - 1,))],
        name="all_reduce_small",
        compiler_params=pltpu.CompilerParams(vmem_limit_bytes=VMEM_LIMIT_BYTES))(v, after)


def adamw(w, g, m, v):
    rows, cols = w.shape
    tr = rows
    for cand in (512, 256, 128, 64, 32, 16, 8):
        if rows % cand == 0 and cand * cols * 4 <= (1 << 20):
            tr = cand
            break
    c1 = 1.0 - ADAM_B1 ** ADAM_STEP
    c2 = 1.0 - ADAM_B2 ** ADAM_STEP

    def body(w_ref, g_ref, m_ref, v_ref, go_ref, d_ref, nm_ref, nv_ref):
        gf = g_ref[...]
        go_ref[...] = gf
        nm = ADAM_B1 * m_ref[...] + (1.0 - ADAM_B1) * gf
        nv = ADAM_B2 * v_ref[...] + (1.0 - ADAM_B2) * (gf * gf)
        d_ref[...] = -ADAM_LR * ((nm / c1) / (jnp.sqrt(nv / c2) + ADAM_EPS) + ADAM_WD * w_ref[...])
        nm_ref[...] = nm
        nv_ref[...] = nv

    spec = pl.BlockSpec((tr, cols), lambda i: (i, 0))
    shape = jax.ShapeDtypeStruct((rows, cols), F32)
    return pl.pallas_call(body, grid=(rows // tr,), in_specs=[spec] * 4, out_specs=(spec,) * 4, out_shape=(shape,) * 4,
                          name="adamw", compiler_params=_params("parallel"))(w, g, m, v)


TAP_ROWS_ALIGN = 16
FLAT_ALIGN = 128 * SMALL_CHUNK_ROWS


def _pad_to(a, n):
    return jnp.pad(a, (0, n - a.shape[0]))


def _round_up(n, m):
    return (n + m - 1) // m * m


class Exchanges:
    def __init__(self, a, chip_core):
        self.a, self.chip_core = a, chip_core
        self.bufs = []
        for g in range(len(SUBLAYERS)):
            row = []
            for n in sublayer_weight_names(g):
                w = a[n].reshape(-1, a[n].shape[-1])
                if _kind(n) == "tap":
                    w = jnp.pad(w, ((0, _round_up(w.shape[0], TAP_ROWS_ALIGN) - w.shape[0]), (0, 0)))
                row.append(place_block(w, chip_core, F32 if _kind(n) == "tap" else BF16))
            self.bufs.append(row)
        self.started = None
        self.after = chip_core
        self.stage = [None, None, None]
        self.results = {}

    def _unpack(self, g, gathered, W):
        for n, v in zip(sublayer_weight_names(g), gathered):
            kind = _kind(n)
            if kind == "col":
                W[n] = v
            elif kind == "row":
                W[n] = v.reshape(-1, v.shape[-1])
            elif kind == "grp":
                grp, r, pg = self.a[n].shape
                W[n] = v.reshape(N_CHIPS, grp, r, pg).transpose(1, 0, 2, 3).reshape(grp, N_CHIPS * r, pg)
            else:
                nt = self.a[n].shape[0]
                W[n] = v[:, :nt].transpose(1, 0, 2).reshape(nt, -1)

    def gather_first(self, W):
        gathered = all_gather_chips(self.bufs[0])
        self._unpack(0, gathered, W)
        self.after = gathered[0]

    def forward_begins(self, g, W):
        if g + 1 == len(SUBLAYERS):
            return None
        send, recv, bufs, token = gather_ici_start(self.bufs[g + 1], self.after, f"gather_start_{g + 1}")
        self.started = (send, recv, bufs)
        return token

    def forward_ends(self, g, x, W):
        if g + 1 == len(SUBLAYERS):
            return
        send, recv, bufs = self.started
        gathered = gather_forward_sibling(gather_ici_wait(send, recv, bufs, x, f"gather_wait_{g + 1}"))
        self._unpack(g + 1, gathered, W)
        self.after = gathered[0]

    def gradients_ready(self, g, grads):
        names = [n for n in sublayer_weight_names(g) if _kind(n) != "tap"]
        gl = []
        for n in names:
            v, kind = grads.pop(n), _kind(n)
            if kind == "row":
                v = v.reshape(N_CHIPS, -1, v.shape[-1])
            elif kind == "grp":
                grp, r, pg = self.a[n].shape
                v = v.reshape(grp, N_CHIPS, r, pg).transpose(1, 0, 2, 3).reshape(N_CHIPS, grp * r, pg)
            gl.append(v)
        last = self.advance(gl[0], gl[0])
        send, recv, gl, lands, token = sibling_start(gl, last, f"sibling_start_{g}")
        self.stage[0] = (g, names, send, recv, gl, lands)
        return token

    def advance(self, after, last):
        if self.stage[2] is not None:
            g, names, send, recv, ss = self.stage[2]
            last = self._adamw_matrices(names, halves_wait(send, recv, ss, after, f"halves_wait_{g}"))
        self.stage[2] = None
        if self.stage[1] is not None:
            g, names, send, recv, ps, l2s = self.stage[1]
            ps, l2s = reduce_ici_wait(send, recv, ps, l2s, after, f"reduce_wait_{g}")
            ss = [sum_chip_blocks(p, l2, self.chip_core) for p, l2 in zip(ps, l2s)]
            send, recv, ss, last = halves_start(ss, last, f"halves_start_{g}")
            self.stage[2] = (g, names, send, recv, ss)
        self.stage[1] = None
        if self.stage[0] is not None:
            g, names, send, recv, gl, lands = self.stage[0]
            gl, lands = sibling_wait(send, recv, gl, lands, after, f"sibling_wait_{g}")
            ps = [add_sibling_half(v, l, self.chip_core) for v, l in zip(gl, lands)]
            send, recv, ps, l2s, last = reduce_ici_start(ps, last, f"reduce_start_{g}")
            self.stage[1] = (g, names, send, recv, ps, l2s)
        self.stage[0] = None
        return last

    def _adamw_matrices(self, names, grads):
        for n, g in zip(names, grads):
            shape = self.a[n].shape
            two_d = lambda v: v.reshape(-1, shape[-1])
            outs = adamw(two_d(self.a[n]), g, two_d(self.a["m_" + n]), two_d(self.a["v_" + n]))
            self.results[n] = tuple(v.reshape(shape) for v in outs)
        return outs[0]


def train_step(a):
    x, positions, tgt = a["x"][0], a["positions"][0], a["loss_target"][0]
    mats = [n for n in WEIGHT_NAMES if _kind(n) in ("col", "row", "grp")]
    taps = [n for n in WEIGHT_NAMES if _kind(n) == "tap"]
    reps = [n for n in WEIGHT_NAMES if _kind(n) == "rep"]
    chip = 2 * lax.axis_index("x") + lax.axis_index("y")
    chip_core = jnp.stack([chip, lax.axis_index("c")]).astype(jnp.int32)

    comm = Exchanges(a, chip_core)
    W = {n: a[n] for n in reps}
    comm.gather_first(W)
    loss, dx, grads = local_step(x, positions, tgt, W, comm)
    loss = lax.psum(loss, ("x", "y", "c"))
    last = comm.advance(dx, comm.advance(dx, dx))

    n_rep = _round_up(sum(a[n].size for n in reps), FLAT_ALIGN)
    flat_rep = _pad_to(jnp.concatenate([grads[n].reshape(-1) for n in reps]), n_rep)
    flat_tap = jnp.concatenate([grads[n].reshape(-1) for n in taps])
    flat = jnp.concatenate([flat_rep, _pad_to(flat_tap, _round_up(flat_tap.shape[0], FLAT_ALIGN))])
    summed = all_reduce_small(flat.reshape(-1, 128), last)
    comm.advance(summed, summed)
    rep_rows = n_rep // 128
    tap_flat = summed[rep_rows:].reshape(-1)

    out = dict(comm.results)
    pack = lambda pre: _pad_to(jnp.concatenate([a[pre + n].reshape(-1) for n in reps]), n_rep).reshape(-1, 128)
    g_rep, d_rep, m_rep, v_rep = adamw(pack(""), summed[:rep_rows], pack("m_"), pack("v_"))
    off = 0
    for n in reps:
        size, shape = a[n].size, a[n].shape
        out[n] = tuple(f.reshape(-1)[off:off + size].reshape(shape) for f in (g_rep, d_rep, m_rep, v_rep))
        off += size
    off = 0
    for n in taps:
        nt, cs = a[n].shape
        full = tap_flat[off:off + nt * cs * N_CHIPS].reshape(nt, cs * N_CHIPS)
        off += nt * cs * N_CHIPS
        g = lax.dynamic_slice(full, (0, chip * cs), (nt, cs))
        out[n] = tuple(adamw(a[n], g, a["m_" + n], a["v_" + n]))

    res = [loss, dx[None]]
    for part in range(4):
        res += [out[n][part] for n in WEIGHT_NAMES]
    return tuple(res)


def kernel(x, positions, l0_norm_g, l0_a_w_in, l0_a_b_in, l0_a_dw_w, l0_a_dw_b, l0_a_ln_g, l0_a_ln_b, l0_a_w_out, l0_a_b_out, l0_ffn_norm_g, l0_ffn_w_up, l0_ffn_dw_w, l0_ffn_dw_b, l0_ffn_w_down, l1_norm_g, l1_b_w_group, l1_b_scale, l1_ffn_norm_g, l1_ffn_w_up, l1_ffn_dw_w, l1_ffn_dw_b, l1_ffn_w_down, l2_norm_g, l2_c_w_qkv, l2_c_q_norm_g, l2_c_k_norm_g, l2_c_sinks, l2_c_w_o, l2_ffn_norm_g, l2_ffn_w_up, l2_ffn_dw_w, l2_ffn_dw_b, l2_ffn_w_down, l3_norm_g, l3_a_w_in, l3_a_b_in, l3_a_dw_w, l3_a_dw_b, l3_a_ln_g, l3_a_ln_b, l3_a_w_out, l3_a_b_out, l3_ffn_norm_g, l3_ffn_w_up, l3_ffn_dw_w, l3_ffn_dw_b, l3_ffn_w_down, loss_target, m_l0_norm_g, m_l0_a_w_in, m_l0_a_b_in, m_l0_a_dw_w, m_l0_a_dw_b, m_l0_a_ln_g, m_l0_a_ln_b, m_l0_a_w_out, m_l0_a_b_out, m_l0_ffn_norm_g, m_l0_ffn_w_up, m_l0_ffn_dw_w, m_l0_ffn_dw_b, m_l0_ffn_w_down, m_l1_norm_g, m_l1_b_w_group, m_l1_b_scale, m_l1_ffn_norm_g, m_l1_ffn_w_up, m_l1_ffn_dw_w, m_l1_ffn_dw_b, m_l1_ffn_w_down, m_l2_norm_g, m_l2_c_w_qkv, m_l2_c_q_norm_g, m_l2_c_k_norm_g, m_l2_c_sinks, m_l2_c_w_o, m_l2_ffn_norm_g, m_l2_ffn_w_up, m_l2_ffn_dw_w, m_l2_ffn_dw_b, m_l2_ffn_w_down, m_l3_norm_g, m_l3_a_w_in, m_l3_a_b_in, m_l3_a_dw_w, m_l3_a_dw_b, m_l3_a_ln_g, m_l3_a_ln_b, m_l3_a_w_out, m_l3_a_b_out, m_l3_ffn_norm_g, m_l3_ffn_w_up, m_l3_ffn_dw_w, m_l3_ffn_dw_b, m_l3_ffn_w_down, v_l0_norm_g, v_l0_a_w_in, v_l0_a_b_in, v_l0_a_dw_w, v_l0_a_dw_b, v_l0_a_ln_g, v_l0_a_ln_b, v_l0_a_w_out, v_l0_a_b_out, v_l0_ffn_norm_g, v_l0_ffn_w_up, v_l0_ffn_dw_w, v_l0_ffn_dw_b, v_l0_ffn_w_down, v_l1_norm_g, v_l1_b_w_group, v_l1_b_scale, v_l1_ffn_norm_g, v_l1_ffn_w_up, v_l1_ffn_dw_w, v_l1_ffn_dw_b, v_l1_ffn_w_down, v_l2_norm_g, v_l2_c_w_qkv, v_l2_c_q_norm_g, v_l2_c_k_norm_g, v_l2_c_sinks, v_l2_c_w_o, v_l2_ffn_norm_g, v_l2_ffn_w_up, v_l2_ffn_dw_w, v_l2_ffn_dw_b, v_l2_ffn_w_down, v_l3_norm_g, v_l3_a_w_in, v_l3_a_b_in, v_l3_a_dw_w, v_l3_a_dw_b, v_l3_a_ln_g, v_l3_a_ln_b, v_l3_a_w_out, v_l3_a_b_out, v_l3_ffn_norm_g, v_l3_ffn_w_up, v_l3_ffn_dw_w, v_l3_ffn_dw_b, v_l3_ffn_w_down):
    return train_step(dict(locals()))
```

```python
import functools

import jax
import jax.numpy as jnp
from jax import lax
from jax.experimental import pallas as pl
from jax.experimental.pallas import tpu as pltpu

F32 = jnp.float32
BF16 = jnp.bfloat16
EPS = 1e-6
HEAD_DIM = 64
KV_GROUP = 8
ATT_BLOCK = 128
ROT_DIM = 16
ROPE_THETA = 500000.0
POOL_GROUPS = 4
CONF_TAPS = 31
FFN_TAPS = 3
N_CHIPS = 4
N_DEV = 8
ADAM_LR, ADAM_B1, ADAM_B2, ADAM_EPS, ADAM_WD, ADAM_STEP = 0.001, 0.9, 0.999, 1e-08, 0.01, 10
VMEM_LIMIT_BYTES = 56 * 1024 * 1024
MESH = pl.DeviceIdType.MESH

CONF_NAMES = ["norm_g", "a_w_in", "a_b_in", "a_dw_w", "a_dw_b", "a_ln_g", "a_ln_b", "a_w_out", "a_b_out"]
FFN_NAMES = ["ffn_norm_g", "ffn_w_up", "ffn_dw_w", "ffn_dw_b", "ffn_w_down"]
POOL_NAMES = ["norm_g", "b_w_group", "b_scale"]
ATT_NAMES = ["norm_g", "c_w_qkv", "c_q_norm_g", "c_k_norm_g", "c_sinks", "c_w_o"]
WEIGHT_NAMES = ([f"l0_{n}" for n in CONF_NAMES + FFN_NAMES] + [f"l1_{n}" for n in POOL_NAMES + FFN_NAMES]
                + [f"l2_{n}" for n in ATT_NAMES + FFN_NAMES] + [f"l3_{n}" for n in CONF_NAMES + FFN_NAMES])
COL_SHARDED = ("a_w_in", "ffn_w_up", "c_w_qkv")
ROW_SHARDED = ("a_w_out", "ffn_w_down", "c_w_o")
TAP_SHARDED = ("a_dw_w", "ffn_dw_w")


def _kind(name):
    base = name[3:]
    if base in COL_SHARDED:
        return "col"
    if base in ROW_SHARDED:
        return "row"
    if base in TAP_SHARDED:
        return "tap"
    if base == "b_w_group":
        return "grp"
    return "rep"


def _pick(n, prefs):
    for p in prefs:
        if p <= n and n % p == 0:
            return p
    return n


def _params(*sem):
    return pltpu.CompilerParams(dimension_semantics=sem, vmem_limit_bytes=VMEM_LIMIT_BYTES)


def _sigmoid(x):
    return 1.0 / (1.0 + jnp.exp(-x))


NN = (((1,), (0,)), ((), ()))
NT = (((1,), (1,)), ((), ()))
TN = (((0,), (0,)), ((), ()))


MM_VMEM_BUDGET = 44 * 1024 * 1024
MM_STEP_SECONDS = 0.35e-6
MM_FLOPS, MM_HBM_BYTES = 9.0e14, 3.0e12
TILE_SIZES = (4096, 2816, 2048, 1408, 1024, 704, 640, 512, 256, 128)


def _tile_options(n, lane):
    opts = [c for c in TILE_SIZES if c <= n and n % c == 0 and (not lane or c % 128 == 0)]
    return opts or [n]


def _mm_plan(m, n, k, *, n_unit=None, k_unit=None, a_bytes=2, b_bytes=2, o_bytes=2, extra_bytes=0):
    best = None
    for tm in _tile_options(m, False):
        for tn in _tile_options(n_unit or n, True):
            for tk in _tile_options(k_unit or k, True) + ([k] if not k_unit else []):
                nk = k // tk
                vmem = 2 * (tm * tk * a_bytes + tk * tn * b_bytes + tm * tn * (o_bytes + extra_bytes)) + tm * tn * 4 * (2 if nk > 1 else 1)
                if vmem > MM_VMEM_BUDGET:
                    continue
                ni, nj = m // tm, n // tn
                a_all, b_all, o_all = m * k * a_bytes, k * n * b_bytes, m * n * (o_bytes + extra_bytes)
                for i_inner in (False, True):
                    if nk > 1:
                        traffic = a_all * nj + b_all * ni + o_all
                    elif i_inner:
                        traffic = a_all * nj + b_all + o_all
                    else:
                        traffic = a_all + b_all * ni + o_all
                    cost = ni * nj * nk * MM_STEP_SECONDS + max(2.0 * m * n * k / MM_FLOPS, traffic / MM_HBM_BYTES)
                    if best is None or cost < best[0]:
                        best = (cost, tm, tn, tk, i_inner)
    assert best is not None, (m, n, k)
    return best[1:]


def _mm(a, b, *, dims, sizes, plan, a_blk, a_idx, b_blk, b_idx, o_blk, o_idx, out_shape, name,
        bias=None, scale=None, vec_blk=None, vec_idx=None, resid=None, raw_shape=None):
    m, n, k = sizes
    tm, tn, tk, i_inner = plan
    ni, nj, nk = m // tm, n // tn, k // tk
    has_bias, has_scale, has_resid, want_raw = bias is not None, scale is not None, resid is not None, raw_shape is not None

    def body(*refs):
        a_ref, b_ref = refs[0], refs[1]
        pos = 2
        bias_ref = scale_ref = resid_ref = raw_ref = None
        if has_bias:
            bias_ref = refs[pos]; pos += 1
        if has_scale:
            scale_ref = refs[pos]; pos += 1
        if has_resid:
            resid_ref = refs[pos]; pos += 1
        o_ref = refs[pos]; pos += 1
        if want_raw:
            raw_ref = refs[pos]; pos += 1
        part = lax.dot_general(a_ref[...].astype(BF16), b_ref[...].astype(BF16), dims, preferred_element_type=F32)

        def finish(r):
            if want_raw:
                raw_ref[...] = r.astype(raw_ref.dtype)
            if has_bias:
                r = r + bias_ref[...]
            if has_scale:
                r = r * scale_ref[...]
            if has_resid:
                r = r + resid_ref[...]
            o_ref[...] = r.astype(o_ref.dtype)

        if nk == 1:
            finish(part)
        else:
            acc_ref = refs[pos]
            kk = pl.program_id(2)

            @pl.when(kk == 0)
            def _():
                acc_ref[...] = part

            @pl.when(kk > 0)
            def _():
                acc_ref[...] += part

            @pl.when(kk == nk - 1)
            def _():
                finish(acc_ref[...])

    order = (lambda f: (lambda j, i, kk: f(i, j, kk))) if i_inner else (lambda f: f)
    spec = lambda blk, idx: pl.BlockSpec(blk, order(idx))
    operands, in_specs = [a, b], [spec(a_blk, a_idx), spec(b_blk, b_idx)]
    for v in (bias, scale):
        if v is not None:
            operands.append(v); in_specs.append(spec(vec_blk, vec_idx))
    if has_resid:
        operands.append(resid); in_specs.append(spec(o_blk, o_idx))
    out_shapes, out_specs = out_shape, spec(o_blk, o_idx)
    if want_raw:
        out_shapes, out_specs = (out_shape, raw_shape), (spec(o_blk, o_idx), spec(o_blk, o_idx))
    return pl.pallas_call(
        body, grid=(nj, ni, nk) if i_inner else (ni, nj, nk), in_specs=in_specs, out_specs=out_specs,
        out_shape=out_shapes, scratch_shapes=[pltpu.VMEM((tm, tn), F32)] if nk > 1 else [], name=name,
        compiler_params=_params("parallel", "parallel", "arbitrary"))(*operands)


def mm_nn_cols(a, g, *, split, out_dtype, bias=None, name):
    t, k = a.shape
    ns = g.shape[2]
    n = N_CHIPS * ns
    plan = _mm_plan(t, n, k, n_unit=ns, o_bytes=jnp.dtype(out_dtype).itemsize)
    tm, tn, tk, _ = plan
    nj = ns // tn
    if split:
        o_blk, o_idx = (None, tm, tn), (lambda i, j, kk: (j // (2 * nj), i, j % (2 * nj)))
        out_shape = jax.ShapeDtypeStruct((2, t, 2 * ns), out_dtype)
        vec_blk, vec_idx = (None, 1, tn), (lambda i, j, kk: (j // (2 * nj), 0, j % (2 * nj)))
        if bias is not None:
            bias = bias.reshape(2, 1, 2 * ns)
    else:
        o_blk, o_idx = (tm, tn), (lambda i, j, kk: (i, j))
        out_shape = jax.ShapeDtypeStruct((t, n), out_dtype)
        vec_blk, vec_idx = (1, tn), (lambda i, j, kk: (0, j))
        if bias is not None:
            bias = bias.reshape(1, n)
    return _mm(a, g, dims=NN, sizes=(t, n, k), plan=plan, a_blk=(tm, tk), a_idx=lambda i, j, kk: (i, kk),
               b_blk=(None, tk, tn), b_idx=lambda i, j, kk: (j // nj, kk, j % nj), o_blk=o_blk, o_idx=o_idx,
               out_shape=out_shape, name=name, bias=bias, vec_blk=vec_blk, vec_idx=vec_idx)


def mm_nn(a, w, *, out_dtype, bias=None, scale=None, resid=None, raw_dtype=None, name):
    t, k = a.shape
    n = w.shape[1]
    extra = (4 if resid is not None else 0) + (0 if raw_dtype is None else jnp.dtype(raw_dtype).itemsize)
    plan = _mm_plan(t, n, k, a_bytes=a.dtype.itemsize, o_bytes=jnp.dtype(out_dtype).itemsize, extra_bytes=extra)
    tm, tn, tk, _ = plan
    raw_shape = None if raw_dtype is None else jax.ShapeDtypeStruct((t, n), raw_dtype)
    return _mm(a, w, dims=NN, sizes=(t, n, k), plan=plan, a_blk=(tm, tk), a_idx=lambda i, j, kk: (i, kk),
               b_blk=(tk, tn), b_idx=lambda i, j, kk: (kk, j), o_blk=(tm, tn), o_idx=lambda i, j, kk: (i, j),
               out_shape=jax.ShapeDtypeStruct((t, n), out_dtype), name=name,
               bias=None if bias is None else bias.reshape(1, n), scale=None if scale is None else scale.reshape(1, n),
               vec_blk=(1, tn), vec_idx=lambda i, j, kk: (0, j), resid=resid, raw_shape=raw_shape)


def mm_nt(dy, w, *, out_dtype, name):
    t, n = dy.shape
    kdim = w.shape[0]
    plan = _mm_plan(t, kdim, n, a_bytes=dy.dtype.itemsize, o_bytes=jnp.dtype(out_dtype).itemsize)
    tm, tn, tk, _ = plan
    return _mm(dy, w, dims=NT, sizes=(t, kdim, n), plan=plan, a_blk=(tm, tk), a_idx=lambda i, j, kk: (i, kk),
               b_blk=(tn, tk), b_idx=lambda i, j, kk: (j, kk), o_blk=(tm, tn), o_idx=lambda i, j, kk: (i, j),
               out_shape=jax.ShapeDtypeStruct((t, kdim), out_dtype), name=name)


def mm_nt_cols(du, g, *, split, out_dtype, name):
    kdim, ns = g.shape[1], g.shape[2]
    t = du.shape[1] if split else du.shape[0]
    plan = _mm_plan(t, kdim, N_CHIPS * ns, k_unit=ns, o_bytes=jnp.dtype(out_dtype).itemsize)
    tm, tn, tk, _ = plan
    nkb = ns // tk
    if split:
        a_blk, a_idx = (None, tm, tk), (lambda i, j, kk: (kk // (2 * nkb), i, kk % (2 * nkb)))
    else:
        a_blk, a_idx = (tm, tk), (lambda i, j, kk: (i, kk))
    return _mm(du, g, dims=NT, sizes=(t, kdim, N_CHIPS * ns), plan=plan, a_blk=a_blk, a_idx=a_idx,
               b_blk=(None, tn, tk), b_idx=lambda i, j, kk: (kk // nkb, j, kk % nkb),
               o_blk=(tm, tn), o_idx=lambda i, j, kk: (i, j),
               out_shape=jax.ShapeDtypeStruct((t, kdim), out_dtype), name=name)


def mm_wgrad(at, dy, *, out_dtype, name):
    return mm_nn(at, dy, out_dtype=out_dtype, name=name)


def mm_wgrad_cols(ht, du, *, split, out_dtype, name):
    kdim, t = ht.shape
    ns = (du.shape[2] // 2) if split else (du.shape[1] // N_CHIPS)
    plan = _mm_plan(kdim, N_CHIPS * ns, t, n_unit=ns, o_bytes=jnp.dtype(out_dtype).itemsize)
    tm, tn, tk, _ = plan
    nj = ns // tn
    if split:
        b_blk, b_idx = (None, tk, tn), (lambda i, j, kk: (j // (2 * nj), kk, j % (2 * nj)))
    else:
        b_blk, b_idx = (tk, tn), (lambda i, j, kk: (kk, j))
    return _mm(ht, du, dims=NN, sizes=(kdim, N_CHIPS * ns, t), plan=plan, a_blk=(tm, tk), a_idx=lambda i, j, kk: (i, kk),
               b_blk=b_blk, b_idx=b_idx, o_blk=(None, tm, tn), o_idx=lambda i, j, kk: (j // nj, i, j % nj),
               out_shape=jax.ShapeDtypeStruct((N_CHIPS, kdim, ns), out_dtype), name=name)


def _row_tile(t):
    return _pick(t, (256, 128))


def _acc_rows(ref, part, i):
    @pl.when(i == 0)
    def _():
        ref[...] = part

    @pl.when(i > 0)
    def _():
        ref[...] += part


TOKEN_SHAPE = (8, 128)


def _token_operand(token):
    if token is None:
        return [], []
    return [token], [pl.BlockSpec(TOKEN_SHAPE, lambda i: (0, 0))]


def rms_fwd(x, g, out_dtype, token=None, transposed=False):
    t, d = x.shape
    tr = _row_tile(t)

    def body(x_ref, g_ref, *rest):
        outs = rest[-2:] if transposed else rest[-1:]
        xf = x_ref[...]
        r = lax.rsqrt(jnp.mean(xf * xf, axis=-1, keepdims=True) + EPS)
        y = xf * r * g_ref[...]
        outs[0][...] = y.astype(outs[0].dtype)
        if transposed:
            outs[1][...] = y.T.astype(BF16)

    row = pl.BlockSpec((tr, d), lambda i: (i, 0))
    tok, tok_spec = _token_operand(token)
    out_specs, out_shape = row, jax.ShapeDtypeStruct((t, d), out_dtype)
    if transposed:
        out_specs = (row, pl.BlockSpec((d, tr), lambda i: (0, i)))
        out_shape = (out_shape, jax.ShapeDtypeStruct((d, t), BF16))
    return pl.pallas_call(body, grid=(t // tr,), in_specs=[row, pl.BlockSpec((1, d), lambda i: (0, 0))] + tok_spec,
                          out_specs=out_specs, out_shape=out_shape, name="rms_fwd",
                          compiler_params=_params("parallel"))(x, g.reshape(1, d), *tok)


def rms_bwd(x, g, dh, dres, token=None):
    t, d = x.shape
    tr = _row_tile(t)

    def body(x_ref, g_ref, dh_ref, dres_ref, *rest):
        dx_ref, dx16_ref, dg_ref = rest[-3:]
        i = pl.program_id(0)
        xf = x_ref[...]
        r = lax.rsqrt(jnp.mean(xf * xf, axis=-1, keepdims=True) + EPS)
        xhat = xf * r
        dhf = dh_ref[...].astype(F32)
        dxh = dhf * g_ref[...]
        m = jnp.mean(dxh * xhat, axis=-1, keepdims=True)
        dx = dres_ref[...] + r * (dxh - xhat * m)
        dx_ref[...] = dx
        dx16_ref[...] = dx.astype(BF16)
        _acc_rows(dg_ref, jnp.sum(dhf * xhat, axis=0, keepdims=True), i)

    row = pl.BlockSpec((tr, d), lambda i: (i, 0))
    vec = pl.BlockSpec((1, d), lambda i: (0, 0))
    tok, tok_spec = _token_operand(token)
    return pl.pallas_call(body, grid=(t // tr,), in_specs=[row, vec, row, row] + tok_spec, out_specs=(row, row, vec),
                          out_shape=(jax.ShapeDtypeStruct((t, d), F32), jax.ShapeDtypeStruct((t, d), BF16),
                                     jax.ShapeDtypeStruct((1, d), F32)),
                          name="rms_bwd", compiler_params=_params("arbitrary"))(x, g.reshape(1, d), dh, dres, *tok)


def ln_silu_fwd(c, g, b):
    t, d = c.shape
    tr = _row_tile(t)

    def body(c_ref, g_ref, b_ref, o_ref, ot_ref):
        xf = c_ref[...]
        mu = jnp.mean(xf, axis=-1, keepdims=True)
        xc = xf - mu
        var = jnp.mean(xc * xc, axis=-1, keepdims=True)
        n = xc * lax.rsqrt(var + EPS) * g_ref[...] + b_ref[...]
        s = n * _sigmoid(n)
        o_ref[...] = s.astype(o_ref.dtype)
        ot_ref[...] = s.T.astype(ot_ref.dtype)

    row = pl.BlockSpec((tr, d), lambda i: (i, 0))
    vec = pl.BlockSpec((1, d), lambda i: (0, 0))
    return pl.pallas_call(body, grid=(t // tr,), in_specs=[row, vec, vec],
                          out_specs=(row, pl.BlockSpec((d, tr), lambda i: (0, i))),
                          out_shape=(jax.ShapeDtypeStruct((t, d), BF16), jax.ShapeDtypeStruct((d, t), BF16)),
                          name="ln_silu_fwd", compiler_params=_params("parallel"))(c, g.reshape(1, d), b.reshape(1, d))


def ln_silu_bwd(c, g, b, ds):
    t, d = c.shape
    tr = _row_tile(t)

    def body(c_ref, g_ref, b_ref, ds_ref, dc_ref, dg_ref, db_ref):
        i = pl.program_id(0)
        xf = c_ref[...]
        mu = jnp.mean(xf, axis=-1, keepdims=True)
        xc = xf - mu
        var = jnp.mean(xc * xc, axis=-1, keepdims=True)
        rstd = lax.rsqrt(var + EPS)
        xhat = xc * rstd
        n = xhat * g_ref[...] + b_ref[...]
        sg = _sigmoid(n)
        dn = ds_ref[...].astype(F32) * (sg * (1.0 + n * (1.0 - sg)))
        dxh = dn * g_ref[...]
        m1 = jnp.mean(dxh, axis=-1, keepdims=True)
        m2 = jnp.mean(dxh * xhat, axis=-1, keepdims=True)
        dc_ref[...] = rstd * (dxh - m1 - xhat * m2)
        _acc_rows(dg_ref, jnp.sum(dn * xhat, axis=0, keepdims=True), i)
        _acc_rows(db_ref, jnp.sum(dn, axis=0, keepdims=True), i)

    row = pl.BlockSpec((tr, d), lambda i: (i, 0))
    vec = pl.BlockSpec((1, d), lambda i: (0, 0))
    vshape = jax.ShapeDtypeStruct((1, d), F32)
    return pl.pallas_call(body, grid=(t // tr,), in_specs=[row, vec, vec, row], out_specs=(row, vec, vec),
                          out_shape=(jax.ShapeDtypeStruct((t, d), F32), vshape, vshape), name="ln_silu_bwd",
                          compiler_params=_params("arbitrary"))(c, g.reshape(1, d), b.reshape(1, d), ds)


def loss_grad(y, tgt):
    t, d = y.shape
    tr = _row_tile(t)

    def body(y_ref, t_ref, dy_ref, dy16_ref, sq_ref):
        i = pl.program_id(0)
        err = y_ref[...] - t_ref[...]
        dy = err * (1.0 / d)
        dy_ref[...] = dy
        dy16_ref[...] = dy.astype(BF16)
        _acc_rows(sq_ref, jnp.sum(err * err, axis=0, keepdims=True), i)

    row = pl.BlockSpec((tr, d), lambda i: (i, 0))
    vec = pl.BlockSpec((1, d), lambda i: (0, 0))
    return pl.pallas_call(body, grid=(t // tr,), in_specs=[row, row], out_specs=(row, row, vec),
                          out_shape=(jax.ShapeDtypeStruct((t, d), F32), jax.ShapeDtypeStruct((t, d), BF16),
                                     jax.ShapeDtypeStruct((1, d), F32)),
                          name="loss_grad", compiler_params=_params("arbitrary"))(y, tgt)


def col_sum(a):
    t, d = a.shape
    tr = _row_tile(t)

    def body(a_ref, o_ref):
        _acc_rows(o_ref, jnp.sum(a_ref[...].astype(F32), axis=0, keepdims=True), pl.program_id(0))

    return pl.pallas_call(body, grid=(t // tr,), in_specs=[pl.BlockSpec((tr, d), lambda i: (i, 0))],
                          out_specs=pl.BlockSpec((1, d), lambda i: (0, 0)), out_shape=jax.ShapeDtypeStruct((1, d), F32),
                          name="col_sum", compiler_params=_params("arbitrary"))(a)


def scale_bwd(dx, ypre, scale):
    t, d = dx.shape
    tr = _row_tile(t)

    def body(dx_ref, y_ref, s_ref, dy_ref, ds_ref):
        dxf = dx_ref[...]
        dy_ref[...] = (dxf * s_ref[...]).astype(dy_ref.dtype)
        _acc_rows(ds_ref, jnp.sum(dxf * y_ref[...], axis=0, keepdims=True), pl.program_id(0))

    row = pl.BlockSpec((tr, d), lambda i: (i, 0))
    vec = pl.BlockSpec((1, d), lambda i: (0, 0))
    return pl.pallas_call(body, grid=(t // tr,), in_specs=[row, row, vec], out_specs=(row, vec),
                          out_shape=(jax.ShapeDtypeStruct((t, d), BF16), jax.ShapeDtypeStruct((1, d), F32)),
                          name="scale_bwd", compiler_params=_params("arbitrary"))(dx, ypre, scale.reshape(1, d))


def _chunk_rows(t):
    return _pick(t, (256, 128))


def _load_halo(ref, lead, base, rows, t, first, last, pre, post):
    idx = (lambda s, n: (pl.ds(s, n), slice(None))) if lead is None else (lambda s, n: (lead, pl.ds(s, n), slice(None)))
    parts = []
    if pre:
        start = pl.multiple_of(jnp.maximum(base - pre, 0), pre)
        parts.append(ref[idx(start, pre)].astype(F32) * jnp.where(first, 0.0, 1.0))
    parts.append(ref[idx(base, rows)].astype(F32))
    if post:
        start = pl.multiple_of(jnp.minimum(base + rows, t - post), post)
        parts.append(ref[idx(start, post)].astype(F32) * jnp.where(last, 0.0, 1.0))
    return parts[0] if len(parts) == 1 else jnp.concatenate(parts, axis=0)


def _fold8(x):
    r, c = x.shape
    return x.reshape(r // 8, 8, c).sum(axis=0)


def ffn_gate_fwd(u2, w3, b2):
    _, t, f = u2.shape
    tc = _pick(f, (256, 128))
    rows = _chunk_rows(t)
    nch = t // rows
    halo = 16

    def body(u_ref, w_ref, b_ref, a_ref, at_ref):
        def conv(p, base, first):
            xs = _load_halo(u_ref, p, base, rows, t, first, False, halo, 0)
            wp = w_ref[p]
            return (wp[0:1] * pltpu.roll(xs, 2, 0)[halo:] + wp[1:2] * pltpu.roll(xs, 1, 0)[halo:]
                    + wp[2:3] * xs[halo:] + b_ref[p])

        def chunk(i, carry):
            base = pl.multiple_of(i * rows, rows)
            gate, val = conv(0, base, i == 0), conv(1, base, i == 0)
            act = gate * _sigmoid(gate) * val
            act = act.astype(a_ref.dtype)
            a_ref[pl.ds(base, rows), :] = act
            at_ref[:, pl.ds(base, rows)] = act.T
            return carry

        lax.fori_loop(0, nch, chunk, 0)

    return pl.pallas_call(
        body, grid=(f // tc,),
        in_specs=[pl.BlockSpec((2, t, tc), lambda j: (0, 0, j)), pl.BlockSpec((2, 3, tc), lambda j: (0, 0, j)),
                  pl.BlockSpec((2, 1, tc), lambda j: (0, 0, j))],
        out_specs=(pl.BlockSpec((t, tc), lambda j: (0, j)), pl.BlockSpec((tc, t), lambda j: (j, 0))),
        out_shape=(jax.ShapeDtypeStruct((t, f), BF16), jax.ShapeDtypeStruct((f, t), BF16)),
        name="ffn_gate_fwd", compiler_params=_params("parallel"))(u2, w3, b2)


def ffn_gate_bwd(u2, da, w3, b2):
    _, t, f = u2.shape
    tc = _pick(f, (256, 128))
    rows = _chunk_rows(t)
    nch = t // rows
    halo = 16
    n = rows + 2 * halo

    def body(u_ref, da_ref, w_ref, b_ref, du_ref, dw_ref, db_ref, acc_ref):
        acc_ref[...] = jnp.zeros_like(acc_ref)

        def chunk(i, carry):
            base = pl.multiple_of(i * rows, rows)
            first, last = i == 0, i == nch - 1
            daf = jnp.concatenate(
                [jnp.zeros((halo, tc), F32), _load_halo(da_ref, None, base, rows, t, first, last, 0, halo)], axis=0)
            pre, shifted = [], []
            for p in range(2):
                xs = _load_halo(u_ref, p, base, rows, t, first, last, halo, halo)
                x1, x2 = pltpu.roll(xs, 1, 0), pltpu.roll(xs, 2, 0)
                wp = w_ref[p]
                pre.append(wp[0:1] * x2 + wp[1:2] * x1 + wp[2:3] * xs + b_ref[p])
                shifted.append((x2, x1, xs))
            gate, val = pre
            sg = _sigmoid(gate)
            d_pre = (daf * val * (sg * (1.0 + gate * (1.0 - sg))), daf * gate * sg)
            for p in range(2):
                dp = d_pre[p]
                wp = w_ref[p]
                du = wp[2:3] * dp + wp[1:2] * pltpu.roll(dp, n - 1, 0) + wp[0:1] * pltpu.roll(dp, n - 2, 0)
                du_ref[p, pl.ds(base, rows), :] = du[halo:halo + rows].astype(du_ref.dtype)
                own = dp[halo:halo + rows]
                for k in range(3):
                    acc_ref[p, k] += _fold8(own * shifted[p][k][halo:halo + rows])
                acc_ref[p, 3] += _fold8(own)
            return carry

        lax.fori_loop(0, nch, chunk, 0)
        for p in range(2):
            for k in range(3):
                dw_ref[p, k:k + 1, :] = jnp.sum(acc_ref[p, k], axis=0, keepdims=True)
            db_ref[p] = jnp.sum(acc_ref[p, 3], axis=0, keepdims=True)

    blk = pl.BlockSpec((2, t, tc), lambda j: (0, 0, j))
    wspec = pl.BlockSpec((2, 3, tc), lambda j: (0, 0, j))
    bspec = pl.BlockSpec((2, 1, tc), lambda j: (0, 0, j))
    return pl.pallas_call(
        body, grid=(f // tc,), in_specs=[blk, pl.BlockSpec((t, tc), lambda j: (0, j)), wspec, bspec],
        out_specs=(blk, wspec, bspec),
        out_shape=(jax.ShapeDtypeStruct((2, t, f), BF16), jax.ShapeDtypeStruct((2, 3, f), F32),
                   jax.ShapeDtypeStruct((2, 1, f), F32)),
        scratch_shapes=[pltpu.VMEM((2, 4, 8, tc), F32)], name="ffn_gate_bwd",
        compiler_params=_params("parallel"))(u2, da, w3, b2)


def glu_conv_fwd(u2, w, b):
    _, t, d = u2.shape
    taps = w.shape[0]
    tc = 128
    rows = _chunk_rows(t)
    nch = t // rows
    halo = 32

    def body(u_ref, w_ref, b_ref, c_ref):
        def chunk(i, carry):
            base = pl.multiple_of(i * rows, rows)
            a = _load_halo(u_ref, 0, base, rows, t, i == 0, False, halo, 0)
            g = _load_halo(u_ref, 1, base, rows, t, i == 0, False, halo, 0)
            xs = a * _sigmoid(g)
            acc = w_ref[taps - 1:taps, :] * xs[halo:] + b_ref[...]
            for j in range(taps - 1):
                acc = acc + w_ref[j:j + 1, :] * pltpu.roll(xs, taps - 1 - j, 0)[halo:]
            c_ref[pl.ds(base, rows), :] = acc
            return carry

        lax.fori_loop(0, nch, chunk, 0)

    return pl.pallas_call(
        body, grid=(d // tc,),
        in_specs=[pl.BlockSpec((2, t, tc), lambda j: (0, 0, j)), pl.BlockSpec((taps, tc), lambda j: (0, j)),
                  pl.BlockSpec((1, tc), lambda j: (0, j))],
        out_specs=pl.BlockSpec((t, tc), lambda j: (0, j)), out_shape=jax.ShapeDtypeStruct((t, d), F32),
        name="glu_conv_fwd", compiler_params=_params("parallel"))(u2, w, b)


def glu_conv_bwd(u2, dc, w):
    _, t, d = u2.shape
    taps = w.shape[0]
    tc = 128
    rows = _chunk_rows(t)
    nch = t // rows
    halo = 32
    n = rows + halo

    def body(u_ref, dc_ref, w_ref, du_ref, dw_ref, dwb_ref, dbin_ref, acc_ref, bacc_ref):
        acc_ref[...] = jnp.zeros_like(acc_ref)
        bacc_ref[...] = jnp.zeros_like(bacc_ref)

        def chunk(i, carry):
            base = pl.multiple_of(i * rows, rows)
            first, last = i == 0, i == nch - 1
            a = _load_halo(u_ref, 0, base, rows, t, first, False, halo, 0)
            g = _load_halo(u_ref, 1, base, rows, t, first, False, halo, 0)
            sg = _sigmoid(g)
            xs = a * sg
            dcs = _load_halo(dc_ref, None, base, rows, t, first, last, 0, halo)
            own = dcs[:rows]
            dglu = w_ref[taps - 1:taps, :] * own
            acc_ref[taps - 1] += _fold8(own * xs[halo:])
            for j in range(taps - 1):
                s = taps - 1 - j
                dglu = dglu + w_ref[j:j + 1, :] * pltpu.roll(dcs, n - s, 0)[:rows]
                acc_ref[j] += _fold8(own * pltpu.roll(xs, s, 0)[halo:])
            a_c, sg_c = a[halo:], sg[halo:]
            da = dglu * sg_c
            dg = dglu * a_c * sg_c * (1.0 - sg_c)
            du_ref[0, pl.ds(base, rows), :] = da.astype(du_ref.dtype)
            du_ref[1, pl.ds(base, rows), :] = dg.astype(du_ref.dtype)
            bacc_ref[0] += _fold8(own)
            bacc_ref[1] += _fold8(da)
            bacc_ref[2] += _fold8(dg)
            return carry

        lax.fori_loop(0, nch, chunk, 0)
        for j in range(taps):
            dw_ref[j:j + 1, :] = jnp.sum(acc_ref[j], axis=0, keepdims=True)
        dwb_ref[...] = jnp.sum(bacc_ref[0], axis=0, keepdims=True)
        dbin_ref[0] = jnp.sum(bacc_ref[1], axis=0, keepdims=True)
        dbin_ref[1] = jnp.sum(bacc_ref[2], axis=0, keepdims=True)

    blk = pl.BlockSpec((2, t, tc), lambda j: (0, 0, j))
    col = pl.BlockSpec((t, tc), lambda j: (0, j))
    return pl.pallas_call(
        body, grid=(d // tc,), in_specs=[blk, col, pl.BlockSpec((taps, tc), lambda j: (0, j))],
        out_specs=(blk, pl.BlockSpec((taps, tc), lambda j: (0, j)), pl.BlockSpec((1, tc), lambda j: (0, j)),
                   pl.BlockSpec((2, 1, tc), lambda j: (0, 0, j))),
        out_shape=(jax.ShapeDtypeStruct((2, t, d), BF16), jax.ShapeDtypeStruct((taps, d), F32),
                   jax.ShapeDtypeStruct((1, d), F32), jax.ShapeDtypeStruct((2, 1, d), F32)),
        scratch_shapes=[pltpu.VMEM((taps, 8, tc), F32), pltpu.VMEM((3, 8, tc), F32)], name="glu_conv_bwd",
        compiler_params=_params("parallel"))(u2, dc, w)


def _pool_select(grp, levels):
    out = levels[3]
    for k in (2, 1, 0):
        out = jnp.where(grp == k, levels[k], out)
    return out


def _pool_count(base, rows, tc, grp):
    tpos = (base + lax.broadcasted_iota(jnp.int32, (rows, tc), 0) + 1).astype(F32)
    window = jnp.left_shift(2, grp).astype(F32)
    return jnp.minimum(tpos, window)


def pool_fwd(h):
    t, d = h.shape
    pg = d // POOL_GROUPS
    tc = _pick(pg, (256, 128))
    rows = _chunk_rows(t)
    nch = t // rows
    halo = 16

    def body(h_ref, o_ref, ot_ref):
        grp = (pl.program_id(0) * tc) // pg

        def chunk(i, carry):
            base = pl.multiple_of(i * rows, rows)
            xs = _load_halo(h_ref, None, base, rows, t, i == 0, False, halo, 0)
            levels, cur = [], xs
            for k in range(4):
                cur = cur + pltpu.roll(cur, 1 << k, 0)
                levels.append(cur[halo:])
            pooled = _pool_select(grp, levels) / _pool_count(base, rows, tc, grp)
            mixed = pooled - xs[halo:]
            o_ref[pl.ds(base, rows), :] = mixed.astype(o_ref.dtype)
            ot_ref[:, pl.ds(base, rows)] = mixed.T.astype(ot_ref.dtype)
            return carry

        lax.fori_loop(0, nch, chunk, 0)

    col = pl.BlockSpec((t, tc), lambda j: (0, j))
    return pl.pallas_call(body, grid=(d // tc,), in_specs=[col], out_specs=(col, pl.BlockSpec((tc, t), lambda j: (j, 0))),
                          out_shape=(jax.ShapeDtypeStruct((t, d), BF16), jax.ShapeDtypeStruct((d, t), BF16)),
                          name="pool_fwd", compiler_params=_params("parallel"))(h)


def pool_bwd(dmix):
    t, d = dmix.shape
    pg = d // POOL_GROUPS
    tc = _pick(pg, (256, 128))
    rows = _chunk_rows(t)
    nch = t // rows
    halo = 16
    n = rows + halo

    def body(d_ref, o_ref):
        grp = (pl.program_id(0) * tc) // pg

        def chunk(i, carry):
            base = pl.multiple_of(i * rows, rows)
            ds = _load_halo(d_ref, None, base, rows, t, i == 0, i == nch - 1, 0, halo)
            levels, cur = [], ds / _pool_count(base, n, tc, grp)
            for k in range(4):
                cur = cur + pltpu.roll(cur, n - (1 << k), 0)
                levels.append(cur[:rows])
            o_ref[pl.ds(base, rows), :] = _pool_select(grp, levels) - ds[:rows]
            return carry

        lax.fori_loop(0, nch, chunk, 0)

    col = pl.BlockSpec((t, tc), lambda j: (0, j))
    return pl.pallas_call(body, grid=(d // tc,), in_specs=[col], out_specs=col,
                          out_shape=jax.ShapeDtypeStruct((t, d), F32), name="pool_bwd",
                          compiler_params=_params("parallel"))(dmix)


def mm_groups(a, wg, *, mode, out_dtype, scale=None, resid=None, raw_dtype=None, name):
    t, d = a.shape
    pg = wg.shape[1]
    tm = _pick(t, (1024, 512, 256, 128))
    tn = pg
    if mode == "nn":
        dims, b_blk, b_idx = NN, (None, pg, tn), (lambda i, j, kk: (j, 0, 0))
    else:
        dims, b_blk, b_idx = NT, (None, tn, pg), (lambda i, j, kk: (j, 0, 0))
    raw_shape = None if raw_dtype is None else jax.ShapeDtypeStruct((t, d), raw_dtype)
    return _mm(a, wg, dims=dims, sizes=(t, d, pg), plan=(tm, tn, pg, False), a_blk=(tm, pg), a_idx=lambda i, j, kk: (i, j),
               b_blk=b_blk, b_idx=b_idx, o_blk=(tm, tn), o_idx=lambda i, j, kk: (i, j),
               out_shape=jax.ShapeDtypeStruct((t, d), out_dtype), name=name,
               scale=None if scale is None else scale.reshape(1, d), vec_blk=(1, tn), vec_idx=lambda i, j, kk: (0, j),
               resid=resid, raw_shape=raw_shape)


def mm_groups_wgrad(at, dy, groups, *, out_dtype, name):
    d, t = at.shape
    pg = d // groups
    tk = _pick(t, (2048, 1024, 512, 256, 128))
    return _mm(at, dy, dims=NN, sizes=(d, pg, t), plan=(pg, pg, tk, False), a_blk=(pg, tk), a_idx=lambda i, j, kk: (i, kk),
               b_blk=(tk, pg), b_idx=lambda i, j, kk: (kk, i), o_blk=(None, pg, pg), o_idx=lambda i, j, kk: (i, 0, 0),
               out_shape=jax.ShapeDtypeStruct((groups, pg, pg), out_dtype), name=name)


def _split_dot(y, p):
    hi = y.astype(BF16)
    r1 = y - hi.astype(F32)
    mid = r1.astype(BF16)
    lo = (r1 - mid.astype(F32)).astype(BF16)
    pb = p.astype(BF16)
    dot = lambda v: jnp.dot(v, pb, preferred_element_type=F32)
    return (dot(hi) + dot(mid)) + dot(lo)


def rope_tables(positions):
    half = ROT_DIM // 2
    inv_freq = ROPE_THETA ** (-jnp.arange(0, ROT_DIM, 2, dtype=F32) / ROT_DIM)
    ang = positions.astype(F32)[:, None] * inv_freq
    t = positions.shape[0]
    cos, sin = jnp.cos(ang), jnp.sin(ang)
    rest = HEAD_DIM - ROT_DIM
    cosf = jnp.concatenate([cos, cos, jnp.ones((t, rest), F32)], axis=1)
    sinf = jnp.concatenate([-sin, sin, jnp.zeros((t, rest), F32)], axis=1)
    idx = jnp.arange(HEAD_DIM)
    partner = jnp.where(idx < half, idx + half, jnp.where(idx < ROT_DIM, idx - half, idx))
    pmat = (idx[:, None] == partner[None, :]).astype(F32)
    return cosf, sinf, pmat


def qk_rope_fwd(x, g, cosf, sinf, pmat, out_scale):
    hn, t, hd = x.shape
    tq = _pick(t, (512, 256, 128))

    def body(x_ref, g_ref, c_ref, s_ref, p_ref, o_ref):
        xf = x_ref[...]
        r = lax.rsqrt(jnp.mean(xf * xf, axis=-1, keepdims=True) + EPS)
        y = xf * r * g_ref[...]
        rot = y * c_ref[...] + _split_dot(y, p_ref[...]) * s_ref[...]
        o_ref[...] = (rot * out_scale).astype(o_ref.dtype)

    blk = pl.BlockSpec((None, tq, hd), lambda h, i: (h, i, 0))
    tab = pl.BlockSpec((tq, hd), lambda h, i: (i, 0))
    return pl.pallas_call(
        body, grid=(hn, t // tq),
        in_specs=[blk, pl.BlockSpec((1, hd), lambda h, i: (0, 0)), tab, tab, pl.BlockSpec((hd, hd), lambda h, i: (0, 0))],
        out_specs=blk, out_shape=jax.ShapeDtypeStruct((hn, t, hd), BF16), name="qk_rope_fwd",
        compiler_params=_params("parallel", "parallel"))(x, g.reshape(1, hd), cosf, sinf, pmat)


def qk_rope_bwd(dy, x, g, cosf, sinf, pmat_t, in_scale):
    hn, t, hd = x.shape
    tq = _pick(t, (512, 256, 128))

    def body(dy_ref, x_ref, g_ref, c_ref, s_ref, p_ref, dx_ref, dg_ref):
        step = pl.program_id(0) * pl.num_programs(1) + pl.program_id(1)
        dr = dy_ref[...] * in_scale
        dyn = dr * c_ref[...] + _split_dot(dr * s_ref[...], p_ref[...])
        xf = x_ref[...]
        r = lax.rsqrt(jnp.mean(xf * xf, axis=-1, keepdims=True) + EPS)
        xhat = xf * r
        dxh = dyn * g_ref[...]
        m = jnp.mean(dxh * xhat, axis=-1, keepdims=True)
        dx_ref[...] = r * (dxh - xhat * m)
        _acc_rows(dg_ref, jnp.sum(dyn * xhat, axis=0, keepdims=True), step)

    blk = pl.BlockSpec((None, tq, hd), lambda h, i: (h, i, 0))
    tab = pl.BlockSpec((tq, hd), lambda h, i: (i, 0))
    vec = pl.BlockSpec((1, hd), lambda h, i: (0, 0))
    return pl.pallas_call(
        body, grid=(hn, t // tq),
        in_specs=[blk, blk, vec, tab, tab, pl.BlockSpec((hd, hd), lambda h, i: (0, 0))],
        out_specs=(blk, vec), out_shape=(jax.ShapeDtypeStruct((hn, t, hd), F32), jax.ShapeDtypeStruct((1, hd), F32)),
        name="qk_rope_bwd", compiler_params=_params("arbitrary", "arbitrary"))(dy, x, g.reshape(1, hd), cosf, sinf, pmat_t)


NEG_BIG = -1e30


ATT_ROWS = KV_GROUP * ATT_BLOCK


def _att_mask(i, rows):
    shape = (rows, 2 * ATT_BLOCK)
    qi = jnp.bitwise_and(lax.broadcasted_iota(jnp.int32, shape, 0), ATT_BLOCK - 1)
    kj = lax.broadcasted_iota(jnp.int32, shape, 1)
    cur = jnp.logical_and(kj >= ATT_BLOCK, kj - ATT_BLOCK <= qi)
    prev = jnp.logical_and(jnp.logical_and(kj < ATT_BLOCK, kj > qi), i > 0)
    return jnp.logical_or(cur, prev)


def _att_sinks(sink_ref, kv):
    return jnp.concatenate([jnp.full((ATT_BLOCK, 1), sink_ref[kv * KV_GROUP + g], F32) for g in range(KV_GROUP)], axis=0)


def _att_probs(q, k2, mask, sink):
    s = jnp.where(mask, lax.dot_general(q, k2, NT, preferred_element_type=F32), NEG_BIG)
    m = jnp.maximum(jnp.max(s, axis=-1, keepdims=True), sink)
    p = jnp.exp(s - m)
    p_s = jnp.exp(sink - m)
    return p, p_s, jnp.sum(p, axis=-1, keepdims=True) + p_s


def _att_specs(t):
    nb = t // ATT_BLOCK
    qblk = pl.BlockSpec((KV_GROUP, ATT_BLOCK, HEAD_DIM), lambda kv, i: (kv, i, 0))
    cur = pl.BlockSpec((None, ATT_BLOCK, HEAD_DIM), lambda kv, i: (kv, i, 0))
    prev = pl.BlockSpec((None, ATT_BLOCK, HEAD_DIM), lambda kv, i: (kv, jnp.maximum(i - 1, 0), 0))
    return nb, qblk, cur, prev, pl.BlockSpec(memory_space=pltpu.SMEM)


def attn_fwd(q, k, v, sinks):
    h, t, hd = q.shape
    nb, qblk, cur, prev, smem = _att_specs(t)

    def body(q_ref, kc_ref, kp_ref, vc_ref, vp_ref, sink_ref, o_ref):
        kv, i = pl.program_id(0), pl.program_id(1)
        k2 = jnp.concatenate([kp_ref[...], kc_ref[...]], axis=0)
        v2 = jnp.concatenate([vp_ref[...], vc_ref[...]], axis=0)
        mask = _att_mask(i, ATT_BLOCK)
        for g in range(KV_GROUP):
            p, _, denom = _att_probs(q_ref[g], k2, mask, sink_ref[kv * KV_GROUP + g])
            o_ref[g] = (jnp.dot(p.astype(BF16), v2, preferred_element_type=F32) / denom).astype(o_ref.dtype)

    return pl.pallas_call(
        body, grid=(h // KV_GROUP, nb), in_specs=[qblk, cur, prev, cur, prev, smem], out_specs=qblk,
        out_shape=jax.ShapeDtypeStruct((h, t, hd), BF16), name="attn_fwd",
        compiler_params=_params("parallel", "parallel"))(q, k, k, v, v, sinks)


def attn_bwd(q, k, v, do, sinks):
    h, t, hd = q.shape
    kvh = h // KV_GROUP
    nb, qblk, cur, prev, smem = _att_specs(t)

    def body(q_ref, kc_ref, kp_ref, vc_ref, vp_ref, do_ref, sink_ref, dq_ref, dk_ref, dv_ref, dsk_ref):
        kv, i = pl.program_id(0), pl.program_id(1)

        @pl.when(i == 0)
        def _():
            dk_ref[...] = jnp.zeros_like(dk_ref)
            dv_ref[...] = jnp.zeros_like(dv_ref)
            dsk_ref[...] = jnp.zeros_like(dsk_ref)

        k2 = jnp.concatenate([kp_ref[...], kc_ref[...]], axis=0)
        v2 = jnp.concatenate([vp_ref[...], vc_ref[...]], axis=0)
        q = q_ref[...].reshape(ATT_ROWS, hd)
        p, p_s, denom = _att_probs(q, k2, _att_mask(i, ATT_ROWS), _att_sinks(sink_ref, kv))
        inv = 1.0 / denom
        pn = p * inv
        dob = do_ref[...].reshape(ATT_ROWS, hd).astype(BF16)
        dp = lax.dot_general(dob, v2, NT, preferred_element_type=F32)
        dsum = jnp.sum(pn * dp, axis=-1, keepdims=True)
        ds = (pn * (dp - dsum)).astype(BF16)
        dq_ref[...] = jnp.dot(ds, k2, preferred_element_type=F32).reshape(KV_GROUP, ATT_BLOCK, hd)
        dk2 = lax.dot_general(ds, q, TN, preferred_element_type=F32)
        dv2 = lax.dot_general(pn.astype(BF16), dob, TN, preferred_element_type=F32)
        dsink = p_s * inv * dsum
        dsink_rows = [jnp.broadcast_to(-jnp.sum(dsink[g * ATT_BLOCK:(g + 1) * ATT_BLOCK], axis=0, keepdims=True), (1, 128))
                      for g in range(KV_GROUP)]
        here = pl.ds(pl.multiple_of(i * ATT_BLOCK, ATT_BLOCK), ATT_BLOCK)
        before = pl.ds(pl.multiple_of(jnp.maximum(i - 1, 0) * ATT_BLOCK, ATT_BLOCK), ATT_BLOCK)
        dk_ref[before, :] += dk2[:ATT_BLOCK]
        dv_ref[before, :] += dv2[:ATT_BLOCK]
        dk_ref[here, :] += dk2[ATT_BLOCK:]
        dv_ref[here, :] += dv2[ATT_BLOCK:]
        dsk_ref[...] += jnp.concatenate(dsink_rows, axis=0)

    whole = pl.BlockSpec((None, t, hd), lambda kv, i: (kv, 0, 0))
    return pl.pallas_call(
        body, grid=(kvh, nb), in_specs=[qblk, cur, prev, cur, prev, qblk, smem],
        out_specs=(qblk, whole, whole, pl.BlockSpec((None, KV_GROUP, 128), lambda kv, i: (kv, 0, 0))),
        out_shape=(jax.ShapeDtypeStruct((h, t, hd), F32), jax.ShapeDtypeStruct((kvh, t, hd), F32),
                   jax.ShapeDtypeStruct((kvh, t, hd), F32), jax.ShapeDtypeStruct((kvh, KV_GROUP, 128), F32)),
        name="attn_bwd", compiler_params=_params("parallel", "arbitrary"))(q, k, k, v, v, do, sinks)


def _ffn_taps(w, b):
    f2 = w.shape[1]
    return w.reshape(FFN_TAPS, 2, f2 // 2).transpose(1, 0, 2), b.reshape(2, 1, f2 // 2)


def ffn_fwd(x, W, p, tables=None, token=None):
    h, ht = rms_fwd(x, W[p + "ffn_norm_g"], BF16, token, transposed=True)
    u2 = mm_nn_cols(h, W[p + "ffn_w_up"], split=True, out_dtype=BF16, name="ffn_up")
    w3, b2 = _ffn_taps(W[p + "ffn_dw_w"], W[p + "ffn_dw_b"])
    a, at = ffn_gate_fwd(u2, w3, b2)
    y = mm_nn(a, W[p + "ffn_w_down"], out_dtype=F32, resid=x, name="ffn_down")
    return y, (x, ht, u2, at)


def ffn_bwd(saved, W, p, dy, tables=None):
    x, ht, u2, at = saved
    dyf, dyb = dy
    w3, b2 = _ffn_taps(W[p + "ffn_dw_w"], W[p + "ffn_dw_b"])
    grads = {p + "ffn_w_down": mm_wgrad(at, dyb, out_dtype=BF16, name="ffn_down_dw")}
    da = mm_nt(dyb, W[p + "ffn_w_down"], out_dtype=BF16, name="ffn_down_dx")
    du2, dw3, db2 = ffn_gate_bwd(u2, da, w3, b2)
    grads[p + "ffn_dw_w"] = dw3.transpose(1, 0, 2).reshape(FFN_TAPS, -1)
    grads[p + "ffn_dw_b"] = db2.reshape(-1)
    grads[p + "ffn_w_up"] = mm_wgrad_cols(ht, du2, split=True, out_dtype=BF16, name="ffn_up_dw")
    dh = mm_nt_cols(du2, W[p + "ffn_w_up"], split=True, out_dtype=F32, name="ffn_up_dx")
    return (x, p + "ffn_norm_g", dh), grads


def conf_fwd(x, W, p, tables=None, token=None):
    d = x.shape[1]
    h, ht = rms_fwd(x, W[p + "norm_g"], BF16, token, transposed=True)
    u2 = mm_nn_cols(h, W[p + "a_w_in"], split=True, out_dtype=BF16, bias=W[p + "a_b_in"], name="conf_in")
    c = glu_conv_fwd(u2, W[p + "a_dw_w"], W[p + "a_dw_b"].reshape(1, d))
    s, st = ln_silu_fwd(c, W[p + "a_ln_g"], W[p + "a_ln_b"])
    y = mm_nn(s, W[p + "a_w_out"], out_dtype=F32, bias=W[p + "a_b_out"], resid=x, name="conf_out")
    return y, (x, ht, u2, c, st)


def conf_bwd(saved, W, p, dy, tables=None):
    x, ht, u2, c, st = saved
    dyf, dyb = dy
    grads = {p + "a_w_out": mm_wgrad(st, dyb, out_dtype=BF16, name="conf_out_dw"), p + "a_b_out": col_sum(dyf).reshape(-1)}
    ds = mm_nt(dyb, W[p + "a_w_out"], out_dtype=BF16, name="conf_out_dx")
    dc, dlg, dlb = ln_silu_bwd(c, W[p + "a_ln_g"], W[p + "a_ln_b"], ds)
    grads[p + "a_ln_g"], grads[p + "a_ln_b"] = dlg.reshape(-1), dlb.reshape(-1)
    du2, ddw, ddwb, dbin = glu_conv_bwd(u2, dc, W[p + "a_dw_w"])
    grads[p + "a_dw_w"], grads[p + "a_dw_b"], grads[p + "a_b_in"] = ddw, ddwb.reshape(-1), dbin.reshape(-1)
    grads[p + "a_w_in"] = mm_wgrad_cols(ht, du2, split=True, out_dtype=BF16, name="conf_in_dw")
    dh = mm_nt_cols(du2, W[p + "a_w_in"], split=True, out_dtype=F32, name="conf_in_dx")
    return (x, p + "norm_g", dh), grads


def pool_layer_fwd(x, W, p, tables=None, token=None):
    h = rms_fwd(x, W[p + "norm_g"], F32, token)
    mixed, mixed_t = pool_fwd(h)
    y, ypre = mm_groups(mixed, W[p + "b_w_group"], mode="nn", out_dtype=F32, scale=W[p + "b_scale"], resid=x,
                        raw_dtype=F32, name="pool_mix")
    return y, (x, mixed_t, ypre)


def pool_layer_bwd(saved, W, p, dy, tables=None):
    x, mixed_t, ypre = saved
    dyf, dyb = dy
    dyp, dscale = scale_bwd(dyf, ypre, W[p + "b_scale"])
    grads = {p + "b_scale": dscale.reshape(-1),
             p + "b_w_group": mm_groups_wgrad(mixed_t, dyp, POOL_GROUPS, out_dtype=BF16, name="pool_mix_dw")}
    dmix = mm_groups(dyp, W[p + "b_w_group"], mode="nt", out_dtype=F32, name="pool_mix_dx")
    dh = pool_bwd(dmix)
    return (x, p + "norm_g", dh), grads


def _heads(a, n):
    t = a.shape[0]
    return a.reshape(t, n, HEAD_DIM).transpose(1, 0, 2)


def _unheads(a):
    n, t, _ = a.shape
    return a.transpose(1, 0, 2).reshape(t, n * HEAD_DIM)


def attn_layer_fwd(x, W, p, tables, token=None):
    d = x.shape[1]
    nh = d // HEAD_DIM
    nkv = nh // KV_GROUP
    cosf, sinf, pmat = tables
    h, ht = rms_fwd(x, W[p + "norm_g"], BF16, token, transposed=True)
    qkv = mm_nn_cols(h, W[p + "c_w_qkv"], split=False, out_dtype=F32, name="att_qkv")
    q = _heads(qkv[:, :d], nh)
    k = _heads(qkv[:, d:d + nkv * HEAD_DIM], nkv)
    v = _heads(qkv[:, d + nkv * HEAD_DIM:], nkv).astype(BF16)
    qr = qk_rope_fwd(q, W[p + "c_q_norm_g"], cosf, sinf, pmat, HEAD_DIM ** -0.5)
    kr = qk_rope_fwd(k, W[p + "c_k_norm_g"], cosf, sinf, pmat, 1.0)
    o = attn_fwd(qr, kr, v, W[p + "c_sinks"])
    o2 = _unheads(o)
    y = mm_nn(o2, W[p + "c_w_o"], out_dtype=F32, resid=x, name="att_out")
    return y, (x, ht, q, k, v, qr, kr, o2)


def attn_layer_bwd(saved, W, p, dy, tables):
    x, ht, q, k, v, qr, kr, o2 = saved
    dyf, dyb = dy
    cosf, sinf, pmat = tables
    nh = q.shape[0]
    grads = {p + "c_w_o": mm_wgrad(o2.T, dyb, out_dtype=BF16, name="att_out_dw")}
    do = _heads(mm_nt(dyb, W[p + "c_w_o"], out_dtype=BF16, name="att_out_dx"), nh)
    dqr, dkr, dv, dsk = attn_bwd(qr, kr, v, do, W[p + "c_sinks"])
    grads[p + "c_sinks"] = dsk[:, :, 0].reshape(-1)
    dq, dqg = qk_rope_bwd(dqr, q, W[p + "c_q_norm_g"], cosf, sinf, pmat.T, HEAD_DIM ** -0.5)
    dk, dkg = qk_rope_bwd(dkr, k, W[p + "c_k_norm_g"], cosf, sinf, pmat.T, 1.0)
    grads[p + "c_q_norm_g"], grads[p + "c_k_norm_g"] = dqg.reshape(-1), dkg.reshape(-1)
    dqkv = jnp.concatenate([_unheads(dq), _unheads(dk), _unheads(dv)], axis=1).astype(BF16)
    grads[p + "c_w_qkv"] = mm_wgrad_cols(ht, dqkv, split=False, out_dtype=BF16, name="att_qkv_dw")
    dh = mm_nt_cols(dqkv, W[p + "c_w_qkv"], split=False, out_dtype=F32, name="att_qkv_dx")
    return (x, p + "norm_g", dh), grads


def local_step(x, positions, tgt, W, comm=None):
    tables = rope_tables(positions)
    saved = []
    for g, (fwd, _, p) in enumerate(SUBLAYERS):
        token = comm.forward_begins(g, W) if comm else None
        x, s = fwd(x, W, p, tables, token)
        saved.append(s)
        if comm:
            comm.forward_ends(g, x, W)
    dyf, dyb, sq = loss_grad(x, tgt)
    loss = 0.5 * jnp.sum(sq) / x.shape[1]
    grads = {}
    for g in reversed(range(len(SUBLAYERS))):
        _, bwd, p = SUBLAYERS[g]
        (xin, gain, dh), gr = bwd(saved[g], W, p, (dyf, dyb), tables)
        token = comm.gradients_ready(g, gr) if comm else None
        dyf, dyb, dg = rms_bwd(xin, W[gain], dh, dyf, token)
        gr[gain] = dg.reshape(-1)
        grads.update(gr)
    return loss, dyf, grads


SUBLAYERS = [(conf_fwd, conf_bwd, "l0_"), (ffn_fwd, ffn_bwd, "l0_"), (pool_layer_fwd, pool_layer_bwd, "l1_"),
             (ffn_fwd, ffn_bwd, "l1_"), (attn_layer_fwd, attn_layer_bwd, "l2_"), (ffn_fwd, ffn_bwd, "l2_"),
             (conf_fwd, conf_bwd, "l3_"), (ffn_fwd, ffn_bwd, "l3_")]
SUBLAYER_WEIGHTS = {conf_fwd: ("a_w_in", "a_w_out", "a_dw_w"), ffn_fwd: ("ffn_w_up", "ffn_w_down", "ffn_dw_w"),
                    pool_layer_fwd: ("b_w_group",), attn_layer_fwd: ("c_w_qkv", "c_w_o")}


def sublayer_weight_names(g):
    fwd, _, p = SUBLAYERS[g]
    return [p + n for n in SUBLAYER_WEIGHTS[fwd]]


ANY = pl.BlockSpec(memory_space=pl.ANY)


def _place():
    x, y, c = lax.axis_index("x"), lax.axis_index("y"), lax.axis_index("c")
    chips = [(1 - x, y), (x, 1 - y), (1 - x, 1 - y)]
    return x, y, c, 2 * x + y, (x, y, 1 - c), chips


def _half(rows, which):
    return pl.ds(which * (rows // 2), rows // 2)


def place_block(shard, chip_core, out_dtype):
    rows, cols = shard.shape
    tr = rows
    for cand in (512, 256, 128, 64, 32, 16):
        if rows % cand == 0 and cand * cols * 4 <= (2 << 20):
            tr = cand
            break

    def body(pos_ref, s_ref, o_ref):
        o_ref[...] = s_ref[...].astype(o_ref.dtype)

    grid_spec = pltpu.PrefetchScalarGridSpec(
        num_scalar_prefetch=1, grid=(rows // tr,), in_specs=[pl.BlockSpec((tr, cols), lambda i, pos: (i, 0))],
        out_specs=pl.BlockSpec((None, tr, cols), lambda i, pos: (pos[0], i, 0)))
    return pl.pallas_call(body, grid_spec=grid_spec, out_shape=jax.ShapeDtypeStruct((N_CHIPS, rows, cols), out_dtype),
                          name="place_block", compiler_params=_params("parallel"))(chip_core, shard)


def all_gather_chips(bufs):
    n = len(bufs)

    def body(*refs):
        outs = refs[n:2 * n]
        ici_send, ici_recv, d2d_send, d2d_recv = refs[2 * n:]
        x, y, c, k, sibling, chips = _place()

        def rdma(src, dst, send, recv, dev):
            return pltpu.make_async_remote_copy(src_ref=src, dst_ref=dst, send_sem=send, recv_sem=recv,
                                                device_id=dev, device_id_type=MESH)

        sends = []
        for t in range(n):
            rows = bufs[t].shape[1]
            for j, (px, py) in enumerate(chips):
                mine = outs[t].at[k, _half(rows, c)]
                sends.append(rdma(mine, mine, ici_send.at[t, j], ici_recv.at[t, j], (px, py, c)))
        for cp in sends:
            cp.start()
        for t in range(n):
            rows = bufs[t].shape[1]
            for j, (px, py) in enumerate(chips):
                landed = outs[t].at[2 * px + py, _half(rows, c)]
                rdma(landed, landed, ici_send.at[t, j], ici_recv.at[t, j], sibling).wait_recv()
                fwd = rdma(landed, landed, d2d_send.at[t, j], d2d_recv.at[t, j], sibling)
                fwd.start()
                sends.append(fwd)
        for t in range(n):
            rows = bufs[t].shape[1]
            for j, (px, py) in enumerate(chips):
                other = outs[t].at[2 * px + py, _half(rows, 1 - c)]
                rdma(other, other, d2d_send.at[t, j], d2d_recv.at[t, j], sibling).wait_recv()
        for cp in sends:
            cp.wait_send()

    return pl.pallas_call(
        body, in_specs=[ANY] * n, out_specs=[ANY] * n,
        out_shape=[jax.ShapeDtypeStruct(b.shape, b.dtype) for b in bufs],
        input_output_aliases={t: t for t in range(n)},
        scratch_shapes=[pltpu.SemaphoreType.DMA((n, 3))] * 4,
        name="all_gather_chips", compiler_params=pltpu.CompilerParams())(*bufs)


def _sum_rows_tile(rows):
    return _pick(rows, (256, 352, 128, 64, 32, 16))


def add_sibling_half(g, land, core):
    nb, half, cols = land.shape
    tr = _sum_rows_tile(half)
    nrb = half // tr

    def body(c_ref, g_ref, l_ref, o_ref):
        o_ref[...] = (g_ref[...].astype(F32) + l_ref[...].astype(F32)).astype(o_ref.dtype)

    spec = pl.BlockSpec((None, tr, cols), lambda b, i, c_ref: (b, i, 0))
    grid_spec = pltpu.PrefetchScalarGridSpec(
        num_scalar_prefetch=1, grid=(nb, nrb),
        in_specs=[pl.BlockSpec((None, tr, cols), lambda b, i, c_ref: (b, c_ref[1] * nrb + i, 0)), spec], out_specs=spec)
    return pl.pallas_call(body, grid_spec=grid_spec, out_shape=jax.ShapeDtypeStruct(land.shape, BF16),
                          name="add_sibling_half", compiler_params=_params("parallel", "parallel"))(core, g, land)


def sum_chip_blocks(p, l2, chip_core):
    nb, half, cols = l2.shape
    tr = _sum_rows_tile(half)
    nrb = half // tr

    def body(pos_ref, p_ref, l_ref, o_ref):
        acc = p_ref[...].astype(F32)
        for b in range(nb):
            acc = acc + l_ref[b].astype(F32)
        o_ref[...] = acc

    grid_spec = pltpu.PrefetchScalarGridSpec(
        num_scalar_prefetch=1, grid=(nrb,),
        in_specs=[pl.BlockSpec((None, tr, cols), lambda i, pos: (pos[0], i, 0)),
                  pl.BlockSpec((nb, tr, cols), lambda i, pos: (0, i, 0))],
        out_specs=pl.BlockSpec((tr, cols), lambda i, pos: (pos[1] * nrb + i, 0)))
    return pl.pallas_call(body, grid_spec=grid_spec, out_shape=jax.ShapeDtypeStruct((2 * half, cols), F32),
                          name="sum_chip_blocks", compiler_params=_params("parallel"))(chip_core, p, l2)


HBM_SPEC = pl.BlockSpec(memory_space=pltpu.HBM)
SEM_SPEC = pl.BlockSpec(memory_space=pltpu.SEMAPHORE)
SPLIT_EFFECT = pltpu.SideEffectType.DATAFLOW_SIDE_EFFECTING


def _in_hbm(v):
    return pltpu.with_memory_space_constraint(v, pltpu.HBM)


def _gather_ici_copies(bufs, refs, send, recv):
    x, y, c, k, sibling, chips = _place()
    cps = []
    for t in range(len(bufs)):
        rows = bufs[t].shape[1]
        for j, (px, py) in enumerate(chips):
            cps.append(pltpu.make_async_remote_copy(
                src_ref=refs[t].at[k, _half(rows, c)], dst_ref=refs[t].at[k, _half(rows, c)],
                send_sem=send.at[3 * t + j], recv_sem=recv.at[3 * t + j], device_id=(px, py, c), device_id_type=MESH))
    return cps


def gather_ici_start(bufs, after, name):
    n = len(bufs)

    def body(*refs):
        send, recv, token = refs[n + 1], refs[n + 2], refs[-1]
        for cp in _gather_ici_copies(bufs, refs[:n], send, recv):
            cp.start()
        token[...] = jnp.zeros_like(token)

    outs = pl.pallas_call(
        body, name=name, in_specs=[HBM_SPEC] * n + [ANY],
        out_shape=(pltpu.SemaphoreType.DMA((3 * n,)), pltpu.SemaphoreType.DMA((3 * n,)),
                   *[pltpu.HBM(b.shape, b.dtype) for b in bufs], jax.ShapeDtypeStruct(TOKEN_SHAPE, F32)),
        out_specs=(SEM_SPEC, SEM_SPEC, *[HBM_SPEC] * n, pl.BlockSpec(memory_space=pltpu.VMEM)),
        input_output_aliases={t: 2 + t for t in range(n)},
        compiler_params=pltpu.CompilerParams(has_side_effects=SPLIT_EFFECT))(*[_in_hbm(b) for b in bufs], after)
    return outs[0], outs[1], list(outs[2:2 + n]), outs[-1]


def gather_ici_wait(send, recv, bufs, after, name):
    n = len(bufs)

    def body(*refs):
        x, y, c, k, sibling, chips = _place()
        for t in range(n):
            rows = bufs[t].shape[1]
            for j, (px, py) in enumerate(chips):
                cp = pltpu.make_async_remote_copy(
                    src_ref=refs[t].at[k, _half(rows, c)], dst_ref=refs[t].at[2 * px + py, _half(rows, c)],
                    send_sem=refs[n].at[3 * t + j], recv_sem=refs[n + 1].at[3 * t + j], device_id=(px, py, c),
                    device_id_type=MESH)
                cp.wait_send()
                cp.wait_recv()

    return list(pl.pallas_call(
        body, name=name, in_specs=[HBM_SPEC] * n + [SEM_SPEC, SEM_SPEC, ANY],
        out_shape=tuple(pltpu.HBM(b.shape, b.dtype) for b in bufs), out_specs=tuple([HBM_SPEC] * n),
        input_output_aliases={t: t for t in range(n)},
        compiler_params=pltpu.CompilerParams(has_side_effects=SPLIT_EFFECT))(*bufs, send, recv, after))


def gather_forward_sibling(bufs):
    n = len(bufs)

    def body(*refs):
        outs = refs[n:2 * n]
        send, recv = refs[2 * n:]
        x, y, c, k, sibling, chips = _place()
        cps = []
        for t in range(n):
            rows = bufs[t].shape[1]
            for j, (px, py) in enumerate(chips):
                landed = outs[t].at[2 * px + py, _half(rows, c)]
                cps.append(pltpu.make_async_remote_copy(src_ref=landed, dst_ref=landed, send_sem=send.at[t, j],
                                                        recv_sem=recv.at[t, j], device_id=sibling, device_id_type=MESH))
        for cp in cps:
            cp.start()
        for t in range(n):
            rows = bufs[t].shape[1]
            for j, (px, py) in enumerate(chips):
                other = outs[t].at[2 * px + py, _half(rows, 1 - c)]
                pltpu.make_async_remote_copy(src_ref=other, dst_ref=other, send_sem=send.at[t, j], recv_sem=recv.at[t, j],
                                             device_id=sibling, device_id_type=MESH).wait_recv()
        for cp in cps:
            cp.wait_send()

    return pl.pallas_call(
        body, in_specs=[ANY] * n, out_specs=[ANY] * n, out_shape=[jax.ShapeDtypeStruct(b.shape, b.dtype) for b in bufs],
        input_output_aliases={t: t for t in range(n)}, scratch_shapes=[pltpu.SemaphoreType.DMA((n, 3))] * 2,
        name="gather_forward_sibling", compiler_params=pltpu.CompilerParams())(*bufs)


def _sibling_copies(gs, src_refs, dst_refs, send, recv):
    x, y, c, k, sibling, chips = _place()
    return [pltpu.make_async_remote_copy(
        src_ref=src_refs[t].at[:, _half(gs[t].shape[1], 1 - c), :], dst_ref=dst_refs[t], send_sem=send.at[t],
        recv_sem=recv.at[t], device_id=sibling, device_id_type=MESH) for t in range(len(gs))]


def sibling_start(gs, after, name):
    n = len(gs)
    lands = [lax.empty((g.shape[0], g.shape[1] // 2, g.shape[2]), g.dtype) for g in gs]

    def body(*refs):
        send, recv, token = refs[2 * n + 1], refs[2 * n + 2], refs[-1]
        for cp in _sibling_copies(gs, refs[:n], refs[n:2 * n], send, recv):
            cp.start()
        token[...] = jnp.zeros_like(token)

    outs = pl.pallas_call(
        body, name=name, in_specs=[HBM_SPEC] * (2 * n) + [ANY],
        out_shape=(pltpu.SemaphoreType.DMA((n,)), pltpu.SemaphoreType.DMA((n,)),
                   *[pltpu.HBM(v.shape, v.dtype) for v in gs + lands], jax.ShapeDtypeStruct(TOKEN_SHAPE, F32)),
        out_specs=(SEM_SPEC, SEM_SPEC, *[HBM_SPEC] * (2 * n), pl.BlockSpec(memory_space=pltpu.VMEM)),
        input_output_aliases={t: 2 + t for t in range(2 * n)},
        compiler_params=pltpu.CompilerParams(has_side_effects=SPLIT_EFFECT))(*[_in_hbm(v) for v in gs + lands], after)
    return outs[0], outs[1], list(outs[2:2 + n]), list(outs[2 + n:2 + 2 * n]), outs[-1]


def sibling_wait(send, recv, gs, lands, after, name):
    n = len(gs)

    def body(*refs):
        for cp in _sibling_copies(gs, refs[:n], refs[n:2 * n], refs[2 * n], refs[2 * n + 1]):
            cp.wait_send()
            cp.wait_recv()

    outs = pl.pallas_call(
        body, name=name, in_specs=[HBM_SPEC] * (2 * n) + [SEM_SPEC, SEM_SPEC, ANY],
        out_shape=tuple(pltpu.HBM(v.shape, v.dtype) for v in gs + lands), out_specs=tuple([HBM_SPEC] * (2 * n)),
        input_output_aliases={t: t for t in range(2 * n)},
        compiler_params=pltpu.CompilerParams(has_side_effects=SPLIT_EFFECT))(*gs, *lands, send, recv, after)
    return list(outs[:n]), list(outs[n:])


def _reduce_ici_copies(ps, src_refs, dst_refs, send, recv):
    x, y, c, k, sibling, chips = _place()
    cps = []
    for t in range(len(ps)):
        for j, (px, py) in enumerate(chips):
            cps.append(pltpu.make_async_remote_copy(
                src_ref=src_refs[t].at[2 * px + py], dst_ref=dst_refs[t].at[j], send_sem=send.at[3 * t + j],
                recv_sem=recv.at[3 * t + j],
                device_id=(px, py, c), device_id_type=MESH))
    return cps


def reduce_ici_start(ps, after, name):
    n = len(ps)
    lands = [lax.empty((3,) + p.shape[1:], p.dtype) for p in ps]

    def body(*refs):
        send, recv, token = refs[2 * n + 1], refs[2 * n + 2], refs[-1]
        for cp in _reduce_ici_copies(ps, refs[:n], refs[n:2 * n], send, recv):
            cp.start()
        token[...] = jnp.zeros_like(token)

    outs = pl.pallas_call(
        body, name=name, in_specs=[HBM_SPEC] * (2 * n) + [ANY],
        out_shape=(pltpu.SemaphoreType.DMA((3 * n,)), pltpu.SemaphoreType.DMA((3 * n,)),
                   *[pltpu.HBM(v.shape, v.dtype) for v in ps + lands], jax.ShapeDtypeStruct(TOKEN_SHAPE, F32)),
        out_specs=(SEM_SPEC, SEM_SPEC, *[HBM_SPEC] * (2 * n), pl.BlockSpec(memory_space=pltpu.VMEM)),
        input_output_aliases={t: 2 + t for t in range(2 * n)},
        compiler_params=pltpu.CompilerParams(has_side_effects=SPLIT_EFFECT))(*[_in_hbm(v) for v in ps + lands], after)
    return outs[0], outs[1], list(outs[2:2 + n]), list(outs[2 + n:2 + 2 * n]), outs[-1]


def _halves_copies(ss, refs, send, recv):
    x, y, c, k, sibling, chips = _place()
    return [pltpu.make_async_remote_copy(
        src_ref=refs[t].at[_half(ss[t].shape[0], c)], dst_ref=refs[t].at[_half(ss[t].shape[0], c)], send_sem=send.at[t],
        recv_sem=recv.at[t], device_id=sibling, device_id_type=MESH) for t in range(len(ss))]


def halves_start(ss, after, name):
    n = len(ss)

    def body(*refs):
        send, recv, token = refs[n + 1], refs[n + 2], refs[-1]
        for cp in _halves_copies(ss, refs[:n], send, recv):
            cp.start()
        token[...] = jnp.zeros_like(token)

    outs = pl.pallas_call(
        body, name=name, in_specs=[HBM_SPEC] * n + [ANY],
        out_shape=(pltpu.SemaphoreType.DMA((n,)), pltpu.SemaphoreType.DMA((n,)), *[pltpu.HBM(s.shape, s.dtype) for s in ss],
                   jax.ShapeDtypeStruct(TOKEN_SHAPE, F32)),
        out_specs=(SEM_SPEC, SEM_SPEC, *[HBM_SPEC] * n, pl.BlockSpec(memory_space=pltpu.VMEM)),
        input_output_aliases={t: 2 + t for t in range(n)},
        compiler_params=pltpu.CompilerParams(has_side_effects=SPLIT_EFFECT))(*[_in_hbm(s) for s in ss], after)
    return outs[0], outs[1], list(outs[2:2 + n]), outs[-1]


def halves_wait(send, recv, ss, after, name):
    n = len(ss)

    def body(*refs):
        x, y, c, k, sibling, chips = _place()
        for t in range(n):
            rows = ss[t].shape[0]
            cp = pltpu.make_async_remote_copy(
                src_ref=refs[t].at[_half(rows, c)], dst_ref=refs[t].at[_half(rows, 1 - c)], send_sem=refs[n].at[t],
                recv_sem=refs[n + 1].at[t], device_id=sibling, device_id_type=MESH)
            cp.wait_send()
            cp.wait_recv()

    return list(pl.pallas_call(
        body, name=name, in_specs=[HBM_SPEC] * n + [SEM_SPEC, SEM_SPEC, ANY],
        out_shape=tuple(pltpu.HBM(s.shape, s.dtype) for s in ss), out_specs=tuple([HBM_SPEC] * n),
        input_output_aliases={t: t for t in range(n)},
        compiler_params=pltpu.CompilerParams(has_side_effects=SPLIT_EFFECT))(*ss, send, recv, after))


def reduce_ici_wait(send, recv, ps, lands, after, name):
    n = len(ps)

    def body(*refs):
        for cp in _reduce_ici_copies(ps, refs[:n], refs[n:2 * n], refs[2 * n], refs[2 * n + 1]):
            cp.wait_send()
            cp.wait_recv()

    outs = pl.pallas_call(
        body, name=name, in_specs=[HBM_SPEC] * (2 * n) + [SEM_SPEC, SEM_SPEC, ANY],
        out_shape=tuple(pltpu.HBM(v.shape, v.dtype) for v in ps + lands), out_specs=tuple([HBM_SPEC] * (2 * n)),
        input_output_aliases={t: t for t in range(2 * n)},
        compiler_params=pltpu.CompilerParams(has_side_effects=SPLIT_EFFECT))(*ps, *lands, send, recv, after)
    return list(outs[:n]), list(outs[n:])


SMALL_CHUNK_ROWS = 256


def all_reduce_small(v, after):
    rows = v.shape[0]
    nchunk = rows // SMALL_CHUNK_ROWS

    def body(v_ref, after_ref, o_ref, buf, send, recv):
        x, y, c = lax.axis_index("x"), lax.axis_index("y"), lax.axis_index("c")
        me = 4 * x + 2 * y + c
        buf[me] = v_ref[...]
        cps = []
        for d in range(1, N_DEV):
            peer = (x ^ ((d >> 2) & 1), y ^ ((d >> 1) & 1), c ^ (d & 1))
            cps.append(pltpu.make_async_remote_copy(src_ref=v_ref, dst_ref=buf.at[me], send_sem=send.at[d - 1],
                                                    recv_sem=recv.at[d - 1], device_id=peer, device_id_type=MESH))
        for cp in cps:
            cp.start()
        for d in range(1, N_DEV):
            got = buf.at[me ^ d]
            pltpu.make_async_remote_copy(src_ref=got, dst_ref=got, send_sem=send.at[d - 1], recv_sem=recv.at[d - 1],
                                         device_id=(x, y, c), device_id_type=MESH).wait_recv()
        for cp in cps:
            cp.wait_send()

        def chunk(i, carry):
            sl = pl.ds(pl.multiple_of(i * SMALL_CHUNK_ROWS, SMALL_CHUNK_ROWS), SMALL_CHUNK_ROWS)
            acc = buf[0, sl, :]
            for s in range(1, N_DEV):
                acc = acc + buf[s, sl, :]
            o_ref[sl, :] = acc
            return carry

        lax.fori_loop(0, nchunk, chunk, 0)

    vmem = pl.BlockSpec(memory_space=pltpu.VMEM)
    return pl.pallas_call(
        body, in_specs=[vmem, ANY], out_specs=vmem, out_shape=jax.ShapeDtypeStruct(v.shape, F32),
        scratch_shapes=[pltpu.VMEM((N_DEV,) + v.shape, F32), pltpu.SemaphoreType.DMA((N_DEV - 1,)),
                        pltpu.SemaphoreType.DMA((N_DEV - 1,))],
        name="all_reduce_small",
        compiler_params=pltpu.CompilerParams(vmem_limit_bytes=VMEM_LIMIT_BYTES))(v, after)


def adamw(w, g, m, v):
    rows, cols = w.shape
    tr = rows
    for cand in (512, 256, 128, 64, 32, 16, 8):
        if rows % cand == 0 and cand * cols * 4 <= (1 << 20):
            tr = cand
            break
    c1 = 1.0 - ADAM_B1 ** ADAM_STEP
    c2 = 1.0 - ADAM_B2 ** ADAM_STEP

    def body(w_ref, g_ref, m_ref, v_ref, go_ref, d_ref, nm_ref, nv_ref):
        gf = g_ref[...]
        go_ref[...] = gf
        nm = ADAM_B1 * m_ref[...] + (1.0 - ADAM_B1) * gf
        nv = ADAM_B2 * v_ref[...] + (1.0 - ADAM_B2) * (gf * gf)
        d_ref[...] = -ADAM_LR * ((nm / c1) / (jnp.sqrt(nv / c2) + ADAM_EPS) + ADAM_WD * w_ref[...])
        nm_ref[...] = nm
        nv_ref[...] = nv

    spec = pl.BlockSpec((tr, cols), lambda i: (i, 0))
    shape = jax.ShapeDtypeStruct((rows, cols), F32)
    return pl.pallas_call(body, grid=(rows // tr,), in_specs=[spec] * 4, out_specs=(spec,) * 4, out_shape=(shape,) * 4,
                          name="adamw", compiler_params=_params("parallel"))(w, g, m, v)


TAP_ROWS_ALIGN = 16
FLAT_ALIGN = 128 * SMALL_CHUNK_ROWS


def _pad_to(a, n):
    return jnp.pad(a, (0, n - a.shape[0]))


def _round_up(n, m):
    return (n + m - 1) // m * m


class Exchanges:
    def __init__(self, a, chip_core):
        self.a, self.chip_core = a, chip_core
        self.bufs = []
        for g in range(len(SUBLAYERS)):
            row = []
            for n in sublayer_weight_names(g):
                w = a[n].reshape(-1, a[n].shape[-1])
                if _kind(n) == "tap":
                    w = jnp.pad(w, ((0, _round_up(w.shape[0], TAP_ROWS_ALIGN) - w.shape[0]), (0, 0)))
                row.append(place_block(w, chip_core, F32 if _kind(n) == "tap" else BF16))
            self.bufs.append(row)
        self.started = {}
        self.after = chip_core
        self.stage = [None, None, None]
        self.reduced = []
        self.results = {}

    def _unpack(self, g, gathered, W):
        for n, v in zip(sublayer_weight_names(g), gathered):
            kind = _kind(n)
            if kind == "col":
                W[n] = v
            elif kind == "row":
                W[n] = v.reshape(-1, v.shape[-1])
            elif kind == "grp":
                grp, r, pg = self.a[n].shape
                W[n] = v.reshape(N_CHIPS, grp, r, pg).transpose(1, 0, 2, 3).reshape(grp, N_CHIPS * r, pg)
            else:
                nt = self.a[n].shape[0]
                W[n] = v[:, :nt].transpose(1, 0, 2).reshape(nt, -1)

    def gather_first(self, W):
        gathered = all_gather_chips(self.bufs[0])
        self._unpack(0, gathered, W)
        self.after = gathered[0]

    def forward_begins(self, g, W):
        token = None
        for h in (g + 1, g + 2):
            if h < len(SUBLAYERS) and h not in self.started:
                send, recv, bufs, token = gather_ici_start(self.bufs[h], self.after, f"gather_start_{h}")
                self.started[h] = (send, recv, bufs)
                self.after = token
        return token

    def forward_ends(self, g, x, W):
        if g + 1 == len(SUBLAYERS):
            return
        send, recv, bufs = self.started[g + 1]
        gathered = gather_forward_sibling(gather_ici_wait(send, recv, bufs, x, f"gather_wait_{g + 1}"))
        self._unpack(g + 1, gathered, W)
        self.after = gathered[0]

    def gradients_ready(self, g, grads):
        names = [n for n in sublayer_weight_names(g) if _kind(n) != "tap"]
        gl = []
        for n in names:
            v, kind = grads.pop(n), _kind(n)
            if kind == "row":
                v = v.reshape(N_CHIPS, -1, v.shape[-1])
            elif kind == "grp":
                grp, r, pg = self.a[n].shape
                v = v.reshape(grp, N_CHIPS, r, pg).transpose(1, 0, 2, 3).reshape(N_CHIPS, grp * r, pg)
            gl.append(v)
        last = self.advance(gl[0], gl[0])
        send, recv, gl, lands, token = sibling_start(gl, last, f"sibling_start_{g}")
        self.stage[0] = (g, names, send, recv, gl, lands)
        return token

    def advance(self, after, last):
        if self.stage[2] is not None:
            self.reduced.append(self.stage[2])
        self.stage[2], last = self._to_halves(self.stage[1], after, last)
        self.stage[1], last = self._to_ici(self.stage[0], after, last)
        self.stage[0] = None
        return last

    def _to_ici(self, entry, after, last):
        if entry is None:
            return None, last
        g, names, send, recv, gl, lands = entry
        gl, lands = sibling_wait(send, recv, gl, lands, after, f"sibling_wait_{g}")
        ps = [add_sibling_half(v, l, self.chip_core) for v, l in zip(gl, lands)]
        send, recv, ps, l2s, last = reduce_ici_start(ps, last, f"reduce_start_{g}")
        return (g, names, send, recv, ps, l2s), last

    def _to_halves(self, entry, after, last):
        if entry is None:
            return None, last
        g, names, send, recv, ps, l2s = entry
        ps, l2s = reduce_ici_wait(send, recv, ps, l2s, after, f"reduce_wait_{g}")
        ss = [sum_chip_blocks(p, l2, self.chip_core) for p, l2 in zip(ps, l2s)]
        send, recv, ss, last = halves_start(ss, last, f"halves_start_{g}")
        return (g, names, send, recv, ss), last

    def update_matrices(self, entry, after):
        g, names, send, recv, ss = entry
        for n, grad in zip(names, halves_wait(send, recv, ss, after, f"halves_wait_{g}")):
            shape = self.a[n].shape
            two_d = lambda v: v.reshape(-1, shape[-1])
            outs = adamw(two_d(self.a[n]), grad, two_d(self.a["m_" + n]), two_d(self.a["v_" + n]))
            self.results[n] = tuple(v.reshape(shape) for v in outs)
        return outs[0]

    def finish(self, after):
        first, last = self._to_ici(self.stage[0], after, after)
        for entry in self.reduced + [self.stage[2]]:
            if entry is not None:
                last = self.update_matrices(entry, last)
        second, last = self._to_halves(self.stage[1], last, last)
        first, last = self._to_halves(first, last, last)
        return [second, first], last


def train_step(a):
    x, positions, tgt = a["x"][0], a["positions"][0], a["loss_target"][0]
    mats = [n for n in WEIGHT_NAMES if _kind(n) in ("col", "row", "grp")]
    taps = [n for n in WEIGHT_NAMES if _kind(n) == "tap"]
    reps = [n for n in WEIGHT_NAMES if _kind(n) == "rep"]
    chip = 2 * lax.axis_index("x") + lax.axis_index("y")
    chip_core = jnp.stack([chip, lax.axis_index("c")]).astype(jnp.int32)

    comm = Exchanges(a, chip_core)
    W = {n: a[n] for n in reps}
    comm.gather_first(W)
    loss, dx, grads = local_step(x, positions, tgt, W, comm)
    loss = lax.psum(loss, ("x", "y", "c"))
    left, last = comm.finish(dx)

    n_rep = _round_up(sum(a[n].size for n in reps), FLAT_ALIGN)
    flat_rep = _pad_to(jnp.concatenate([grads[n].reshape(-1) for n in reps]), n_rep)
    flat_tap = jnp.concatenate([grads[n].reshape(-1) for n in taps])
    flat = jnp.concatenate([flat_rep, _pad_to(flat_tap, _round_up(flat_tap.shape[0], FLAT_ALIGN))])
    summed = all_reduce_small(flat.reshape(-1, 128), last)
    for entry in left:
        comm.update_matrices(entry, summed)
    rep_rows = n_rep // 128
    tap_flat = summed[rep_rows:].reshape(-1)

    out = dict(comm.results)
    pack = lambda pre: _pad_to(jnp.concatenate([a[pre + n].reshape(-1) for n in reps]), n_rep).reshape(-1, 128)
    g_rep, d_rep, m_rep, v_rep = adamw(pack(""), summed[:rep_rows], pack("m_"), pack("v_"))
    off = 0
    for n in reps:
        size, shape = a[n].size, a[n].shape
        out[n] = tuple(f.reshape(-1)[off:off + size].reshape(shape) for f in (g_rep, d_rep, m_rep, v_rep))
        off += size
    off = 0
    for n in taps:
        nt, cs = a[n].shape
        full = tap_flat[off:off + nt * cs * N_CHIPS].reshape(nt, cs * N_CHIPS)
        off += nt * cs * N_CHIPS
        g = lax.dynamic_slice(full, (0, chip * cs), (nt, cs))
        out[n] = tuple(adamw(a[n], g, a["m_" + n], a["v_" + n]))

    res = [loss, dx[None]]
    for part in range(4):
        res += [out[n][part] for n in WEIGHT_NAMES]
    return tuple(res)


def kernel(x, positions, l0_norm_g, l0_a_w_in, l0_a_b_in, l0_a_dw_w, l0_a_dw_b, l0_a_ln_g, l0_a_ln_b, l0_a_w_out, l0_a_b_out, l0_ffn_norm_g, l0_ffn_w_up, l0_ffn_dw_w, l0_ffn_dw_b, l0_ffn_w_down, l1_norm_g, l1_b_w_group, l1_b_scale, l1_ffn_norm_g, l1_ffn_w_up, l1_ffn_dw_w, l1_ffn_dw_b, l1_ffn_w_down, l2_norm_g, l2_c_w_qkv, l2_c_q_norm_g, l2_c_k_norm_g, l2_c_sinks, l2_c_w_o, l2_ffn_norm_g, l2_ffn_w_up, l2_ffn_dw_w, l2_ffn_dw_b, l2_ffn_w_down, l3_norm_g, l3_a_w_in, l3_a_b_in, l3_a_dw_w, l3_a_dw_b, l3_a_ln_g, l3_a_ln_b, l3_a_w_out, l3_a_b_out, l3_ffn_norm_g, l3_ffn_w_up, l3_ffn_dw_w, l3_ffn_dw_b, l3_ffn_w_down, loss_target, m_l0_norm_g, m_l0_a_w_in, m_l0_a_b_in, m_l0_a_dw_w, m_l0_a_dw_b, m_l0_a_ln_g, m_l0_a_ln_b, m_l0_a_w_out, m_l0_a_b_out, m_l0_ffn_norm_g, m_l0_ffn_w_up, m_l0_ffn_dw_w, m_l0_ffn_dw_b, m_l0_ffn_w_down, m_l1_norm_g, m_l1_b_w_group, m_l1_b_scale, m_l1_ffn_norm_g, m_l1_ffn_w_up, m_l1_ffn_dw_w, m_l1_ffn_dw_b, m_l1_ffn_w_down, m_l2_norm_g, m_l2_c_w_qkv, m_l2_c_q_norm_g, m_l2_c_k_norm_g, m_l2_c_sinks, m_l2_c_w_o, m_l2_ffn_norm_g, m_l2_ffn_w_up, m_l2_ffn_dw_w, m_l2_ffn_dw_b, m_l2_ffn_w_down, m_l3_norm_g, m_l3_a_w_in, m_l3_a_b_in, m_l3_a_dw_w, m_l3_a_dw_b, m_l3_a_ln_g, m_l3_a_ln_b, m_l3_a_w_out, m_l3_a_b_out, m_l3_ffn_norm_g, m_l3_ffn_w_up, m_l3_ffn_dw_w, m_l3_ffn_dw_b, m_l3_ffn_w_down, v_l0_norm_g, v_l0_a_w_in, v_l0_a_b_in, v_l0_a_dw_w, v_l0_a_dw_b, v_l0_a_ln_g, v_l0_a_ln_b, v_l0_a_w_out, v_l0_a_b_out, v_l0_ffn_norm_g, v_l0_ffn_w_up, v_l0_ffn_dw_w, v_l0_ffn_dw_b, v_l0_ffn_w_down, v_l1_norm_g, v_l1_b_w_group, v_l1_b_scale, v_l1_ffn_norm_g, v_l1_ffn_w_up, v_l1_ffn_dw_w, v_l1_ffn_dw_b, v_l1_ffn_w_down, v_l2_norm_g, v_l2_c_w_qkv, v_l2_c_q_norm_g, v_l2_c_k_norm_g, v_l2_c_sinks, v_l2_c_w_o, v_l2_ffn_norm_g, v_l2_ffn_w_up, v_l2_ffn_dw_w, v_l2_ffn_dw_b, v_l2_ffn_w_down, v_l3_norm_g, v_l3_a_w_in, v_l3_a_b_in, v_l3_a_dw_w, v_l3_a_dw_b, v_l3_a_ln_g, v_l3_a_ln_b, v_l3_a_w_out, v_l3_a_b_out, v_l3_ffn_norm_g, v_l3_ffn_w_up, v_l3_ffn_dw_w, v_l3_ffn_dw_b, v_l3_ffn_w_down):
    return train_step(dict(locals()))
```

```python
import functools

import jax
import jax.numpy as jnp
from jax import lax
from jax.experimental import pallas as pl
from jax.experimental.pallas import tpu as pltpu

F32 = jnp.float32
BF16 = jnp.bfloat16
EPS = 1e-6
HEAD_DIM = 64
KV_GROUP = 8
ATT_BLOCK = 128
ROT_DIM = 16
ROPE_THETA = 500000.0
POOL_GROUPS = 4
CONF_TAPS = 31
FFN_TAPS = 3
N_CHIPS = 4
N_DEV = 8
ADAM_LR, ADAM_B1, ADAM_B2, ADAM_EPS, ADAM_WD, ADAM_STEP = 0.001, 0.9, 0.999, 1e-08, 0.01, 10
VMEM_LIMIT_BYTES = 56 * 1024 * 1024
MESH = pl.DeviceIdType.MESH

CONF_NAMES = ["norm_g", "a_w_in", "a_b_in", "a_dw_w", "a_dw_b", "a_ln_g", "a_ln_b", "a_w_out", "a_b_out"]
FFN_NAMES = ["ffn_norm_g", "ffn_w_up", "ffn_dw_w", "ffn_dw_b", "ffn_w_down"]
POOL_NAMES = ["norm_g", "b_w_group", "b_scale"]
ATT_NAMES = ["norm_g", "c_w_qkv", "c_q_norm_g", "c_k_norm_g", "c_sinks", "c_w_o"]
WEIGHT_NAMES = ([f"l0_{n}" for n in CONF_NAMES + FFN_NAMES] + [f"l1_{n}" for n in POOL_NAMES + FFN_NAMES]
                + [f"l2_{n}" for n in ATT_NAMES + FFN_NAMES] + [f"l3_{n}" for n in CONF_NAMES + FFN_NAMES])
COL_SHARDED = ("a_w_in", "ffn_w_up", "c_w_qkv")
ROW_SHARDED = ("a_w_out", "ffn_w_down", "c_w_o")
TAP_SHARDED = ("a_dw_w", "ffn_dw_w")


def _kind(name):
    base = name[3:]
    if base in COL_SHARDED:
        return "col"
    if base in ROW_SHARDED:
        return "row"
    if base in TAP_SHARDED:
        return "tap"
    if base == "b_w_group":
        return "grp"
    return "rep"


def _pick(n, prefs):
    for p in prefs:
        if p <= n and n % p == 0:
            return p
    return n


def _params(*sem):
    return pltpu.CompilerParams(dimension_semantics=sem, vmem_limit_bytes=VMEM_LIMIT_BYTES)


def _sigmoid(x):
    return 1.0 / (1.0 + jnp.exp(-x))


NN = (((1,), (0,)), ((), ()))
NT = (((1,), (1,)), ((), ()))
TN = (((0,), (0,)), ((), ()))


MM_VMEM_BUDGET = 44 * 1024 * 1024
MM_STEP_SECONDS = 0.35e-6
MM_FLOPS, MM_HBM_BYTES = 9.0e14, 3.0e12
TILE_SIZES = (4096, 2816, 2048, 1408, 1024, 704, 640, 512, 256, 128)


def _tile_options(n, lane):
    opts = [c for c in TILE_SIZES if c <= n and n % c == 0 and (not lane or c % 128 == 0)]
    return opts or [n]


def _mm_plan(m, n, k, *, n_unit=None, k_unit=None, a_bytes=2, b_bytes=2, o_bytes=2, extra_bytes=0):
    best = None
    for tm in _tile_options(m, False):
        for tn in _tile_options(n_unit or n, True):
            for tk in _tile_options(k_unit or k, True) + ([k] if not k_unit else []):
                nk = k // tk
                vmem = 2 * (tm * tk * a_bytes + tk * tn * b_bytes + tm * tn * (o_bytes + extra_bytes)) + tm * tn * 4 * (2 if nk > 1 else 1)
                if vmem > MM_VMEM_BUDGET:
                    continue
                ni, nj = m // tm, n // tn
                a_all, b_all, o_all = m * k * a_bytes, k * n * b_bytes, m * n * (o_bytes + extra_bytes)
                for i_inner in (False, True):
                    if nk > 1:
                        traffic = a_all * nj + b_all * ni + o_all
                    elif i_inner:
                        traffic = a_all * nj + b_all + o_all
                    else:
                        traffic = a_all + b_all * ni + o_all
                    cost = ni * nj * nk * MM_STEP_SECONDS + max(2.0 * m * n * k / MM_FLOPS, traffic / MM_HBM_BYTES)
                    if best is None or cost < best[0]:
                        best = (cost, tm, tn, tk, i_inner)
    assert best is not None, (m, n, k)
    return best[1:]


def _mm(a, b, *, dims, sizes, plan, a_blk, a_idx, b_blk, b_idx, o_blk, o_idx, out_shape, name,
        bias=None, scale=None, vec_blk=None, vec_idx=None, resid=None, raw_shape=None):
    m, n, k = sizes
    tm, tn, tk, i_inner = plan
    ni, nj, nk = m // tm, n // tn, k // tk
    has_bias, has_scale, has_resid, want_raw = bias is not None, scale is not None, resid is not None, raw_shape is not None

    def body(*refs):
        a_ref, b_ref = refs[0], refs[1]
        pos = 2
        bias_ref = scale_ref = resid_ref = raw_ref = None
        if has_bias:
            bias_ref = refs[pos]; pos += 1
        if has_scale:
            scale_ref = refs[pos]; pos += 1
        if has_resid:
            resid_ref = refs[pos]; pos += 1
        o_ref = refs[pos]; pos += 1
        if want_raw:
            raw_ref = refs[pos]; pos += 1
        part = lax.dot_general(a_ref[...].astype(BF16), b_ref[...].astype(BF16), dims, preferred_element_type=F32)

        def finish(r):
            if want_raw:
                raw_ref[...] = r.astype(raw_ref.dtype)
            if has_bias:
                r = r + bias_ref[...]
            if has_scale:
                r = r * scale_ref[...]
            if has_resid:
                r = r + resid_ref[...]
            o_ref[...] = r.astype(o_ref.dtype)

        if nk == 1:
            finish(part)
        else:
            acc_ref = refs[pos]
            kk = pl.program_id(2)

            @pl.when(kk == 0)
            def _():
                acc_ref[...] = part

            @pl.when(kk > 0)
            def _():
                acc_ref[...] += part

            @pl.when(kk == nk - 1)
            def _():
                finish(acc_ref[...])

    order = (lambda f: (lambda j, i, kk: f(i, j, kk))) if i_inner else (lambda f: f)
    spec = lambda blk, idx: pl.BlockSpec(blk, order(idx))
    operands, in_specs = [a, b], [spec(a_blk, a_idx), spec(b_blk, b_idx)]
    for v in (bias, scale):
        if v is not None:
            operands.append(v); in_specs.append(spec(vec_blk, vec_idx))
    if has_resid:
        operands.append(resid); in_specs.append(spec(o_blk, o_idx))
    out_shapes, out_specs = out_shape, spec(o_blk, o_idx)
    if want_raw:
        out_shapes, out_specs = (out_shape, raw_shape), (spec(o_blk, o_idx), spec(o_blk, o_idx))
    return pl.pallas_call(
        body, grid=(nj, ni, nk) if i_inner else (ni, nj, nk), in_specs=in_specs, out_specs=out_specs,
        out_shape=out_shapes, scratch_shapes=[pltpu.VMEM((tm, tn), F32)] if nk > 1 else [], name=name,
        compiler_params=_params("parallel", "parallel", "arbitrary"))(*operands)


def mm_nn_cols(a, g, *, split, out_dtype, bias=None, name):
    t, k = a.shape
    ns = g.shape[2]
    n = N_CHIPS * ns
    plan = _mm_plan(t, n, k, n_unit=ns, o_bytes=jnp.dtype(out_dtype).itemsize)
    tm, tn, tk, _ = plan
    nj = ns // tn
    if split:
        o_blk, o_idx = (None, tm, tn), (lambda i, j, kk: (j // (2 * nj), i, j % (2 * nj)))
        out_shape = jax.ShapeDtypeStruct((2, t, 2 * ns), out_dtype)
        vec_blk, vec_idx = (None, 1, tn), (lambda i, j, kk: (j // (2 * nj), 0, j % (2 * nj)))
        if bias is not None:
            bias = bias.reshape(2, 1, 2 * ns)
    else:
        o_blk, o_idx = (tm, tn), (lambda i, j, kk: (i, j))
        out_shape = jax.ShapeDtypeStruct((t, n), out_dtype)
        vec_blk, vec_idx = (1, tn), (lambda i, j, kk: (0, j))
        if bias is not None:
            bias = bias.reshape(1, n)
    return _mm(a, g, dims=NN, sizes=(t, n, k), plan=plan, a_blk=(tm, tk), a_idx=lambda i, j, kk: (i, kk),
               b_blk=(None, tk, tn), b_idx=lambda i, j, kk: (j // nj, kk, j % nj), o_blk=o_blk, o_idx=o_idx,
               out_shape=out_shape, name=name, bias=bias, vec_blk=vec_blk, vec_idx=vec_idx)


def mm_nn(a, w, *, out_dtype, bias=None, scale=None, resid=None, raw_dtype=None, name):
    t, k = a.shape
    n = w.shape[1]
    extra = (4 if resid is not None else 0) + (0 if raw_dtype is None else jnp.dtype(raw_dtype).itemsize)
    plan = _mm_plan(t, n, k, a_bytes=a.dtype.itemsize, o_bytes=jnp.dtype(out_dtype).itemsize, extra_bytes=extra)
    tm, tn, tk, _ = plan
    raw_shape = None if raw_dtype is None else jax.ShapeDtypeStruct((t, n), raw_dtype)
    return _mm(a, w, dims=NN, sizes=(t, n, k), plan=plan, a_blk=(tm, tk), a_idx=lambda i, j, kk: (i, kk),
               b_blk=(tk, tn), b_idx=lambda i, j, kk: (kk, j), o_blk=(tm, tn), o_idx=lambda i, j, kk: (i, j),
               out_shape=jax.ShapeDtypeStruct((t, n), out_dtype), name=name,
               bias=None if bias is None else bias.reshape(1, n), scale=None if scale is None else scale.reshape(1, n),
               vec_blk=(1, tn), vec_idx=lambda i, j, kk: (0, j), resid=resid, raw_shape=raw_shape)


def mm_nt(dy, w, *, out_dtype, name):
    t, n = dy.shape
    kdim = w.shape[0]
    plan = _mm_plan(t, kdim, n, a_bytes=dy.dtype.itemsize, o_bytes=jnp.dtype(out_dtype).itemsize)
    tm, tn, tk, _ = plan
    return _mm(dy, w, dims=NT, sizes=(t, kdim, n), plan=plan, a_blk=(tm, tk), a_idx=lambda i, j, kk: (i, kk),
               b_blk=(tn, tk), b_idx=lambda i, j, kk: (j, kk), o_blk=(tm, tn), o_idx=lambda i, j, kk: (i, j),
               out_shape=jax.ShapeDtypeStruct((t, kdim), out_dtype), name=name)


def mm_nt_cols(du, g, *, split, out_dtype, name):
    kdim, ns = g.shape[1], g.shape[2]
    t = du.shape[1] if split else du.shape[0]
    plan = _mm_plan(t, kdim, N_CHIPS * ns, k_unit=ns, o_bytes=jnp.dtype(out_dtype).itemsize)
    tm, tn, tk, _ = plan
    nkb = ns // tk
    if split:
        a_blk, a_idx = (None, tm, tk), (lambda i, j, kk: (kk // (2 * nkb), i, kk % (2 * nkb)))
    else:
        a_blk, a_idx = (tm, tk), (lambda i, j, kk: (i, kk))
    return _mm(du, g, dims=NT, sizes=(t, kdim, N_CHIPS * ns), plan=plan, a_blk=a_blk, a_idx=a_idx,
               b_blk=(None, tn, tk), b_idx=lambda i, j, kk: (kk // nkb, j, kk % nkb),
               o_blk=(tm, tn), o_idx=lambda i, j, kk: (i, j),
               out_shape=jax.ShapeDtypeStruct((t, kdim), out_dtype), name=name)


def mm_wgrad(at, dy, *, out_dtype, name):
    return mm_nn(at, dy, out_dtype=out_dtype, name=name)


def mm_wgrad_cols(ht, du, *, split, out_dtype, name):
    kdim, t = ht.shape
    ns = (du.shape[2] // 2) if split else (du.shape[1] // N_CHIPS)
    plan = _mm_plan(kdim, N_CHIPS * ns, t, n_unit=ns, o_bytes=jnp.dtype(out_dtype).itemsize)
    tm, tn, tk, _ = plan
    nj = ns // tn
    if split:
        b_blk, b_idx = (None, tk, tn), (lambda i, j, kk: (j // (2 * nj), kk, j % (2 * nj)))
    else:
        b_blk, b_idx = (tk, tn), (lambda i, j, kk: (kk, j))
    return _mm(ht, du, dims=NN, sizes=(kdim, N_CHIPS * ns, t), plan=plan, a_blk=(tm, tk), a_idx=lambda i, j, kk: (i, kk),
               b_blk=b_blk, b_idx=b_idx, o_blk=(None, tm, tn), o_idx=lambda i, j, kk: (j // nj, i, j % nj),
               out_shape=jax.ShapeDtypeStruct((N_CHIPS, kdim, ns), out_dtype), name=name)


def _row_tile(t):
    return _pick(t, (256, 128))


def _acc_rows(ref, part, i):
    @pl.when(i == 0)
    def _():
        ref[...] = part

    @pl.when(i > 0)
    def _():
        ref[...] += part


TOKEN_SHAPE = (8, 128)


def _token_operand(token):
    if token is None:
        return [], []
    return [token], [pl.BlockSpec(TOKEN_SHAPE, lambda i: (0, 0))]


def rms_fwd(x, g, out_dtype, token=None, transposed=False):
    t, d = x.shape
    tr = _row_tile(t)

    def body(x_ref, g_ref, *rest):
        outs = rest[-2:] if transposed else rest[-1:]
        xf = x_ref[...]
        r = lax.rsqrt(jnp.mean(xf * xf, axis=-1, keepdims=True) + EPS)
        y = xf * r * g_ref[...]
        outs[0][...] = y.astype(outs[0].dtype)
        if transposed:
            outs[1][...] = y.T.astype(BF16)

    row = pl.BlockSpec((tr, d), lambda i: (i, 0))
    tok, tok_spec = _token_operand(token)
    out_specs, out_shape = row, jax.ShapeDtypeStruct((t, d), out_dtype)
    if transposed:
        out_specs = (row, pl.BlockSpec((d, tr), lambda i: (0, i)))
        out_shape = (out_shape, jax.ShapeDtypeStruct((d, t), BF16))
    return pl.pallas_call(body, grid=(t // tr,), in_specs=[row, pl.BlockSpec((1, d), lambda i: (0, 0))] + tok_spec,
                          out_specs=out_specs, out_shape=out_shape, name="rms_fwd",
                          compiler_params=_params("parallel"))(x, g.reshape(1, d), *tok)


def rms_bwd(x, g, dh, dres, token=None):
    t, d = x.shape
    tr = _row_tile(t)

    def body(x_ref, g_ref, dh_ref, dres_ref, *rest):
        dx_ref, dx16_ref, dg_ref = rest[-3:]
        i = pl.program_id(0)
        xf = x_ref[...]
        r = lax.rsqrt(jnp.mean(xf * xf, axis=-1, keepdims=True) + EPS)
        xhat = xf * r
        dhf = dh_ref[...].astype(F32)
        dxh = dhf * g_ref[...]
        m = jnp.mean(dxh * xhat, axis=-1, keepdims=True)
        dx = dres_ref[...] + r * (dxh - xhat * m)
        dx_ref[...] = dx
        dx16_ref[...] = dx.astype(BF16)
        _acc_rows(dg_ref, jnp.sum(dhf * xhat, axis=0, keepdims=True), i)

    row = pl.BlockSpec((tr, d), lambda i: (i, 0))
    vec = pl.BlockSpec((1, d), lambda i: (0, 0))
    tok, tok_spec = _token_operand(token)
    return pl.pallas_call(body, grid=(t // tr,), in_specs=[row, vec, row, row] + tok_spec, out_specs=(row, row, vec),
                          out_shape=(jax.ShapeDtypeStruct((t, d), F32), jax.ShapeDtypeStruct((t, d), BF16),
                                     jax.ShapeDtypeStruct((1, d), F32)),
                          name="rms_bwd", compiler_params=_params("arbitrary"))(x, g.reshape(1, d), dh, dres, *tok)


def ln_silu_fwd(c, g, b):
    t, d = c.shape
    tr = _row_tile(t)

    def body(c_ref, g_ref, b_ref, o_ref, ot_ref):
        xf = c_ref[...]
        mu = jnp.mean(xf, axis=-1, keepdims=True)
        xc = xf - mu
        var = jnp.mean(xc * xc, axis=-1, keepdims=True)
        n = xc * lax.rsqrt(var + EPS) * g_ref[...] + b_ref[...]
        s = n * _sigmoid(n)
        o_ref[...] = s.astype(o_ref.dtype)
        ot_ref[...] = s.T.astype(ot_ref.dtype)

    row = pl.BlockSpec((tr, d), lambda i: (i, 0))
    vec = pl.BlockSpec((1, d), lambda i: (0, 0))
    return pl.pallas_call(body, grid=(t // tr,), in_specs=[row, vec, vec],
                          out_specs=(row, pl.BlockSpec((d, tr), lambda i: (0, i))),
                          out_shape=(jax.ShapeDtypeStruct((t, d), BF16), jax.ShapeDtypeStruct((d, t), BF16)),
                          name="ln_silu_fwd", compiler_params=_params("parallel"))(c, g.reshape(1, d), b.reshape(1, d))


def ln_silu_bwd(c, g, b, ds, token=None):
    t, d = c.shape
    tr = _row_tile(t)

    def body(c_ref, g_ref, b_ref, ds_ref, *rest):
        dc_ref, dg_ref, db_ref = rest[-3:]
        i = pl.program_id(0)
        xf = c_ref[...]
        mu = jnp.mean(xf, axis=-1, keepdims=True)
        xc = xf - mu
        var = jnp.mean(xc * xc, axis=-1, keepdims=True)
        rstd = lax.rsqrt(var + EPS)
        xhat = xc * rstd
        n = xhat * g_ref[...] + b_ref[...]
        sg = _sigmoid(n)
        dn = ds_ref[...].astype(F32) * (sg * (1.0 + n * (1.0 - sg)))
        dxh = dn * g_ref[...]
        m1 = jnp.mean(dxh, axis=-1, keepdims=True)
        m2 = jnp.mean(dxh * xhat, axis=-1, keepdims=True)
        dc_ref[...] = rstd * (dxh - m1 - xhat * m2)
        _acc_rows(dg_ref, jnp.sum(dn * xhat, axis=0, keepdims=True), i)
        _acc_rows(db_ref, jnp.sum(dn, axis=0, keepdims=True), i)

    row = pl.BlockSpec((tr, d), lambda i: (i, 0))
    vec = pl.BlockSpec((1, d), lambda i: (0, 0))
    vshape = jax.ShapeDtypeStruct((1, d), F32)
    tok, tok_spec = _token_operand(token)
    return pl.pallas_call(body, grid=(t // tr,), in_specs=[row, vec, vec, row] + tok_spec, out_specs=(row, vec, vec),
                          out_shape=(jax.ShapeDtypeStruct((t, d), F32), vshape, vshape), name="ln_silu_bwd",
                          compiler_params=_params("arbitrary"))(c, g.reshape(1, d), b.reshape(1, d), ds, *tok)


def loss_grad(y, tgt):
    t, d = y.shape
    tr = _row_tile(t)

    def body(y_ref, t_ref, dy_ref, dy16_ref, sq_ref):
        i = pl.program_id(0)
        err = y_ref[...] - t_ref[...]
        dy = err * (1.0 / d)
        dy_ref[...] = dy
        dy16_ref[...] = dy.astype(BF16)
        _acc_rows(sq_ref, jnp.sum(err * err, axis=0, keepdims=True), i)

    row = pl.BlockSpec((tr, d), lambda i: (i, 0))
    vec = pl.BlockSpec((1, d), lambda i: (0, 0))
    return pl.pallas_call(body, grid=(t // tr,), in_specs=[row, row], out_specs=(row, row, vec),
                          out_shape=(jax.ShapeDtypeStruct((t, d), F32), jax.ShapeDtypeStruct((t, d), BF16),
                                     jax.ShapeDtypeStruct((1, d), F32)),
                          name="loss_grad", compiler_params=_params("arbitrary"))(y, tgt)


def col_sum(a):
    t, d = a.shape
    tr = _row_tile(t)

    def body(a_ref, o_ref):
        _acc_rows(o_ref, jnp.sum(a_ref[...].astype(F32), axis=0, keepdims=True), pl.program_id(0))

    return pl.pallas_call(body, grid=(t // tr,), in_specs=[pl.BlockSpec((tr, d), lambda i: (i, 0))],
                          out_specs=pl.BlockSpec((1, d), lambda i: (0, 0)), out_shape=jax.ShapeDtypeStruct((1, d), F32),
                          name="col_sum", compiler_params=_params("arbitrary"))(a)


def scale_bwd(dx, ypre, scale):
    t, d = dx.shape
    tr = _row_tile(t)

    def body(dx_ref, y_ref, s_ref, dy_ref, ds_ref):
        dxf = dx_ref[...]
        dy_ref[...] = (dxf * s_ref[...]).astype(dy_ref.dtype)
        _acc_rows(ds_ref, jnp.sum(dxf * y_ref[...], axis=0, keepdims=True), pl.program_id(0))

    row = pl.BlockSpec((tr, d), lambda i: (i, 0))
    vec = pl.BlockSpec((1, d), lambda i: (0, 0))
    return pl.pallas_call(body, grid=(t // tr,), in_specs=[row, row, vec], out_specs=(row, vec),
                          out_shape=(jax.ShapeDtypeStruct((t, d), BF16), jax.ShapeDtypeStruct((1, d), F32)),
                          name="scale_bwd", compiler_params=_params("arbitrary"))(dx, ypre, scale.reshape(1, d))


def _chunk_rows(t):
    return _pick(t, (256, 128))


def _load_halo(ref, lead, base, rows, t, first, last, pre, post):
    idx = (lambda s, n: (pl.ds(s, n), slice(None))) if lead is None else (lambda s, n: (lead, pl.ds(s, n), slice(None)))
    parts = []
    if pre:
        start = pl.multiple_of(jnp.maximum(base - pre, 0), pre)
        parts.append(ref[idx(start, pre)].astype(F32) * jnp.where(first, 0.0, 1.0))
    parts.append(ref[idx(base, rows)].astype(F32))
    if post:
        start = pl.multiple_of(jnp.minimum(base + rows, t - post), post)
        parts.append(ref[idx(start, post)].astype(F32) * jnp.where(last, 0.0, 1.0))
    return parts[0] if len(parts) == 1 else jnp.concatenate(parts, axis=0)


def _fold8(x):
    r, c = x.shape
    return x.reshape(r // 8, 8, c).sum(axis=0)


def ffn_gate_fwd(u2, w3, b2):
    _, t, f = u2.shape
    tc = _pick(f, (256, 128))
    rows = _chunk_rows(t)
    nch = t // rows
    halo = 16

    def body(u_ref, w_ref, b_ref, a_ref, at_ref):
        def conv(p, base, first):
            xs = _load_halo(u_ref, p, base, rows, t, first, False, halo, 0)
            wp = w_ref[p]
            return (wp[0:1] * pltpu.roll(xs, 2, 0)[halo:] + wp[1:2] * pltpu.roll(xs, 1, 0)[halo:]
                    + wp[2:3] * xs[halo:] + b_ref[p])

        def chunk(i, carry):
            base = pl.multiple_of(i * rows, rows)
            gate, val = conv(0, base, i == 0), conv(1, base, i == 0)
            act = gate * _sigmoid(gate) * val
            act = act.astype(a_ref.dtype)
            a_ref[pl.ds(base, rows), :] = act
            at_ref[:, pl.ds(base, rows)] = act.T
            return carry

        lax.fori_loop(0, nch, chunk, 0)

    return pl.pallas_call(
        body, grid=(f // tc,),
        in_specs=[pl.BlockSpec((2, t, tc), lambda j: (0, 0, j)), pl.BlockSpec((2, 3, tc), lambda j: (0, 0, j)),
                  pl.BlockSpec((2, 1, tc), lambda j: (0, 0, j))],
        out_specs=(pl.BlockSpec((t, tc), lambda j: (0, j)), pl.BlockSpec((tc, t), lambda j: (j, 0))),
        out_shape=(jax.ShapeDtypeStruct((t, f), BF16), jax.ShapeDtypeStruct((f, t), BF16)),
        name="ffn_gate_fwd", compiler_params=_params("parallel"))(u2, w3, b2)


def ffn_gate_bwd(u2, da, w3, b2):
    _, t, f = u2.shape
    tc = _pick(f, (256, 128))
    rows = _chunk_rows(t)
    nch = t // rows
    halo = 16
    n = rows + 2 * halo

    def body(u_ref, da_ref, w_ref, b_ref, du_ref, dw_ref, db_ref, acc_ref):
        acc_ref[...] = jnp.zeros_like(acc_ref)

        def chunk(i, carry):
            base = pl.multiple_of(i * rows, rows)
            first, last = i == 0, i == nch - 1
            daf = jnp.concatenate(
                [jnp.zeros((halo, tc), F32), _load_halo(da_ref, None, base, rows, t, first, last, 0, halo)], axis=0)
            pre, shifted = [], []
            for p in range(2):
                xs = _load_halo(u_ref, p, base, rows, t, first, last, halo, halo)
                x1, x2 = pltpu.roll(xs, 1, 0), pltpu.roll(xs, 2, 0)
                wp = w_ref[p]
                pre.append(wp[0:1] * x2 + wp[1:2] * x1 + wp[2:3] * xs + b_ref[p])
                shifted.append((x2, x1, xs))
            gate, val = pre
            sg = _sigmoid(gate)
            d_pre = (daf * val * (sg * (1.0 + gate * (1.0 - sg))), daf * gate * sg)
            for p in range(2):
                dp = d_pre[p]
                wp = w_ref[p]
                du = wp[2:3] * dp + wp[1:2] * pltpu.roll(dp, n - 1, 0) + wp[0:1] * pltpu.roll(dp, n - 2, 0)
                du_ref[p, pl.ds(base, rows), :] = du[halo:halo + rows].astype(du_ref.dtype)
                own = dp[halo:halo + rows]
                for k in range(3):
                    acc_ref[p, k] += _fold8(own * shifted[p][k][halo:halo + rows])
                acc_ref[p, 3] += _fold8(own)
            return carry

        lax.fori_loop(0, nch, chunk, 0)
        for p in range(2):
            for k in range(3):
                dw_ref[p, k:k + 1, :] = jnp.sum(acc_ref[p, k], axis=0, keepdims=True)
            db_ref[p] = jnp.sum(acc_ref[p, 3], axis=0, keepdims=True)

    blk = pl.BlockSpec((2, t, tc), lambda j: (0, 0, j))
    wspec = pl.BlockSpec((2, 3, tc), lambda j: (0, 0, j))
    bspec = pl.BlockSpec((2, 1, tc), lambda j: (0, 0, j))
    return pl.pallas_call(
        body, grid=(f // tc,), in_specs=[blk, pl.BlockSpec((t, tc), lambda j: (0, j)), wspec, bspec],
        out_specs=(blk, wspec, bspec),
        out_shape=(jax.ShapeDtypeStruct((2, t, f), BF16), jax.ShapeDtypeStruct((2, 3, f), F32),
                   jax.ShapeDtypeStruct((2, 1, f), F32)),
        scratch_shapes=[pltpu.VMEM((2, 4, 8, tc), F32)], name="ffn_gate_bwd",
        compiler_params=_params("parallel"))(u2, da, w3, b2)


def glu_conv_fwd(u2, w, b):
    _, t, d = u2.shape
    taps = w.shape[0]
    tc = 128
    rows = _chunk_rows(t)
    nch = t // rows
    halo = 32

    def body(u_ref, w_ref, b_ref, c_ref):
        def chunk(i, carry):
            base = pl.multiple_of(i * rows, rows)
            a = _load_halo(u_ref, 0, base, rows, t, i == 0, False, halo, 0)
            g = _load_halo(u_ref, 1, base, rows, t, i == 0, False, halo, 0)
            xs = a * _sigmoid(g)
            acc = w_ref[taps - 1:taps, :] * xs[halo:] + b_ref[...]
            for j in range(taps - 1):
                acc = acc + w_ref[j:j + 1, :] * pltpu.roll(xs, taps - 1 - j, 0)[halo:]
            c_ref[pl.ds(base, rows), :] = acc
            return carry

        lax.fori_loop(0, nch, chunk, 0)

    return pl.pallas_call(
        body, grid=(d // tc,),
        in_specs=[pl.BlockSpec((2, t, tc), lambda j: (0, 0, j)), pl.BlockSpec((taps, tc), lambda j: (0, j)),
                  pl.BlockSpec((1, tc), lambda j: (0, j))],
        out_specs=pl.BlockSpec((t, tc), lambda j: (0, j)), out_shape=jax.ShapeDtypeStruct((t, d), F32),
        name="glu_conv_fwd", compiler_params=_params("parallel"))(u2, w, b)


def glu_conv_bwd(u2, dc, w):
    _, t, d = u2.shape
    taps = w.shape[0]
    tc = 128
    rows = _chunk_rows(t)
    nch = t // rows
    halo = 32
    n = rows + halo

    def body(u_ref, dc_ref, w_ref, du_ref, dw_ref, dwb_ref, dbin_ref, acc_ref, bacc_ref):
        acc_ref[...] = jnp.zeros_like(acc_ref)
        bacc_ref[...] = jnp.zeros_like(bacc_ref)

        def chunk(i, carry):
            base = pl.multiple_of(i * rows, rows)
            first, last = i == 0, i == nch - 1
            a = _load_halo(u_ref, 0, base, rows, t, first, False, halo, 0)
            g = _load_halo(u_ref, 1, base, rows, t, first, False, halo, 0)
            sg = _sigmoid(g)
            xs = a * sg
            dcs = _load_halo(dc_ref, None, base, rows, t, first, last, 0, halo)
            own = dcs[:rows]
            dglu = w_ref[taps - 1:taps, :] * own
            acc_ref[taps - 1] += _fold8(own * xs[halo:])
            for j in range(taps - 1):
                s = taps - 1 - j
                dglu = dglu + w_ref[j:j + 1, :] * pltpu.roll(dcs, n - s, 0)[:rows]
                acc_ref[j] += _fold8(own * pltpu.roll(xs, s, 0)[halo:])
            a_c, sg_c = a[halo:], sg[halo:]
            da = dglu * sg_c
            dg = dglu * a_c * sg_c * (1.0 - sg_c)
            du_ref[0, pl.ds(base, rows), :] = da.astype(du_ref.dtype)
            du_ref[1, pl.ds(base, rows), :] = dg.astype(du_ref.dtype)
            bacc_ref[0] += _fold8(own)
            bacc_ref[1] += _fold8(da)
            bacc_ref[2] += _fold8(dg)
            return carry

        lax.fori_loop(0, nch, chunk, 0)
        for j in range(taps):
            dw_ref[j:j + 1, :] = jnp.sum(acc_ref[j], axis=0, keepdims=True)
        dwb_ref[...] = jnp.sum(bacc_ref[0], axis=0, keepdims=True)
        dbin_ref[0] = jnp.sum(bacc_ref[1], axis=0, keepdims=True)
        dbin_ref[1] = jnp.sum(bacc_ref[2], axis=0, keepdims=True)

    blk = pl.BlockSpec((2, t, tc), lambda j: (0, 0, j))
    col = pl.BlockSpec((t, tc), lambda j: (0, j))
    return pl.pallas_call(
        body, grid=(d // tc,), in_specs=[blk, col, pl.BlockSpec((taps, tc), lambda j: (0, j))],
        out_specs=(blk, pl.BlockSpec((taps, tc), lambda j: (0, j)), pl.BlockSpec((1, tc), lambda j: (0, j)),
                   pl.BlockSpec((2, 1, tc), lambda j: (0, 0, j))),
        out_shape=(jax.ShapeDtypeStruct((2, t, d), BF16), jax.ShapeDtypeStruct((taps, d), F32),
                   jax.ShapeDtypeStruct((1, d), F32), jax.ShapeDtypeStruct((2, 1, d), F32)),
        scratch_shapes=[pltpu.VMEM((taps, 8, tc), F32), pltpu.VMEM((3, 8, tc), F32)], name="glu_conv_bwd",
        compiler_params=_params("parallel"))(u2, dc, w)


def _pool_select(grp, levels):
    out = levels[3]
    for k in (2, 1, 0):
        out = jnp.where(grp == k, levels[k], out)
    return out


def _pool_count(base, rows, tc, grp):
    tpos = (base + lax.broadcasted_iota(jnp.int32, (rows, tc), 0) + 1).astype(F32)
    window = jnp.left_shift(2, grp).astype(F32)
    return jnp.minimum(tpos, window)


def pool_fwd(h):
    t, d = h.shape
    pg = d // POOL_GROUPS
    tc = _pick(pg, (256, 128))
    rows = _chunk_rows(t)
    nch = t // rows
    halo = 16

    def body(h_ref, o_ref, ot_ref):
        grp = (pl.program_id(0) * tc) // pg

        def chunk(i, carry):
            base = pl.multiple_of(i * rows, rows)
            xs = _load_halo(h_ref, None, base, rows, t, i == 0, False, halo, 0)
            levels, cur = [], xs
            for k in range(4):
                cur = cur + pltpu.roll(cur, 1 << k, 0)
                levels.append(cur[halo:])
            pooled = _pool_select(grp, levels) / _pool_count(base, rows, tc, grp)
            mixed = pooled - xs[halo:]
            o_ref[pl.ds(base, rows), :] = mixed.astype(o_ref.dtype)
            ot_ref[:, pl.ds(base, rows)] = mixed.T.astype(ot_ref.dtype)
            return carry

        lax.fori_loop(0, nch, chunk, 0)

    col = pl.BlockSpec((t, tc), lambda j: (0, j))
    return pl.pallas_call(body, grid=(d // tc,), in_specs=[col], out_specs=(col, pl.BlockSpec((tc, t), lambda j: (j, 0))),
                          out_shape=(jax.ShapeDtypeStruct((t, d), BF16), jax.ShapeDtypeStruct((d, t), BF16)),
                          name="pool_fwd", compiler_params=_params("parallel"))(h)


def pool_bwd(dmix):
    t, d = dmix.shape
    pg = d // POOL_GROUPS
    tc = _pick(pg, (256, 128))
    rows = _chunk_rows(t)
    nch = t // rows
    halo = 16
    n = rows + halo

    def body(d_ref, o_ref):
        grp = (pl.program_id(0) * tc) // pg

        def chunk(i, carry):
            base = pl.multiple_of(i * rows, rows)
            ds = _load_halo(d_ref, None, base, rows, t, i == 0, i == nch - 1, 0, halo)
            levels, cur = [], ds / _pool_count(base, n, tc, grp)
            for k in range(4):
                cur = cur + pltpu.roll(cur, n - (1 << k), 0)
                levels.append(cur[:rows])
            o_ref[pl.ds(base, rows), :] = _pool_select(grp, levels) - ds[:rows]
            return carry

        lax.fori_loop(0, nch, chunk, 0)

    col = pl.BlockSpec((t, tc), lambda j: (0, j))
    return pl.pallas_call(body, grid=(d // tc,), in_specs=[col], out_specs=col,
                          out_shape=jax.ShapeDtypeStruct((t, d), F32), name="pool_bwd",
                          compiler_params=_params("parallel"))(dmix)


def mm_groups(a, wg, *, mode, out_dtype, scale=None, resid=None, raw_dtype=None, name):
    t, d = a.shape
    pg = wg.shape[1]
    tm = _pick(t, (1024, 512, 256, 128))
    tn = pg
    if mode == "nn":
        dims, b_blk, b_idx = NN, (None, pg, tn), (lambda i, j, kk: (j, 0, 0))
    else:
        dims, b_blk, b_idx = NT, (None, tn, pg), (lambda i, j, kk: (j, 0, 0))
    raw_shape = None if raw_dtype is None else jax.ShapeDtypeStruct((t, d), raw_dtype)
    return _mm(a, wg, dims=dims, sizes=(t, d, pg), plan=(tm, tn, pg, False), a_blk=(tm, pg), a_idx=lambda i, j, kk: (i, j),
               b_blk=b_blk, b_idx=b_idx, o_blk=(tm, tn), o_idx=lambda i, j, kk: (i, j),
               out_shape=jax.ShapeDtypeStruct((t, d), out_dtype), name=name,
               scale=None if scale is None else scale.reshape(1, d), vec_blk=(1, tn), vec_idx=lambda i, j, kk: (0, j),
               resid=resid, raw_shape=raw_shape)


def mm_groups_wgrad(at, dy, groups, *, out_dtype, name):
    d, t = at.shape
    pg = d // groups
    tk = _pick(t, (2048, 1024, 512, 256, 128))
    return _mm(at, dy, dims=NN, sizes=(d, pg, t), plan=(pg, pg, tk, False), a_blk=(pg, tk), a_idx=lambda i, j, kk: (i, kk),
               b_blk=(tk, pg), b_idx=lambda i, j, kk: (kk, i), o_blk=(None, pg, pg), o_idx=lambda i, j, kk: (i, 0, 0),
               out_shape=jax.ShapeDtypeStruct((groups, pg, pg), out_dtype), name=name)


def _split_dot(y, p):
    hi = y.astype(BF16)
    r1 = y - hi.astype(F32)
    mid = r1.astype(BF16)
    lo = (r1 - mid.astype(F32)).astype(BF16)
    pb = p.astype(BF16)
    dot = lambda v: jnp.dot(v, pb, preferred_element_type=F32)
    return (dot(hi) + dot(mid)) + dot(lo)


def rope_tables(positions):
    half = ROT_DIM // 2
    inv_freq = ROPE_THETA ** (-jnp.arange(0, ROT_DIM, 2, dtype=F32) / ROT_DIM)
    ang = positions.astype(F32)[:, None] * inv_freq
    t = positions.shape[0]
    cos, sin = jnp.cos(ang), jnp.sin(ang)
    rest = HEAD_DIM - ROT_DIM
    cosf = jnp.concatenate([cos, cos, jnp.ones((t, rest), F32)], axis=1)
    sinf = jnp.concatenate([-sin, sin, jnp.zeros((t, rest), F32)], axis=1)
    idx = jnp.arange(HEAD_DIM)
    partner = jnp.where(idx < half, idx + half, jnp.where(idx < ROT_DIM, idx - half, idx))
    pmat = (idx[:, None] == partner[None, :]).astype(F32)
    return cosf, sinf, pmat


def qk_rope_fwd(x, g, cosf, sinf, pmat, out_scale):
    hn, t, hd = x.shape
    tq = _pick(t, (4096, 2048, 1024, 512, 256, 128))

    def body(x_ref, g_ref, c_ref, s_ref, p_ref, o_ref):
        xf = x_ref[...]
        r = lax.rsqrt(jnp.mean(xf * xf, axis=-1, keepdims=True) + EPS)
        y = xf * r * g_ref[...]
        rot = y * c_ref[...] + _split_dot(y, p_ref[...]) * s_ref[...]
        o_ref[...] = (rot * out_scale).astype(o_ref.dtype)

    blk = pl.BlockSpec((None, tq, hd), lambda h, i: (h, i, 0))
    tab = pl.BlockSpec((tq, hd), lambda h, i: (i, 0))
    return pl.pallas_call(
        body, grid=(hn, t // tq),
        in_specs=[blk, pl.BlockSpec((1, hd), lambda h, i: (0, 0)), tab, tab, pl.BlockSpec((hd, hd), lambda h, i: (0, 0))],
        out_specs=blk, out_shape=jax.ShapeDtypeStruct((hn, t, hd), BF16), name="qk_rope_fwd",
        compiler_params=_params("parallel", "parallel"))(x, g.reshape(1, hd), cosf, sinf, pmat)


def qk_rope_bwd(dy, x, g, cosf, sinf, pmat_t, in_scale):
    hn, t, hd = x.shape
    tq = _pick(t, (4096, 2048, 1024, 512, 256, 128))

    def body(dy_ref, x_ref, g_ref, c_ref, s_ref, p_ref, dx_ref, dg_ref):
        step = pl.program_id(0) * pl.num_programs(1) + pl.program_id(1)
        dr = dy_ref[...] * in_scale
        dyn = dr * c_ref[...] + _split_dot(dr * s_ref[...], p_ref[...])
        xf = x_ref[...]
        r = lax.rsqrt(jnp.mean(xf * xf, axis=-1, keepdims=True) + EPS)
        xhat = xf * r
        dxh = dyn * g_ref[...]
        m = jnp.mean(dxh * xhat, axis=-1, keepdims=True)
        dx_ref[...] = r * (dxh - xhat * m)
        _acc_rows(dg_ref, jnp.sum(dyn * xhat, axis=0, keepdims=True), step)

    blk = pl.BlockSpec((None, tq, hd), lambda h, i: (h, i, 0))
    tab = pl.BlockSpec((tq, hd), lambda h, i: (i, 0))
    vec = pl.BlockSpec((1, hd), lambda h, i: (0, 0))
    return pl.pallas_call(
        body, grid=(hn, t // tq),
        in_specs=[blk, blk, vec, tab, tab, pl.BlockSpec((hd, hd), lambda h, i: (0, 0))],
        out_specs=(blk, vec), out_shape=(jax.ShapeDtypeStruct((hn, t, hd), F32), jax.ShapeDtypeStruct((1, hd), F32)),
        name="qk_rope_bwd", compiler_params=_params("arbitrary", "arbitrary"))(dy, x, g.reshape(1, hd), cosf, sinf, pmat_t)


NEG_BIG = -1e30


ATT_ROWS = KV_GROUP * ATT_BLOCK


def _att_sinks(sink_ref, kv):
    return jnp.concatenate([jnp.full((ATT_BLOCK, 1), sink_ref[kv * KV_GROUP + g], F32) for g in range(KV_GROUP)], axis=0)


def _att_mask(i, rows):
    shape = (rows, 2 * ATT_BLOCK)
    qi = jnp.bitwise_and(lax.broadcasted_iota(jnp.int32, shape, 0), ATT_BLOCK - 1)
    kj = lax.broadcasted_iota(jnp.int32, shape, 1)
    cur = jnp.logical_and(kj >= ATT_BLOCK, kj - ATT_BLOCK <= qi)
    prev = jnp.logical_and(jnp.logical_and(kj < ATT_BLOCK, kj > qi), i > 0)
    return jnp.logical_or(cur, prev)


def _att_probs(q, k2, mask, sink):
    s = jnp.where(mask, lax.dot_general(q, k2, NT, preferred_element_type=F32), NEG_BIG)
    m = jnp.maximum(jnp.max(s, axis=-1, keepdims=True), sink)
    p = jnp.exp(s - m)
    p_s = jnp.exp(sink - m)
    return p, p_s, jnp.sum(p, axis=-1, keepdims=True) + p_s


def _att_specs(t):
    nb = t // ATT_BLOCK
    qblk = pl.BlockSpec((KV_GROUP, ATT_BLOCK, HEAD_DIM), lambda kv, i: (kv, i, 0))
    cur = pl.BlockSpec((None, ATT_BLOCK, HEAD_DIM), lambda kv, i: (kv, i, 0))
    prev = pl.BlockSpec((None, ATT_BLOCK, HEAD_DIM), lambda kv, i: (kv, jnp.maximum(i - 1, 0), 0))
    return nb, qblk, cur, prev, pl.BlockSpec(memory_space=pltpu.SMEM)


def attn_fwd(q, k, v, sinks):
    h, t, hd = q.shape
    nb, qblk, cur, prev, smem = _att_specs(t)

    def body(q_ref, kc_ref, kp_ref, vc_ref, vp_ref, sink_ref, o_ref):
        kv, i = pl.program_id(0), pl.program_id(1)
        k2 = jnp.concatenate([kp_ref[...], kc_ref[...]], axis=0)
        v2 = jnp.concatenate([vp_ref[...], vc_ref[...]], axis=0)
        mask = _att_mask(i, ATT_BLOCK)
        for g in range(KV_GROUP):
            p, _, denom = _att_probs(q_ref[g], k2, mask, sink_ref[kv * KV_GROUP + g])
            o_ref[g] = (jnp.dot(p.astype(BF16), v2, preferred_element_type=F32) / denom).astype(o_ref.dtype)

    return pl.pallas_call(
        body, grid=(h // KV_GROUP, nb), in_specs=[qblk, cur, prev, cur, prev, smem], out_specs=qblk,
        out_shape=jax.ShapeDtypeStruct((h, t, hd), BF16), name="attn_fwd",
        compiler_params=_params("parallel", "parallel"))(q, k, k, v, v, sinks)


def attn_bwd(q, k, v, do, sinks):
    h, t, hd = q.shape
    kvh = h // KV_GROUP
    nb, qblk, cur, prev, smem = _att_specs(t)

    def body(q_ref, kc_ref, kp_ref, vc_ref, vp_ref, do_ref, sink_ref, dq_ref, dk_ref, dv_ref, dsk_ref):
        kv, i = pl.program_id(0), pl.program_id(1)

        @pl.when(i == 0)
        def _():
            dk_ref[...] = jnp.zeros_like(dk_ref)
            dv_ref[...] = jnp.zeros_like(dv_ref)
            dsk_ref[...] = jnp.zeros_like(dsk_ref)

        k2 = jnp.concatenate([kp_ref[...], kc_ref[...]], axis=0)
        v2 = jnp.concatenate([vp_ref[...], vc_ref[...]], axis=0)
        q = q_ref[...].reshape(ATT_ROWS, hd)
        p, p_s, denom = _att_probs(q, k2, _att_mask(i, ATT_ROWS), _att_sinks(sink_ref, kv))
        inv = 1.0 / denom
        pn = p * inv
        dob = do_ref[...].reshape(ATT_ROWS, hd).astype(BF16)
        dp = lax.dot_general(dob, v2, NT, preferred_element_type=F32)
        dsum = jnp.sum(pn * dp, axis=-1, keepdims=True)
        ds = (pn * (dp - dsum)).astype(BF16)
        dq_ref[...] = jnp.dot(ds, k2, preferred_element_type=F32).reshape(KV_GROUP, ATT_BLOCK, hd)
        dk2 = lax.dot_general(ds, q, TN, preferred_element_type=F32)
        dv2 = lax.dot_general(pn.astype(BF16), dob, TN, preferred_element_type=F32)
        dsink = p_s * inv * dsum
        dsink_rows = [jnp.broadcast_to(-jnp.sum(dsink[g * ATT_BLOCK:(g + 1) * ATT_BLOCK], axis=0, keepdims=True), (1, 128))
                      for g in range(KV_GROUP)]
        here = pl.ds(pl.multiple_of(i * ATT_BLOCK, ATT_BLOCK), ATT_BLOCK)
        before = pl.ds(pl.multiple_of(jnp.maximum(i - 1, 0) * ATT_BLOCK, ATT_BLOCK), ATT_BLOCK)
        dk_ref[before, :] += dk2[:ATT_BLOCK]
        dv_ref[before, :] += dv2[:ATT_BLOCK]
        dk_ref[here, :] += dk2[ATT_BLOCK:]
        dv_ref[here, :] += dv2[ATT_BLOCK:]
        dsk_ref[...] += jnp.concatenate(dsink_rows, axis=0)

    whole = pl.BlockSpec((None, t, hd), lambda kv, i: (kv, 0, 0))
    return pl.pallas_call(
        body, grid=(kvh, nb), in_specs=[qblk, cur, prev, cur, prev, qblk, smem],
        out_specs=(qblk, whole, whole, pl.BlockSpec((None, KV_GROUP, 128), lambda kv, i: (kv, 0, 0))),
        out_shape=(jax.ShapeDtypeStruct((h, t, hd), F32), jax.ShapeDtypeStruct((kvh, t, hd), F32),
                   jax.ShapeDtypeStruct((kvh, t, hd), F32), jax.ShapeDtypeStruct((kvh, KV_GROUP, 128), F32)),
        name="attn_bwd", compiler_params=_params("parallel", "arbitrary"))(q, k, k, v, v, do, sinks)


def _ffn_taps(w, b):
    f2 = w.shape[1]
    return w.reshape(FFN_TAPS, 2, f2 // 2).transpose(1, 0, 2), b.reshape(2, 1, f2 // 2)


def ffn_fwd(x, W, p, tables=None, token=None):
    h, ht = rms_fwd(x, W[p + "ffn_norm_g"], BF16, token, transposed=True)
    u2 = mm_nn_cols(h, W[p + "ffn_w_up"], split=True, out_dtype=BF16, name="ffn_up")
    w3, b2 = _ffn_taps(W[p + "ffn_dw_w"], W[p + "ffn_dw_b"])
    a, at = ffn_gate_fwd(u2, w3, b2)
    y = mm_nn(a, W[p + "ffn_w_down"], out_dtype=F32, resid=x, name="ffn_down")
    return y, (x, ht, u2, at)


def ffn_bwd(saved, W, p, dy, tables=None):
    x, ht, u2, at = saved
    dyf, dyb = dy
    w3, b2 = _ffn_taps(W[p + "ffn_dw_w"], W[p + "ffn_dw_b"])
    grads = {p + "ffn_w_down": mm_wgrad(at, dyb, out_dtype=BF16, name="ffn_down_dw")}
    da = mm_nt(dyb, W[p + "ffn_w_down"], out_dtype=BF16, name="ffn_down_dx")
    du2, dw3, db2 = ffn_gate_bwd(u2, da, w3, b2)
    grads[p + "ffn_dw_w"] = dw3.transpose(1, 0, 2).reshape(FFN_TAPS, -1)
    grads[p + "ffn_dw_b"] = db2.reshape(-1)
    grads[p + "ffn_w_up"] = mm_wgrad_cols(ht, du2, split=True, out_dtype=BF16, name="ffn_up_dw")
    dh = mm_nt_cols(du2, W[p + "ffn_w_up"], split=True, out_dtype=F32, name="ffn_up_dx")
    return (x, p + "ffn_norm_g", dh), grads


def conf_fwd(x, W, p, tables=None, token=None):
    d = x.shape[1]
    h, ht = rms_fwd(x, W[p + "norm_g"], BF16, token, transposed=True)
    u2 = mm_nn_cols(h, W[p + "a_w_in"], split=True, out_dtype=BF16, bias=W[p + "a_b_in"], name="conf_in")
    c = glu_conv_fwd(u2, W[p + "a_dw_w"], W[p + "a_dw_b"].reshape(1, d))
    s, st = ln_silu_fwd(c, W[p + "a_ln_g"], W[p + "a_ln_b"])
    y = mm_nn(s, W[p + "a_w_out"], out_dtype=F32, bias=W[p + "a_b_out"], resid=x, name="conf_out")
    return y, (x, ht, u2, c, st)


def conf_bwd(saved, W, p, dy, tables=None, midway=None):
    x, ht, u2, c, st = saved
    dyf, dyb = dy
    grads = {p + "a_w_out": mm_wgrad(st, dyb, out_dtype=BF16, name="conf_out_dw"), p + "a_b_out": col_sum(dyf).reshape(-1)}
    ds = mm_nt(dyb, W[p + "a_w_out"], out_dtype=BF16, name="conf_out_dx")
    dc, dlg, dlb = ln_silu_bwd(c, W[p + "a_ln_g"], W[p + "a_ln_b"], ds, midway(ds) if midway else None)
    grads[p + "a_ln_g"], grads[p + "a_ln_b"] = dlg.reshape(-1), dlb.reshape(-1)
    du2, ddw, ddwb, dbin = glu_conv_bwd(u2, dc, W[p + "a_dw_w"])
    grads[p + "a_dw_w"], grads[p + "a_dw_b"], grads[p + "a_b_in"] = ddw, ddwb.reshape(-1), dbin.reshape(-1)
    grads[p + "a_w_in"] = mm_wgrad_cols(ht, du2, split=True, out_dtype=BF16, name="conf_in_dw")
    dh = mm_nt_cols(du2, W[p + "a_w_in"], split=True, out_dtype=F32, name="conf_in_dx")
    return (x, p + "norm_g", dh), grads


def pool_layer_fwd(x, W, p, tables=None, token=None):
    h = rms_fwd(x, W[p + "norm_g"], F32, token)
    mixed, mixed_t = pool_fwd(h)
    y, ypre = mm_groups(mixed, W[p + "b_w_group"], mode="nn", out_dtype=F32, scale=W[p + "b_scale"], resid=x,
                        raw_dtype=F32, name="pool_mix")
    return y, (x, mixed_t, ypre)


def pool_layer_bwd(saved, W, p, dy, tables=None):
    x, mixed_t, ypre = saved
    dyf, dyb = dy
    dyp, dscale = scale_bwd(dyf, ypre, W[p + "b_scale"])
    grads = {p + "b_scale": dscale.reshape(-1),
             p + "b_w_group": mm_groups_wgrad(mixed_t, dyp, POOL_GROUPS, out_dtype=BF16, name="pool_mix_dw")}
    dmix = mm_groups(dyp, W[p + "b_w_group"], mode="nt", out_dtype=F32, name="pool_mix_dx")
    dh = pool_bwd(dmix)
    return (x, p + "norm_g", dh), grads


def _heads(a, n):
    t = a.shape[0]
    return a.reshape(t, n, HEAD_DIM).transpose(1, 0, 2)


def _unheads(a):
    n, t, _ = a.shape
    return a.transpose(1, 0, 2).reshape(t, n * HEAD_DIM)


def attn_layer_fwd(x, W, p, tables, token=None):
    d = x.shape[1]
    nh = d // HEAD_DIM
    nkv = nh // KV_GROUP
    cosf, sinf, pmat = tables
    h, ht = rms_fwd(x, W[p + "norm_g"], BF16, token, transposed=True)
    qkv = mm_nn_cols(h, W[p + "c_w_qkv"], split=False, out_dtype=F32, name="att_qkv")
    q = _heads(qkv[:, :d], nh)
    k = _heads(qkv[:, d:d + nkv * HEAD_DIM], nkv)
    v = _heads(qkv[:, d + nkv * HEAD_DIM:], nkv).astype(BF16)
    qr = qk_rope_fwd(q, W[p + "c_q_norm_g"], cosf, sinf, pmat, HEAD_DIM ** -0.5)
    kr = qk_rope_fwd(k, W[p + "c_k_norm_g"], cosf, sinf, pmat, 1.0)
    o = attn_fwd(qr, kr, v, W[p + "c_sinks"])
    o2 = _unheads(o)
    y = mm_nn(o2, W[p + "c_w_o"], out_dtype=F32, resid=x, name="att_out")
    return y, (x, ht, q, k, v, qr, kr, o2)


def attn_layer_bwd(saved, W, p, dy, tables):
    x, ht, q, k, v, qr, kr, o2 = saved
    dyf, dyb = dy
    cosf, sinf, pmat = tables
    nh = q.shape[0]
    grads = {p + "c_w_o": mm_wgrad(o2.T, dyb, out_dtype=BF16, name="att_out_dw")}
    do = _heads(mm_nt(dyb, W[p + "c_w_o"], out_dtype=BF16, name="att_out_dx"), nh)
    dqr, dkr, dv, dsk = attn_bwd(qr, kr, v, do, W[p + "c_sinks"])
    grads[p + "c_sinks"] = dsk[:, :, 0].reshape(-1)
    dq, dqg = qk_rope_bwd(dqr, q, W[p + "c_q_norm_g"], cosf, sinf, pmat.T, HEAD_DIM ** -0.5)
    dk, dkg = qk_rope_bwd(dkr, k, W[p + "c_k_norm_g"], cosf, sinf, pmat.T, 1.0)
    grads[p + "c_q_norm_g"], grads[p + "c_k_norm_g"] = dqg.reshape(-1), dkg.reshape(-1)
    dqkv = jnp.concatenate([_unheads(dq), _unheads(dk), _unheads(dv)], axis=1).astype(BF16)
    grads[p + "c_w_qkv"] = mm_wgrad_cols(ht, dqkv, split=False, out_dtype=BF16, name="att_qkv_dw")
    dh = mm_nt_cols(dqkv, W[p + "c_w_qkv"], split=False, out_dtype=F32, name="att_qkv_dx")
    return (x, p + "norm_g", dh), grads


def local_step(x, positions, tgt, W, comm=None):
    tables = rope_tables(positions)
    saved = []
    for g, (fwd, _, p) in enumerate(SUBLAYERS):
        token = comm.forward_begins(g, W) if comm else None
        x, s = fwd(x, W, p, tables, token)
        saved.append(s)
        if comm:
            comm.forward_ends(g, x, W)
    dyf, dyb, sq = loss_grad(x, tgt)
    loss = 0.5 * jnp.sum(sq) / x.shape[1]
    grads = {}
    for g in reversed(range(len(SUBLAYERS))):
        _, bwd, p = SUBLAYERS[g]
        if comm and g == 0:
            (xin, gain, dh), gr = bwd(saved[g], W, p, (dyf, dyb), tables, comm.midway)
        else:
            (xin, gain, dh), gr = bwd(saved[g], W, p, (dyf, dyb), tables)
        token = comm.gradients_ready(g, gr) if comm else None
        dyf, dyb, dg = rms_bwd(xin, W[gain], dh, dyf, token)
        gr[gain] = dg.reshape(-1)
        grads.update(gr)
    return loss, dyf, grads


SUBLAYERS = [(conf_fwd, conf_bwd, "l0_"), (ffn_fwd, ffn_bwd, "l0_"), (pool_layer_fwd, pool_layer_bwd, "l1_"),
             (ffn_fwd, ffn_bwd, "l1_"), (attn_layer_fwd, attn_layer_bwd, "l2_"), (ffn_fwd, ffn_bwd, "l2_"),
             (conf_fwd, conf_bwd, "l3_"), (ffn_fwd, ffn_bwd, "l3_")]
SUBLAYER_WEIGHTS = {conf_fwd: ("a_w_in", "a_w_out", "a_dw_w"), ffn_fwd: ("ffn_w_up", "ffn_w_down", "ffn_dw_w"),
                    pool_layer_fwd: ("b_w_group",), attn_layer_fwd: ("c_w_qkv", "c_w_o")}


def sublayer_weight_names(g):
    fwd, _, p = SUBLAYERS[g]
    return [p + n for n in SUBLAYER_WEIGHTS[fwd]]


ANY = pl.BlockSpec(memory_space=pl.ANY)


def _place():
    x, y, c = lax.axis_index("x"), lax.axis_index("y"), lax.axis_index("c")
    chips = [(1 - x, y), (x, 1 - y), (1 - x, 1 - y)]
    return x, y, c, 2 * x + y, (x, y, 1 - c), chips


def _half(rows, which):
    return pl.ds(which * (rows // 2), rows // 2)


def place_block(shard, chip_core, out_dtype):
    rows, cols = shard.shape
    tr = rows
    for cand in (512, 256, 128, 64, 32, 16):
        if rows % cand == 0 and cand * cols * 4 <= (2 << 20):
            tr = cand
            break

    def body(pos_ref, s_ref, o_ref):
        o_ref[...] = s_ref[...].astype(o_ref.dtype)

    grid_spec = pltpu.PrefetchScalarGridSpec(
        num_scalar_prefetch=1, grid=(rows // tr,), in_specs=[pl.BlockSpec((tr, cols), lambda i, pos: (i, 0))],
        out_specs=pl.BlockSpec((None, tr, cols), lambda i, pos: (pos[0], i, 0)))
    return pl.pallas_call(body, grid_spec=grid_spec, out_shape=jax.ShapeDtypeStruct((N_CHIPS, rows, cols), out_dtype),
                          name="place_block", compiler_params=_params("parallel"))(chip_core, shard)


def all_gather_chips(bufs):
    n = len(bufs)

    def body(*refs):
        outs = refs[n:2 * n]
        ici_send, ici_recv, d2d_send, d2d_recv = refs[2 * n:]
        x, y, c, k, sibling, chips = _place()

        def rdma(src, dst, send, recv, dev):
            return pltpu.make_async_remote_copy(src_ref=src, dst_ref=dst, send_sem=send, recv_sem=recv,
                                                device_id=dev, device_id_type=MESH)

        sends = []
        for t in range(n):
            rows = bufs[t].shape[1]
            for j, (px, py) in enumerate(chips):
                mine = outs[t].at[k, _half(rows, c)]
                sends.append(rdma(mine, mine, ici_send.at[t, j], ici_recv.at[t, j], (px, py, c)))
        for cp in sends:
            cp.start()
        for t in range(n):
            rows = bufs[t].shape[1]
            for j, (px, py) in enumerate(chips):
                landed = outs[t].at[2 * px + py, _half(rows, c)]
                rdma(landed, landed, ici_send.at[t, j], ici_recv.at[t, j], sibling).wait_recv()
                fwd = rdma(landed, landed, d2d_send.at[t, j], d2d_recv.at[t, j], sibling)
                fwd.start()
                sends.append(fwd)
        for t in range(n):
            rows = bufs[t].shape[1]
            for j, (px, py) in enumerate(chips):
                other = outs[t].at[2 * px + py, _half(rows, 1 - c)]
                rdma(other, other, d2d_send.at[t, j], d2d_recv.at[t, j], sibling).wait_recv()
        for cp in sends:
            cp.wait_send()

    return pl.pallas_call(
        body, in_specs=[ANY] * n, out_specs=[ANY] * n,
        out_shape=[jax.ShapeDtypeStruct(b.shape, b.dtype) for b in bufs],
        input_output_aliases={t: t for t in range(n)},
        scratch_shapes=[pltpu.SemaphoreType.DMA((n, 3))] * 4,
        name="all_gather_chips", compiler_params=pltpu.CompilerParams())(*bufs)


def _sum_rows_tile(rows):
    return _pick(rows, (256, 352, 128, 64, 32, 16))


def add_sibling_half(g, land, core):
    nb, half, cols = land.shape
    tr = _sum_rows_tile(half)
    nrb = half // tr

    def body(c_ref, g_ref, l_ref, o_ref):
        o_ref[...] = (g_ref[...].astype(F32) + l_ref[...].astype(F32)).astype(o_ref.dtype)

    spec = pl.BlockSpec((None, tr, cols), lambda b, i, c_ref: (b, i, 0))
    grid_spec = pltpu.PrefetchScalarGridSpec(
        num_scalar_prefetch=1, grid=(nb, nrb),
        in_specs=[pl.BlockSpec((None, tr, cols), lambda b, i, c_ref: (b, c_ref[1] * nrb + i, 0)), spec], out_specs=spec)
    return pl.pallas_call(body, grid_spec=grid_spec, out_shape=jax.ShapeDtypeStruct(land.shape, BF16),
                          name="add_sibling_half", compiler_params=_params("parallel", "parallel"))(core, g, land)


def sum_chip_blocks(p, l2, chip_core):
    nb, half, cols = l2.shape
    tr = _sum_rows_tile(half)
    nrb = half // tr

    def body(pos_ref, p_ref, l_ref, o_ref):
        acc = p_ref[...].astype(F32)
        for b in range(nb):
            acc = acc + l_ref[b].astype(F32)
        o_ref[...] = acc

    grid_spec = pltpu.PrefetchScalarGridSpec(
        num_scalar_prefetch=1, grid=(nrb,),
        in_specs=[pl.BlockSpec((None, tr, cols), lambda i, pos: (pos[0], i, 0)),
                  pl.BlockSpec((nb, tr, cols), lambda i, pos: (0, i, 0))],
        out_specs=pl.BlockSpec((tr, cols), lambda i, pos: (pos[1] * nrb + i, 0)))
    return pl.pallas_call(body, grid_spec=grid_spec, out_shape=jax.ShapeDtypeStruct((2 * half, cols), F32),
                          name="sum_chip_blocks", compiler_params=_params("parallel"))(chip_core, p, l2)


HBM_SPEC = pl.BlockSpec(memory_space=pltpu.HBM)
SEM_SPEC = pl.BlockSpec(memory_space=pltpu.SEMAPHORE)
SPLIT_EFFECT = pltpu.SideEffectType.DATAFLOW_SIDE_EFFECTING


def _in_hbm(v):
    return pltpu.with_memory_space_constraint(v, pltpu.HBM)


def _gather_ici_copies(bufs, refs, send, recv):
    x, y, c, k, sibling, chips = _place()
    cps = []
    for t in range(len(bufs)):
        rows = bufs[t].shape[1]
        for j, (px, py) in enumerate(chips):
            cps.append(pltpu.make_async_remote_copy(
                src_ref=refs[t].at[k, _half(rows, c)], dst_ref=refs[t].at[k, _half(rows, c)],
                send_sem=send.at[3 * t + j], recv_sem=recv.at[3 * t + j], device_id=(px, py, c), device_id_type=MESH))
    return cps


def gather_ici_start(bufs, after, name):
    n = len(bufs)

    def body(*refs):
        send, recv, token = refs[n + 1], refs[n + 2], refs[-1]
        for cp in _gather_ici_copies(bufs, refs[:n], send, recv):
            cp.start()
        token[...] = jnp.zeros_like(token)

    outs = pl.pallas_call(
        body, name=name, in_specs=[HBM_SPEC] * n + [ANY],
        out_shape=(pltpu.SemaphoreType.DMA((3 * n,)), pltpu.SemaphoreType.DMA((3 * n,)),
                   *[pltpu.HBM(b.shape, b.dtype) for b in bufs], jax.ShapeDtypeStruct(TOKEN_SHAPE, F32)),
        out_specs=(SEM_SPEC, SEM_SPEC, *[HBM_SPEC] * n, pl.BlockSpec(memory_space=pltpu.VMEM)),
        input_output_aliases={t: 2 + t for t in range(n)},
        compiler_params=pltpu.CompilerParams(has_side_effects=SPLIT_EFFECT))(*[_in_hbm(b) for b in bufs], after)
    return outs[0], outs[1], list(outs[2:2 + n]), outs[-1]


def gather_ici_wait(send, recv, bufs, after, name):
    n = len(bufs)

    def body(*refs):
        x, y, c, k, sibling, chips = _place()
        for t in range(n):
            rows = bufs[t].shape[1]
            for j, (px, py) in enumerate(chips):
                cp = pltpu.make_async_remote_copy(
                    src_ref=refs[t].at[k, _half(rows, c)], dst_ref=refs[t].at[2 * px + py, _half(rows, c)],
                    send_sem=refs[n].at[3 * t + j], recv_sem=refs[n + 1].at[3 * t + j], device_id=(px, py, c),
                    device_id_type=MESH)
                cp.wait_send()
                cp.wait_recv()

    return list(pl.pallas_call(
        body, name=name, in_specs=[HBM_SPEC] * n + [SEM_SPEC, SEM_SPEC, ANY],
        out_shape=tuple(pltpu.HBM(b.shape, b.dtype) for b in bufs), out_specs=tuple([HBM_SPEC] * n),
        input_output_aliases={t: t for t in range(n)},
        compiler_params=pltpu.CompilerParams(has_side_effects=SPLIT_EFFECT))(*bufs, send, recv, after))


def gather_forward_sibling(bufs):
    n = len(bufs)

    def body(*refs):
        outs = refs[n:2 * n]
        send, recv = refs[2 * n:]
        x, y, c, k, sibling, chips = _place()
        cps = []
        for t in range(n):
            rows = bufs[t].shape[1]
            for j, (px, py) in enumerate(chips):
                landed = outs[t].at[2 * px + py, _half(rows, c)]
                cps.append(pltpu.make_async_remote_copy(src_ref=landed, dst_ref=landed, send_sem=send.at[t, j],
                                                        recv_sem=recv.at[t, j], device_id=sibling, device_id_type=MESH))
        for cp in cps:
            cp.start()
        for t in range(n):
            rows = bufs[t].shape[1]
            for j, (px, py) in enumerate(chips):
                other = outs[t].at[2 * px + py, _half(rows, 1 - c)]
                pltpu.make_async_remote_copy(src_ref=other, dst_ref=other, send_sem=send.at[t, j], recv_sem=recv.at[t, j],
                                             device_id=sibling, device_id_type=MESH).wait_recv()
        for cp in cps:
            cp.wait_send()

    return pl.pallas_call(
        body, in_specs=[ANY] * n, out_specs=[ANY] * n, out_shape=[jax.ShapeDtypeStruct(b.shape, b.dtype) for b in bufs],
        input_output_aliases={t: t for t in range(n)}, scratch_shapes=[pltpu.SemaphoreType.DMA((n, 3))] * 2,
        name="gather_forward_sibling", compiler_params=pltpu.CompilerParams())(*bufs)


def _sibling_copies(gs, src_refs, dst_refs, send, recv):
    x, y, c, k, sibling, chips = _place()
    return [pltpu.make_async_remote_copy(
        src_ref=src_refs[t].at[:, _half(gs[t].shape[1], 1 - c), :], dst_ref=dst_refs[t], send_sem=send.at[t],
        recv_sem=recv.at[t], device_id=sibling, device_id_type=MESH) for t in range(len(gs))]


def sibling_start(gs, after, name):
    n = len(gs)
    lands = [lax.empty((g.shape[0], g.shape[1] // 2, g.shape[2]), g.dtype) for g in gs]

    def body(*refs):
        send, recv, token = refs[2 * n + 1], refs[2 * n + 2], refs[-1]
        for cp in _sibling_copies(gs, refs[:n], refs[n:2 * n], send, recv):
            cp.start()
        token[...] = jnp.zeros_like(token)

    outs = pl.pallas_call(
        body, name=name, in_specs=[HBM_SPEC] * (2 * n) + [ANY],
        out_shape=(pltpu.SemaphoreType.DMA((n,)), pltpu.SemaphoreType.DMA((n,)),
                   *[pltpu.HBM(v.shape, v.dtype) for v in gs + lands], jax.ShapeDtypeStruct(TOKEN_SHAPE, F32)),
        out_specs=(SEM_SPEC, SEM_SPEC, *[HBM_SPEC] * (2 * n), pl.BlockSpec(memory_space=pltpu.VMEM)),
        input_output_aliases={t: 2 + t for t in range(2 * n)},
        compiler_params=pltpu.CompilerParams(has_side_effects=SPLIT_EFFECT))(*[_in_hbm(v) for v in gs + lands], after)
    return outs[0], outs[1], list(outs[2:2 + n]), list(outs[2 + n:2 + 2 * n]), outs[-1]


def sibling_wait(send, recv, gs, lands, after, name):
    n = len(gs)

    def body(*refs):
        for cp in _sibling_copies(gs, refs[:n], refs[n:2 * n], refs[2 * n], refs[2 * n + 1]):
            cp.wait_send()
            cp.wait_recv()

    outs = pl.pallas_call(
        body, name=name, in_specs=[HBM_SPEC] * (2 * n) + [SEM_SPEC, SEM_SPEC, ANY],
        out_shape=tuple(pltpu.HBM(v.shape, v.dtype) for v in gs + lands), out_specs=tuple([HBM_SPEC] * (2 * n)),
        input_output_aliases={t: t for t in range(2 * n)},
        compiler_params=pltpu.CompilerParams(has_side_effects=SPLIT_EFFECT))(*gs, *lands, send, recv, after)
    return list(outs[:n]), list(outs[n:])


def _reduce_ici_copies(ps, src_refs, dst_refs, send, recv):
    x, y, c, k, sibling, chips = _place()
    cps = []
    for t in range(len(ps)):
        for j, (px, py) in enumerate(chips):
            cps.append(pltpu.make_async_remote_copy(
                src_ref=src_refs[t].at[2 * px + py], dst_ref=dst_refs[t].at[j], send_sem=send.at[3 * t + j],
                recv_sem=recv.at[3 * t + j],
                device_id=(px, py, c), device_id_type=MESH))
    return cps


def reduce_ici_start(ps, after, name):
    n = len(ps)
    lands = [lax.empty((3,) + p.shape[1:], p.dtype) for p in ps]

    def body(*refs):
        send, recv, token = refs[2 * n + 1], refs[2 * n + 2], refs[-1]
        for cp in _reduce_ici_copies(ps, refs[:n], refs[n:2 * n], send, recv):
            cp.start()
        token[...] = jnp.zeros_like(token)

    outs = pl.pallas_call(
        body, name=name, in_specs=[HBM_SPEC] * (2 * n) + [ANY],
        out_shape=(pltpu.SemaphoreType.DMA((3 * n,)), pltpu.SemaphoreType.DMA((3 * n,)),
                   *[pltpu.HBM(v.shape, v.dtype) for v in ps + lands], jax.ShapeDtypeStruct(TOKEN_SHAPE, F32)),
        out_specs=(SEM_SPEC, SEM_SPEC, *[HBM_SPEC] * (2 * n), pl.BlockSpec(memory_space=pltpu.VMEM)),
        input_output_aliases={t: 2 + t for t in range(2 * n)},
        compiler_params=pltpu.CompilerParams(has_side_effects=SPLIT_EFFECT))(*[_in_hbm(v) for v in ps + lands], after)
    return outs[0], outs[1], list(outs[2:2 + n]), list(outs[2 + n:2 + 2 * n]), outs[-1]


def _halves_copies(ss, refs, send, recv):
    x, y, c, k, sibling, chips = _place()
    return [pltpu.make_async_remote_copy(
        src_ref=refs[t].at[_half(ss[t].shape[0], c)], dst_ref=refs[t].at[_half(ss[t].shape[0], c)], send_sem=send.at[t],
        recv_sem=recv.at[t], device_id=sibling, device_id_type=MESH) for t in range(len(ss))]


def halves_start(ss, after, name):
    n = len(ss)

    def body(*refs):
        send, recv, token = refs[n + 1], refs[n + 2], refs[-1]
        for cp in _halves_copies(ss, refs[:n], send, recv):
            cp.start()
        token[...] = jnp.zeros_like(token)

    outs = pl.pallas_call(
        body, name=name, in_specs=[HBM_SPEC] * n + [ANY],
        out_shape=(pltpu.SemaphoreType.DMA((n,)), pltpu.SemaphoreType.DMA((n,)), *[pltpu.HBM(s.shape, s.dtype) for s in ss],
                   jax.ShapeDtypeStruct(TOKEN_SHAPE, F32)),
        out_specs=(SEM_SPEC, SEM_SPEC, *[HBM_SPEC] * n, pl.BlockSpec(memory_space=pltpu.VMEM)),
        input_output_aliases={t: 2 + t for t in range(n)},
        compiler_params=pltpu.CompilerParams(has_side_effects=SPLIT_EFFECT))(*[_in_hbm(s) for s in ss], after)
    return outs[0], outs[1], list(outs[2:2 + n]), outs[-1]


def halves_wait(send, recv, ss, after, name):
    n = len(ss)

    def body(*refs):
        x, y, c, k, sibling, chips = _place()
        for t in range(n):
            rows = ss[t].shape[0]
            cp = pltpu.make_async_remote_copy(
                src_ref=refs[t].at[_half(rows, c)], dst_ref=refs[t].at[_half(rows, 1 - c)], send_sem=refs[n].at[t],
                recv_sem=refs[n + 1].at[t], device_id=sibling, device_id_type=MESH)
            cp.wait_send()
            cp.wait_recv()

    return list(pl.pallas_call(
        body, name=name, in_specs=[HBM_SPEC] * n + [SEM_SPEC, SEM_SPEC, ANY],
        out_shape=tuple(pltpu.HBM(s.shape, s.dtype) for s in ss), out_specs=tuple([HBM_SPEC] * n),
        input_output_aliases={t: t for t in range(n)},
        compiler_params=pltpu.CompilerParams(has_side_effects=SPLIT_EFFECT))(*ss, send, recv, after))


def reduce_ici_wait(send, recv, ps, lands, after, name):
    n = len(ps)

    def body(*refs):
        for cp in _reduce_ici_copies(ps, refs[:n], refs[n:2 * n], refs[2 * n], refs[2 * n + 1]):
            cp.wait_send()
            cp.wait_recv()

    outs = pl.pallas_call(
        body, name=name, in_specs=[HBM_SPEC] * (2 * n) + [SEM_SPEC, SEM_SPEC, ANY],
        out_shape=tuple(pltpu.HBM(v.shape, v.dtype) for v in ps + lands), out_specs=tuple([HBM_SPEC] * (2 * n)),
        input_output_aliases={t: t for t in range(2 * n)},
        compiler_params=pltpu.CompilerParams(has_side_effects=SPLIT_EFFECT))(*ps, *lands, send, recv, after)
    return list(outs[:n]), list(outs[n:])


SMALL_CHUNK_ROWS = 256


def all_reduce_small(v, after):
    rows = v.shape[0]
    nchunk = rows // SMALL_CHUNK_ROWS

    def body(v_ref, after_ref, o_ref, buf, send, recv):
        x, y, c = lax.axis_index("x"), lax.axis_index("y"), lax.axis_index("c")
        me = 4 * x + 2 * y + c
        buf[me] = v_ref[...]
        cps = []
        for d in range(1, N_DEV):
            peer = (x ^ ((d >> 2) & 1), y ^ ((d >> 1) & 1), c ^ (d & 1))
            cps.append(pltpu.make_async_remote_copy(src_ref=v_ref, dst_ref=buf.at[me], send_sem=send.at[d - 1],
                                                    recv_sem=recv.at[d - 1], device_id=peer, device_id_type=MESH))
        for cp in cps:
            cp.start()
        for d in range(1, N_DEV):
            got = buf.at[me ^ d]
            pltpu.make_async_remote_copy(src_ref=got, dst_ref=got, send_sem=send.at[d - 1], recv_sem=recv.at[d - 1],
                                         device_id=(x, y, c), device_id_type=MESH).wait_recv()
        for cp in cps:
            cp.wait_send()

        def chunk(i, carry):
            sl = pl.ds(pl.multiple_of(i * SMALL_CHUNK_ROWS, SMALL_CHUNK_ROWS), SMALL_CHUNK_ROWS)
            acc = buf[0, sl, :]
            for s in range(1, N_DEV):
                acc = acc + buf[s, sl, :]
            o_ref[sl, :] = acc
            return carry

        lax.fori_loop(0, nchunk, chunk, 0)

    vmem = pl.BlockSpec(memory_space=pltpu.VMEM)
    return pl.pallas_call(
        body, in_specs=[vmem, ANY], out_specs=vmem, out_shape=jax.ShapeDtypeStruct(v.shape, F32),
        scratch_shapes=[pltpu.VMEM((N_DEV,) + v.shape, F32), pltpu.SemaphoreType.DMA((N_DEV - 1,)),
                        pltpu.SemaphoreType.DMA((N_DEV - 1,))],
        name="all_reduce_small",
        compiler_params=pltpu.CompilerParams(vmem_limit_bytes=VMEM_LIMIT_BYTES))(v, after)


def adamw(w, g, m, v):
    rows, cols = w.shape
    tr = rows
    for cand in (512, 256, 128, 64, 32, 16, 8):
        if rows % cand == 0 and cand * cols * 4 <= (1 << 20):
            tr = cand
            break
    c1 = 1.0 - ADAM_B1 ** ADAM_STEP
    c2 = 1.0 - ADAM_B2 ** ADAM_STEP

    def body(w_ref, g_ref, m_ref, v_ref, go_ref, d_ref, nm_ref, nv_ref):
        gf = g_ref[...]
        go_ref[...] = gf
        nm = ADAM_B1 * m_ref[...] + (1.0 - ADAM_B1) * gf
        nv = ADAM_B2 * v_ref[...] + (1.0 - ADAM_B2) * (gf * gf)
        d_ref[...] = -ADAM_LR * ((nm / c1) / (jnp.sqrt(nv / c2) + ADAM_EPS) + ADAM_WD * w_ref[...])
        nm_ref[...] = nm
        nv_ref[...] = nv

    spec = pl.BlockSpec((tr, cols), lambda i: (i, 0))
    shape = jax.ShapeDtypeStruct((rows, cols), F32)
    return pl.pallas_call(body, grid=(rows // tr,), in_specs=[spec] * 4, out_specs=(spec,) * 4, out_shape=(shape,) * 4,
                          name="adamw", compiler_params=_params("parallel"))(w, g, m, v)


TAP_ROWS_ALIGN = 16
FLAT_ALIGN = 128 * SMALL_CHUNK_ROWS


def _pad_to(a, n):
    return jnp.pad(a, (0, n - a.shape[0]))


def _round_up(n, m):
    return (n + m - 1) // m * m


class Exchanges:
    def __init__(self, a, chip_core):
        self.a, self.chip_core = a, chip_core
        self.bufs = []
        for g in range(len(SUBLAYERS)):
            row = []
            for n in sublayer_weight_names(g):
                w = a[n].reshape(-1, a[n].shape[-1])
                if _kind(n) == "tap":
                    w = jnp.pad(w, ((0, _round_up(w.shape[0], TAP_ROWS_ALIGN) - w.shape[0]), (0, 0)))
                row.append(place_block(w, chip_core, F32 if _kind(n) == "tap" else BF16))
            self.bufs.append(row)
        self.started = {}
        self.after = chip_core
        self.stage = [None, None, None]
        self.reduced = []
        self.advanced_midway = False
        self.results = {}

    def _unpack(self, g, gathered, W):
        for n, v in zip(sublayer_weight_names(g), gathered):
            kind = _kind(n)
            if kind == "col":
                W[n] = v
            elif kind == "row":
                W[n] = v.reshape(-1, v.shape[-1])
            elif kind == "grp":
                grp, r, pg = self.a[n].shape
                W[n] = v.reshape(N_CHIPS, grp, r, pg).transpose(1, 0, 2, 3).reshape(grp, N_CHIPS * r, pg)
            else:
                nt = self.a[n].shape[0]
                W[n] = v[:, :nt].transpose(1, 0, 2).reshape(nt, -1)

    def gather_first(self, W):
        gathered = all_gather_chips(self.bufs[0])
        self._unpack(0, gathered, W)
        self.after = gathered[0]

    def forward_begins(self, g, W):
        token = None
        for h in (g + 1, g + 2):
            if h < len(SUBLAYERS) and h not in self.started:
                send, recv, bufs, token = gather_ici_start(self.bufs[h], self.after, f"gather_start_{h}")
                self.started[h] = (send, recv, bufs)
                self.after = token
        return token

    def forward_ends(self, g, x, W):
        if g + 1 == len(SUBLAYERS):
            return
        send, recv, bufs = self.started[g + 1]
        gathered = gather_forward_sibling(gather_ici_wait(send, recv, bufs, x, f"gather_wait_{g + 1}"))
        self._unpack(g + 1, gathered, W)
        self.after = gathered[0]

    def gradients_ready(self, g, grads):
        names = [n for n in sublayer_weight_names(g) if _kind(n) != "tap"]
        gl = []
        for n in names:
            v, kind = grads.pop(n), _kind(n)
            if kind == "row":
                v = v.reshape(N_CHIPS, -1, v.shape[-1])
            elif kind == "grp":
                grp, r, pg = self.a[n].shape
                v = v.reshape(grp, N_CHIPS, r, pg).transpose(1, 0, 2, 3).reshape(N_CHIPS, grp * r, pg)
            gl.append(v)
        last = gl[0] if self.advanced_midway else self.advance(gl[0], gl[0])
        send, recv, gl, lands, token = sibling_start(gl, last, f"sibling_start_{g}")
        self.stage[0] = (g, names, send, recv, gl, lands)
        return token

    def midway(self, after):
        self.advanced_midway = True
        last = self.advance(after, after)
        return last if last.shape == TOKEN_SHAPE else None

    def advance(self, after, last):
        if self.stage[2] is not None:
            self.reduced.append(self.stage[2])
        self.stage[2], last = self._to_halves(self.stage[1], after, last)
        self.stage[1], last = self._to_ici(self.stage[0], after, last)
        self.stage[0] = None
        return last

    def _to_ici(self, entry, after, last):
        if entry is None:
            return None, last
        g, names, send, recv, gl, lands = entry
        gl, lands = sibling_wait(send, recv, gl, lands, after, f"sibling_wait_{g}")
        ps = [add_sibling_half(v, l, self.chip_core) for v, l in zip(gl, lands)]
        send, recv, ps, l2s, last = reduce_ici_start(ps, last, f"reduce_start_{g}")
        return (g, names, send, recv, ps, l2s), last

    def _to_halves(self, entry, after, last):
        if entry is None:
            return None, last
        g, names, send, recv, ps, l2s = entry
        ps, l2s = reduce_ici_wait(send, recv, ps, l2s, after, f"reduce_wait_{g}")
        ss = [sum_chip_blocks(p, l2, self.chip_core) for p, l2 in zip(ps, l2s)]
        send, recv, ss, last = halves_start(ss, last, f"halves_start_{g}")
        return (g, names, send, recv, ss), last

    def update_matrices(self, entry, after):
        g, names, send, recv, ss = entry
        for n, grad in zip(names, halves_wait(send, recv, ss, after, f"halves_wait_{g}")):
            shape = self.a[n].shape
            two_d = lambda v: v.reshape(-1, shape[-1])
            outs = adamw(two_d(self.a[n]), grad, two_d(self.a["m_" + n]), two_d(self.a["v_" + n]))
            self.results[n] = tuple(v.reshape(shape) for v in outs)
        return outs[0]

    def finish(self, after):
        first, last = self._to_ici(self.stage[0], after, after)
        for entry in self.reduced + [self.stage[2]]:
            if entry is not None:
                last = self.update_matrices(entry, last)
        second, last = self._to_halves(self.stage[1], last, last)
        first, last = self._to_halves(first, last, last)
        return [second, first], last


def train_step(a):
    x, positions, tgt = a["x"][0], a["positions"][0], a["loss_target"][0]
    mats = [n for n in WEIGHT_NAMES if _kind(n) in ("col", "row", "grp")]
    taps = [n for n in WEIGHT_NAMES if _kind(n) == "tap"]
    reps = [n for n in WEIGHT_NAMES if _kind(n) == "rep"]
    chip = 2 * lax.axis_index("x") + lax.axis_index("y")
    chip_core = jnp.stack([chip, lax.axis_index("c")]).astype(jnp.int32)

    comm = Exchanges(a, chip_core)
    W = {n: a[n] for n in reps}
    comm.gather_first(W)
    loss, dx, grads = local_step(x, positions, tgt, W, comm)
    loss = lax.psum(loss, ("x", "y", "c"))
    left, last = comm.finish(dx)

    n_rep = _round_up(sum(a[n].size for n in reps), FLAT_ALIGN)
    flat_rep = _pad_to(jnp.concatenate([grads[n].reshape(-1) for n in reps]), n_rep)
    flat_tap = jnp.concatenate([grads[n].reshape(-1) for n in taps])
    flat = jnp.concatenate([flat_rep, _pad_to(flat_tap, _round_up(flat_tap.shape[0], FLAT_ALIGN))])
    summed = all_reduce_small(flat.reshape(-1, 128), last)
    for entry in left:
        comm.update_matrices(entry, summed)
    rep_rows = n_rep // 128
    tap_flat = summed[rep_rows:].reshape(-1)

    out = dict(comm.results)
    pack = lambda pre: _pad_to(jnp.concatenate([a[pre + n].reshape(-1) for n in reps]), n_rep).reshape(-1, 128)
    g_rep, d_rep, m_rep, v_rep = adamw(pack(""), summed[:rep_rows], pack("m_"), pack("v_"))
    off = 0
    for n in reps:
        size, shape = a[n].size, a[n].shape
        out[n] = tuple(f.reshape(-1)[off:off + size].reshape(shape) for f in (g_rep, d_rep, m_rep, v_rep))
        off += size
    off = 0
    for n in taps:
        nt, cs = a[n].shape
        full = tap_flat[off:off + nt * cs * N_CHIPS].reshape(nt, cs * N_CHIPS)
        off += nt * cs * N_CHIPS
        g = lax.dynamic_slice(full, (0, chip * cs), (nt, cs))
        out[n] = tuple(adamw(a[n], g, a["m_" + n], a["v_" + n]))

    res = [loss, dx[None]]
    for part in range(4):
        res += [out[n][part] for n in WEIGHT_NAMES]
    return tuple(res)


def kernel(x, positions, l0_norm_g, l0_a_w_in, l0_a_b_in, l0_a_dw_w, l0_a_dw_b, l0_a_ln_g, l0_a_ln_b, l0_a_w_out, l0_a_b_out, l0_ffn_norm_g, l0_ffn_w_up, l0_ffn_dw_w, l0_ffn_dw_b, l0_ffn_w_down, l1_norm_g, l1_b_w_group, l1_b_scale, l1_ffn_norm_g, l1_ffn_w_up, l1_ffn_dw_w, l1_ffn_dw_b, l1_ffn_w_down, l2_norm_g, l2_c_w_qkv, l2_c_q_norm_g, l2_c_k_norm_g, l2_c_sinks, l2_c_w_o, l2_ffn_norm_g, l2_ffn_w_up, l2_ffn_dw_w, l2_ffn_dw_b, l2_ffn_w_down, l3_norm_g, l3_a_w_in, l3_a_b_in, l3_a_dw_w, l3_a_dw_b, l3_a_ln_g, l3_a_ln_b, l3_a_w_out, l3_a_b_out, l3_ffn_norm_g, l3_ffn_w_up, l3_ffn_dw_w, l3_ffn_dw_b, l3_ffn_w_down, loss_target, m_l0_norm_g, m_l0_a_w_in, m_l0_a_b_in, m_l0_a_dw_w, m_l0_a_dw_b, m_l0_a_ln_g, m_l0_a_ln_b, m_l0_a_w_out, m_l0_a_b_out, m_l0_ffn_norm_g, m_l0_ffn_w_up, m_l0_ffn_dw_w, m_l0_ffn_dw_b, m_l0_ffn_w_down, m_l1_norm_g, m_l1_b_w_group, m_l1_b_scale, m_l1_ffn_norm_g, m_l1_ffn_w_up, m_l1_ffn_dw_w, m_l1_ffn_dw_b, m_l1_ffn_w_down, m_l2_norm_g, m_l2_c_w_qkv, m_l2_c_q_norm_g, m_l2_c_k_norm_g, m_l2_c_sinks, m_l2_c_w_o, m_l2_ffn_norm_g, m_l2_ffn_w_up, m_l2_ffn_dw_w, m_l2_ffn_dw_b, m_l2_ffn_w_down, m_l3_norm_g, m_l3_a_w_in, m_l3_a_b_in, m_l3_a_dw_w, m_l3_a_dw_b, m_l3_a_ln_g, m_l3_a_ln_b, m_l3_a_w_out, m_l3_a_b_out, m_l3_ffn_norm_g, m_l3_ffn_w_up, m_l3_ffn_dw_w, m_l3_ffn_dw_b, m_l3_ffn_w_down, v_l0_norm_g, v_l0_a_w_in, v_l0_a_b_in, v_l0_a_dw_w, v_l0_a_dw_b, v_l0_a_ln_g, v_l0_a_ln_b, v_l0_a_w_out, v_l0_a_b_out, v_l0_ffn_norm_g, v_l0_ffn_w_up, v_l0_ffn_dw_w, v_l0_ffn_dw_b, v_l0_ffn_w_down, v_l1_norm_g, v_l1_b_w_group, v_l1_b_scale, v_l1_ffn_norm_g, v_l1_ffn_w_up, v_l1_ffn_dw_w, v_l1_ffn_dw_b, v_l1_ffn_w_down, v_l2_norm_g, v_l2_c_w_qkv, v_l2_c_q_norm_g, v_l2_c_k_norm_g, v_l2_c_sinks, v_l2_c_w_o, v_l2_ffn_norm_g, v_l2_ffn_w_up, v_l2_ffn_dw_w, v_l2_ffn_dw_b, v_l2_ffn_w_down, v_l3_norm_g, v_l3_a_w_in, v_l3_a_b_in, v_l3_a_dw_w, v_l3_a_dw_b, v_l3_a_ln_g, v_l3_a_ln_b, v_l3_a_w_out, v_l3_a_b_out, v_l3_ffn_norm_g, v_l3_ffn_w_up, v_l3_ffn_dw_w, v_l3_ffn_dw_b, v_l3_ffn_w_down):
    return train_step(dict(locals()))
```

```python
import functools

import jax
import jax.numpy as jnp
from jax import lax
from jax.experimental import pallas as pl
from jax.experimental.pallas import tpu as pltpu

F32 = jnp.float32
BF16 = jnp.bfloat16
EPS = 1e-6
HEAD_DIM = 64
KV_GROUP = 8
ATT_BLOCK = 128
ROT_DIM = 16
ROPE_THETA = 500000.0
POOL_GROUPS = 4
CONF_TAPS = 31
FFN_TAPS = 3
N_CHIPS = 4
N_DEV = 8
ADAM_LR, ADAM_B1, ADAM_B2, ADAM_EPS, ADAM_WD, ADAM_STEP = 0.001, 0.9, 0.999, 1e-08, 0.01, 10
VMEM_LIMIT_BYTES = 56 * 1024 * 1024
MESH = pl.DeviceIdType.MESH

CONF_NAMES = ["norm_g", "a_w_in", "a_b_in", "a_dw_w", "a_dw_b", "a_ln_g", "a_ln_b", "a_w_out", "a_b_out"]
FFN_NAMES = ["ffn_norm_g", "ffn_w_up", "ffn_dw_w", "ffn_dw_b", "ffn_w_down"]
POOL_NAMES = ["norm_g", "b_w_group", "b_scale"]
ATT_NAMES = ["norm_g", "c_w_qkv", "c_q_norm_g", "c_k_norm_g", "c_sinks", "c_w_o"]
WEIGHT_NAMES = ([f"l0_{n}" for n in CONF_NAMES + FFN_NAMES] + [f"l1_{n}" for n in POOL_NAMES + FFN_NAMES]
                + [f"l2_{n}" for n in ATT_NAMES + FFN_NAMES] + [f"l3_{n}" for n in CONF_NAMES + FFN_NAMES])
COL_SHARDED = ("a_w_in", "ffn_w_up", "c_w_qkv")
ROW_SHARDED = ("a_w_out", "ffn_w_down", "c_w_o")
TAP_SHARDED = ("a_dw_w", "ffn_dw_w")


def _kind(name):
    base = name[3:]
    if base in COL_SHARDED:
        return "col"
    if base in ROW_SHARDED:
        return "row"
    if base in TAP_SHARDED:
        return "tap"
    if base == "b_w_group":
        return "grp"
    return "rep"


def _pick(n, prefs):
    for p in prefs:
        if p <= n and n % p == 0:
            return p
    return n


def _params(*sem):
    return pltpu.CompilerParams(dimension_semantics=sem, vmem_limit_bytes=VMEM_LIMIT_BYTES)


def _sigmoid(x):
    return 1.0 / (1.0 + jnp.exp(-x))


NN = (((1,), (0,)), ((), ()))
NT = (((1,), (1,)), ((), ()))
TN = (((0,), (0,)), ((), ()))


MM_VMEM_BUDGET = 44 * 1024 * 1024
MM_STEP_SECONDS = 0.35e-6
MM_FLOPS, MM_HBM_BYTES = 9.0e14, 3.0e12
TILE_SIZES = (4096, 2816, 2048, 1408, 1024, 704, 640, 512, 256, 128)


def _tile_options(n, lane):
    opts = [c for c in TILE_SIZES if c <= n and n % c == 0 and (not lane or c % 128 == 0)]
    return opts or [n]


def _mm_plan(m, n, k, *, n_unit=None, k_unit=None, a_bytes=2, b_bytes=2, o_bytes=2, extra_bytes=0):
    best = None
    for tm in _tile_options(m, False):
        for tn in _tile_options(n_unit or n, True):
            for tk in _tile_options(k_unit or k, True) + ([k] if not k_unit else []):
                nk = k // tk
                vmem = 2 * (tm * tk * a_bytes + tk * tn * b_bytes + tm * tn * (o_bytes + extra_bytes)) + tm * tn * 4 * (2 if nk > 1 else 1)
                if vmem > MM_VMEM_BUDGET:
                    continue
                ni, nj = m // tm, n // tn
                a_all, b_all, o_all = m * k * a_bytes, k * n * b_bytes, m * n * (o_bytes + extra_bytes)
                for i_inner in (False, True):
                    if nk > 1:
                        traffic = a_all * nj + b_all * ni + o_all
                    elif i_inner:
                        traffic = a_all * nj + b_all + o_all
                    else:
                        traffic = a_all + b_all * ni + o_all
                    cost = ni * nj * nk * MM_STEP_SECONDS + max(2.0 * m * n * k / MM_FLOPS, traffic / MM_HBM_BYTES)
                    if best is None or cost < best[0]:
                        best = (cost, tm, tn, tk, i_inner)
    assert best is not None, (m, n, k)
    return best[1:]


def _mm(a, b, *, dims, sizes, plan, a_blk, a_idx, b_blk, b_idx, o_blk, o_idx, out_shape, name,
        bias=None, scale=None, vec_blk=None, vec_idx=None, resid=None, raw_shape=None):
    m, n, k = sizes
    tm, tn, tk, i_inner = plan
    ni, nj, nk = m // tm, n // tn, k // tk
    has_bias, has_scale, has_resid, want_raw = bias is not None, scale is not None, resid is not None, raw_shape is not None

    def body(*refs):
        a_ref, b_ref = refs[0], refs[1]
        pos = 2
        bias_ref = scale_ref = resid_ref = raw_ref = None
        if has_bias:
            bias_ref = refs[pos]; pos += 1
        if has_scale:
            scale_ref = refs[pos]; pos += 1
        if has_resid:
            resid_ref = refs[pos]; pos += 1
        o_ref = refs[pos]; pos += 1
        if want_raw:
            raw_ref = refs[pos]; pos += 1
        part = lax.dot_general(a_ref[...].astype(BF16), b_ref[...].astype(BF16), dims, preferred_element_type=F32)

        def finish(r):
            if want_raw:
                raw_ref[...] = r.astype(raw_ref.dtype)
            if has_bias:
                r = r + bias_ref[...]
            if has_scale:
                r = r * scale_ref[...]
            if has_resid:
                r = r + resid_ref[...]
            o_ref[...] = r.astype(o_ref.dtype)

        if nk == 1:
            finish(part)
        else:
            acc_ref = refs[pos]
            kk = pl.program_id(2)

            @pl.when(kk == 0)
            def _():
                acc_ref[...] = part

            @pl.when(kk > 0)
            def _():
                acc_ref[...] += part

            @pl.when(kk == nk - 1)
            def _():
                finish(acc_ref[...])

    order = (lambda f: (lambda j, i, kk: f(i, j, kk))) if i_inner else (lambda f: f)
    spec = lambda blk, idx: pl.BlockSpec(blk, order(idx))
    operands, in_specs = [a, b], [spec(a_blk, a_idx), spec(b_blk, b_idx)]
    for v in (bias, scale):
        if v is not None:
            operands.append(v); in_specs.append(spec(vec_blk, vec_idx))
    if has_resid:
        operands.append(resid); in_specs.append(spec(o_blk, o_idx))
    out_shapes, out_specs = out_shape, spec(o_blk, o_idx)
    if want_raw:
        out_shapes, out_specs = (out_shape, raw_shape), (spec(o_blk, o_idx), spec(o_blk, o_idx))
    return pl.pallas_call(
        body, grid=(nj, ni, nk) if i_inner else (ni, nj, nk), in_specs=in_specs, out_specs=out_specs,
        out_shape=out_shapes, scratch_shapes=[pltpu.VMEM((tm, tn), F32)] if nk > 1 else [], name=name,
        compiler_params=_params("parallel", "parallel", "arbitrary"))(*operands)


def mm_nn_cols(a, g, *, split, out_dtype, bias=None, name):
    t, k = a.shape
    ns = g.shape[2]
    n = N_CHIPS * ns
    plan = _mm_plan(t, n, k, n_unit=ns, o_bytes=jnp.dtype(out_dtype).itemsize)
    tm, tn, tk, _ = plan
    nj = ns // tn
    if split:
        o_blk, o_idx = (None, tm, tn), (lambda i, j, kk: (j // (2 * nj), i, j % (2 * nj)))
        out_shape = jax.ShapeDtypeStruct((2, t, 2 * ns), out_dtype)
        vec_blk, vec_idx = (None, 1, tn), (lambda i, j, kk: (j // (2 * nj), 0, j % (2 * nj)))
        if bias is not None:
            bias = bias.reshape(2, 1, 2 * ns)
    else:
        o_blk, o_idx = (tm, tn), (lambda i, j, kk: (i, j))
        out_shape = jax.ShapeDtypeStruct((t, n), out_dtype)
        vec_blk, vec_idx = (1, tn), (lambda i, j, kk: (0, j))
        if bias is not None:
            bias = bias.reshape(1, n)
    return _mm(a, g, dims=NN, sizes=(t, n, k), plan=plan, a_blk=(tm, tk), a_idx=lambda i, j, kk: (i, kk),
               b_blk=(None, tk, tn), b_idx=lambda i, j, kk: (j // nj, kk, j % nj), o_blk=o_blk, o_idx=o_idx,
               out_shape=out_shape, name=name, bias=bias, vec_blk=vec_blk, vec_idx=vec_idx)


def mm_nn(a, w, *, out_dtype, bias=None, scale=None, resid=None, raw_dtype=None, name):
    t, k = a.shape
    n = w.shape[1]
    extra = (4 if resid is not None else 0) + (0 if raw_dtype is None else jnp.dtype(raw_dtype).itemsize)
    plan = _mm_plan(t, n, k, a_bytes=a.dtype.itemsize, o_bytes=jnp.dtype(out_dtype).itemsize, extra_bytes=extra)
    tm, tn, tk, _ = plan
    raw_shape = None if raw_dtype is None else jax.ShapeDtypeStruct((t, n), raw_dtype)
    return _mm(a, w, dims=NN, sizes=(t, n, k), plan=plan, a_blk=(tm, tk), a_idx=lambda i, j, kk: (i, kk),
               b_blk=(tk, tn), b_idx=lambda i, j, kk: (kk, j), o_blk=(tm, tn), o_idx=lambda i, j, kk: (i, j),
               out_shape=jax.ShapeDtypeStruct((t, n), out_dtype), name=name,
               bias=None if bias is None else bias.reshape(1, n), scale=None if scale is None else scale.reshape(1, n),
               vec_blk=(1, tn), vec_idx=lambda i, j, kk: (0, j), resid=resid, raw_shape=raw_shape)


def mm_nt(dy, w, *, out_dtype, name):
    t, n = dy.shape
    kdim = w.shape[0]
    plan = _mm_plan(t, kdim, n, a_bytes=dy.dtype.itemsize, o_bytes=jnp.dtype(out_dtype).itemsize)
    tm, tn, tk, _ = plan
    return _mm(dy, w, dims=NT, sizes=(t, kdim, n), plan=plan, a_blk=(tm, tk), a_idx=lambda i, j, kk: (i, kk),
               b_blk=(tn, tk), b_idx=lambda i, j, kk: (j, kk), o_blk=(tm, tn), o_idx=lambda i, j, kk: (i, j),
               out_shape=jax.ShapeDtypeStruct((t, kdim), out_dtype), name=name)


def mm_nt_cols(du, g, *, split, out_dtype, name):
    kdim, ns = g.shape[1], g.shape[2]
    t = du.shape[1] if split else du.shape[0]
    plan = _mm_plan(t, kdim, N_CHIPS * ns, k_unit=ns, o_bytes=jnp.dtype(out_dtype).itemsize)
    tm, tn, tk, _ = plan
    nkb = ns // tk
    if split:
        a_blk, a_idx = (None, tm, tk), (lambda i, j, kk: (kk // (2 * nkb), i, kk % (2 * nkb)))
    else:
        a_blk, a_idx = (tm, tk), (lambda i, j, kk: (i, kk))
    return _mm(du, g, dims=NT, sizes=(t, kdim, N_CHIPS * ns), plan=plan, a_blk=a_blk, a_idx=a_idx,
               b_blk=(None, tn, tk), b_idx=lambda i, j, kk: (kk // nkb, j, kk % nkb),
               o_blk=(tm, tn), o_idx=lambda i, j, kk: (i, j),
               out_shape=jax.ShapeDtypeStruct((t, kdim), out_dtype), name=name)


def mm_wgrad(at, dy, *, out_dtype, name):
    return mm_nn(at, dy, out_dtype=out_dtype, name=name)


def mm_wgrad_cols(ht, du, *, split, out_dtype, name):
    kdim, t = ht.shape
    ns = (du.shape[2] // 2) if split else (du.shape[1] // N_CHIPS)
    plan = _mm_plan(kdim, N_CHIPS * ns, t, n_unit=ns, o_bytes=jnp.dtype(out_dtype).itemsize)
    tm, tn, tk, _ = plan
    nj = ns // tn
    if split:
        b_blk, b_idx = (None, tk, tn), (lambda i, j, kk: (j // (2 * nj), kk, j % (2 * nj)))
    else:
        b_blk, b_idx = (tk, tn), (lambda i, j, kk: (kk, j))
    return _mm(ht, du, dims=NN, sizes=(kdim, N_CHIPS * ns, t), plan=plan, a_blk=(tm, tk), a_idx=lambda i, j, kk: (i, kk),
               b_blk=b_blk, b_idx=b_idx, o_blk=(None, tm, tn), o_idx=lambda i, j, kk: (j // nj, i, j % nj),
               out_shape=jax.ShapeDtypeStruct((N_CHIPS, kdim, ns), out_dtype), name=name)


def _row_tile(t):
    return _pick(t, (256, 128))


def _acc_rows(ref, part, i):
    @pl.when(i == 0)
    def _():
        ref[...] = part

    @pl.when(i > 0)
    def _():
        ref[...] += part


TOKEN_SHAPE = (8, 128)


def _token_operand(token):
    if token is None:
        return [], []
    return [token], [pl.BlockSpec(TOKEN_SHAPE, lambda i: (0, 0))]


def rms_fwd(x, g, out_dtype, token=None, transposed=False):
    t, d = x.shape
    tr = _row_tile(t)

    def body(x_ref, g_ref, *rest):
        outs = rest[-2:] if transposed else rest[-1:]
        xf = x_ref[...]
        r = lax.rsqrt(jnp.mean(xf * xf, axis=-1, keepdims=True) + EPS)
        y = xf * r * g_ref[...]
        outs[0][...] = y.astype(outs[0].dtype)
        if transposed:
            outs[1][...] = y.T.astype(BF16)

    row = pl.BlockSpec((tr, d), lambda i: (i, 0))
    tok, tok_spec = _token_operand(token)
    out_specs, out_shape = row, jax.ShapeDtypeStruct((t, d), out_dtype)
    if transposed:
        out_specs = (row, pl.BlockSpec((d, tr), lambda i: (0, i)))
        out_shape = (out_shape, jax.ShapeDtypeStruct((d, t), BF16))
    return pl.pallas_call(body, grid=(t // tr,), in_specs=[row, pl.BlockSpec((1, d), lambda i: (0, 0))] + tok_spec,
                          out_specs=out_specs, out_shape=out_shape, name="rms_fwd",
                          compiler_params=_params("parallel"))(x, g.reshape(1, d), *tok)


def rms_bwd(x, g, dh, dres, token=None):
    t, d = x.shape
    tr = _row_tile(t)

    def body(x_ref, g_ref, dh_ref, dres_ref, *rest):
        dx_ref, dx16_ref, dg_ref = rest[-3:]
        i = pl.program_id(0)
        xf = x_ref[...]
        r = lax.rsqrt(jnp.mean(xf * xf, axis=-1, keepdims=True) + EPS)
        xhat = xf * r
        dhf = dh_ref[...].astype(F32)
        dxh = dhf * g_ref[...]
        m = jnp.mean(dxh * xhat, axis=-1, keepdims=True)
        dx = dres_ref[...] + r * (dxh - xhat * m)
        dx_ref[...] = dx
        dx16_ref[...] = dx.astype(BF16)
        _acc_rows(dg_ref, jnp.sum(dhf * xhat, axis=0, keepdims=True), i)

    row = pl.BlockSpec((tr, d), lambda i: (i, 0))
    vec = pl.BlockSpec((1, d), lambda i: (0, 0))
    tok, tok_spec = _token_operand(token)
    return pl.pallas_call(body, grid=(t // tr,), in_specs=[row, vec, row, row] + tok_spec, out_specs=(row, row, vec),
                          out_shape=(jax.ShapeDtypeStruct((t, d), F32), jax.ShapeDtypeStruct((t, d), BF16),
                                     jax.ShapeDtypeStruct((1, d), F32)),
                          name="rms_bwd", compiler_params=_params("arbitrary"))(x, g.reshape(1, d), dh, dres, *tok)


def ln_silu_fwd(c, g, b):
    t, d = c.shape
    tr = _row_tile(t)

    def body(c_ref, g_ref, b_ref, o_ref, ot_ref):
        xf = c_ref[...]
        mu = jnp.mean(xf, axis=-1, keepdims=True)
        xc = xf - mu
        var = jnp.mean(xc * xc, axis=-1, keepdims=True)
        n = xc * lax.rsqrt(var + EPS) * g_ref[...] + b_ref[...]
        s = n * _sigmoid(n)
        o_ref[...] = s.astype(o_ref.dtype)
        ot_ref[...] = s.T.astype(ot_ref.dtype)

    row = pl.BlockSpec((tr, d), lambda i: (i, 0))
    vec = pl.BlockSpec((1, d), lambda i: (0, 0))
    return pl.pallas_call(body, grid=(t // tr,), in_specs=[row, vec, vec],
                          out_specs=(row, pl.BlockSpec((d, tr), lambda i: (0, i))),
                          out_shape=(jax.ShapeDtypeStruct((t, d), BF16), jax.ShapeDtypeStruct((d, t), BF16)),
                          name="ln_silu_fwd", compiler_params=_params("parallel"))(c, g.reshape(1, d), b.reshape(1, d))


def ln_silu_bwd(c, g, b, ds, token=None):
    t, d = c.shape
    tr = _row_tile(t)

    def body(c_ref, g_ref, b_ref, ds_ref, *rest):
        dc_ref, dg_ref, db_ref = rest[-3:]
        i = pl.program_id(0)
        xf = c_ref[...]
        mu = jnp.mean(xf, axis=-1, keepdims=True)
        xc = xf - mu
        var = jnp.mean(xc * xc, axis=-1, keepdims=True)
        rstd = lax.rsqrt(var + EPS)
        xhat = xc * rstd
        n = xhat * g_ref[...] + b_ref[...]
        sg = _sigmoid(n)
        dn = ds_ref[...].astype(F32) * (sg * (1.0 + n * (1.0 - sg)))
        dxh = dn * g_ref[...]
        m1 = jnp.mean(dxh, axis=-1, keepdims=True)
        m2 = jnp.mean(dxh * xhat, axis=-1, keepdims=True)
        dc_ref[...] = rstd * (dxh - m1 - xhat * m2)
        _acc_rows(dg_ref, jnp.sum(dn * xhat, axis=0, keepdims=True), i)
        _acc_rows(db_ref, jnp.sum(dn, axis=0, keepdims=True), i)

    row = pl.BlockSpec((tr, d), lambda i: (i, 0))
    vec = pl.BlockSpec((1, d), lambda i: (0, 0))
    vshape = jax.ShapeDtypeStruct((1, d), F32)
    tok, tok_spec = _token_operand(token)
    return pl.pallas_call(body, grid=(t // tr,), in_specs=[row, vec, vec, row] + tok_spec, out_specs=(row, vec, vec),
                          out_shape=(jax.ShapeDtypeStruct((t, d), F32), vshape, vshape), name="ln_silu_bwd",
                          compiler_params=_params("arbitrary"))(c, g.reshape(1, d), b.reshape(1, d), ds, *tok)


def loss_grad(y, tgt):
    t, d = y.shape
    tr = _row_tile(t)

    def body(y_ref, t_ref, dy_ref, dy16_ref, sq_ref):
        i = pl.program_id(0)
        err = y_ref[...] - t_ref[...]
        dy = err * (1.0 / d)
        dy_ref[...] = dy
        dy16_ref[...] = dy.astype(BF16)
        _acc_rows(sq_ref, jnp.sum(err * err, axis=0, keepdims=True), i)

    row = pl.BlockSpec((tr, d), lambda i: (i, 0))
    vec = pl.BlockSpec((1, d), lambda i: (0, 0))
    return pl.pallas_call(body, grid=(t // tr,), in_specs=[row, row], out_specs=(row, row, vec),
                          out_shape=(jax.ShapeDtypeStruct((t, d), F32), jax.ShapeDtypeStruct((t, d), BF16),
                                     jax.ShapeDtypeStruct((1, d), F32)),
                          name="loss_grad", compiler_params=_params("arbitrary"))(y, tgt)


def col_sum(a):
    t, d = a.shape
    tr = _row_tile(t)

    def body(a_ref, o_ref):
        _acc_rows(o_ref, jnp.sum(a_ref[...].astype(F32), axis=0, keepdims=True), pl.program_id(0))

    return pl.pallas_call(body, grid=(t // tr,), in_specs=[pl.BlockSpec((tr, d), lambda i: (i, 0))],
                          out_specs=pl.BlockSpec((1, d), lambda i: (0, 0)), out_shape=jax.ShapeDtypeStruct((1, d), F32),
                          name="col_sum", compiler_params=_params("arbitrary"))(a)


def scale_bwd(dx, ypre, scale):
    t, d = dx.shape
    tr = _row_tile(t)

    def body(dx_ref, y_ref, s_ref, dy_ref, ds_ref):
        dxf = dx_ref[...]
        dy_ref[...] = (dxf * s_ref[...]).astype(dy_ref.dtype)
        _acc_rows(ds_ref, jnp.sum(dxf * y_ref[...], axis=0, keepdims=True), pl.program_id(0))

    row = pl.BlockSpec((tr, d), lambda i: (i, 0))
    vec = pl.BlockSpec((1, d), lambda i: (0, 0))
    return pl.pallas_call(body, grid=(t // tr,), in_specs=[row, row, vec], out_specs=(row, vec),
                          out_shape=(jax.ShapeDtypeStruct((t, d), BF16), jax.ShapeDtypeStruct((1, d), F32)),
                          name="scale_bwd", compiler_params=_params("arbitrary"))(dx, ypre, scale.reshape(1, d))


def _chunk_rows(t):
    return _pick(t, (256, 128))


def _load_halo(ref, lead, base, rows, t, first, last, pre, post):
    idx = (lambda s, n: (pl.ds(s, n), slice(None))) if lead is None else (lambda s, n: (lead, pl.ds(s, n), slice(None)))
    parts = []
    if pre:
        start = pl.multiple_of(jnp.maximum(base - pre, 0), pre)
        parts.append(ref[idx(start, pre)].astype(F32) * jnp.where(first, 0.0, 1.0))
    parts.append(ref[idx(base, rows)].astype(F32))
    if post:
        start = pl.multiple_of(jnp.minimum(base + rows, t - post), post)
        parts.append(ref[idx(start, post)].astype(F32) * jnp.where(last, 0.0, 1.0))
    return parts[0] if len(parts) == 1 else jnp.concatenate(parts, axis=0)


def _fold8(x):
    r, c = x.shape
    return x.reshape(r // 8, 8, c).sum(axis=0)


def ffn_gate_fwd(u2, w3, b2):
    _, t, f = u2.shape
    tc = _pick(f, (256, 128))
    rows = _chunk_rows(t)
    nch = t // rows
    halo = 16

    def body(u_ref, w_ref, b_ref, a_ref, at_ref):
        def conv(p, base, first):
            xs = _load_halo(u_ref, p, base, rows, t, first, False, halo, 0)
            wp = w_ref[p]
            return (wp[0:1] * pltpu.roll(xs, 2, 0)[halo:] + wp[1:2] * pltpu.roll(xs, 1, 0)[halo:]
                    + wp[2:3] * xs[halo:] + b_ref[p])

        def chunk(i, carry):
            base = pl.multiple_of(i * rows, rows)
            gate, val = conv(0, base, i == 0), conv(1, base, i == 0)
            act = gate * _sigmoid(gate) * val
            act = act.astype(a_ref.dtype)
            a_ref[pl.ds(base, rows), :] = act
            at_ref[:, pl.ds(base, rows)] = act.T
            return carry

        lax.fori_loop(0, nch, chunk, 0)

    return pl.pallas_call(
        body, grid=(f // tc,),
        in_specs=[pl.BlockSpec((2, t, tc), lambda j: (0, 0, j)), pl.BlockSpec((2, 3, tc), lambda j: (0, 0, j)),
                  pl.BlockSpec((2, 1, tc), lambda j: (0, 0, j))],
        out_specs=(pl.BlockSpec((t, tc), lambda j: (0, j)), pl.BlockSpec((tc, t), lambda j: (j, 0))),
        out_shape=(jax.ShapeDtypeStruct((t, f), BF16), jax.ShapeDtypeStruct((f, t), BF16)),
        name="ffn_gate_fwd", compiler_params=_params("parallel"))(u2, w3, b2)


def ffn_gate_bwd(u2, da, w3, b2):
    _, t, f = u2.shape
    tc = _pick(f, (256, 128))
    rows = _chunk_rows(t)
    nch = t // rows
    halo = 16
    n = rows + 2 * halo

    def body(u_ref, da_ref, w_ref, b_ref, du_ref, dw_ref, db_ref, acc_ref):
        acc_ref[...] = jnp.zeros_like(acc_ref)

        def chunk(i, carry):
            base = pl.multiple_of(i * rows, rows)
            first, last = i == 0, i == nch - 1
            daf = jnp.concatenate(
                [jnp.zeros((halo, tc), F32), _load_halo(da_ref, None, base, rows, t, first, last, 0, halo)], axis=0)
            pre, shifted = [], []
            for p in range(2):
                xs = _load_halo(u_ref, p, base, rows, t, first, last, halo, halo)
                x1, x2 = pltpu.roll(xs, 1, 0), pltpu.roll(xs, 2, 0)
                wp = w_ref[p]
                pre.append(wp[0:1] * x2 + wp[1:2] * x1 + wp[2:3] * xs + b_ref[p])
                shifted.append((x2, x1, xs))
            gate, val = pre
            sg = _sigmoid(gate)
            d_pre = (daf * val * (sg * (1.0 + gate * (1.0 - sg))), daf * gate * sg)
            for p in range(2):
                dp = d_pre[p]
                wp = w_ref[p]
                du = wp[2:3] * dp + wp[1:2] * pltpu.roll(dp, n - 1, 0) + wp[0:1] * pltpu.roll(dp, n - 2, 0)
                du_ref[p, pl.ds(base, rows), :] = du[halo:halo + rows].astype(du_ref.dtype)
                own = dp[halo:halo + rows]
                for k in range(3):
                    acc_ref[p, k] += _fold8(own * shifted[p][k][halo:halo + rows])
                acc_ref[p, 3] += _fold8(own)
            return carry

        lax.fori_loop(0, nch, chunk, 0)
        for p in range(2):
            for k in range(3):
                dw_ref[p, k:k + 1, :] = jnp.sum(acc_ref[p, k], axis=0, keepdims=True)
            db_ref[p] = jnp.sum(acc_ref[p, 3], axis=0, keepdims=True)

    blk = pl.BlockSpec((2, t, tc), lambda j: (0, 0, j))
    wspec = pl.BlockSpec((2, 3, tc), lambda j: (0, 0, j))
    bspec = pl.BlockSpec((2, 1, tc), lambda j: (0, 0, j))
    return pl.pallas_call(
        body, grid=(f // tc,), in_specs=[blk, pl.BlockSpec((t, tc), lambda j: (0, j)), wspec, bspec],
        out_specs=(blk, wspec, bspec),
        out_shape=(jax.ShapeDtypeStruct((2, t, f), BF16), jax.ShapeDtypeStruct((2, 3, f), F32),
                   jax.ShapeDtypeStruct((2, 1, f), F32)),
        scratch_shapes=[pltpu.VMEM((2, 4, 8, tc), F32)], name="ffn_gate_bwd",
        compiler_params=_params("parallel"))(u2, da, w3, b2)


def glu_conv_fwd(u2, w, b):
    _, t, d = u2.shape
    taps = w.shape[0]
    tc = 128
    rows = _chunk_rows(t)
    nch = t // rows
    halo = 32

    def body(u_ref, w_ref, b_ref, c_ref):
        def chunk(i, carry):
            base = pl.multiple_of(i * rows, rows)
            a = _load_halo(u_ref, 0, base, rows, t, i == 0, False, halo, 0)
            g = _load_halo(u_ref, 1, base, rows, t, i == 0, False, halo, 0)
            xs = a * _sigmoid(g)
            acc = w_ref[taps - 1:taps, :] * xs[halo:] + b_ref[...]
            for j in range(taps - 1):
                acc = acc + w_ref[j:j + 1, :] * pltpu.roll(xs, taps - 1 - j, 0)[halo:]
            c_ref[pl.ds(base, rows), :] = acc
            return carry

        lax.fori_loop(0, nch, chunk, 0)

    return pl.pallas_call(
        body, grid=(d // tc,),
        in_specs=[pl.BlockSpec((2, t, tc), lambda j: (0, 0, j)), pl.BlockSpec((taps, tc), lambda j: (0, j)),
                  pl.BlockSpec((1, tc), lambda j: (0, j))],
        out_specs=pl.BlockSpec((t, tc), lambda j: (0, j)), out_shape=jax.ShapeDtypeStruct((t, d), F32),
        name="glu_conv_fwd", compiler_params=_params("parallel"))(u2, w, b)


def glu_conv_bwd(u2, dc, w):
    _, t, d = u2.shape
    taps = w.shape[0]
    tc = 128
    rows = _chunk_rows(t)
    nch = t // rows
    halo = 32
    n = rows + halo

    def body(u_ref, dc_ref, w_ref, du_ref, dw_ref, dwb_ref, dbin_ref, acc_ref, bacc_ref):
        acc_ref[...] = jnp.zeros_like(acc_ref)
        bacc_ref[...] = jnp.zeros_like(bacc_ref)

        def chunk(i, carry):
            base = pl.multiple_of(i * rows, rows)
            first, last = i == 0, i == nch - 1
            a = _load_halo(u_ref, 0, base, rows, t, first, False, halo, 0)
            g = _load_halo(u_ref, 1, base, rows, t, first, False, halo, 0)
            sg = _sigmoid(g)
            xs = a * sg
            dcs = _load_halo(dc_ref, None, base, rows, t, first, last, 0, halo)
            own = dcs[:rows]
            dglu = w_ref[taps - 1:taps, :] * own
            acc_ref[taps - 1] += _fold8(own * xs[halo:])
            for j in range(taps - 1):
                s = taps - 1 - j
                dglu = dglu + w_ref[j:j + 1, :] * pltpu.roll(dcs, n - s, 0)[:rows]
                acc_ref[j] += _fold8(own * pltpu.roll(xs, s, 0)[halo:])
            a_c, sg_c = a[halo:], sg[halo:]
            da = dglu * sg_c
            dg = dglu * a_c * sg_c * (1.0 - sg_c)
            du_ref[0, pl.ds(base, rows), :] = da.astype(du_ref.dtype)
            du_ref[1, pl.ds(base, rows), :] = dg.astype(du_ref.dtype)
            bacc_ref[0] += _fold8(own)
            bacc_ref[1] += _fold8(da)
            bacc_ref[2] += _fold8(dg)
            return carry

        lax.fori_loop(0, nch, chunk, 0)
        for j in range(taps):
            dw_ref[j:j + 1, :] = jnp.sum(acc_ref[j], axis=0, keepdims=True)
        dwb_ref[...] = jnp.sum(bacc_ref[0], axis=0, keepdims=True)
        dbin_ref[0] = jnp.sum(bacc_ref[1], axis=0, keepdims=True)
        dbin_ref[1] = jnp.sum(bacc_ref[2], axis=0, keepdims=True)

    blk = pl.BlockSpec((2, t, tc), lambda j: (0, 0, j))
    col = pl.BlockSpec((t, tc), lambda j: (0, j))
    return pl.pallas_call(
        body, grid=(d // tc,), in_specs=[blk, col, pl.BlockSpec((taps, tc), lambda j: (0, j))],
        out_specs=(blk, pl.BlockSpec((taps, tc), lambda j: (0, j)), pl.BlockSpec((1, tc), lambda j: (0, j)),
                   pl.BlockSpec((2, 1, tc), lambda j: (0, 0, j))),
        out_shape=(jax.ShapeDtypeStruct((2, t, d), BF16), jax.ShapeDtypeStruct((taps, d), F32),
                   jax.ShapeDtypeStruct((1, d), F32), jax.ShapeDtypeStruct((2, 1, d), F32)),
        scratch_shapes=[pltpu.VMEM((taps, 8, tc), F32), pltpu.VMEM((3, 8, tc), F32)], name="glu_conv_bwd",
        compiler_params=_params("parallel"))(u2, dc, w)


def _pool_select(grp, levels):
    out = levels[3]
    for k in (2, 1, 0):
        out = jnp.where(grp == k, levels[k], out)
    return out


def _pool_count(base, rows, tc, grp):
    tpos = (base + lax.broadcasted_iota(jnp.int32, (rows, tc), 0) + 1).astype(F32)
    window = jnp.left_shift(2, grp).astype(F32)
    return jnp.minimum(tpos, window)


def pool_fwd(h):
    t, d = h.shape
    pg = d // POOL_GROUPS
    tc = _pick(pg, (256, 128))
    rows = _chunk_rows(t)
    nch = t // rows
    halo = 16

    def body(h_ref, o_ref, ot_ref):
        grp = (pl.program_id(0) * tc) // pg

        def chunk(i, carry):
            base = pl.multiple_of(i * rows, rows)
            xs = _load_halo(h_ref, None, base, rows, t, i == 0, False, halo, 0)
            levels, cur = [], xs
            for k in range(4):
                cur = cur + pltpu.roll(cur, 1 << k, 0)
                levels.append(cur[halo:])
            pooled = _pool_select(grp, levels) / _pool_count(base, rows, tc, grp)
            mixed = pooled - xs[halo:]
            o_ref[pl.ds(base, rows), :] = mixed.astype(o_ref.dtype)
            ot_ref[:, pl.ds(base, rows)] = mixed.T.astype(ot_ref.dtype)
            return carry

        lax.fori_loop(0, nch, chunk, 0)

    col = pl.BlockSpec((t, tc), lambda j: (0, j))
    return pl.pallas_call(body, grid=(d // tc,), in_specs=[col], out_specs=(col, pl.BlockSpec((tc, t), lambda j: (j, 0))),
                          out_shape=(jax.ShapeDtypeStruct((t, d), BF16), jax.ShapeDtypeStruct((d, t), BF16)),
                          name="pool_fwd", compiler_params=_params("parallel"))(h)


def pool_bwd(dmix):
    t, d = dmix.shape
    pg = d // POOL_GROUPS
    tc = _pick(pg, (256, 128))
    rows = _chunk_rows(t)
    nch = t // rows
    halo = 16
    n = rows + halo

    def body(d_ref, o_ref):
        grp = (pl.program_id(0) * tc) // pg

        def chunk(i, carry):
            base = pl.multiple_of(i * rows, rows)
            ds = _load_halo(d_ref, None, base, rows, t, i == 0, i == nch - 1, 0, halo)
            levels, cur = [], ds / _pool_count(base, n, tc, grp)
            for k in range(4):
                cur = cur + pltpu.roll(cur, n - (1 << k), 0)
                levels.append(cur[:rows])
            o_ref[pl.ds(base, rows), :] = _pool_select(grp, levels) - ds[:rows]
            return carry

        lax.fori_loop(0, nch, chunk, 0)

    col = pl.BlockSpec((t, tc), lambda j: (0, j))
    return pl.pallas_call(body, grid=(d // tc,), in_specs=[col], out_specs=col,
                          out_shape=jax.ShapeDtypeStruct((t, d), F32), name="pool_bwd",
                          compiler_params=_params("parallel"))(dmix)


def mm_groups(a, wg, *, mode, out_dtype, scale=None, resid=None, raw_dtype=None, name):
    t, d = a.shape
    pg = wg.shape[1]
    tm = _pick(t, (1024, 512, 256, 128))
    tn = pg
    if mode == "nn":
        dims, b_blk, b_idx = NN, (None, pg, tn), (lambda i, j, kk: (j, 0, 0))
    else:
        dims, b_blk, b_idx = NT, (None, tn, pg), (lambda i, j, kk: (j, 0, 0))
    raw_shape = None if raw_dtype is None else jax.ShapeDtypeStruct((t, d), raw_dtype)
    return _mm(a, wg, dims=dims, sizes=(t, d, pg), plan=(tm, tn, pg, False), a_blk=(tm, pg), a_idx=lambda i, j, kk: (i, j),
               b_blk=b_blk, b_idx=b_idx, o_blk=(tm, tn), o_idx=lambda i, j, kk: (i, j),
               out_shape=jax.ShapeDtypeStruct((t, d), out_dtype), name=name,
               scale=None if scale is None else scale.reshape(1, d), vec_blk=(1, tn), vec_idx=lambda i, j, kk: (0, j),
               resid=resid, raw_shape=raw_shape)


def mm_groups_wgrad(at, dy, groups, *, out_dtype, name):
    d, t = at.shape
    pg = d // groups
    tk = _pick(t, (2048, 1024, 512, 256, 128))
    return _mm(at, dy, dims=NN, sizes=(d, pg, t), plan=(pg, pg, tk, False), a_blk=(pg, tk), a_idx=lambda i, j, kk: (i, kk),
               b_blk=(tk, pg), b_idx=lambda i, j, kk: (kk, i), o_blk=(None, pg, pg), o_idx=lambda i, j, kk: (i, 0, 0),
               out_shape=jax.ShapeDtypeStruct((groups, pg, pg), out_dtype), name=name)


def _split_dot(y, p):
    hi = y.astype(BF16)
    r1 = y - hi.astype(F32)
    mid = r1.astype(BF16)
    lo = (r1 - mid.astype(F32)).astype(BF16)
    pb = p.astype(BF16)
    dot = lambda v: jnp.dot(v, pb, preferred_element_type=F32)
    return (dot(hi) + dot(mid)) + dot(lo)


def rope_tables(positions):
    half = ROT_DIM // 2
    inv_freq = ROPE_THETA ** (-jnp.arange(0, ROT_DIM, 2, dtype=F32) / ROT_DIM)
    ang = positions.astype(F32)[:, None] * inv_freq
    t = positions.shape[0]
    cos, sin = jnp.cos(ang), jnp.sin(ang)
    rest = HEAD_DIM - ROT_DIM
    cosf = jnp.concatenate([cos, cos, jnp.ones((t, rest), F32)], axis=1)
    sinf = jnp.concatenate([-sin, sin, jnp.zeros((t, rest), F32)], axis=1)
    idx = jnp.arange(HEAD_DIM)
    partner = jnp.where(idx < half, idx + half, jnp.where(idx < ROT_DIM, idx - half, idx))
    pmat = (idx[:, None] == partner[None, :]).astype(F32)
    return cosf, sinf, pmat


def qk_rope_fwd(x, g, cosf, sinf, pmat, out_scale):
    hn, t, hd = x.shape
    tq = _pick(t, (4096, 2048, 1024, 512, 256, 128))

    def body(x_ref, g_ref, c_ref, s_ref, p_ref, o_ref):
        xf = x_ref[...]
        r = lax.rsqrt(jnp.mean(xf * xf, axis=-1, keepdims=True) + EPS)
        y = xf * r * g_ref[...]
        rot = y * c_ref[...] + _split_dot(y, p_ref[...]) * s_ref[...]
        o_ref[...] = (rot * out_scale).astype(o_ref.dtype)

    blk = pl.BlockSpec((None, tq, hd), lambda h, i: (h, i, 0))
    tab = pl.BlockSpec((tq, hd), lambda h, i: (i, 0))
    return pl.pallas_call(
        body, grid=(hn, t // tq),
        in_specs=[blk, pl.BlockSpec((1, hd), lambda h, i: (0, 0)), tab, tab, pl.BlockSpec((hd, hd), lambda h, i: (0, 0))],
        out_specs=blk, out_shape=jax.ShapeDtypeStruct((hn, t, hd), BF16), name="qk_rope_fwd",
        compiler_params=_params("parallel", "parallel"))(x, g.reshape(1, hd), cosf, sinf, pmat)


def qk_rope_bwd(dy, x, g, cosf, sinf, pmat_t, in_scale):
    hn, t, hd = x.shape
    tq = _pick(t, (4096, 2048, 1024, 512, 256, 128))

    def body(dy_ref, x_ref, g_ref, c_ref, s_ref, p_ref, dx_ref, dg_ref):
        step = pl.program_id(0) * pl.num_programs(1) + pl.program_id(1)
        dr = dy_ref[...] * in_scale
        dyn = dr * c_ref[...] + _split_dot(dr * s_ref[...], p_ref[...])
        xf = x_ref[...]
        r = lax.rsqrt(jnp.mean(xf * xf, axis=-1, keepdims=True) + EPS)
        xhat = xf * r
        dxh = dyn * g_ref[...]
        m = jnp.mean(dxh * xhat, axis=-1, keepdims=True)
        dx_ref[...] = r * (dxh - xhat * m)
        _acc_rows(dg_ref, jnp.sum(dyn * xhat, axis=0, keepdims=True), step)

    blk = pl.BlockSpec((None, tq, hd), lambda h, i: (h, i, 0))
    tab = pl.BlockSpec((tq, hd), lambda h, i: (i, 0))
    vec = pl.BlockSpec((1, hd), lambda h, i: (0, 0))
    return pl.pallas_call(
        body, grid=(hn, t // tq),
        in_specs=[blk, blk, vec, tab, tab, pl.BlockSpec((hd, hd), lambda h, i: (0, 0))],
        out_specs=(blk, vec), out_shape=(jax.ShapeDtypeStruct((hn, t, hd), F32), jax.ShapeDtypeStruct((1, hd), F32)),
        name="qk_rope_bwd", compiler_params=_params("arbitrary", "arbitrary"))(dy, x, g.reshape(1, hd), cosf, sinf, pmat_t)


NEG_BIG = -1e30


ATT_ROWS = KV_GROUP * ATT_BLOCK


def _att_sinks(sink_ref, kv):
    return jnp.concatenate([jnp.full((ATT_BLOCK, 1), sink_ref[kv * KV_GROUP + g], F32) for g in range(KV_GROUP)], axis=0)


def _att_mask(i, rows):
    shape = (rows, 2 * ATT_BLOCK)
    qi = jnp.bitwise_and(lax.broadcasted_iota(jnp.int32, shape, 0), ATT_BLOCK - 1)
    kj = lax.broadcasted_iota(jnp.int32, shape, 1)
    cur = jnp.logical_and(kj >= ATT_BLOCK, kj - ATT_BLOCK <= qi)
    prev = jnp.logical_and(jnp.logical_and(kj < ATT_BLOCK, kj > qi), i > 0)
    return jnp.logical_or(cur, prev)


def _att_probs(q, k2, mask, sink):
    s = jnp.where(mask, lax.dot_general(q, k2, NT, preferred_element_type=F32), NEG_BIG)
    m = jnp.maximum(jnp.max(s, axis=-1, keepdims=True), sink)
    p = jnp.exp(s - m)
    p_s = jnp.exp(sink - m)
    return p, p_s, jnp.sum(p, axis=-1, keepdims=True) + p_s


def _att_specs(t):
    nb = t // ATT_BLOCK
    qblk = pl.BlockSpec((KV_GROUP, ATT_BLOCK, HEAD_DIM), lambda kv, i: (kv, i, 0))
    cur = pl.BlockSpec((None, ATT_BLOCK, HEAD_DIM), lambda kv, i: (kv, i, 0))
    prev = pl.BlockSpec((None, ATT_BLOCK, HEAD_DIM), lambda kv, i: (kv, jnp.maximum(i - 1, 0), 0))
    return nb, qblk, cur, prev, pl.BlockSpec(memory_space=pltpu.SMEM)


def attn_fwd(q, k, v, sinks):
    h, t, hd = q.shape
    nb, qblk, cur, prev, smem = _att_specs(t)

    def body(q_ref, kc_ref, kp_ref, vc_ref, vp_ref, sink_ref, o_ref):
        kv, i = pl.program_id(0), pl.program_id(1)
        k2 = jnp.concatenate([kp_ref[...], kc_ref[...]], axis=0)
        v2 = jnp.concatenate([vp_ref[...], vc_ref[...]], axis=0)
        mask = _att_mask(i, ATT_BLOCK)
        for g in range(KV_GROUP):
            p, _, denom = _att_probs(q_ref[g], k2, mask, sink_ref[kv * KV_GROUP + g])
            o_ref[g] = (jnp.dot(p.astype(BF16), v2, preferred_element_type=F32) / denom).astype(o_ref.dtype)

    return pl.pallas_call(
        body, grid=(h // KV_GROUP, nb), in_specs=[qblk, cur, prev, cur, prev, smem], out_specs=qblk,
        out_shape=jax.ShapeDtypeStruct((h, t, hd), BF16), name="attn_fwd",
        compiler_params=_params("parallel", "parallel"))(q, k, k, v, v, sinks)


def attn_bwd(q, k, v, do, sinks):
    h, t, hd = q.shape
    kvh = h // KV_GROUP
    nb, qblk, cur, prev, smem = _att_specs(t)

    def body(q_ref, kc_ref, kp_ref, vc_ref, vp_ref, do_ref, sink_ref, dq_ref, dk_ref, dv_ref, dsk_ref):
        kv, i = pl.program_id(0), pl.program_id(1)

        @pl.when(i == 0)
        def _():
            dk_ref[...] = jnp.zeros_like(dk_ref)
            dv_ref[...] = jnp.zeros_like(dv_ref)
            dsk_ref[...] = jnp.zeros_like(dsk_ref)

        k2 = jnp.concatenate([kp_ref[...], kc_ref[...]], axis=0)
        v2 = jnp.concatenate([vp_ref[...], vc_ref[...]], axis=0)
        q = q_ref[...].reshape(ATT_ROWS, hd)
        p, p_s, denom = _att_probs(q, k2, _att_mask(i, ATT_ROWS), _att_sinks(sink_ref, kv))
        inv = 1.0 / denom
        pn = p * inv
        dob = do_ref[...].reshape(ATT_ROWS, hd).astype(BF16)
        dp = lax.dot_general(dob, v2, NT, preferred_element_type=F32)
        dsum = jnp.sum(pn * dp, axis=-1, keepdims=True)
        ds = (pn * (dp - dsum)).astype(BF16)
        dq_ref[...] = jnp.dot(ds, k2, preferred_element_type=F32).reshape(KV_GROUP, ATT_BLOCK, hd)
        dk2 = lax.dot_general(ds, q, TN, preferred_element_type=F32)
        dv2 = lax.dot_general(pn.astype(BF16), dob, TN, preferred_element_type=F32)
        dsink = p_s * inv * dsum
        dsink_rows = [jnp.broadcast_to(-jnp.sum(dsink[g * ATT_BLOCK:(g + 1) * ATT_BLOCK], axis=0, keepdims=True), (1, 128))
                      for g in range(KV_GROUP)]
        here = pl.ds(pl.multiple_of(i * ATT_BLOCK, ATT_BLOCK), ATT_BLOCK)
        before = pl.ds(pl.multiple_of(jnp.maximum(i - 1, 0) * ATT_BLOCK, ATT_BLOCK), ATT_BLOCK)
        dk_ref[before, :] += dk2[:ATT_BLOCK]
        dv_ref[before, :] += dv2[:ATT_BLOCK]
        dk_ref[here, :] += dk2[ATT_BLOCK:]
        dv_ref[here, :] += dv2[ATT_BLOCK:]
        dsk_ref[...] += jnp.concatenate(dsink_rows, axis=0)

    whole = pl.BlockSpec((None, t, hd), lambda kv, i: (kv, 0, 0))
    return pl.pallas_call(
        body, grid=(kvh, nb), in_specs=[qblk, cur, prev, cur, prev, qblk, smem],
        out_specs=(qblk, whole, whole, pl.BlockSpec((None, KV_GROUP, 128), lambda kv, i: (kv, 0, 0))),
        out_shape=(jax.ShapeDtypeStruct((h, t, hd), F32), jax.ShapeDtypeStruct((kvh, t, hd), F32),
                   jax.ShapeDtypeStruct((kvh, t, hd), F32), jax.ShapeDtypeStruct((kvh, KV_GROUP, 128), F32)),
        name="attn_bwd", compiler_params=_params("parallel", "arbitrary"))(q, k, k, v, v, do, sinks)


def _ffn_taps(w, b):
    f2 = w.shape[1]
    return w.reshape(FFN_TAPS, 2, f2 // 2).transpose(1, 0, 2), b.reshape(2, 1, f2 // 2)


def ffn_fwd(x, W, p, tables=None, token=None):
    h, ht = rms_fwd(x, W[p + "ffn_norm_g"], BF16, token, transposed=True)
    u2 = mm_nn_cols(h, W[p + "ffn_w_up"], split=True, out_dtype=BF16, name="ffn_up")
    w3, b2 = _ffn_taps(W[p + "ffn_dw_w"], W[p + "ffn_dw_b"])
    a, at = ffn_gate_fwd(u2, w3, b2)
    y = mm_nn(a, W[p + "ffn_w_down"], out_dtype=F32, resid=x, name="ffn_down")
    return y, (x, ht, u2, at)


def ffn_bwd(saved, W, p, dy, tables=None):
    x, ht, u2, at = saved
    dyf, dyb = dy
    w3, b2 = _ffn_taps(W[p + "ffn_dw_w"], W[p + "ffn_dw_b"])
    grads = {p + "ffn_w_down": mm_wgrad(at, dyb, out_dtype=BF16, name="ffn_down_dw")}
    da = mm_nt(dyb, W[p + "ffn_w_down"], out_dtype=BF16, name="ffn_down_dx")
    du2, dw3, db2 = ffn_gate_bwd(u2, da, w3, b2)
    grads[p + "ffn_dw_w"] = dw3.transpose(1, 0, 2).reshape(FFN_TAPS, -1)
    grads[p + "ffn_dw_b"] = db2.reshape(-1)
    grads[p + "ffn_w_up"] = mm_wgrad_cols(ht, du2, split=True, out_dtype=BF16, name="ffn_up_dw")
    dh = mm_nt_cols(du2, W[p + "ffn_w_up"], split=True, out_dtype=F32, name="ffn_up_dx")
    return (x, p + "ffn_norm_g", dh), grads


def conf_fwd(x, W, p, tables=None, token=None):
    d = x.shape[1]
    h, ht = rms_fwd(x, W[p + "norm_g"], BF16, token, transposed=True)
    u2 = mm_nn_cols(h, W[p + "a_w_in"], split=True, out_dtype=BF16, bias=W[p + "a_b_in"], name="conf_in")
    c = glu_conv_fwd(u2, W[p + "a_dw_w"], W[p + "a_dw_b"].reshape(1, d))
    s, st = ln_silu_fwd(c, W[p + "a_ln_g"], W[p + "a_ln_b"])
    y = mm_nn(s, W[p + "a_w_out"], out_dtype=F32, bias=W[p + "a_b_out"], resid=x, name="conf_out")
    return y, (x, ht, u2, c, st)


def conf_bwd(saved, W, p, dy, tables=None, midway=None):
    x, ht, u2, c, st = saved
    dyf, dyb = dy
    grads = {p + "a_w_out": mm_wgrad(st, dyb, out_dtype=BF16, name="conf_out_dw"), p + "a_b_out": col_sum(dyf).reshape(-1)}
    ds = mm_nt(dyb, W[p + "a_w_out"], out_dtype=BF16, name="conf_out_dx")
    dc, dlg, dlb = ln_silu_bwd(c, W[p + "a_ln_g"], W[p + "a_ln_b"], ds, midway(ds) if midway else None)
    grads[p + "a_ln_g"], grads[p + "a_ln_b"] = dlg.reshape(-1), dlb.reshape(-1)
    du2, ddw, ddwb, dbin = glu_conv_bwd(u2, dc, W[p + "a_dw_w"])
    grads[p + "a_dw_w"], grads[p + "a_dw_b"], grads[p + "a_b_in"] = ddw, ddwb.reshape(-1), dbin.reshape(-1)
    grads[p + "a_w_in"] = mm_wgrad_cols(ht, du2, split=True, out_dtype=BF16, name="conf_in_dw")
    dh = mm_nt_cols(du2, W[p + "a_w_in"], split=True, out_dtype=F32, name="conf_in_dx")
    return (x, p + "norm_g", dh), grads


def pool_layer_fwd(x, W, p, tables=None, token=None):
    h = rms_fwd(x, W[p + "norm_g"], F32, token)
    mixed, mixed_t = pool_fwd(h)
    y, ypre = mm_groups(mixed, W[p + "b_w_group"], mode="nn", out_dtype=F32, scale=W[p + "b_scale"], resid=x,
                        raw_dtype=F32, name="pool_mix")
    return y, (x, mixed_t, ypre)


def pool_layer_bwd(saved, W, p, dy, tables=None):
    x, mixed_t, ypre = saved
    dyf, dyb = dy
    dyp, dscale = scale_bwd(dyf, ypre, W[p + "b_scale"])
    grads = {p + "b_scale": dscale.reshape(-1),
             p + "b_w_group": mm_groups_wgrad(mixed_t, dyp, POOL_GROUPS, out_dtype=BF16, name="pool_mix_dw")}
    dmix = mm_groups(dyp, W[p + "b_w_group"], mode="nt", out_dtype=F32, name="pool_mix_dx")
    dh = pool_bwd(dmix)
    return (x, p + "norm_g", dh), grads


def _heads(a, n):
    t = a.shape[0]
    return a.reshape(t, n, HEAD_DIM).transpose(1, 0, 2)


def _unheads(a):
    n, t, _ = a.shape
    return a.transpose(1, 0, 2).reshape(t, n * HEAD_DIM)


def attn_layer_fwd(x, W, p, tables, token=None):
    d = x.shape[1]
    nh = d // HEAD_DIM
    nkv = nh // KV_GROUP
    cosf, sinf, pmat = tables
    h, ht = rms_fwd(x, W[p + "norm_g"], BF16, token, transposed=True)
    qkv = mm_nn_cols(h, W[p + "c_w_qkv"], split=False, out_dtype=F32, name="att_qkv")
    q = _heads(qkv[:, :d], nh)
    k = _heads(qkv[:, d:d + nkv * HEAD_DIM], nkv)
    v = _heads(qkv[:, d + nkv * HEAD_DIM:], nkv).astype(BF16)
    qr = qk_rope_fwd(q, W[p + "c_q_norm_g"], cosf, sinf, pmat, HEAD_DIM ** -0.5)
    kr = qk_rope_fwd(k, W[p + "c_k_norm_g"], cosf, sinf, pmat, 1.0)
    o = attn_fwd(qr, kr, v, W[p + "c_sinks"])
    o2 = _unheads(o)
    y = mm_nn(o2, W[p + "c_w_o"], out_dtype=F32, resid=x, name="att_out")
    return y, (x, ht, q, k, v, qr, kr, o2)


def attn_layer_bwd(saved, W, p, dy, tables):
    x, ht, q, k, v, qr, kr, o2 = saved
    dyf, dyb = dy
    cosf, sinf, pmat = tables
    nh = q.shape[0]
    grads = {p + "c_w_o": mm_wgrad(o2.T, dyb, out_dtype=BF16, name="att_out_dw")}
    do = _heads(mm_nt(dyb, W[p + "c_w_o"], out_dtype=BF16, name="att_out_dx"), nh)
    dqr, dkr, dv, dsk = attn_bwd(qr, kr, v, do, W[p + "c_sinks"])
    grads[p + "c_sinks"] = dsk[:, :, 0].reshape(-1)
    dq, dqg = qk_rope_bwd(dqr, q, W[p + "c_q_norm_g"], cosf, sinf, pmat.T, HEAD_DIM ** -0.5)
    dk, dkg = qk_rope_bwd(dkr, k, W[p + "c_k_norm_g"], cosf, sinf, pmat.T, 1.0)
    grads[p + "c_q_norm_g"], grads[p + "c_k_norm_g"] = dqg.reshape(-1), dkg.reshape(-1)
    dqkv = jnp.concatenate([_unheads(dq), _unheads(dk), _unheads(dv)], axis=1).astype(BF16)
    grads[p + "c_w_qkv"] = mm_wgrad_cols(ht, dqkv, split=False, out_dtype=BF16, name="att_qkv_dw")
    dh = mm_nt_cols(dqkv, W[p + "c_w_qkv"], split=False, out_dtype=F32, name="att_qkv_dx")
    return (x, p + "norm_g", dh), grads


def local_step(x, positions, tgt, W, comm=None):
    tables = rope_tables(positions)
    saved = []
    for g, (fwd, _, p) in enumerate(SUBLAYERS):
        token = comm.forward_begins(g, W) if comm else None
        x, s = fwd(x, W, p, tables, token)
        saved.append(s)
        if comm:
            comm.forward_ends(g, x, W)
    dyf, dyb, sq = loss_grad(x, tgt)
    loss = 0.5 * jnp.sum(sq) / x.shape[1]
    grads = {}
    for g in reversed(range(len(SUBLAYERS))):
        _, bwd, p = SUBLAYERS[g]
        if comm and g == 0:
            (xin, gain, dh), gr = bwd(saved[g], W, p, (dyf, dyb), tables, comm.midway)
        else:
            (xin, gain, dh), gr = bwd(saved[g], W, p, (dyf, dyb), tables)
        token = comm.gradients_ready(g, gr) if comm else None
        dyf, dyb, dg = rms_bwd(xin, W[gain], dh, dyf, token)
        gr[gain] = dg.reshape(-1)
        grads.update(gr)
    return loss, dyf, grads


SUBLAYERS = [(conf_fwd, conf_bwd, "l0_"), (ffn_fwd, ffn_bwd, "l0_"), (pool_layer_fwd, pool_layer_bwd, "l1_"),
             (ffn_fwd, ffn_bwd, "l1_"), (attn_layer_fwd, attn_layer_bwd, "l2_"), (ffn_fwd, ffn_bwd, "l2_"),
             (conf_fwd, conf_bwd, "l3_"), (ffn_fwd, ffn_bwd, "l3_")]
SUBLAYER_WEIGHTS = {conf_fwd: ("a_w_in", "a_w_out", "a_dw_w"), ffn_fwd: ("ffn_w_up", "ffn_w_down", "ffn_dw_w"),
                    pool_layer_fwd: ("b_w_group",), attn_layer_fwd: ("c_w_qkv", "c_w_o")}


def sublayer_weight_names(g):
    fwd, _, p = SUBLAYERS[g]
    return [p + n for n in SUBLAYER_WEIGHTS[fwd]]


ANY = pl.BlockSpec(memory_space=pl.ANY)


def _place():
    x, y, c = lax.axis_index("x"), lax.axis_index("y"), lax.axis_index("c")
    chips = [(1 - x, y), (x, 1 - y), (1 - x, 1 - y)]
    return x, y, c, 2 * x + y, (x, y, 1 - c), chips


def _half(rows, which):
    return pl.ds(which * (rows // 2), rows // 2)


def place_block(shard, chip_core, out_dtype):
    rows, cols = shard.shape
    tr = rows
    for cand in (512, 256, 128, 64, 32, 16):
        if rows % cand == 0 and cand * cols * 4 <= (2 << 20):
            tr = cand
            break

    def body(pos_ref, s_ref, o_ref):
        o_ref[...] = s_ref[...].astype(o_ref.dtype)

    grid_spec = pltpu.PrefetchScalarGridSpec(
        num_scalar_prefetch=1, grid=(rows // tr,), in_specs=[pl.BlockSpec((tr, cols), lambda i, pos: (i, 0))],
        out_specs=pl.BlockSpec((None, tr, cols), lambda i, pos: (pos[0], i, 0)))
    return pl.pallas_call(body, grid_spec=grid_spec, out_shape=jax.ShapeDtypeStruct((N_CHIPS, rows, cols), out_dtype),
                          name="place_block", compiler_params=_params("parallel"))(chip_core, shard)


def all_gather_chips(bufs):
    n = len(bufs)

    def body(*refs):
        outs = refs[n:2 * n]
        ici_send, ici_recv, d2d_send, d2d_recv = refs[2 * n:]
        x, y, c, k, sibling, chips = _place()

        def rdma(src, dst, send, recv, dev):
            return pltpu.make_async_remote_copy(src_ref=src, dst_ref=dst, send_sem=send, recv_sem=recv,
                                                device_id=dev, device_id_type=MESH)

        sends = []
        for t in range(n):
            rows = bufs[t].shape[1]
            for j, (px, py) in enumerate(chips):
                mine = outs[t].at[k, _half(rows, c)]
                sends.append(rdma(mine, mine, ici_send.at[t, j], ici_recv.at[t, j], (px, py, c)))
        for cp in sends:
            cp.start()
        for t in range(n):
            rows = bufs[t].shape[1]
            for j, (px, py) in enumerate(chips):
                landed = outs[t].at[2 * px + py, _half(rows, c)]
                rdma(landed, landed, ici_send.at[t, j], ici_recv.at[t, j], sibling).wait_recv()
                fwd = rdma(landed, landed, d2d_send.at[t, j], d2d_recv.at[t, j], sibling)
                fwd.start()
                sends.append(fwd)
        for t in range(n):
            rows = bufs[t].shape[1]
            for j, (px, py) in enumerate(chips):
                other = outs[t].at[2 * px + py, _half(rows, 1 - c)]
                rdma(other, other, d2d_send.at[t, j], d2d_recv.at[t, j], sibling).wait_recv()
        for cp in sends:
            cp.wait_send()

    return pl.pallas_call(
        body, in_specs=[ANY] * n, out_specs=[ANY] * n,
        out_shape=[jax.ShapeDtypeStruct(b.shape, b.dtype) for b in bufs],
        input_output_aliases={t: t for t in range(n)},
        scratch_shapes=[pltpu.SemaphoreType.DMA((n, 3))] * 4,
        name="all_gather_chips", compiler_params=pltpu.CompilerParams())(*bufs)


def _sum_rows_tile(rows):
    return _pick(rows, (256, 352, 128, 64, 32, 16))


def add_sibling_half(g, land, core):
    nb, half, cols = land.shape
    tr = _sum_rows_tile(half)
    nrb = half // tr

    def body(c_ref, g_ref, l_ref, o_ref):
        o_ref[...] = (g_ref[...].astype(F32) + l_ref[...].astype(F32)).astype(o_ref.dtype)

    spec = pl.BlockSpec((None, tr, cols), lambda b, i, c_ref: (b, i, 0))
    grid_spec = pltpu.PrefetchScalarGridSpec(
        num_scalar_prefetch=1, grid=(nb, nrb),
        in_specs=[pl.BlockSpec((None, tr, cols), lambda b, i, c_ref: (b, c_ref[1] * nrb + i, 0)), spec], out_specs=spec)
    return pl.pallas_call(body, grid_spec=grid_spec, out_shape=jax.ShapeDtypeStruct(land.shape, BF16),
                          name="add_sibling_half", compiler_params=_params("parallel", "parallel"))(core, g, land)


def sum_chip_blocks(p, l2, chip_core):
    nb, half, cols = l2.shape
    tr = _sum_rows_tile(half)
    nrb = half // tr

    def body(pos_ref, p_ref, l_ref, o_ref):
        acc = p_ref[...].astype(F32)
        for b in range(nb):
            acc = acc + l_ref[b].astype(F32)
        o_ref[...] = acc

    grid_spec = pltpu.PrefetchScalarGridSpec(
        num_scalar_prefetch=1, grid=(nrb,),
        in_specs=[pl.BlockSpec((None, tr, cols), lambda i, pos: (pos[0], i, 0)),
                  pl.BlockSpec((nb, tr, cols), lambda i, pos: (0, i, 0))],
        out_specs=pl.BlockSpec((tr, cols), lambda i, pos: (pos[1] * nrb + i, 0)))
    return pl.pallas_call(body, grid_spec=grid_spec, out_shape=jax.ShapeDtypeStruct((2 * half, cols), F32),
                          name="sum_chip_blocks", compiler_params=_params("parallel"))(chip_core, p, l2)


HBM_SPEC = pl.BlockSpec(memory_space=pltpu.HBM)
SEM_SPEC = pl.BlockSpec(memory_space=pltpu.SEMAPHORE)
SPLIT_EFFECT = pltpu.SideEffectType.DATAFLOW_SIDE_EFFECTING


def _in_hbm(v):
    return pltpu.with_memory_space_constraint(v, pltpu.HBM)


def _gather_ici_copies(bufs, refs, send, recv):
    x, y, c, k, sibling, chips = _place()
    cps = []
    for t in range(len(bufs)):
        rows = bufs[t].shape[1]
        for j, (px, py) in enumerate(chips):
            cps.append(pltpu.make_async_remote_copy(
                src_ref=refs[t].at[k, _half(rows, c)], dst_ref=refs[t].at[k, _half(rows, c)],
                send_sem=send.at[3 * t + j], recv_sem=recv.at[3 * t + j], device_id=(px, py, c), device_id_type=MESH))
    return cps


def gather_ici_start(bufs, after, name):
    n = len(bufs)

    def body(*refs):
        send, recv, token = refs[n + 1], refs[n + 2], refs[-1]
        for cp in _gather_ici_copies(bufs, refs[:n], send, recv):
            cp.start()
        token[...] = jnp.zeros_like(token)

    outs = pl.pallas_call(
        body, name=name, in_specs=[HBM_SPEC] * n + [ANY],
        out_shape=(pltpu.SemaphoreType.DMA((3 * n,)), pltpu.SemaphoreType.DMA((3 * n,)),
                   *[pltpu.HBM(b.shape, b.dtype) for b in bufs], jax.ShapeDtypeStruct(TOKEN_SHAPE, F32)),
        out_specs=(SEM_SPEC, SEM_SPEC, *[HBM_SPEC] * n, pl.BlockSpec(memory_space=pltpu.VMEM)),
        input_output_aliases={t: 2 + t for t in range(n)},
        compiler_params=pltpu.CompilerParams(has_side_effects=SPLIT_EFFECT))(*[_in_hbm(b) for b in bufs], after)
    return outs[0], outs[1], list(outs[2:2 + n]), outs[-1]


def gather_ici_wait(send, recv, bufs, after, name):
    n = len(bufs)

    def body(*refs):
        x, y, c, k, sibling, chips = _place()
        for t in range(n):
            rows = bufs[t].shape[1]
            for j, (px, py) in enumerate(chips):
                cp = pltpu.make_async_remote_copy(
                    src_ref=refs[t].at[k, _half(rows, c)], dst_ref=refs[t].at[2 * px + py, _half(rows, c)],
                    send_sem=refs[n].at[3 * t + j], recv_sem=refs[n + 1].at[3 * t + j], device_id=(px, py, c),
                    device_id_type=MESH)
                cp.wait_send()
                cp.wait_recv()

    return list(pl.pallas_call(
        body, name=name, in_specs=[HBM_SPEC] * n + [SEM_SPEC, SEM_SPEC, ANY],
        out_shape=tuple(pltpu.HBM(b.shape, b.dtype) for b in bufs), out_specs=tuple([HBM_SPEC] * n),
        input_output_aliases={t: t for t in range(n)},
        compiler_params=pltpu.CompilerParams(has_side_effects=SPLIT_EFFECT))(*bufs, send, recv, after))


def gather_forward_sibling(bufs):
    n = len(bufs)

    def body(*refs):
        outs = refs[n:2 * n]
        send, recv = refs[2 * n:]
        x, y, c, k, sibling, chips = _place()
        cps = []
        for t in range(n):
            rows = bufs[t].shape[1]
            for j, (px, py) in enumerate(chips):
                landed = outs[t].at[2 * px + py, _half(rows, c)]
                cps.append(pltpu.make_async_remote_copy(src_ref=landed, dst_ref=landed, send_sem=send.at[t, j],
                                                        recv_sem=recv.at[t, j], device_id=sibling, device_id_type=MESH))
        for cp in cps:
            cp.start()
        for t in range(n):
            rows = bufs[t].shape[1]
            for j, (px, py) in enumerate(chips):
                other = outs[t].at[2 * px + py, _half(rows, 1 - c)]
                pltpu.make_async_remote_copy(src_ref=other, dst_ref=other, send_sem=send.at[t, j], recv_sem=recv.at[t, j],
                                             device_id=sibling, device_id_type=MESH).wait_recv()
        for cp in cps:
            cp.wait_send()

    return pl.pallas_call(
        body, in_specs=[ANY] * n, out_specs=[ANY] * n, out_shape=[jax.ShapeDtypeStruct(b.shape, b.dtype) for b in bufs],
        input_output_aliases={t: t for t in range(n)}, scratch_shapes=[pltpu.SemaphoreType.DMA((n, 3))] * 2,
        name="gather_forward_sibling", compiler_params=pltpu.CompilerParams())(*bufs)


def _forward_copies(bufs, refs, send, recv, wait):
    x, y, c, k, sibling, chips = _place()
    cps = []
    for t in range(len(bufs)):
        rows = bufs[t].shape[1]
        for j, (px, py) in enumerate(chips):
            landed = refs[t].at[2 * px + py, _half(rows, c)]
            dst = refs[t].at[2 * px + py, _half(rows, 1 - c)] if wait else landed
            cps.append(pltpu.make_async_remote_copy(src_ref=landed, dst_ref=dst, send_sem=send.at[3 * t + j],
                                                    recv_sem=recv.at[3 * t + j], device_id=sibling, device_id_type=MESH))
    return cps


def forward_start(bufs, after, name):
    n = len(bufs)

    def body(*refs):
        send, recv, token = refs[n + 1], refs[n + 2], refs[-1]
        for cp in _forward_copies(bufs, refs[:n], send, recv, False):
            cp.start()
        token[...] = jnp.zeros_like(token)

    outs = pl.pallas_call(
        body, name=name, in_specs=[HBM_SPEC] * n + [ANY],
        out_shape=(pltpu.SemaphoreType.DMA((3 * n,)), pltpu.SemaphoreType.DMA((3 * n,)),
                   *[pltpu.HBM(b.shape, b.dtype) for b in bufs], jax.ShapeDtypeStruct(TOKEN_SHAPE, F32)),
        out_specs=(SEM_SPEC, SEM_SPEC, *[HBM_SPEC] * n, pl.BlockSpec(memory_space=pltpu.VMEM)),
        input_output_aliases={t: 2 + t for t in range(n)},
        compiler_params=pltpu.CompilerParams(has_side_effects=SPLIT_EFFECT))(*[_in_hbm(b) for b in bufs], after)
    return outs[0], outs[1], list(outs[2:2 + n]), outs[-1]


def forward_wait(send, recv, bufs, after, name):
    n = len(bufs)

    def body(*refs):
        for cp in _forward_copies(bufs, refs[:n], refs[n], refs[n + 1], True):
            cp.wait_send()
            cp.wait_recv()

    return list(pl.pallas_call(
        body, name=name, in_specs=[HBM_SPEC] * n + [SEM_SPEC, SEM_SPEC, ANY],
        out_shape=tuple(pltpu.HBM(b.shape, b.dtype) for b in bufs), out_specs=tuple([HBM_SPEC] * n),
        input_output_aliases={t: t for t in range(n)},
        compiler_params=pltpu.CompilerParams(has_side_effects=SPLIT_EFFECT))(*bufs, send, recv, after))


def _sibling_copies(gs, src_refs, dst_refs, send, recv):
    x, y, c, k, sibling, chips = _place()
    return [pltpu.make_async_remote_copy(
        src_ref=src_refs[t].at[:, _half(gs[t].shape[1], 1 - c), :], dst_ref=dst_refs[t], send_sem=send.at[t],
        recv_sem=recv.at[t], device_id=sibling, device_id_type=MESH) for t in range(len(gs))]


def sibling_start(gs, after, name):
    n = len(gs)
    lands = [lax.empty((g.shape[0], g.shape[1] // 2, g.shape[2]), g.dtype) for g in gs]

    def body(*refs):
        send, recv, token = refs[2 * n + 1], refs[2 * n + 2], refs[-1]
        for cp in _sibling_copies(gs, refs[:n], refs[n:2 * n], send, recv):
            cp.start()
        token[...] = jnp.zeros_like(token)

    outs = pl.pallas_call(
        body, name=name, in_specs=[HBM_SPEC] * (2 * n) + [ANY],
        out_shape=(pltpu.SemaphoreType.DMA((n,)), pltpu.SemaphoreType.DMA((n,)),
                   *[pltpu.HBM(v.shape, v.dtype) for v in gs + lands], jax.ShapeDtypeStruct(TOKEN_SHAPE, F32)),
        out_specs=(SEM_SPEC, SEM_SPEC, *[HBM_SPEC] * (2 * n), pl.BlockSpec(memory_space=pltpu.VMEM)),
        input_output_aliases={t: 2 + t for t in range(2 * n)},
        compiler_params=pltpu.CompilerParams(has_side_effects=SPLIT_EFFECT))(*[_in_hbm(v) for v in gs + lands], after)
    return outs[0], outs[1], list(outs[2:2 + n]), list(outs[2 + n:2 + 2 * n]), outs[-1]


def sibling_wait(send, recv, gs, lands, after, name):
    n = len(gs)

    def body(*refs):
        for cp in _sibling_copies(gs, refs[:n], refs[n:2 * n], refs[2 * n], refs[2 * n + 1]):
            cp.wait_send()
            cp.wait_recv()

    outs = pl.pallas_call(
        body, name=name, in_specs=[HBM_SPEC] * (2 * n) + [SEM_SPEC, SEM_SPEC, ANY],
        out_shape=tuple(pltpu.HBM(v.shape, v.dtype) for v in gs + lands), out_specs=tuple([HBM_SPEC] * (2 * n)),
        input_output_aliases={t: t for t in range(2 * n)},
        compiler_params=pltpu.CompilerParams(has_side_effects=SPLIT_EFFECT))(*gs, *lands, send, recv, after)
    return list(outs[:n]), list(outs[n:])


def _reduce_ici_copies(ps, src_refs, dst_refs, send, recv):
    x, y, c, k, sibling, chips = _place()
    cps = []
    for t in range(len(ps)):
        for j, (px, py) in enumerate(chips):
            cps.append(pltpu.make_async_remote_copy(
                src_ref=src_refs[t].at[2 * px + py], dst_ref=dst_refs[t].at[j], send_sem=send.at[3 * t + j],
                recv_sem=recv.at[3 * t + j],
                device_id=(px, py, c), device_id_type=MESH))
    return cps


def reduce_ici_start(ps, after, name):
    n = len(ps)
    lands = [lax.empty((3,) + p.shape[1:], p.dtype) for p in ps]

    def body(*refs):
        send, recv, token = refs[2 * n + 1], refs[2 * n + 2], refs[-1]
        for cp in _reduce_ici_copies(ps, refs[:n], refs[n:2 * n], send, recv):
            cp.start()
        token[...] = jnp.zeros_like(token)

    outs = pl.pallas_call(
        body, name=name, in_specs=[HBM_SPEC] * (2 * n) + [ANY],
        out_shape=(pltpu.SemaphoreType.DMA((3 * n,)), pltpu.SemaphoreType.DMA((3 * n,)),
                   *[pltpu.HBM(v.shape, v.dtype) for v in ps + lands], jax.ShapeDtypeStruct(TOKEN_SHAPE, F32)),
        out_specs=(SEM_SPEC, SEM_SPEC, *[HBM_SPEC] * (2 * n), pl.BlockSpec(memory_space=pltpu.VMEM)),
        input_output_aliases={t: 2 + t for t in range(2 * n)},
        compiler_params=pltpu.CompilerParams(has_side_effects=SPLIT_EFFECT))(*[_in_hbm(v) for v in ps + lands], after)
    return outs[0], outs[1], list(outs[2:2 + n]), list(outs[2 + n:2 + 2 * n]), outs[-1]


def _halves_copies(ss, refs, send, recv):
    x, y, c, k, sibling, chips = _place()
    return [pltpu.make_async_remote_copy(
        src_ref=refs[t].at[_half(ss[t].shape[0], c)], dst_ref=refs[t].at[_half(ss[t].shape[0], c)], send_sem=send.at[t],
        recv_sem=recv.at[t], device_id=sibling, device_id_type=MESH) for t in range(len(ss))]


def halves_start(ss, after, name):
    n = len(ss)

    def body(*refs):
        send, recv, token = refs[n + 1], refs[n + 2], refs[-1]
        for cp in _halves_copies(ss, refs[:n], send, recv):
            cp.start()
        token[...] = jnp.zeros_like(token)

    outs = pl.pallas_call(
        body, name=name, in_specs=[HBM_SPEC] * n + [ANY],
        out_shape=(pltpu.SemaphoreType.DMA((n,)), pltpu.SemaphoreType.DMA((n,)), *[pltpu.HBM(s.shape, s.dtype) for s in ss],
                   jax.ShapeDtypeStruct(TOKEN_SHAPE, F32)),
        out_specs=(SEM_SPEC, SEM_SPEC, *[HBM_SPEC] * n, pl.BlockSpec(memory_space=pltpu.VMEM)),
        input_output_aliases={t: 2 + t for t in range(n)},
        compiler_params=pltpu.CompilerParams(has_side_effects=SPLIT_EFFECT))(*[_in_hbm(s) for s in ss], after)
    return outs[0], outs[1], list(outs[2:2 + n]), outs[-1]


def halves_wait(send, recv, ss, after, name):
    n = len(ss)

    def body(*refs):
        x, y, c, k, sibling, chips = _place()
        for t in range(n):
            rows = ss[t].shape[0]
            cp = pltpu.make_async_remote_copy(
                src_ref=refs[t].at[_half(rows, c)], dst_ref=refs[t].at[_half(rows, 1 - c)], send_sem=refs[n].at[t],
                recv_sem=refs[n + 1].at[t], device_id=sibling, device_id_type=MESH)
            cp.wait_send()
            cp.wait_recv()

    return list(pl.pallas_call(
        body, name=name, in_specs=[HBM_SPEC] * n + [SEM_SPEC, SEM_SPEC, ANY],
        out_shape=tuple(pltpu.HBM(s.shape, s.dtype) for s in ss), out_specs=tuple([HBM_SPEC] * n),
        input_output_aliases={t: t for t in range(n)},
        compiler_params=pltpu.CompilerParams(has_side_effects=SPLIT_EFFECT))(*ss, send, recv, after))


def reduce_ici_wait(send, recv, ps, lands, after, name):
    n = len(ps)

    def body(*refs):
        for cp in _reduce_ici_copies(ps, refs[:n], refs[n:2 * n], refs[2 * n], refs[2 * n + 1]):
            cp.wait_send()
            cp.wait_recv()

    outs = pl.pallas_call(
        body, name=name, in_specs=[HBM_SPEC] * (2 * n) + [SEM_SPEC, SEM_SPEC, ANY],
        out_shape=tuple(pltpu.HBM(v.shape, v.dtype) for v in ps + lands), out_specs=tuple([HBM_SPEC] * (2 * n)),
        input_output_aliases={t: t for t in range(2 * n)},
        compiler_params=pltpu.CompilerParams(has_side_effects=SPLIT_EFFECT))(*ps, *lands, send, recv, after)
    return list(outs[:n]), list(outs[n:])


SMALL_CHUNK_ROWS = 256


def place_slot(v, me):
    rows = v.shape[0]

    def body(me_ref, v_ref, o_ref):
        o_ref[...] = v_ref[...]

    grid_spec = pltpu.PrefetchScalarGridSpec(
        num_scalar_prefetch=1, grid=(rows // SMALL_CHUNK_ROWS,),
        in_specs=[pl.BlockSpec((SMALL_CHUNK_ROWS, 128), lambda i, me_ref: (i, 0))],
        out_specs=pl.BlockSpec((None, SMALL_CHUNK_ROWS, 128), lambda i, me_ref: (me_ref[0], i, 0)))
    return pl.pallas_call(body, grid_spec=grid_spec, out_shape=jax.ShapeDtypeStruct((N_DEV, rows, 128), F32),
                          name="place_slot", compiler_params=_params("parallel"))(me, v)


def _slot_copies(ref, send, recv, wait):
    x, y, c = lax.axis_index("x"), lax.axis_index("y"), lax.axis_index("c")
    me = 4 * x + 2 * y + c
    cps = []
    for d in range(1, N_DEV):
        peer = (x ^ ((d >> 2) & 1), y ^ ((d >> 1) & 1), c ^ (d & 1))
        dst = ref.at[me ^ d] if wait else ref.at[me]
        cps.append(pltpu.make_async_remote_copy(src_ref=ref.at[me], dst_ref=dst, send_sem=send.at[d - 1],
                                                recv_sem=recv.at[d - 1], device_id=peer, device_id_type=MESH))
    return cps


def slots_start(buf, after, name):
    def body(buf_ref, after_ref, send, recv, thru, token):
        for cp in _slot_copies(buf_ref, send, recv, False):
            cp.start()
        token[...] = jnp.zeros_like(token)

    outs = pl.pallas_call(
        body, name=name, in_specs=[HBM_SPEC, ANY],
        out_shape=(pltpu.SemaphoreType.DMA((N_DEV - 1,)), pltpu.SemaphoreType.DMA((N_DEV - 1,)),
                   pltpu.HBM(buf.shape, buf.dtype), jax.ShapeDtypeStruct(TOKEN_SHAPE, F32)),
        out_specs=(SEM_SPEC, SEM_SPEC, HBM_SPEC, pl.BlockSpec(memory_space=pltpu.VMEM)),
        input_output_aliases={0: 2},
        compiler_params=pltpu.CompilerParams(has_side_effects=SPLIT_EFFECT))(_in_hbm(buf), after)
    return outs


def slots_wait(send, recv, buf, after, name):
    def body(buf_ref, send_ref, recv_ref, after_ref, out_ref):
        for cp in _slot_copies(buf_ref, send_ref, recv_ref, True):
            cp.wait_send()
            cp.wait_recv()

    return pl.pallas_call(
        body, name=name, in_specs=[HBM_SPEC, SEM_SPEC, SEM_SPEC, ANY], out_shape=pltpu.HBM(buf.shape, buf.dtype),
        out_specs=HBM_SPEC, input_output_aliases={0: 0},
        compiler_params=pltpu.CompilerParams(has_side_effects=SPLIT_EFFECT))(buf, send, recv, after)


def sum_slots(buf):
    rows = buf.shape[1]

    def body(b_ref, o_ref):
        acc = b_ref[0]
        for s in range(1, N_DEV):
            acc = acc + b_ref[s]
        o_ref[...] = acc

    return pl.pallas_call(
        body, grid=(rows // SMALL_CHUNK_ROWS,), in_specs=[pl.BlockSpec((N_DEV, SMALL_CHUNK_ROWS, 128), lambda i: (0, i, 0))],
        out_specs=pl.BlockSpec((SMALL_CHUNK_ROWS, 128), lambda i: (i, 0)), out_shape=jax.ShapeDtypeStruct((rows, 128), F32),
        name="sum_slots", compiler_params=_params("parallel"))(buf)


def adamw(w, g, m, v):
    rows, cols = w.shape
    tr = rows
    for cand in (512, 256, 128, 64, 32, 16, 8):
        if rows % cand == 0 and cand * cols * 4 <= (1 << 20):
            tr = cand
            break
    c1 = 1.0 - ADAM_B1 ** ADAM_STEP
    c2 = 1.0 - ADAM_B2 ** ADAM_STEP

    def body(w_ref, g_ref, m_ref, v_ref, go_ref, d_ref, nm_ref, nv_ref):
        gf = g_ref[...]
        go_ref[...] = gf
        nm = ADAM_B1 * m_ref[...] + (1.0 - ADAM_B1) * gf
        nv = ADAM_B2 * v_ref[...] + (1.0 - ADAM_B2) * (gf * gf)
        d_ref[...] = -ADAM_LR * ((nm / c1) / (jnp.sqrt(nv / c2) + ADAM_EPS) + ADAM_WD * w_ref[...])
        nm_ref[...] = nm
        nv_ref[...] = nv

    spec = pl.BlockSpec((tr, cols), lambda i: (i, 0))
    shape = jax.ShapeDtypeStruct((rows, cols), F32)
    return pl.pallas_call(body, grid=(rows // tr,), in_specs=[spec] * 4, out_specs=(spec,) * 4, out_shape=(shape,) * 4,
                          name="adamw", compiler_params=_params("parallel"))(w, g, m, v)


TAP_ROWS_ALIGN = 16
FLAT_ALIGN = 128 * SMALL_CHUNK_ROWS


def _pad_to(a, n):
    return jnp.pad(a, (0, n - a.shape[0]))


def _round_up(n, m):
    return (n + m - 1) // m * m


class Exchanges:
    def __init__(self, a, chip_core):
        self.a, self.chip_core = a, chip_core
        self.bufs = []
        for g in range(len(SUBLAYERS)):
            row = []
            for n in sublayer_weight_names(g):
                w = a[n].reshape(-1, a[n].shape[-1])
                if _kind(n) == "tap":
                    w = jnp.pad(w, ((0, _round_up(w.shape[0], TAP_ROWS_ALIGN) - w.shape[0]), (0, 0)))
                row.append(place_block(w, chip_core, F32 if _kind(n) == "tap" else BF16))
            self.bufs.append(row)
        self.started = {}
        self.forwarding = {}
        self.token = None
        self.after = chip_core
        self.stage = [None, None, None]
        self.reduced = []
        self.advanced_midway = False
        self.results = {}

    def _unpack(self, g, gathered, W):
        for n, v in zip(sublayer_weight_names(g), gathered):
            kind = _kind(n)
            if kind == "col":
                W[n] = v
            elif kind == "row":
                W[n] = v.reshape(-1, v.shape[-1])
            elif kind == "grp":
                grp, r, pg = self.a[n].shape
                W[n] = v.reshape(N_CHIPS, grp, r, pg).transpose(1, 0, 2, 3).reshape(grp, N_CHIPS * r, pg)
            else:
                nt = self.a[n].shape[0]
                W[n] = v[:, :nt].transpose(1, 0, 2).reshape(nt, -1)

    def gather_first(self, W):
        gathered = all_gather_chips(self.bufs[0])
        self._unpack(0, gathered, W)
        self.after = gathered[0]

    def forward_begins(self, g, W):
        token, self.token = self.token, None
        for h in (g + 1, g + 2, g + 3):
            if h < len(SUBLAYERS) and h not in self.started:
                send, recv, bufs, token = gather_ici_start(self.bufs[h], self.after, f"gather_start_{h}")
                self.started[h] = (send, recv, bufs)
                self.after = token
        return token

    def forward_ends(self, g, x, W):
        if g + 1 < len(SUBLAYERS):
            if g + 1 in self.forwarding:
                send, recv, bufs = self.forwarding.pop(g + 1)
                gathered = forward_wait(send, recv, bufs, x, f"forward_wait_{g + 1}")
            else:
                send, recv, bufs = self.started[g + 1]
                gathered = gather_forward_sibling(gather_ici_wait(send, recv, bufs, x, f"gather_wait_{g + 1}"))
            self._unpack(g + 1, gathered, W)
            self.after = gathered[0]
        if g + 2 < len(SUBLAYERS):
            send, recv, bufs = self.started[g + 2]
            bufs = gather_ici_wait(send, recv, bufs, x, f"gather_wait_{g + 2}")
            send, recv, bufs, self.token = forward_start(bufs, self.after, f"forward_start_{g + 2}")
            self.forwarding[g + 2] = (send, recv, bufs)
            self.after = self.token

    def gradients_ready(self, g, grads):
        names = [n for n in sublayer_weight_names(g) if _kind(n) != "tap"]
        gl = []
        for n in names:
            v, kind = grads.pop(n), _kind(n)
            if kind == "row":
                v = v.reshape(N_CHIPS, -1, v.shape[-1])
            elif kind == "grp":
                grp, r, pg = self.a[n].shape
                v = v.reshape(grp, N_CHIPS, r, pg).transpose(1, 0, 2, 3).reshape(N_CHIPS, grp * r, pg)
            gl.append(v)
        last = gl[0] if self.advanced_midway else self.advance(gl[0], gl[0])
        send, recv, gl, lands, token = sibling_start(gl, last, f"sibling_start_{g}")
        self.stage[0] = (g, names, send, recv, gl, lands)
        return token

    def midway(self, after):
        self.advanced_midway = True
        last = self.advance(after, after)
        return last if last.shape == TOKEN_SHAPE else None

    def advance(self, after, last):
        if self.stage[2] is not None:
            self.reduced.append(self.stage[2])
        self.stage[2], last = self._to_halves(self.stage[1], after, last)
        self.stage[1], last = self._to_ici(self.stage[0], after, last)
        self.stage[0] = None
        return last

    def _to_ici(self, entry, after, last):
        if entry is None:
            return None, last
        g, names, send, recv, gl, lands = entry
        gl, lands = sibling_wait(send, recv, gl, lands, after, f"sibling_wait_{g}")
        ps = [add_sibling_half(v, l, self.chip_core) for v, l in zip(gl, lands)]
        send, recv, ps, l2s, last = reduce_ici_start(ps, last, f"reduce_start_{g}")
        return (g, names, send, recv, ps, l2s), last

    def _to_halves(self, entry, after, last):
        if entry is None:
            return None, last
        g, names, send, recv, ps, l2s = entry
        ps, l2s = reduce_ici_wait(send, recv, ps, l2s, after, f"reduce_wait_{g}")
        ss = [sum_chip_blocks(p, l2, self.chip_core) for p, l2 in zip(ps, l2s)]
        send, recv, ss, last = halves_start(ss, last, f"halves_start_{g}")
        return (g, names, send, recv, ss), last

    def update_matrices(self, entry, after):
        g, names, send, recv, ss = entry
        for n, grad in zip(names, halves_wait(send, recv, ss, after, f"halves_wait_{g}")):
            shape = self.a[n].shape
            two_d = lambda v: v.reshape(-1, shape[-1])
            outs = adamw(two_d(self.a[n]), grad, two_d(self.a["m_" + n]), two_d(self.a["v_" + n]))
            self.results[n] = tuple(v.reshape(shape) for v in outs)
        return outs[0]

    def finish(self, after):
        first, last = self._to_ici(self.stage[0], after, after)
        for entry in self.reduced + [self.stage[2]]:
            if entry is not None:
                last = self.update_matrices(entry, last)
        second, last = self._to_halves(self.stage[1], last, last)
        first, last = self._to_halves(first, last, last)
        return [second, first], last


def train_step(a):
    x, positions, tgt = a["x"][0], a["positions"][0], a["loss_target"][0]
    mats = [n for n in WEIGHT_NAMES if _kind(n) in ("col", "row", "grp")]
    taps = [n for n in WEIGHT_NAMES if _kind(n) == "tap"]
    reps = [n for n in WEIGHT_NAMES if _kind(n) == "rep"]
    chip = 2 * lax.axis_index("x") + lax.axis_index("y")
    chip_core = jnp.stack([chip, lax.axis_index("c")]).astype(jnp.int32)

    comm = Exchanges(a, chip_core)
    W = {n: a[n] for n in reps}
    comm.gather_first(W)
    loss, dx, grads = local_step(x, positions, tgt, W, comm)
    loss = lax.psum(loss, ("x", "y", "c"))
    left, last = comm.finish(dx)

    n_rep = _round_up(sum(a[n].size for n in reps), FLAT_ALIGN)
    flat_rep = _pad_to(jnp.concatenate([grads[n].reshape(-1) for n in reps]), n_rep)
    flat_tap = jnp.concatenate([grads[n].reshape(-1) for n in taps])
    flat = jnp.concatenate([flat_rep, _pad_to(flat_tap, _round_up(flat_tap.shape[0], FLAT_ALIGN))])
    me = (2 * chip + lax.axis_index("c")).astype(jnp.int32).reshape(1)
    send, recv, slots, token = slots_start(place_slot(flat.reshape(-1, 128), me), last, "slots_start")
    for entry in left:
        token = comm.update_matrices(entry, token)
    summed = sum_slots(slots_wait(send, recv, slots, token, "slots_wait"))
    rep_rows = n_rep // 128
    tap_flat = summed[rep_rows:].reshape(-1)

    out = dict(comm.results)
    pack = lambda pre: _pad_to(jnp.concatenate([a[pre + n].reshape(-1) for n in reps]), n_rep).reshape(-1, 128)
    g_rep, d_rep, m_rep, v_rep = adamw(pack(""), summed[:rep_rows], pack("m_"), pack("v_"))
    off = 0
    for n in reps:
        size, shape = a[n].size, a[n].shape
        out[n] = tuple(f.reshape(-1)[off:off + size].reshape(shape) for f in (g_rep, d_rep, m_rep, v_rep))
        off += size
    off = 0
    for n in taps:
        nt, cs = a[n].shape
        full = tap_flat[off:off + nt * cs * N_CHIPS].reshape(nt, cs * N_CHIPS)
        off += nt * cs * N_CHIPS
        g = lax.dynamic_slice(full, (0, chip * cs), (nt, cs))
        out[n] = tuple(adamw(a[n], g, a["m_" + n], a["v_" + n]))

    res = [loss, dx[None]]
    for part in range(4):
        res += [out[n][part] for n in WEIGHT_NAMES]
    return tuple(res)


def kernel(x, positions, l0_norm_g, l0_a_w_in, l0_a_b_in, l0_a_dw_w, l0_a_dw_b, l0_a_ln_g, l0_a_ln_b, l0_a_w_out, l0_a_b_out, l0_ffn_norm_g, l0_ffn_w_up, l0_ffn_dw_w, l0_ffn_dw_b, l0_ffn_w_down, l1_norm_g, l1_b_w_group, l1_b_scale, l1_ffn_norm_g, l1_ffn_w_up, l1_ffn_dw_w, l1_ffn_dw_b, l1_ffn_w_down, l2_norm_g, l2_c_w_qkv, l2_c_q_norm_g, l2_c_k_norm_g, l2_c_sinks, l2_c_w_o, l2_ffn_norm_g, l2_ffn_w_up, l2_ffn_dw_w, l2_ffn_dw_b, l2_ffn_w_down, l3_norm_g, l3_a_w_in, l3_a_b_in, l3_a_dw_w, l3_a_dw_b, l3_a_ln_g, l3_a_ln_b, l3_a_w_out, l3_a_b_out, l3_ffn_norm_g, l3_ffn_w_up, l3_ffn_dw_w, l3_ffn_dw_b, l3_ffn_w_down, loss_target, m_l0_norm_g, m_l0_a_w_in, m_l0_a_b_in, m_l0_a_dw_w, m_l0_a_dw_b, m_l0_a_ln_g, m_l0_a_ln_b, m_l0_a_w_out, m_l0_a_b_out, m_l0_ffn_norm_g, m_l0_ffn_w_up, m_l0_ffn_dw_w, m_l0_ffn_dw_b, m_l0_ffn_w_down, m_l1_norm_g, m_l1_b_w_group, m_l1_b_scale, m_l1_ffn_norm_g, m_l1_ffn_w_up, m_l1_ffn_dw_w, m_l1_ffn_dw_b, m_l1_ffn_w_down, m_l2_norm_g, m_l2_c_w_qkv, m_l2_c_q_norm_g, m_l2_c_k_norm_g, m_l2_c_sinks, m_l2_c_w_o, m_l2_ffn_norm_g, m_l2_ffn_w_up, m_l2_ffn_dw_w, m_l2_ffn_dw_b, m_l2_ffn_w_down, m_l3_norm_g, m_l3_a_w_in, m_l3_a_b_in, m_l3_a_dw_w, m_l3_a_dw_b, m_l3_a_ln_g, m_l3_a_ln_b, m_l3_a_w_out, m_l3_a_b_out, m_l3_ffn_norm_g, m_l3_ffn_w_up, m_l3_ffn_dw_w, m_l3_ffn_dw_b, m_l3_ffn_w_down, v_l0_norm_g, v_l0_a_w_in, v_l0_a_b_in, v_l0_a_dw_w, v_l0_a_dw_b, v_l0_a_ln_g, v_l0_a_ln_b, v_l0_a_w_out, v_l0_a_b_out, v_l0_ffn_norm_g, v_l0_ffn_w_up, v_l0_ffn_dw_w, v_l0_ffn_dw_b, v_l0_ffn_w_down, v_l1_norm_g, v_l1_b_w_group, v_l1_b_scale, v_l1_ffn_norm_g, v_l1_ffn_w_up, v_l1_ffn_dw_w, v_l1_ffn_dw_b, v_l1_ffn_w_down, v_l2_norm_g, v_l2_c_w_qkv, v_l2_c_q_norm_g, v_l2_c_k_norm_g, v_l2_c_sinks, v_l2_c_w_o, v_l2_ffn_norm_g, v_l2_ffn_w_up, v_l2_ffn_dw_w, v_l2_ffn_dw_b, v_l2_ffn_w_down, v_l3_norm_g, v_l3_a_w_in, v_l3_a_b_in, v_l3_a_dw_w, v_l3_a_dw_b, v_l3_a_ln_g, v_l3_a_ln_b, v_l3_a_w_out, v_l3_a_b_out, v_l3_ffn_norm_g, v_l3_ffn_w_up, v_l3_ffn_dw_w, v_l3_ffn_dw_b, v_l3_ffn_w_down):
    return train_step(dict(locals()))
```

```python
import functools

import jax
import jax.numpy as jnp
from jax import lax
from jax.experimental import pallas as pl
from jax.experimental.pallas import tpu as pltpu

F32 = jnp.float32
BF16 = jnp.bfloat16
EPS = 1e-6
HEAD_DIM = 64
KV_GROUP = 8
ATT_BLOCK = 128
ROT_DIM = 16
ROPE_THETA = 500000.0
POOL_GROUPS = 4
CONF_TAPS = 31
FFN_TAPS = 3
N_CHIPS = 4
N_DEV = 8
ADAM_LR, ADAM_B1, ADAM_B2, ADAM_EPS, ADAM_WD, ADAM_STEP = 0.001, 0.9, 0.999, 1e-08, 0.01, 10
VMEM_LIMIT_BYTES = 56 * 1024 * 1024
MESH = pl.DeviceIdType.MESH

CONF_NAMES = ["norm_g", "a_w_in", "a_b_in", "a_dw_w", "a_dw_b", "a_ln_g", "a_ln_b", "a_w_out", "a_b_out"]
FFN_NAMES = ["ffn_norm_g", "ffn_w_up", "ffn_dw_w", "ffn_dw_b", "ffn_w_down"]
POOL_NAMES = ["norm_g", "b_w_group", "b_scale"]
ATT_NAMES = ["norm_g", "c_w_qkv", "c_q_norm_g", "c_k_norm_g", "c_sinks", "c_w_o"]
WEIGHT_NAMES = ([f"l0_{n}" for n in CONF_NAMES + FFN_NAMES] + [f"l1_{n}" for n in POOL_NAMES + FFN_NAMES]
                + [f"l2_{n}" for n in ATT_NAMES + FFN_NAMES] + [f"l3_{n}" for n in CONF_NAMES + FFN_NAMES])
COL_SHARDED = ("a_w_in", "ffn_w_up", "c_w_qkv")
ROW_SHARDED = ("a_w_out", "ffn_w_down", "c_w_o")
TAP_SHARDED = ("a_dw_w", "ffn_dw_w")


def _kind(name):
    base = name[3:]
    if base in COL_SHARDED:
        return "col"
    if base in ROW_SHARDED:
        return "row"
    if base in TAP_SHARDED:
        return "tap"
    if base == "b_w_group":
        return "grp"
    return "rep"


def _pick(n, prefs):
    for p in prefs:
        if p <= n and n % p == 0:
            return p
    return n


def _params(*sem):
    return pltpu.CompilerParams(dimension_semantics=sem, vmem_limit_bytes=VMEM_LIMIT_BYTES)


def _sigmoid(x):
    return 1.0 / (1.0 + jnp.exp(-x))


NN = (((1,), (0,)), ((), ()))
NT = (((1,), (1,)), ((), ()))
TN = (((0,), (0,)), ((), ()))


MM_VMEM_BUDGET = 44 * 1024 * 1024
MM_STEP_SECONDS = 0.35e-6
MM_FLOPS, MM_HBM_BYTES = 9.0e14, 3.0e12
TILE_SIZES = (4096, 2816, 2048, 1408, 1024, 704, 640, 512, 256, 128)


def _tile_options(n, lane):
    opts = [c for c in TILE_SIZES if c <= n and n % c == 0 and (not lane or c % 128 == 0)]
    return opts or [n]


def _mm_plan(m, n, k, *, n_unit=None, k_unit=None, a_bytes=2, b_bytes=2, o_bytes=2, extra_bytes=0):
    best = None
    for tm in _tile_options(m, False):
        for tn in _tile_options(n_unit or n, True):
            for tk in _tile_options(k_unit or k, True) + ([k] if not k_unit else []):
                nk = k // tk
                vmem = 2 * (tm * tk * a_bytes + tk * tn * b_bytes + tm * tn * (o_bytes + extra_bytes)) + tm * tn * 4 * (2 if nk > 1 else 1)
                if vmem > MM_VMEM_BUDGET:
                    continue
                ni, nj = m // tm, n // tn
                a_all, b_all, o_all = m * k * a_bytes, k * n * b_bytes, m * n * (o_bytes + extra_bytes)
                for i_inner in (False, True):
                    if nk > 1:
                        traffic = a_all * nj + b_all * ni + o_all
                    elif i_inner:
                        traffic = a_all * nj + b_all + o_all
                    else:
                        traffic = a_all + b_all * ni + o_all
                    cost = ni * nj * nk * MM_STEP_SECONDS + max(2.0 * m * n * k / MM_FLOPS, traffic / MM_HBM_BYTES)
                    if best is None or cost < best[0]:
                        best = (cost, tm, tn, tk, i_inner)
    assert best is not None, (m, n, k)
    return best[1:]


def _mm(a, b, *, dims, sizes, plan, a_blk, a_idx, b_blk, b_idx, o_blk, o_idx, out_shape, name,
        bias=None, scale=None, vec_blk=None, vec_idx=None, resid=None, raw_shape=None):
    m, n, k = sizes
    tm, tn, tk, i_inner = plan
    ni, nj, nk = m // tm, n // tn, k // tk
    has_bias, has_scale, has_resid, want_raw = bias is not None, scale is not None, resid is not None, raw_shape is not None

    def body(*refs):
        a_ref, b_ref = refs[0], refs[1]
        pos = 2
        bias_ref = scale_ref = resid_ref = raw_ref = None
        if has_bias:
            bias_ref = refs[pos]; pos += 1
        if has_scale:
            scale_ref = refs[pos]; pos += 1
        if has_resid:
            resid_ref = refs[pos]; pos += 1
        o_ref = refs[pos]; pos += 1
        if want_raw:
            raw_ref = refs[pos]; pos += 1
        part = lax.dot_general(a_ref[...].astype(BF16), b_ref[...].astype(BF16), dims, preferred_element_type=F32)

        def finish(r):
            if want_raw:
                raw_ref[...] = r.astype(raw_ref.dtype)
            if has_bias:
                r = r + bias_ref[...]
            if has_scale:
                r = r * scale_ref[...]
            if has_resid:
                r = r + resid_ref[...]
            o_ref[...] = r.astype(o_ref.dtype)

        if nk == 1:
            finish(part)
        else:
            acc_ref = refs[pos]
            kk = pl.program_id(2)

            @pl.when(kk == 0)
            def _():
                acc_ref[...] = part

            @pl.when(kk > 0)
            def _():
                acc_ref[...] += part

            @pl.when(kk == nk - 1)
            def _():
                finish(acc_ref[...])

    order = (lambda f: (lambda j, i, kk: f(i, j, kk))) if i_inner else (lambda f: f)
    spec = lambda blk, idx: pl.BlockSpec(blk, order(idx))
    operands, in_specs = [a, b], [spec(a_blk, a_idx), spec(b_blk, b_idx)]
    for v in (bias, scale):
        if v is not None:
            operands.append(v); in_specs.append(spec(vec_blk, vec_idx))
    if has_resid:
        operands.append(resid); in_specs.append(spec(o_blk, o_idx))
    out_shapes, out_specs = out_shape, spec(o_blk, o_idx)
    if want_raw:
        out_shapes, out_specs = (out_shape, raw_shape), (spec(o_blk, o_idx), spec(o_blk, o_idx))
    return pl.pallas_call(
        body, grid=(nj, ni, nk) if i_inner else (ni, nj, nk), in_specs=in_specs, out_specs=out_specs,
        out_shape=out_shapes, scratch_shapes=[pltpu.VMEM((tm, tn), F32)] if nk > 1 else [], name=name,
        compiler_params=_params("parallel", "parallel", "arbitrary"))(*operands)


def mm_nn_cols(a, g, *, split, out_dtype, bias=None, name):
    t, k = a.shape
    ns = g.shape[2]
    n = N_CHIPS * ns
    plan = _mm_plan(t, n, k, n_unit=ns, o_bytes=jnp.dtype(out_dtype).itemsize)
    tm, tn, tk, _ = plan
    nj = ns // tn
    if split:
        o_blk, o_idx = (None, tm, tn), (lambda i, j, kk: (j // (2 * nj), i, j % (2 * nj)))
        out_shape = jax.ShapeDtypeStruct((2, t, 2 * ns), out_dtype)
        vec_blk, vec_idx = (None, 1, tn), (lambda i, j, kk: (j // (2 * nj), 0, j % (2 * nj)))
        if bias is not None:
            bias = bias.reshape(2, 1, 2 * ns)
    else:
        o_blk, o_idx = (tm, tn), (lambda i, j, kk: (i, j))
        out_shape = jax.ShapeDtypeStruct((t, n), out_dtype)
        vec_blk, vec_idx = (1, tn), (lambda i, j, kk: (0, j))
        if bias is not None:
            bias = bias.reshape(1, n)
    return _mm(a, g, dims=NN, sizes=(t, n, k), plan=plan, a_blk=(tm, tk), a_idx=lambda i, j, kk: (i, kk),
               b_blk=(None, tk, tn), b_idx=lambda i, j, kk: (j // nj, kk, j % nj), o_blk=o_blk, o_idx=o_idx,
               out_shape=out_shape, name=name, bias=bias, vec_blk=vec_blk, vec_idx=vec_idx)


def mm_nn(a, w, *, out_dtype, bias=None, scale=None, resid=None, raw_dtype=None, name):
    t, k = a.shape
    n = w.shape[1]
    extra = (4 if resid is not None else 0) + (0 if raw_dtype is None else jnp.dtype(raw_dtype).itemsize)
    plan = _mm_plan(t, n, k, a_bytes=a.dtype.itemsize, o_bytes=jnp.dtype(out_dtype).itemsize, extra_bytes=extra)
    tm, tn, tk, _ = plan
    raw_shape = None if raw_dtype is None else jax.ShapeDtypeStruct((t, n), raw_dtype)
    return _mm(a, w, dims=NN, sizes=(t, n, k), plan=plan, a_blk=(tm, tk), a_idx=lambda i, j, kk: (i, kk),
               b_blk=(tk, tn), b_idx=lambda i, j, kk: (kk, j), o_blk=(tm, tn), o_idx=lambda i, j, kk: (i, j),
               out_shape=jax.ShapeDtypeStruct((t, n), out_dtype), name=name,
               bias=None if bias is None else bias.reshape(1, n), scale=None if scale is None else scale.reshape(1, n),
               vec_blk=(1, tn), vec_idx=lambda i, j, kk: (0, j), resid=resid, raw_shape=raw_shape)


def mm_nt(dy, w, *, out_dtype, name):
    t, n = dy.shape
    kdim = w.shape[0]
    plan = _mm_plan(t, kdim, n, a_bytes=dy.dtype.itemsize, o_bytes=jnp.dtype(out_dtype).itemsize)
    tm, tn, tk, _ = plan
    return _mm(dy, w, dims=NT, sizes=(t, kdim, n), plan=plan, a_blk=(tm, tk), a_idx=lambda i, j, kk: (i, kk),
               b_blk=(tn, tk), b_idx=lambda i, j, kk: (j, kk), o_blk=(tm, tn), o_idx=lambda i, j, kk: (i, j),
               out_shape=jax.ShapeDtypeStruct((t, kdim), out_dtype), name=name)


def mm_nt_cols(du, g, *, split, out_dtype, name):
    kdim, ns = g.shape[1], g.shape[2]
    t = du.shape[1] if split else du.shape[0]
    plan = _mm_plan(t, kdim, N_CHIPS * ns, k_unit=ns, o_bytes=jnp.dtype(out_dtype).itemsize)
    tm, tn, tk, _ = plan
    nkb = ns // tk
    if split:
        a_blk, a_idx = (None, tm, tk), (lambda i, j, kk: (kk // (2 * nkb), i, kk % (2 * nkb)))
    else:
        a_blk, a_idx = (tm, tk), (lambda i, j, kk: (i, kk))
    return _mm(du, g, dims=NT, sizes=(t, kdim, N_CHIPS * ns), plan=plan, a_blk=a_blk, a_idx=a_idx,
               b_blk=(None, tn, tk), b_idx=lambda i, j, kk: (kk // nkb, j, kk % nkb),
               o_blk=(tm, tn), o_idx=lambda i, j, kk: (i, j),
               out_shape=jax.ShapeDtypeStruct((t, kdim), out_dtype), name=name)


def mm_wgrad(at, dy, *, out_dtype, name):
    return mm_nn(at, dy, out_dtype=out_dtype, name=name)


def mm_wgrad_cols(ht, du, *, split, out_dtype, name):
    kdim, t = ht.shape
    ns = (du.shape[2] // 2) if split else (du.shape[1] // N_CHIPS)
    plan = _mm_plan(kdim, N_CHIPS * ns, t, n_unit=ns, o_bytes=jnp.dtype(out_dtype).itemsize)
    tm, tn, tk, _ = plan
    nj = ns // tn
    if split:
        b_blk, b_idx = (None, tk, tn), (lambda i, j, kk: (j // (2 * nj), kk, j % (2 * nj)))
    else:
        b_blk, b_idx = (tk, tn), (lambda i, j, kk: (kk, j))
    return _mm(ht, du, dims=NN, sizes=(kdim, N_CHIPS * ns, t), plan=plan, a_blk=(tm, tk), a_idx=lambda i, j, kk: (i, kk),
               b_blk=b_blk, b_idx=b_idx, o_blk=(None, tm, tn), o_idx=lambda i, j, kk: (j // nj, i, j % nj),
               out_shape=jax.ShapeDtypeStruct((N_CHIPS, kdim, ns), out_dtype), name=name)


def _row_tile(t):
    return _pick(t, (256, 128))


def _acc_rows(ref, part, i):
    @pl.when(i == 0)
    def _():
        ref[...] = part

    @pl.when(i > 0)
    def _():
        ref[...] += part


TOKEN_SHAPE = (8, 128)


def _token_operand(token):
    if token is None:
        return [], []
    return [token], [pl.BlockSpec(TOKEN_SHAPE, lambda i: (0, 0))]


def rms_fwd(x, g, out_dtype, token=None, transposed=False):
    t, d = x.shape
    tr = _row_tile(t)

    def body(x_ref, g_ref, *rest):
        outs = rest[-2:] if transposed else rest[-1:]
        xf = x_ref[...]
        r = lax.rsqrt(jnp.mean(xf * xf, axis=-1, keepdims=True) + EPS)
        y = xf * r * g_ref[...]
        outs[0][...] = y.astype(outs[0].dtype)
        if transposed:
            outs[1][...] = y.T.astype(BF16)

    row = pl.BlockSpec((tr, d), lambda i: (i, 0))
    tok, tok_spec = _token_operand(token)
    out_specs, out_shape = row, jax.ShapeDtypeStruct((t, d), out_dtype)
    if transposed:
        out_specs = (row, pl.BlockSpec((d, tr), lambda i: (0, i)))
        out_shape = (out_shape, jax.ShapeDtypeStruct((d, t), BF16))
    return pl.pallas_call(body, grid=(t // tr,), in_specs=[row, pl.BlockSpec((1, d), lambda i: (0, 0))] + tok_spec,
                          out_specs=out_specs, out_shape=out_shape, name="rms_fwd",
                          compiler_params=_params("parallel"))(x, g.reshape(1, d), *tok)


def rms_bwd(x, g, dh, dres, token=None):
    t, d = x.shape
    tr = _row_tile(t)

    def body(x_ref, g_ref, dh_ref, dres_ref, *rest):
        dx_ref, dx16_ref, dg_ref = rest[-3:]
        i = pl.program_id(0)
        xf = x_ref[...]
        r = lax.rsqrt(jnp.mean(xf * xf, axis=-1, keepdims=True) + EPS)
        xhat = xf * r
        dhf = dh_ref[...].astype(F32)
        dxh = dhf * g_ref[...]
        m = jnp.mean(dxh * xhat, axis=-1, keepdims=True)
        dx = dres_ref[...] + r * (dxh - xhat * m)
        dx_ref[...] = dx
        dx16_ref[...] = dx.astype(BF16)
        _acc_rows(dg_ref, jnp.sum(dhf * xhat, axis=0, keepdims=True), i)

    row = pl.BlockSpec((tr, d), lambda i: (i, 0))
    vec = pl.BlockSpec((1, d), lambda i: (0, 0))
    tok, tok_spec = _token_operand(token)
    return pl.pallas_call(body, grid=(t // tr,), in_specs=[row, vec, row, row] + tok_spec, out_specs=(row, row, vec),
                          out_shape=(jax.ShapeDtypeStruct((t, d), F32), jax.ShapeDtypeStruct((t, d), BF16),
                                     jax.ShapeDtypeStruct((1, d), F32)),
                          name="rms_bwd", compiler_params=_params("arbitrary"))(x, g.reshape(1, d), dh, dres, *tok)


def ln_silu_fwd(c, g, b):
    t, d = c.shape
    tr = _row_tile(t)

    def body(c_ref, g_ref, b_ref, o_ref, ot_ref):
        xf = c_ref[...]
        mu = jnp.mean(xf, axis=-1, keepdims=True)
        xc = xf - mu
        var = jnp.mean(xc * xc, axis=-1, keepdims=True)
        n = xc * lax.rsqrt(var + EPS) * g_ref[...] + b_ref[...]
        s = n * _sigmoid(n)
        o_ref[...] = s.astype(o_ref.dtype)
        ot_ref[...] = s.T.astype(ot_ref.dtype)

    row = pl.BlockSpec((tr, d), lambda i: (i, 0))
    vec = pl.BlockSpec((1, d), lambda i: (0, 0))
    return pl.pallas_call(body, grid=(t // tr,), in_specs=[row, vec, vec],
                          out_specs=(row, pl.BlockSpec((d, tr), lambda i: (0, i))),
                          out_shape=(jax.ShapeDtypeStruct((t, d), BF16), jax.ShapeDtypeStruct((d, t), BF16)),
                          name="ln_silu_fwd", compiler_params=_params("parallel"))(c, g.reshape(1, d), b.reshape(1, d))


def ln_silu_bwd(c, g, b, ds, token=None):
    t, d = c.shape
    tr = _row_tile(t)

    def body(c_ref, g_ref, b_ref, ds_ref, *rest):
        dc_ref, dg_ref, db_ref = rest[-3:]
        i = pl.program_id(0)
        xf = c_ref[...]
        mu = jnp.mean(xf, axis=-1, keepdims=True)
        xc = xf - mu
        var = jnp.mean(xc * xc, axis=-1, keepdims=True)
        rstd = lax.rsqrt(var + EPS)
        xhat = xc * rstd
        n = xhat * g_ref[...] + b_ref[...]
        sg = _sigmoid(n)
        dn = ds_ref[...].astype(F32) * (sg * (1.0 + n * (1.0 - sg)))
        dxh = dn * g_ref[...]
        m1 = jnp.mean(dxh, axis=-1, keepdims=True)
        m2 = jnp.mean(dxh * xhat, axis=-1, keepdims=True)
        dc_ref[...] = rstd * (dxh - m1 - xhat * m2)
        _acc_rows(dg_ref, jnp.sum(dn * xhat, axis=0, keepdims=True), i)
        _acc_rows(db_ref, jnp.sum(dn, axis=0, keepdims=True), i)

    row = pl.BlockSpec((tr, d), lambda i: (i, 0))
    vec = pl.BlockSpec((1, d), lambda i: (0, 0))
    vshape = jax.ShapeDtypeStruct((1, d), F32)
    tok, tok_spec = _token_operand(token)
    return pl.pallas_call(body, grid=(t // tr,), in_specs=[row, vec, vec, row] + tok_spec, out_specs=(row, vec, vec),
                          out_shape=(jax.ShapeDtypeStruct((t, d), F32), vshape, vshape), name="ln_silu_bwd",
                          compiler_params=_params("arbitrary"))(c, g.reshape(1, d), b.reshape(1, d), ds, *tok)


def loss_grad(y, tgt):
    t, d = y.shape
    tr = _row_tile(t)

    def body(y_ref, t_ref, dy_ref, dy16_ref, sq_ref):
        i = pl.program_id(0)
        err = y_ref[...] - t_ref[...]
        dy = err * (1.0 / d)
        dy_ref[...] = dy
        dy16_ref[...] = dy.astype(BF16)
        _acc_rows(sq_ref, jnp.sum(err * err, axis=0, keepdims=True), i)

    row = pl.BlockSpec((tr, d), lambda i: (i, 0))
    vec = pl.BlockSpec((1, d), lambda i: (0, 0))
    return pl.pallas_call(body, grid=(t // tr,), in_specs=[row, row], out_specs=(row, row, vec),
                          out_shape=(jax.ShapeDtypeStruct((t, d), F32), jax.ShapeDtypeStruct((t, d), BF16),
                                     jax.ShapeDtypeStruct((1, d), F32)),
                          name="loss_grad", compiler_params=_params("arbitrary"))(y, tgt)


def col_sum(a):
    t, d = a.shape
    tr = _row_tile(t)

    def body(a_ref, o_ref):
        _acc_rows(o_ref, jnp.sum(a_ref[...].astype(F32), axis=0, keepdims=True), pl.program_id(0))

    return pl.pallas_call(body, grid=(t // tr,), in_specs=[pl.BlockSpec((tr, d), lambda i: (i, 0))],
                          out_specs=pl.BlockSpec((1, d), lambda i: (0, 0)), out_shape=jax.ShapeDtypeStruct((1, d), F32),
                          name="col_sum", compiler_params=_params("arbitrary"))(a)


def scale_bwd(dx, ypre, scale):
    t, d = dx.shape
    tr = _row_tile(t)

    def body(dx_ref, y_ref, s_ref, dy_ref, ds_ref):
        dxf = dx_ref[...]
        dy_ref[...] = (dxf * s_ref[...]).astype(dy_ref.dtype)
        _acc_rows(ds_ref, jnp.sum(dxf * y_ref[...], axis=0, keepdims=True), pl.program_id(0))

    row = pl.BlockSpec((tr, d), lambda i: (i, 0))
    vec = pl.BlockSpec((1, d), lambda i: (0, 0))
    return pl.pallas_call(body, grid=(t // tr,), in_specs=[row, row, vec], out_specs=(row, vec),
                          out_shape=(jax.ShapeDtypeStruct((t, d), BF16), jax.ShapeDtypeStruct((1, d), F32)),
                          name="scale_bwd", compiler_params=_params("arbitrary"))(dx, ypre, scale.reshape(1, d))


def _chunk_rows(t):
    return _pick(t, (256, 128))


def _load_halo(ref, lead, base, rows, t, first, last, pre, post):
    idx = (lambda s, n: (pl.ds(s, n), slice(None))) if lead is None else (lambda s, n: (lead, pl.ds(s, n), slice(None)))
    parts = []
    if pre:
        start = pl.multiple_of(jnp.maximum(base - pre, 0), pre)
        parts.append(ref[idx(start, pre)].astype(F32) * jnp.where(first, 0.0, 1.0))
    parts.append(ref[idx(base, rows)].astype(F32))
    if post:
        start = pl.multiple_of(jnp.minimum(base + rows, t - post), post)
        parts.append(ref[idx(start, post)].astype(F32) * jnp.where(last, 0.0, 1.0))
    return parts[0] if len(parts) == 1 else jnp.concatenate(parts, axis=0)


def _fold8(x):
    r, c = x.shape
    return x.reshape(r // 8, 8, c).sum(axis=0)


def ffn_gate_fwd(u2, w3, b2):
    _, t, f = u2.shape
    tc = _pick(f, (256, 128))
    rows = _chunk_rows(t)
    nch = t // rows
    halo = 16

    def body(u_ref, w_ref, b_ref, a_ref, at_ref):
        def conv(p, base, first):
            xs = _load_halo(u_ref, p, base, rows, t, first, False, halo, 0)
            wp = w_ref[p]
            return (wp[0:1] * pltpu.roll(xs, 2, 0)[halo:] + wp[1:2] * pltpu.roll(xs, 1, 0)[halo:]
                    + wp[2:3] * xs[halo:] + b_ref[p])

        def chunk(i, carry):
            base = pl.multiple_of(i * rows, rows)
            gate, val = conv(0, base, i == 0), conv(1, base, i == 0)
            act = gate * _sigmoid(gate) * val
            act = act.astype(a_ref.dtype)
            a_ref[pl.ds(base, rows), :] = act
            at_ref[:, pl.ds(base, rows)] = act.T
            return carry

        lax.fori_loop(0, nch, chunk, 0)

    return pl.pallas_call(
        body, grid=(f // tc,),
        in_specs=[pl.BlockSpec((2, t, tc), lambda j: (0, 0, j)), pl.BlockSpec((2, 3, tc), lambda j: (0, 0, j)),
                  pl.BlockSpec((2, 1, tc), lambda j: (0, 0, j))],
        out_specs=(pl.BlockSpec((t, tc), lambda j: (0, j)), pl.BlockSpec((tc, t), lambda j: (j, 0))),
        out_shape=(jax.ShapeDtypeStruct((t, f), BF16), jax.ShapeDtypeStruct((f, t), BF16)),
        name="ffn_gate_fwd", compiler_params=_params("parallel"))(u2, w3, b2)


def ffn_gate_bwd(u2, da, w3, b2):
    _, t, f = u2.shape
    tc = _pick(f, (256, 128))
    rows = _chunk_rows(t)
    nch = t // rows
    halo = 16
    n = rows + 2 * halo

    def body(u_ref, da_ref, w_ref, b_ref, du_ref, dw_ref, db_ref, acc_ref):
        acc_ref[...] = jnp.zeros_like(acc_ref)

        def chunk(i, carry):
            base = pl.multiple_of(i * rows, rows)
            first, last = i == 0, i == nch - 1
            daf = jnp.concatenate(
                [jnp.zeros((halo, tc), F32), _load_halo(da_ref, None, base, rows, t, first, last, 0, halo)], axis=0)
            pre, shifted = [], []
            for p in range(2):
                xs = _load_halo(u_ref, p, base, rows, t, first, last, halo, halo)
                x1, x2 = pltpu.roll(xs, 1, 0), pltpu.roll(xs, 2, 0)
                wp = w_ref[p]
                pre.append(wp[0:1] * x2 + wp[1:2] * x1 + wp[2:3] * xs + b_ref[p])
                shifted.append((x2, x1, xs))
            gate, val = pre
            sg = _sigmoid(gate)
            d_pre = (daf * val * (sg * (1.0 + gate * (1.0 - sg))), daf * gate * sg)
            for p in range(2):
                dp = d_pre[p]
                wp = w_ref[p]
                du = wp[2:3] * dp + wp[1:2] * pltpu.roll(dp, n - 1, 0) + wp[0:1] * pltpu.roll(dp, n - 2, 0)
                du_ref[p, pl.ds(base, rows), :] = du[halo:halo + rows].astype(du_ref.dtype)
                own = dp[halo:halo + rows]
                for k in range(3):
                    acc_ref[p, k] += _fold8(own * shifted[p][k][halo:halo + rows])
                acc_ref[p, 3] += _fold8(own)
            return carry

        lax.fori_loop(0, nch, chunk, 0)
        for p in range(2):
            for k in range(3):
                dw_ref[p, k:k + 1, :] = jnp.sum(acc_ref[p, k], axis=0, keepdims=True)
            db_ref[p] = jnp.sum(acc_ref[p, 3], axis=0, keepdims=True)

    blk = pl.BlockSpec((2, t, tc), lambda j: (0, 0, j))
    wspec = pl.BlockSpec((2, 3, tc), lambda j: (0, 0, j))
    bspec = pl.BlockSpec((2, 1, tc), lambda j: (0, 0, j))
    return pl.pallas_call(
        body, grid=(f // tc,), in_specs=[blk, pl.BlockSpec((t, tc), lambda j: (0, j)), wspec, bspec],
        out_specs=(blk, wspec, bspec),
        out_shape=(jax.ShapeDtypeStruct((2, t, f), BF16), jax.ShapeDtypeStruct((2, 3, f), F32),
                   jax.ShapeDtypeStruct((2, 1, f), F32)),
        scratch_shapes=[pltpu.VMEM((2, 4, 8, tc), F32)], name="ffn_gate_bwd",
        compiler_params=_params("parallel"))(u2, da, w3, b2)


def glu_conv_fwd(u2, w, b):
    _, t, d = u2.shape
    taps = w.shape[0]
    tc = 128
    rows = _chunk_rows(t)
    nch = t // rows
    halo = 32

    def body(u_ref, w_ref, b_ref, c_ref):
        def chunk(i, carry):
            base = pl.multiple_of(i * rows, rows)
            a = _load_halo(u_ref, 0, base, rows, t, i == 0, False, halo, 0)
            g = _load_halo(u_ref, 1, base, rows, t, i == 0, False, halo, 0)
            xs = a * _sigmoid(g)
            acc = w_ref[taps - 1:taps, :] * xs[halo:] + b_ref[...]
            for j in range(taps - 1):
                acc = acc + w_ref[j:j + 1, :] * pltpu.roll(xs, taps - 1 - j, 0)[halo:]
            c_ref[pl.ds(base, rows), :] = acc
            return carry

        lax.fori_loop(0, nch, chunk, 0)

    return pl.pallas_call(
        body, grid=(d // tc,),
        in_specs=[pl.BlockSpec((2, t, tc), lambda j: (0, 0, j)), pl.BlockSpec((taps, tc), lambda j: (0, j)),
                  pl.BlockSpec((1, tc), lambda j: (0, j))],
        out_specs=pl.BlockSpec((t, tc), lambda j: (0, j)), out_shape=jax.ShapeDtypeStruct((t, d), F32),
        name="glu_conv_fwd", compiler_params=_params("parallel"))(u2, w, b)


def glu_conv_bwd(u2, dc, w):
    _, t, d = u2.shape
    taps = w.shape[0]
    tc = 128
    rows = _chunk_rows(t)
    nch = t // rows
    halo = 32
    n = rows + halo

    def body(u_ref, dc_ref, w_ref, du_ref, dw_ref, dwb_ref, dbin_ref, acc_ref, bacc_ref):
        acc_ref[...] = jnp.zeros_like(acc_ref)
        bacc_ref[...] = jnp.zeros_like(bacc_ref)

        def chunk(i, carry):
            base = pl.multiple_of(i * rows, rows)
            first, last = i == 0, i == nch - 1
            a = _load_halo(u_ref, 0, base, rows, t, first, False, halo, 0)
            g = _load_halo(u_ref, 1, base, rows, t, first, False, halo, 0)
            sg = _sigmoid(g)
            xs = a * sg
            dcs = _load_halo(dc_ref, None, base, rows, t, first, last, 0, halo)
            own = dcs[:rows]
            dglu = w_ref[taps - 1:taps, :] * own
            acc_ref[taps - 1] += _fold8(own * xs[halo:])
            for j in range(taps - 1):
                s = taps - 1 - j
                dglu = dglu + w_ref[j:j + 1, :] * pltpu.roll(dcs, n - s, 0)[:rows]
                acc_ref[j] += _fold8(own * pltpu.roll(xs, s, 0)[halo:])
            a_c, sg_c = a[halo:], sg[halo:]
            da = dglu * sg_c
            dg = dglu * a_c * sg_c * (1.0 - sg_c)
            du_ref[0, pl.ds(base, rows), :] = da.astype(du_ref.dtype)
            du_ref[1, pl.ds(base, rows), :] = dg.astype(du_ref.dtype)
            bacc_ref[0] += _fold8(own)
            bacc_ref[1] += _fold8(da)
            bacc_ref[2] += _fold8(dg)
            return carry

        lax.fori_loop(0, nch, chunk, 0)
        for j in range(taps):
            dw_ref[j:j + 1, :] = jnp.sum(acc_ref[j], axis=0, keepdims=True)
        dwb_ref[...] = jnp.sum(bacc_ref[0], axis=0, keepdims=True)
        dbin_ref[0] = jnp.sum(bacc_ref[1], axis=0, keepdims=True)
        dbin_ref[1] = jnp.sum(bacc_ref[2], axis=0, keepdims=True)

    blk = pl.BlockSpec((2, t, tc), lambda j: (0, 0, j))
    col = pl.BlockSpec((t, tc), lambda j: (0, j))
    return pl.pallas_call(
        body, grid=(d // tc,), in_specs=[blk, col, pl.BlockSpec((taps, tc), lambda j: (0, j))],
        out_specs=(blk, pl.BlockSpec((taps, tc), lambda j: (0, j)), pl.BlockSpec((1, tc), lambda j: (0, j)),
                   pl.BlockSpec((2, 1, tc), lambda j: (0, 0, j))),
        out_shape=(jax.ShapeDtypeStruct((2, t, d), BF16), jax.ShapeDtypeStruct((taps, d), F32),
                   jax.ShapeDtypeStruct((1, d), F32), jax.ShapeDtypeStruct((2, 1, d), F32)),
        scratch_shapes=[pltpu.VMEM((taps, 8, tc), F32), pltpu.VMEM((3, 8, tc), F32)], name="glu_conv_bwd",
        compiler_params=_params("parallel"))(u2, dc, w)


def _pool_select(grp, levels):
    out = levels[3]
    for k in (2, 1, 0):
        out = jnp.where(grp == k, levels[k], out)
    return out


def _pool_count(base, rows, tc, grp):
    tpos = (base + lax.broadcasted_iota(jnp.int32, (rows, tc), 0) + 1).astype(F32)
    window = jnp.left_shift(2, grp).astype(F32)
    return jnp.minimum(tpos, window)


def pool_fwd(h):
    t, d = h.shape
    pg = d // POOL_GROUPS
    tc = _pick(pg, (256, 128))
    rows = _chunk_rows(t)
    nch = t // rows
    halo = 16

    def body(h_ref, o_ref, ot_ref):
        grp = (pl.program_id(0) * tc) // pg

        def chunk(i, carry):
            base = pl.multiple_of(i * rows, rows)
            xs = _load_halo(h_ref, None, base, rows, t, i == 0, False, halo, 0)
            levels, cur = [], xs
            for k in range(4):
                cur = cur + pltpu.roll(cur, 1 << k, 0)
                levels.append(cur[halo:])
            pooled = _pool_select(grp, levels) / _pool_count(base, rows, tc, grp)
            mixed = pooled - xs[halo:]
            o_ref[pl.ds(base, rows), :] = mixed.astype(o_ref.dtype)
            ot_ref[:, pl.ds(base, rows)] = mixed.T.astype(ot_ref.dtype)
            return carry

        lax.fori_loop(0, nch, chunk, 0)

    col = pl.BlockSpec((t, tc), lambda j: (0, j))
    return pl.pallas_call(body, grid=(d // tc,), in_specs=[col], out_specs=(col, pl.BlockSpec((tc, t), lambda j: (j, 0))),
                          out_shape=(jax.ShapeDtypeStruct((t, d), BF16), jax.ShapeDtypeStruct((d, t), BF16)),
                          name="pool_fwd", compiler_params=_params("parallel"))(h)


def pool_bwd(dmix):
    t, d = dmix.shape
    pg = d // POOL_GROUPS
    tc = _pick(pg, (256, 128))
    rows = _chunk_rows(t)
    nch = t // rows
    halo = 16
    n = rows + halo

    def body(d_ref, o_ref):
        grp = (pl.program_id(0) * tc) // pg

        def chunk(i, carry):
            base = pl.multiple_of(i * rows, rows)
            ds = _load_halo(d_ref, None, base, rows, t, i == 0, i == nch - 1, 0, halo)
            levels, cur = [], ds / _pool_count(base, n, tc, grp)
            for k in range(4):
                cur = cur + pltpu.roll(cur, n - (1 << k), 0)
                levels.append(cur[:rows])
            o_ref[pl.ds(base, rows), :] = _pool_select(grp, levels) - ds[:rows]
            return carry

        lax.fori_loop(0, nch, chunk, 0)

    col = pl.BlockSpec((t, tc), lambda j: (0, j))
    return pl.pallas_call(body, grid=(d // tc,), in_specs=[col], out_specs=col,
                          out_shape=jax.ShapeDtypeStruct((t, d), F32), name="pool_bwd",
                          compiler_params=_params("parallel"))(dmix)


def mm_groups(a, wg, *, mode, out_dtype, scale=None, resid=None, raw_dtype=None, name):
    t, d = a.shape
    pg = wg.shape[1]
    tm = _pick(t, (1024, 512, 256, 128))
    tn = pg
    if mode == "nn":
        dims, b_blk, b_idx = NN, (None, pg, tn), (lambda i, j, kk: (j, 0, 0))
    else:
        dims, b_blk, b_idx = NT, (None, tn, pg), (lambda i, j, kk: (j, 0, 0))
    raw_shape = None if raw_dtype is None else jax.ShapeDtypeStruct((t, d), raw_dtype)
    return _mm(a, wg, dims=dims, sizes=(t, d, pg), plan=(tm, tn, pg, False), a_blk=(tm, pg), a_idx=lambda i, j, kk: (i, j),
               b_blk=b_blk, b_idx=b_idx, o_blk=(tm, tn), o_idx=lambda i, j, kk: (i, j),
               out_shape=jax.ShapeDtypeStruct((t, d), out_dtype), name=name,
               scale=None if scale is None else scale.reshape(1, d), vec_blk=(1, tn), vec_idx=lambda i, j, kk: (0, j),
               resid=resid, raw_shape=raw_shape)


def mm_groups_wgrad(at, dy, groups, *, out_dtype, name):
    d, t = at.shape
    pg = d // groups
    tk = _pick(t, (2048, 1024, 512, 256, 128))
    return _mm(at, dy, dims=NN, sizes=(d, pg, t), plan=(pg, pg, tk, False), a_blk=(pg, tk), a_idx=lambda i, j, kk: (i, kk),
               b_blk=(tk, pg), b_idx=lambda i, j, kk: (kk, i), o_blk=(None, pg, pg), o_idx=lambda i, j, kk: (i, 0, 0),
               out_shape=jax.ShapeDtypeStruct((groups, pg, pg), out_dtype), name=name)


def _split_dot(y, p):
    hi = y.astype(BF16)
    r1 = y - hi.astype(F32)
    mid = r1.astype(BF16)
    lo = (r1 - mid.astype(F32)).astype(BF16)
    pb = p.astype(BF16)
    dot = lambda v: jnp.dot(v, pb, preferred_element_type=F32)
    return (dot(hi) + dot(mid)) + dot(lo)


def rope_tables(positions):
    half = ROT_DIM // 2
    inv_freq = ROPE_THETA ** (-jnp.arange(0, ROT_DIM, 2, dtype=F32) / ROT_DIM)
    ang = positions.astype(F32)[:, None] * inv_freq
    t = positions.shape[0]
    cos, sin = jnp.cos(ang), jnp.sin(ang)
    rest = HEAD_DIM - ROT_DIM
    cosf = jnp.concatenate([cos, cos, jnp.ones((t, rest), F32)], axis=1)
    sinf = jnp.concatenate([-sin, sin, jnp.zeros((t, rest), F32)], axis=1)
    idx = jnp.arange(HEAD_DIM)
    partner = jnp.where(idx < half, idx + half, jnp.where(idx < ROT_DIM, idx - half, idx))
    pmat = (idx[:, None] == partner[None, :]).astype(F32)
    return cosf, sinf, pmat


def qk_rope_fwd(x, g, cosf, sinf, pmat, out_scale):
    hn, t, hd = x.shape
    tq = _pick(t, (4096, 2048, 1024, 512, 256, 128))

    def body(x_ref, g_ref, c_ref, s_ref, p_ref, o_ref):
        xf = x_ref[...]
        r = lax.rsqrt(jnp.mean(xf * xf, axis=-1, keepdims=True) + EPS)
        y = xf * r * g_ref[...]
        rot = y * c_ref[...] + _split_dot(y, p_ref[...]) * s_ref[...]
        o_ref[...] = (rot * out_scale).astype(o_ref.dtype)

    blk = pl.BlockSpec((None, tq, hd), lambda h, i: (h, i, 0))
    tab = pl.BlockSpec((tq, hd), lambda h, i: (i, 0))
    return pl.pallas_call(
        body, grid=(hn, t // tq),
        in_specs=[blk, pl.BlockSpec((1, hd), lambda h, i: (0, 0)), tab, tab, pl.BlockSpec((hd, hd), lambda h, i: (0, 0))],
        out_specs=blk, out_shape=jax.ShapeDtypeStruct((hn, t, hd), BF16), name="qk_rope_fwd",
        compiler_params=_params("parallel", "parallel"))(x, g.reshape(1, hd), cosf, sinf, pmat)


def qk_rope_bwd(dy, x, g, cosf, sinf, pmat_t, in_scale):
    hn, t, hd = x.shape
    tq = _pick(t, (4096, 2048, 1024, 512, 256, 128))

    def body(dy_ref, x_ref, g_ref, c_ref, s_ref, p_ref, dx_ref, dg_ref):
        step = pl.program_id(0) * pl.num_programs(1) + pl.program_id(1)
        dr = dy_ref[...] * in_scale
        dyn = dr * c_ref[...] + _split_dot(dr * s_ref[...], p_ref[...])
        xf = x_ref[...]
        r = lax.rsqrt(jnp.mean(xf * xf, axis=-1, keepdims=True) + EPS)
        xhat = xf * r
        dxh = dyn * g_ref[...]
        m = jnp.mean(dxh * xhat, axis=-1, keepdims=True)
        dx_ref[...] = r * (dxh - xhat * m)
        _acc_rows(dg_ref, jnp.sum(dyn * xhat, axis=0, keepdims=True), step)

    blk = pl.BlockSpec((None, tq, hd), lambda h, i: (h, i, 0))
    tab = pl.BlockSpec((tq, hd), lambda h, i: (i, 0))
    vec = pl.BlockSpec((1, hd), lambda h, i: (0, 0))
    return pl.pallas_call(
        body, grid=(hn, t // tq),
        in_specs=[blk, blk, vec, tab, tab, pl.BlockSpec((hd, hd), lambda h, i: (0, 0))],
        out_specs=(blk, vec), out_shape=(jax.ShapeDtypeStruct((hn, t, hd), F32), jax.ShapeDtypeStruct((1, hd), F32)),
        name="qk_rope_bwd", compiler_params=_params("arbitrary", "arbitrary"))(dy, x, g.reshape(1, hd), cosf, sinf, pmat_t)


NEG_BIG = -1e30


ATT_ROWS = KV_GROUP * ATT_BLOCK


def _att_sinks(sink_ref, kv):
    return jnp.concatenate([jnp.full((ATT_BLOCK, 1), sink_ref[kv * KV_GROUP + g], F32) for g in range(KV_GROUP)], axis=0)


def _att_mask(i, rows):
    shape = (rows, 2 * ATT_BLOCK)
    qi = jnp.bitwise_and(lax.broadcasted_iota(jnp.int32, shape, 0), ATT_BLOCK - 1)
    kj = lax.broadcasted_iota(jnp.int32, shape, 1)
    cur = jnp.logical_and(kj >= ATT_BLOCK, kj - ATT_BLOCK <= qi)
    prev = jnp.logical_and(jnp.logical_and(kj < ATT_BLOCK, kj > qi), i > 0)
    return jnp.logical_or(cur, prev)


def _att_probs(q, k2, mask, sink):
    s = jnp.where(mask, lax.dot_general(q, k2, NT, preferred_element_type=F32), NEG_BIG)
    m = jnp.maximum(jnp.max(s, axis=-1, keepdims=True), sink)
    p = jnp.exp(s - m)
    p_s = jnp.exp(sink - m)
    return p, p_s, jnp.sum(p, axis=-1, keepdims=True) + p_s


def _att_specs(t):
    nb = t // ATT_BLOCK
    qblk = pl.BlockSpec((KV_GROUP, ATT_BLOCK, HEAD_DIM), lambda kv, i: (kv, i, 0))
    cur = pl.BlockSpec((None, ATT_BLOCK, HEAD_DIM), lambda kv, i: (kv, i, 0))
    prev = pl.BlockSpec((None, ATT_BLOCK, HEAD_DIM), lambda kv, i: (kv, jnp.maximum(i - 1, 0), 0))
    return nb, qblk, cur, prev, pl.BlockSpec(memory_space=pltpu.SMEM)


def attn_fwd(q, k, v, sinks):
    h, t, hd = q.shape
    nb, qblk, cur, prev, smem = _att_specs(t)

    def body(q_ref, kc_ref, kp_ref, vc_ref, vp_ref, sink_ref, o_ref):
        kv, i = pl.program_id(0), pl.program_id(1)
        k2 = jnp.concatenate([kp_ref[...], kc_ref[...]], axis=0)
        v2 = jnp.concatenate([vp_ref[...], vc_ref[...]], axis=0)
        mask = _att_mask(i, ATT_BLOCK)
        for g in range(KV_GROUP):
            p, _, denom = _att_probs(q_ref[g], k2, mask, sink_ref[kv * KV_GROUP + g])
            o_ref[g] = (jnp.dot(p.astype(BF16), v2, preferred_element_type=F32) / denom).astype(o_ref.dtype)

    return pl.pallas_call(
        body, grid=(h // KV_GROUP, nb), in_specs=[qblk, cur, prev, cur, prev, smem], out_specs=qblk,
        out_shape=jax.ShapeDtypeStruct((h, t, hd), BF16), name="attn_fwd",
        compiler_params=_params("parallel", "parallel"))(q, k, k, v, v, sinks)


def attn_bwd(q, k, v, do, sinks):
    h, t, hd = q.shape
    kvh = h // KV_GROUP
    nb, qblk, cur, prev, smem = _att_specs(t)

    def body(q_ref, kc_ref, kp_ref, vc_ref, vp_ref, do_ref, sink_ref, dq_ref, dk_ref, dv_ref, dsk_ref):
        kv, i = pl.program_id(0), pl.program_id(1)

        @pl.when(i == 0)
        def _():
            dk_ref[...] = jnp.zeros_like(dk_ref)
            dv_ref[...] = jnp.zeros_like(dv_ref)
            dsk_ref[...] = jnp.zeros_like(dsk_ref)

        k2 = jnp.concatenate([kp_ref[...], kc_ref[...]], axis=0)
        v2 = jnp.concatenate([vp_ref[...], vc_ref[...]], axis=0)
        q = q_ref[...].reshape(ATT_ROWS, hd)
        p, p_s, denom = _att_probs(q, k2, _att_mask(i, ATT_ROWS), _att_sinks(sink_ref, kv))
        inv = 1.0 / denom
        pn = p * inv
        dob = do_ref[...].reshape(ATT_ROWS, hd).astype(BF16)
        dp = lax.dot_general(dob, v2, NT, preferred_element_type=F32)
        dsum = jnp.sum(pn * dp, axis=-1, keepdims=True)
        ds = (pn * (dp - dsum)).astype(BF16)
        dq_ref[...] = jnp.dot(ds, k2, preferred_element_type=F32).reshape(KV_GROUP, ATT_BLOCK, hd)
        dk2 = lax.dot_general(ds, q, TN, preferred_element_type=F32)
        dv2 = lax.dot_general(pn.astype(BF16), dob, TN, preferred_element_type=F32)
        dsink = p_s * inv * dsum
        dsink_rows = [jnp.broadcast_to(-jnp.sum(dsink[g * ATT_BLOCK:(g + 1) * ATT_BLOCK], axis=0, keepdims=True), (1, 128))
                      for g in range(KV_GROUP)]
        here = pl.ds(pl.multiple_of(i * ATT_BLOCK, ATT_BLOCK), ATT_BLOCK)
        before = pl.ds(pl.multiple_of(jnp.maximum(i - 1, 0) * ATT_BLOCK, ATT_BLOCK), ATT_BLOCK)
        dk_ref[before, :] += dk2[:ATT_BLOCK]
        dv_ref[before, :] += dv2[:ATT_BLOCK]
        dk_ref[here, :] += dk2[ATT_BLOCK:]
        dv_ref[here, :] += dv2[ATT_BLOCK:]
        dsk_ref[...] += jnp.concatenate(dsink_rows, axis=0)

    whole = pl.BlockSpec((None, t, hd), lambda kv, i: (kv, 0, 0))
    return pl.pallas_call(
        body, grid=(kvh, nb), in_specs=[qblk, cur, prev, cur, prev, qblk, smem],
        out_specs=(qblk, whole, whole, pl.BlockSpec((None, KV_GROUP, 128), lambda kv, i: (kv, 0, 0))),
        out_shape=(jax.ShapeDtypeStruct((h, t, hd), F32), jax.ShapeDtypeStruct((kvh, t, hd), F32),
                   jax.ShapeDtypeStruct((kvh, t, hd), F32), jax.ShapeDtypeStruct((kvh, KV_GROUP, 128), F32)),
        name="attn_bwd", compiler_params=_params("parallel", "arbitrary"))(q, k, k, v, v, do, sinks)


def _ffn_taps(w, b):
    f2 = w.shape[1]
    return w.reshape(FFN_TAPS, 2, f2 // 2).transpose(1, 0, 2), b.reshape(2, 1, f2 // 2)


def ffn_fwd(x, W, p, tables=None, token=None):
    h, ht = rms_fwd(x, W[p + "ffn_norm_g"], BF16, token, transposed=True)
    u2 = mm_nn_cols(h, W[p + "ffn_w_up"], split=True, out_dtype=BF16, name="ffn_up")
    w3, b2 = _ffn_taps(W[p + "ffn_dw_w"], W[p + "ffn_dw_b"])
    a, at = ffn_gate_fwd(u2, w3, b2)
    y = mm_nn(a, W[p + "ffn_w_down"], out_dtype=F32, resid=x, name="ffn_down")
    return y, (x, ht, u2, at)


def ffn_bwd(saved, W, p, dy, tables=None):
    x, ht, u2, at = saved
    dyf, dyb = dy
    w3, b2 = _ffn_taps(W[p + "ffn_dw_w"], W[p + "ffn_dw_b"])
    grads = {p + "ffn_w_down": mm_wgrad(at, dyb, out_dtype=BF16, name="ffn_down_dw")}
    da = mm_nt(dyb, W[p + "ffn_w_down"], out_dtype=BF16, name="ffn_down_dx")
    du2, dw3, db2 = ffn_gate_bwd(u2, da, w3, b2)
    grads[p + "ffn_dw_w"] = dw3.transpose(1, 0, 2).reshape(FFN_TAPS, -1)
    grads[p + "ffn_dw_b"] = db2.reshape(-1)
    grads[p + "ffn_w_up"] = mm_wgrad_cols(ht, du2, split=True, out_dtype=BF16, name="ffn_up_dw")
    dh = mm_nt_cols(du2, W[p + "ffn_w_up"], split=True, out_dtype=BF16, name="ffn_up_dx")
    return (x, p + "ffn_norm_g", dh), grads


def conf_fwd(x, W, p, tables=None, token=None):
    d = x.shape[1]
    h, ht = rms_fwd(x, W[p + "norm_g"], BF16, token, transposed=True)
    u2 = mm_nn_cols(h, W[p + "a_w_in"], split=True, out_dtype=BF16, bias=W[p + "a_b_in"], name="conf_in")
    c = glu_conv_fwd(u2, W[p + "a_dw_w"], W[p + "a_dw_b"].reshape(1, d))
    s, st = ln_silu_fwd(c, W[p + "a_ln_g"], W[p + "a_ln_b"])
    y = mm_nn(s, W[p + "a_w_out"], out_dtype=F32, bias=W[p + "a_b_out"], resid=x, name="conf_out")
    return y, (x, ht, u2, c, st)


def conf_bwd(saved, W, p, dy, tables=None, midway=None):
    x, ht, u2, c, st = saved
    dyf, dyb = dy
    grads = {p + "a_w_out": mm_wgrad(st, dyb, out_dtype=BF16, name="conf_out_dw"), p + "a_b_out": col_sum(dyf).reshape(-1)}
    ds = mm_nt(dyb, W[p + "a_w_out"], out_dtype=BF16, name="conf_out_dx")
    dc, dlg, dlb = ln_silu_bwd(c, W[p + "a_ln_g"], W[p + "a_ln_b"], ds, midway(ds) if midway else None)
    grads[p + "a_ln_g"], grads[p + "a_ln_b"] = dlg.reshape(-1), dlb.reshape(-1)
    du2, ddw, ddwb, dbin = glu_conv_bwd(u2, dc, W[p + "a_dw_w"])
    grads[p + "a_dw_w"], grads[p + "a_dw_b"], grads[p + "a_b_in"] = ddw, ddwb.reshape(-1), dbin.reshape(-1)
    grads[p + "a_w_in"] = mm_wgrad_cols(ht, du2, split=True, out_dtype=BF16, name="conf_in_dw")
    dh = mm_nt_cols(du2, W[p + "a_w_in"], split=True, out_dtype=BF16, name="conf_in_dx")
    return (x, p + "norm_g", dh), grads


def pool_layer_fwd(x, W, p, tables=None, token=None):
    h = rms_fwd(x, W[p + "norm_g"], F32, token)
    mixed, mixed_t = pool_fwd(h)
    y, ypre = mm_groups(mixed, W[p + "b_w_group"], mode="nn", out_dtype=F32, scale=W[p + "b_scale"], resid=x,
                        raw_dtype=F32, name="pool_mix")
    return y, (x, mixed_t, ypre)


def pool_layer_bwd(saved, W, p, dy, tables=None):
    x, mixed_t, ypre = saved
    dyf, dyb = dy
    dyp, dscale = scale_bwd(dyf, ypre, W[p + "b_scale"])
    grads = {p + "b_scale": dscale.reshape(-1),
             p + "b_w_group": mm_groups_wgrad(mixed_t, dyp, POOL_GROUPS, out_dtype=BF16, name="pool_mix_dw")}
    dmix = mm_groups(dyp, W[p + "b_w_group"], mode="nt", out_dtype=F32, name="pool_mix_dx")
    dh = pool_bwd(dmix)
    return (x, p + "norm_g", dh), grads


def _heads(a, n):
    t = a.shape[0]
    return a.reshape(t, n, HEAD_DIM).transpose(1, 0, 2)


def _unheads(a):
    n, t, _ = a.shape
    return a.transpose(1, 0, 2).reshape(t, n * HEAD_DIM)


def attn_layer_fwd(x, W, p, tables, token=None):
    d = x.shape[1]
    nh = d // HEAD_DIM
    nkv = nh // KV_GROUP
    cosf, sinf, pmat = tables
    h, ht = rms_fwd(x, W[p + "norm_g"], BF16, token, transposed=True)
    qkv = mm_nn_cols(h, W[p + "c_w_qkv"], split=False, out_dtype=F32, name="att_qkv")
    q = _heads(qkv[:, :d], nh)
    k = _heads(qkv[:, d:d + nkv * HEAD_DIM], nkv)
    v = _heads(qkv[:, d + nkv * HEAD_DIM:], nkv).astype(BF16)
    qr = qk_rope_fwd(q, W[p + "c_q_norm_g"], cosf, sinf, pmat, HEAD_DIM ** -0.5)
    kr = qk_rope_fwd(k, W[p + "c_k_norm_g"], cosf, sinf, pmat, 1.0)
    o = attn_fwd(qr, kr, v, W[p + "c_sinks"])
    o2 = _unheads(o)
    y = mm_nn(o2, W[p + "c_w_o"], out_dtype=F32, resid=x, name="att_out")
    return y, (x, ht, q, k, v, qr, kr, o2)


def attn_layer_bwd(saved, W, p, dy, tables):
    x, ht, q, k, v, qr, kr, o2 = saved
    dyf, dyb = dy
    cosf, sinf, pmat = tables
    nh = q.shape[0]
    grads = {p + "c_w_o": mm_wgrad(o2.T, dyb, out_dtype=BF16, name="att_out_dw")}
    do = _heads(mm_nt(dyb, W[p + "c_w_o"], out_dtype=BF16, name="att_out_dx"), nh)
    dqr, dkr, dv, dsk = attn_bwd(qr, kr, v, do, W[p + "c_sinks"])
    grads[p + "c_sinks"] = dsk[:, :, 0].reshape(-1)
    dq, dqg = qk_rope_bwd(dqr, q, W[p + "c_q_norm_g"], cosf, sinf, pmat.T, HEAD_DIM ** -0.5)
    dk, dkg = qk_rope_bwd(dkr, k, W[p + "c_k_norm_g"], cosf, sinf, pmat.T, 1.0)
    grads[p + "c_q_norm_g"], grads[p + "c_k_norm_g"] = dqg.reshape(-1), dkg.reshape(-1)
    dqkv = jnp.concatenate([_unheads(dq), _unheads(dk), _unheads(dv)], axis=1).astype(BF16)
    grads[p + "c_w_qkv"] = mm_wgrad_cols(ht, dqkv, split=False, out_dtype=BF16, name="att_qkv_dw")
    dh = mm_nt_cols(dqkv, W[p + "c_w_qkv"], split=False, out_dtype=BF16, name="att_qkv_dx")
    return (x, p + "norm_g", dh), grads


def local_step(x, positions, tgt, W, comm=None):
    tables = rope_tables(positions)
    saved = []
    for g, (fwd, _, p) in enumerate(SUBLAYERS):
        token = comm.forward_begins(g, W) if comm else None
        x, s = fwd(x, W, p, tables, token)
        saved.append(s)
        if comm:
            comm.forward_ends(g, x, W)
    dyf, dyb, sq = loss_grad(x, tgt)
    loss = 0.5 * jnp.sum(sq) / x.shape[1]
    grads = {}
    for g in reversed(range(len(SUBLAYERS))):
        _, bwd, p = SUBLAYERS[g]
        if comm and g == 0:
            (xin, gain, dh), gr = bwd(saved[g], W, p, (dyf, dyb), tables, comm.midway)
        else:
            (xin, gain, dh), gr = bwd(saved[g], W, p, (dyf, dyb), tables)
        token = comm.gradients_ready(g, gr) if comm else None
        dyf, dyb, dg = rms_bwd(xin, W[gain], dh, dyf, token)
        gr[gain] = dg.reshape(-1)
        grads.update(gr)
    return loss, dyf, grads


SUBLAYERS = [(conf_fwd, conf_bwd, "l0_"), (ffn_fwd, ffn_bwd, "l0_"), (pool_layer_fwd, pool_layer_bwd, "l1_"),
             (ffn_fwd, ffn_bwd, "l1_"), (attn_layer_fwd, attn_layer_bwd, "l2_"), (ffn_fwd, ffn_bwd, "l2_"),
             (conf_fwd, conf_bwd, "l3_"), (ffn_fwd, ffn_bwd, "l3_")]
SUBLAYER_WEIGHTS = {conf_fwd: ("a_w_in", "a_w_out", "a_dw_w"), ffn_fwd: ("ffn_w_up", "ffn_w_down", "ffn_dw_w"),
                    pool_layer_fwd: ("b_w_group",), attn_layer_fwd: ("c_w_qkv", "c_w_o")}


def sublayer_weight_names(g):
    fwd, _, p = SUBLAYERS[g]
    return [p + n for n in SUBLAYER_WEIGHTS[fwd]]


ANY = pl.BlockSpec(memory_space=pl.ANY)


def _place():
    x, y, c = lax.axis_index("x"), lax.axis_index("y"), lax.axis_index("c")
    chips = [(1 - x, y), (x, 1 - y), (1 - x, 1 - y)]
    return x, y, c, 2 * x + y, (x, y, 1 - c), chips


def _half(rows, which):
    return pl.ds(which * (rows // 2), rows // 2)


def place_block(shard, chip_core, out_dtype):
    rows, cols = shard.shape
    tr = rows
    for cand in (512, 256, 128, 64, 32, 16):
        if rows % cand == 0 and cand * cols * 4 <= (2 << 20):
            tr = cand
            break

    def body(pos_ref, s_ref, o_ref):
        o_ref[...] = s_ref[...].astype(o_ref.dtype)

    grid_spec = pltpu.PrefetchScalarGridSpec(
        num_scalar_prefetch=1, grid=(rows // tr,), in_specs=[pl.BlockSpec((tr, cols), lambda i, pos: (i, 0))],
        out_specs=pl.BlockSpec((None, tr, cols), lambda i, pos: (pos[0], i, 0)))
    return pl.pallas_call(body, grid_spec=grid_spec, out_shape=jax.ShapeDtypeStruct((N_CHIPS, rows, cols), out_dtype),
                          name="place_block", compiler_params=_params("parallel"))(chip_core, shard)


def all_gather_chips(bufs):
    n = len(bufs)

    def body(*refs):
        outs = refs[n:2 * n]
        ici_send, ici_recv, d2d_send, d2d_recv = refs[2 * n:]
        x, y, c, k, sibling, chips = _place()

        def rdma(src, dst, send, recv, dev):
            return pltpu.make_async_remote_copy(src_ref=src, dst_ref=dst, send_sem=send, recv_sem=recv,
                                                device_id=dev, device_id_type=MESH)

        sends = []
        for t in range(n):
            rows = bufs[t].shape[1]
            for j, (px, py) in enumerate(chips):
                mine = outs[t].at[k, _half(rows, c)]
                sends.append(rdma(mine, mine, ici_send.at[t, j], ici_recv.at[t, j], (px, py, c)))
        for cp in sends:
            cp.start()
        for t in range(n):
            rows = bufs[t].shape[1]
            for j, (px, py) in enumerate(chips):
                landed = outs[t].at[2 * px + py, _half(rows, c)]
                rdma(landed, landed, ici_send.at[t, j], ici_recv.at[t, j], sibling).wait_recv()
                fwd = rdma(landed, landed, d2d_send.at[t, j], d2d_recv.at[t, j], sibling)
                fwd.start()
                sends.append(fwd)
        for t in range(n):
            rows = bufs[t].shape[1]
            for j, (px, py) in enumerate(chips):
                other = outs[t].at[2 * px + py, _half(rows, 1 - c)]
                rdma(other, other, d2d_send.at[t, j], d2d_recv.at[t, j], sibling).wait_recv()
        for cp in sends:
            cp.wait_send()

    return pl.pallas_call(
        body, in_specs=[ANY] * n, out_specs=[ANY] * n,
        out_shape=[jax.ShapeDtypeStruct(b.shape, b.dtype) for b in bufs],
        input_output_aliases={t: t for t in range(n)},
        scratch_shapes=[pltpu.SemaphoreType.DMA((n, 3))] * 4,
        name="all_gather_chips", compiler_params=pltpu.CompilerParams())(*bufs)


def _sum_rows_tile(rows):
    return _pick(rows, (256, 352, 128, 64, 32, 16))


def add_sibling_half(g, land, core):
    nb, half, cols = land.shape
    tr = _sum_rows_tile(half)
    nrb = half // tr

    def body(c_ref, g_ref, l_ref, o_ref):
        o_ref[...] = (g_ref[...].astype(F32) + l_ref[...].astype(F32)).astype(o_ref.dtype)

    spec = pl.BlockSpec((None, tr, cols), lambda b, i, c_ref: (b, i, 0))
    grid_spec = pltpu.PrefetchScalarGridSpec(
        num_scalar_prefetch=1, grid=(nb, nrb),
        in_specs=[pl.BlockSpec((None, tr, cols), lambda b, i, c_ref: (b, c_ref[1] * nrb + i, 0)), spec], out_specs=spec)
    return pl.pallas_call(body, grid_spec=grid_spec, out_shape=jax.ShapeDtypeStruct(land.shape, BF16),
                          name="add_sibling_half", compiler_params=_params("parallel", "parallel"))(core, g, land)


def sum_chip_blocks(p, l2, chip_core):
    nb, half, cols = l2.shape
    tr = _sum_rows_tile(half)
    nrb = half // tr

    def body(pos_ref, p_ref, l_ref, o_ref):
        acc = p_ref[...].astype(F32)
        for b in range(nb):
            acc = acc + l_ref[b].astype(F32)
        o_ref[...] = acc

    grid_spec = pltpu.PrefetchScalarGridSpec(
        num_scalar_prefetch=1, grid=(nrb,),
        in_specs=[pl.BlockSpec((None, tr, cols), lambda i, pos: (pos[0], i, 0)),
                  pl.BlockSpec((nb, tr, cols), lambda i, pos: (0, i, 0))],
        out_specs=pl.BlockSpec((tr, cols), lambda i, pos: (pos[1] * nrb + i, 0)))
    return pl.pallas_call(body, grid_spec=grid_spec, out_shape=jax.ShapeDtypeStruct((2 * half, cols), F32),
                          name="sum_chip_blocks", compiler_params=_params("parallel"))(chip_core, p, l2)


HBM_SPEC = pl.BlockSpec(memory_space=pltpu.HBM)
SEM_SPEC = pl.BlockSpec(memory_space=pltpu.SEMAPHORE)
SPLIT_EFFECT = pltpu.SideEffectType.DATAFLOW_SIDE_EFFECTING


def _in_hbm(v):
    return pltpu.with_memory_space_constraint(v, pltpu.HBM)


def _gather_ici_copies(bufs, refs, send, recv):
    x, y, c, k, sibling, chips = _place()
    cps = []
    for t in range(len(bufs)):
        rows = bufs[t].shape[1]
        for j, (px, py) in enumerate(chips):
            cps.append(pltpu.make_async_remote_copy(
                src_ref=refs[t].at[k, _half(rows, c)], dst_ref=refs[t].at[k, _half(rows, c)],
                send_sem=send.at[3 * t + j], recv_sem=recv.at[3 * t + j], device_id=(px, py, c), device_id_type=MESH))
    return cps


def gather_ici_start(bufs, after, name):
    n = len(bufs)

    def body(*refs):
        send, recv, token = refs[n + 1], refs[n + 2], refs[-1]
        for cp in _gather_ici_copies(bufs, refs[:n], send, recv):
            cp.start()
        token[...] = jnp.zeros_like(token)

    outs = pl.pallas_call(
        body, name=name, in_specs=[HBM_SPEC] * n + [ANY],
        out_shape=(pltpu.SemaphoreType.DMA((3 * n,)), pltpu.SemaphoreType.DMA((3 * n,)),
                   *[pltpu.HBM(b.shape, b.dtype) for b in bufs], jax.ShapeDtypeStruct(TOKEN_SHAPE, F32)),
        out_specs=(SEM_SPEC, SEM_SPEC, *[HBM_SPEC] * n, pl.BlockSpec(memory_space=pltpu.VMEM)),
        input_output_aliases={t: 2 + t for t in range(n)},
        compiler_params=pltpu.CompilerParams(has_side_effects=SPLIT_EFFECT))(*[_in_hbm(b) for b in bufs], after)
    return outs[0], outs[1], list(outs[2:2 + n]), outs[-1]


def gather_ici_wait(send, recv, bufs, after, name):
    n = len(bufs)

    def body(*refs):
        x, y, c, k, sibling, chips = _place()
        for t in range(n):
            rows = bufs[t].shape[1]
            for j, (px, py) in enumerate(chips):
                cp = pltpu.make_async_remote_copy(
                    src_ref=refs[t].at[k, _half(rows, c)], dst_ref=refs[t].at[2 * px + py, _half(rows, c)],
                    send_sem=refs[n].at[3 * t + j], recv_sem=refs[n + 1].at[3 * t + j], device_id=(px, py, c),
                    device_id_type=MESH)
                cp.wait_send()
                cp.wait_recv()

    return list(pl.pallas_call(
        body, name=name, in_specs=[HBM_SPEC] * n + [SEM_SPEC, SEM_SPEC, ANY],
        out_shape=tuple(pltpu.HBM(b.shape, b.dtype) for b in bufs), out_specs=tuple([HBM_SPEC] * n),
        input_output_aliases={t: t for t in range(n)},
        compiler_params=pltpu.CompilerParams(has_side_effects=SPLIT_EFFECT))(*bufs, send, recv, after))


def gather_forward_sibling(bufs):
    n = len(bufs)

    def body(*refs):
        outs = refs[n:2 * n]
        send, recv = refs[2 * n:]
        x, y, c, k, sibling, chips = _place()
        cps = []
        for t in range(n):
            rows = bufs[t].shape[1]
            for j, (px, py) in enumerate(chips):
                landed = outs[t].at[2 * px + py, _half(rows, c)]
                cps.append(pltpu.make_async_remote_copy(src_ref=landed, dst_ref=landed, send_sem=send.at[t, j],
                                                        recv_sem=recv.at[t, j], device_id=sibling, device_id_type=MESH))
        for cp in cps:
            cp.start()
        for t in range(n):
            rows = bufs[t].shape[1]
            for j, (px, py) in enumerate(chips):
                other = outs[t].at[2 * px + py, _half(rows, 1 - c)]
                pltpu.make_async_remote_copy(src_ref=other, dst_ref=other, send_sem=send.at[t, j], recv_sem=recv.at[t, j],
                                             device_id=sibling, device_id_type=MESH).wait_recv()
        for cp in cps:
            cp.wait_send()

    return pl.pallas_call(
        body, in_specs=[ANY] * n, out_specs=[ANY] * n, out_shape=[jax.ShapeDtypeStruct(b.shape, b.dtype) for b in bufs],
        input_output_aliases={t: t for t in range(n)}, scratch_shapes=[pltpu.SemaphoreType.DMA((n, 3))] * 2,
        name="gather_forward_sibling", compiler_params=pltpu.CompilerParams())(*bufs)


def _forward_copies(bufs, refs, send, recv, wait):
    x, y, c, k, sibling, chips = _place()
    cps = []
    for t in range(len(bufs)):
        rows = bufs[t].shape[1]
        for j, (px, py) in enumerate(chips):
            landed = refs[t].at[2 * px + py, _half(rows, c)]
            dst = refs[t].at[2 * px + py, _half(rows, 1 - c)] if wait else landed
            cps.append(pltpu.make_async_remote_copy(src_ref=landed, dst_ref=dst, send_sem=send.at[3 * t + j],
                                                    recv_sem=recv.at[3 * t + j], device_id=sibling, device_id_type=MESH))
    return cps


def forward_start(bufs, after, name):
    n = len(bufs)

    def body(*refs):
        send, recv, token = refs[n + 1], refs[n + 2], refs[-1]
        for cp in _forward_copies(bufs, refs[:n], send, recv, False):
            cp.start()
        token[...] = jnp.zeros_like(token)

    outs = pl.pallas_call(
        body, name=name, in_specs=[HBM_SPEC] * n + [ANY],
        out_shape=(pltpu.SemaphoreType.DMA((3 * n,)), pltpu.SemaphoreType.DMA((3 * n,)),
                   *[pltpu.HBM(b.shape, b.dtype) for b in bufs], jax.ShapeDtypeStruct(TOKEN_SHAPE, F32)),
        out_specs=(SEM_SPEC, SEM_SPEC, *[HBM_SPEC] * n, pl.BlockSpec(memory_space=pltpu.VMEM)),
        input_output_aliases={t: 2 + t for t in range(n)},
        compiler_params=pltpu.CompilerParams(has_side_effects=SPLIT_EFFECT))(*[_in_hbm(b) for b in bufs], after)
    return outs[0], outs[1], list(outs[2:2 + n]), outs[-1]


def forward_wait(send, recv, bufs, after, name):
    n = len(bufs)

    def body(*refs):
        for cp in _forward_copies(bufs, refs[:n], refs[n], refs[n + 1], True):
            cp.wait_send()
            cp.wait_recv()

    return list(pl.pallas_call(
        body, name=name, in_specs=[HBM_SPEC] * n + [SEM_SPEC, SEM_SPEC, ANY],
        out_shape=tuple(pltpu.HBM(b.shape, b.dtype) for b in bufs), out_specs=tuple([HBM_SPEC] * n),
        input_output_aliases={t: t for t in range(n)},
        compiler_params=pltpu.CompilerParams(has_side_effects=SPLIT_EFFECT))(*bufs, send, recv, after))


def _sibling_copies(gs, src_refs, dst_refs, send, recv):
    x, y, c, k, sibling, chips = _place()
    return [pltpu.make_async_remote_copy(
        src_ref=src_refs[t].at[:, _half(gs[t].shape[1], 1 - c), :], dst_ref=dst_refs[t], send_sem=send.at[t],
        recv_sem=recv.at[t], device_id=sibling, device_id_type=MESH) for t in range(len(gs))]


def sibling_start(gs, after, name):
    n = len(gs)
    lands = [lax.empty((g.shape[0], g.shape[1] // 2, g.shape[2]), g.dtype) for g in gs]

    def body(*refs):
        send, recv, token = refs[2 * n + 1], refs[2 * n + 2], refs[-1]
        for cp in _sibling_copies(gs, refs[:n], refs[n:2 * n], send, recv):
            cp.start()
        token[...] = jnp.zeros_like(token)

    outs = pl.pallas_call(
        body, name=name, in_specs=[HBM_SPEC] * (2 * n) + [ANY],
        out_shape=(pltpu.SemaphoreType.DMA((n,)), pltpu.SemaphoreType.DMA((n,)),
                   *[pltpu.HBM(v.shape, v.dtype) for v in gs + lands], jax.ShapeDtypeStruct(TOKEN_SHAPE, F32)),
        out_specs=(SEM_SPEC, SEM_SPEC, *[HBM_SPEC] * (2 * n), pl.BlockSpec(memory_space=pltpu.VMEM)),
        input_output_aliases={t: 2 + t for t in range(2 * n)},
        compiler_params=pltpu.CompilerParams(has_side_effects=SPLIT_EFFECT))(*[_in_hbm(v) for v in gs + lands], after)
    return outs[0], outs[1], list(outs[2:2 + n]), list(outs[2 + n:2 + 2 * n]), outs[-1]


def sibling_wait(send, recv, gs, lands, after, name):
    n = len(gs)

    def body(*refs):
        for cp in _sibling_copies(gs, refs[:n], refs[n:2 * n], refs[2 * n], refs[2 * n + 1]):
            cp.wait_send()
            cp.wait_recv()

    outs = pl.pallas_call(
        body, name=name, in_specs=[HBM_SPEC] * (2 * n) + [SEM_SPEC, SEM_SPEC, ANY],
        out_shape=tuple(pltpu.HBM(v.shape, v.dtype) for v in gs + lands), out_specs=tuple([HBM_SPEC] * (2 * n)),
        input_output_aliases={t: t for t in range(2 * n)},
        compiler_params=pltpu.CompilerParams(has_side_effects=SPLIT_EFFECT))(*gs, *lands, send, recv, after)
    return list(outs[:n]), list(outs[n:])


def _reduce_ici_copies(ps, src_refs, dst_refs, send, recv):
    x, y, c, k, sibling, chips = _place()
    cps = []
    for t in range(len(ps)):
        for j, (px, py) in enumerate(chips):
            cps.append(pltpu.make_async_remote_copy(
                src_ref=src_refs[t].at[2 * px + py], dst_ref=dst_refs[t].at[j], send_sem=send.at[3 * t + j],
                recv_sem=recv.at[3 * t + j],
                device_id=(px, py, c), device_id_type=MESH))
    return cps


def reduce_ici_start(ps, after, name):
    n = len(ps)
    lands = [lax.empty((3,) + p.shape[1:], p.dtype) for p in ps]

    def body(*refs):
        send, recv, token = refs[2 * n + 1], refs[2 * n + 2], refs[-1]
        for cp in _reduce_ici_copies(ps, refs[:n], refs[n:2 * n], send, recv):
            cp.start()
        token[...] = jnp.zeros_like(token)

    outs = pl.pallas_call(
        body, name=name, in_specs=[HBM_SPEC] * (2 * n) + [ANY],
        out_shape=(pltpu.SemaphoreType.DMA((3 * n,)), pltpu.SemaphoreType.DMA((3 * n,)),
                   *[pltpu.HBM(v.shape, v.dtype) for v in ps + lands], jax.ShapeDtypeStruct(TOKEN_SHAPE, F32)),
        out_specs=(SEM_SPEC, SEM_SPEC, *[HBM_SPEC] * (2 * n), pl.BlockSpec(memory_space=pltpu.VMEM)),
        input_output_aliases={t: 2 + t for t in range(2 * n)},
        compiler_params=pltpu.CompilerParams(has_side_effects=SPLIT_EFFECT))(*[_in_hbm(v) for v in ps + lands], after)
    return outs[0], outs[1], list(outs[2:2 + n]), list(outs[2 + n:2 + 2 * n]), outs[-1]


def _halves_copies(ss, refs, send, recv):
    x, y, c, k, sibling, chips = _place()
    return [pltpu.make_async_remote_copy(
        src_ref=refs[t].at[_half(ss[t].shape[0], c)], dst_ref=refs[t].at[_half(ss[t].shape[0], c)], send_sem=send.at[t],
        recv_sem=recv.at[t], device_id=sibling, device_id_type=MESH) for t in range(len(ss))]


def halves_start(ss, after, name):
    n = len(ss)

    def body(*refs):
        send, recv, token = refs[n + 1], refs[n + 2], refs[-1]
        for cp in _halves_copies(ss, refs[:n], send, recv):
            cp.start()
        token[...] = jnp.zeros_like(token)

    outs = pl.pallas_call(
        body, name=name, in_specs=[HBM_SPEC] * n + [ANY],
        out_shape=(pltpu.SemaphoreType.DMA((n,)), pltpu.SemaphoreType.DMA((n,)), *[pltpu.HBM(s.shape, s.dtype) for s in ss],
                   jax.ShapeDtypeStruct(TOKEN_SHAPE, F32)),
        out_specs=(SEM_SPEC, SEM_SPEC, *[HBM_SPEC] * n, pl.BlockSpec(memory_space=pltpu.VMEM)),
        input_output_aliases={t: 2 + t for t in range(n)},
        compiler_params=pltpu.CompilerParams(has_side_effects=SPLIT_EFFECT))(*[_in_hbm(s) for s in ss], after)
    return outs[0], outs[1], list(outs[2:2 + n]), outs[-1]


def halves_wait(send, recv, ss, after, name):
    n = len(ss)

    def body(*refs):
        x, y, c, k, sibling, chips = _place()
        for t in range(n):
            rows = ss[t].shape[0]
            cp = pltpu.make_async_remote_copy(
                src_ref=refs[t].at[_half(rows, c)], dst_ref=refs[t].at[_half(rows, 1 - c)], send_sem=refs[n].at[t],
                recv_sem=refs[n + 1].at[t], device_id=sibling, device_id_type=MESH)
            cp.wait_send()
            cp.wait_recv()

    return list(pl.pallas_call(
        body, name=name, in_specs=[HBM_SPEC] * n + [SEM_SPEC, SEM_SPEC, ANY],
        out_shape=tuple(pltpu.HBM(s.shape, s.dtype) for s in ss), out_specs=tuple([HBM_SPEC] * n),
        input_output_aliases={t: t for t in range(n)},
        compiler_params=pltpu.CompilerParams(has_side_effects=SPLIT_EFFECT))(*ss, send, recv, after))


def reduce_ici_wait(send, recv, ps, lands, after, name):
    n = len(ps)

    def body(*refs):
        for cp in _reduce_ici_copies(ps, refs[:n], refs[n:2 * n], refs[2 * n], refs[2 * n + 1]):
            cp.wait_send()
            cp.wait_recv()

    outs = pl.pallas_call(
        body, name=name, in_specs=[HBM_SPEC] * (2 * n) + [SEM_SPEC, SEM_SPEC, ANY],
        out_shape=tuple(pltpu.HBM(v.shape, v.dtype) for v in ps + lands), out_specs=tuple([HBM_SPEC] * (2 * n)),
        input_output_aliases={t: t for t in range(2 * n)},
        compiler_params=pltpu.CompilerParams(has_side_effects=SPLIT_EFFECT))(*ps, *lands, send, recv, after)
    return list(outs[:n]), list(outs[n:])


SMALL_CHUNK_ROWS = 256


def place_slot(v, me):
    rows = v.shape[0]

    def body(me_ref, v_ref, o_ref):
        o_ref[...] = v_ref[...]

    grid_spec = pltpu.PrefetchScalarGridSpec(
        num_scalar_prefetch=1, grid=(rows // SMALL_CHUNK_ROWS,),
        in_specs=[pl.BlockSpec((SMALL_CHUNK_ROWS, 128), lambda i, me_ref: (i, 0))],
        out_specs=pl.BlockSpec((None, SMALL_CHUNK_ROWS, 128), lambda i, me_ref: (me_ref[0], i, 0)))
    return pl.pallas_call(body, grid_spec=grid_spec, out_shape=jax.ShapeDtypeStruct((N_DEV, rows, 128), F32),
                          name="place_slot", compiler_params=_params("parallel"))(me, v)


def _slot_copies(ref, send, recv, wait):
    x, y, c = lax.axis_index("x"), lax.axis_index("y"), lax.axis_index("c")
    me = 4 * x + 2 * y + c
    cps = []
    for d in range(1, N_DEV):
        peer = (x ^ ((d >> 2) & 1), y ^ ((d >> 1) & 1), c ^ (d & 1))
        dst = ref.at[me ^ d] if wait else ref.at[me]
        cps.append(pltpu.make_async_remote_copy(src_ref=ref.at[me], dst_ref=dst, send_sem=send.at[d - 1],
                                                recv_sem=recv.at[d - 1], device_id=peer, device_id_type=MESH))
    return cps


def slots_start(buf, after, name):
    def body(buf_ref, after_ref, send, recv, thru, token):
        for cp in _slot_copies(buf_ref, send, recv, False):
            cp.start()
        token[...] = jnp.zeros_like(token)

    outs = pl.pallas_call(
        body, name=name, in_specs=[HBM_SPEC, ANY],
        out_shape=(pltpu.SemaphoreType.DMA((N_DEV - 1,)), pltpu.SemaphoreType.DMA((N_DEV - 1,)),
                   pltpu.HBM(buf.shape, buf.dtype), jax.ShapeDtypeStruct(TOKEN_SHAPE, F32)),
        out_specs=(SEM_SPEC, SEM_SPEC, HBM_SPEC, pl.BlockSpec(memory_space=pltpu.VMEM)),
        input_output_aliases={0: 2},
        compiler_params=pltpu.CompilerParams(has_side_effects=SPLIT_EFFECT))(_in_hbm(buf), after)
    return outs


def slots_wait(send, recv, buf, after, name):
    def body(buf_ref, send_ref, recv_ref, after_ref, out_ref):
        for cp in _slot_copies(buf_ref, send_ref, recv_ref, True):
            cp.wait_send()
            cp.wait_recv()

    return pl.pallas_call(
        body, name=name, in_specs=[HBM_SPEC, SEM_SPEC, SEM_SPEC, ANY], out_shape=pltpu.HBM(buf.shape, buf.dtype),
        out_specs=HBM_SPEC, input_output_aliases={0: 0},
        compiler_params=pltpu.CompilerParams(has_side_effects=SPLIT_EFFECT))(buf, send, recv, after)


def sum_slots(buf):
    rows = buf.shape[1]

    def body(b_ref, o_ref):
        acc = b_ref[0]
        for s in range(1, N_DEV):
            acc = acc + b_ref[s]
        o_ref[...] = acc

    return pl.pallas_call(
        body, grid=(rows // SMALL_CHUNK_ROWS,), in_specs=[pl.BlockSpec((N_DEV, SMALL_CHUNK_ROWS, 128), lambda i: (0, i, 0))],
        out_specs=pl.BlockSpec((SMALL_CHUNK_ROWS, 128), lambda i: (i, 0)), out_shape=jax.ShapeDtypeStruct((rows, 128), F32),
        name="sum_slots", compiler_params=_params("parallel"))(buf)


def adamw(w, g, m, v):
    rows, cols = w.shape
    tr = rows
    for cand in (512, 256, 128, 64, 32, 16, 8):
        if rows % cand == 0 and cand * cols * 4 <= (1 << 20):
            tr = cand
            break
    c1 = 1.0 - ADAM_B1 ** ADAM_STEP
    c2 = 1.0 - ADAM_B2 ** ADAM_STEP

    def body(w_ref, g_ref, m_ref, v_ref, go_ref, d_ref, nm_ref, nv_ref):
        gf = g_ref[...]
        go_ref[...] = gf
        nm = ADAM_B1 * m_ref[...] + (1.0 - ADAM_B1) * gf
        nv = ADAM_B2 * v_ref[...] + (1.0 - ADAM_B2) * (gf * gf)
        d_ref[...] = -ADAM_LR * ((nm / c1) / (jnp.sqrt(nv / c2) + ADAM_EPS) + ADAM_WD * w_ref[...])
        nm_ref[...] = nm
        nv_ref[...] = nv

    spec = pl.BlockSpec((tr, cols), lambda i: (i, 0))
    shape = jax.ShapeDtypeStruct((rows, cols), F32)
    return pl.pallas_call(body, grid=(rows // tr,), in_specs=[spec] * 4, out_specs=(spec,) * 4, out_shape=(shape,) * 4,
                          name="adamw", compiler_params=_params("parallel"))(w, g, m, v)


TAP_ROWS_ALIGN = 16
FLAT_ALIGN = 128 * SMALL_CHUNK_ROWS


def _pad_to(a, n):
    return jnp.pad(a, (0, n - a.shape[0]))


def _round_up(n, m):
    return (n + m - 1) // m * m


class Exchanges:
    def __init__(self, a, chip_core):
        self.a, self.chip_core = a, chip_core
        self.bufs = []
        for g in range(len(SUBLAYERS)):
            row = []
            for n in sublayer_weight_names(g):
                w = a[n].reshape(-1, a[n].shape[-1])
                if _kind(n) == "tap":
                    w = jnp.pad(w, ((0, _round_up(w.shape[0], TAP_ROWS_ALIGN) - w.shape[0]), (0, 0)))
                row.append(place_block(w, chip_core, F32 if _kind(n) == "tap" else BF16))
            self.bufs.append(row)
        self.started = {}
        self.forwarding = {}
        self.token = None
        self.after = chip_core
        self.stage = [None, None, None]
        self.reduced = []
        self.advanced_midway = False
        self.results = {}

    def _unpack(self, g, gathered, W):
        for n, v in zip(sublayer_weight_names(g), gathered):
            kind = _kind(n)
            if kind == "col":
                W[n] = v
            elif kind == "row":
                W[n] = v.reshape(-1, v.shape[-1])
            elif kind == "grp":
                grp, r, pg = self.a[n].shape
                W[n] = v.reshape(N_CHIPS, grp, r, pg).transpose(1, 0, 2, 3).reshape(grp, N_CHIPS * r, pg)
            else:
                nt = self.a[n].shape[0]
                W[n] = v[:, :nt].transpose(1, 0, 2).reshape(nt, -1)

    def gather_first(self, W):
        gathered = all_gather_chips(self.bufs[0])
        self._unpack(0, gathered, W)
        self.after = gathered[0]

    def forward_begins(self, g, W):
        token, self.token = self.token, None
        for h in (g + 1, g + 2, g + 3):
            if h < len(SUBLAYERS) and h not in self.started:
                send, recv, bufs, token = gather_ici_start(self.bufs[h], self.after, f"gather_start_{h}")
                self.started[h] = (send, recv, bufs)
                self.after = token
        return token

    def forward_ends(self, g, x, W):
        if g + 1 < len(SUBLAYERS):
            if g + 1 in self.forwarding:
                send, recv, bufs = self.forwarding.pop(g + 1)
                gathered = forward_wait(send, recv, bufs, x, f"forward_wait_{g + 1}")
            else:
                send, recv, bufs = self.started[g + 1]
                gathered = gather_forward_sibling(gather_ici_wait(send, recv, bufs, x, f"gather_wait_{g + 1}"))
            self._unpack(g + 1, gathered, W)
            self.after = gathered[0]
        if g + 2 < len(SUBLAYERS):
            send, recv, bufs = self.started[g + 2]
            bufs = gather_ici_wait(send, recv, bufs, x, f"gather_wait_{g + 2}")
            send, recv, bufs, self.token = forward_start(bufs, self.after, f"forward_start_{g + 2}")
            self.forwarding[g + 2] = (send, recv, bufs)
            self.after = self.token

    def gradients_ready(self, g, grads):
        names = [n for n in sublayer_weight_names(g) if _kind(n) != "tap"]
        gl = []
        for n in names:
            v, kind = grads.pop(n), _kind(n)
            if kind == "row":
                v = v.reshape(N_CHIPS, -1, v.shape[-1])
            elif kind == "grp":
                grp, r, pg = self.a[n].shape
                v = v.reshape(grp, N_CHIPS, r, pg).transpose(1, 0, 2, 3).reshape(N_CHIPS, grp * r, pg)
            gl.append(v)
        last = gl[0] if self.advanced_midway else self.advance(gl[0], gl[0])
        send, recv, gl, lands, token = sibling_start(gl, last, f"sibling_start_{g}")
        self.stage[0] = (g, names, send, recv, gl, lands)
        return token

    def midway(self, after):
        self.advanced_midway = True
        last = self.advance(after, after)
        return last if last.shape == TOKEN_SHAPE else None

    def advance(self, after, last):
        if self.stage[2] is not None:
            self.reduced.append(self.stage[2])
        self.stage[2], last = self._to_halves(self.stage[1], after, last)
        self.stage[1], last = self._to_ici(self.stage[0], after, last)
        self.stage[0] = None
        return last

    def _to_ici(self, entry, after, last):
        if entry is None:
            return None, last
        g, names, send, recv, gl, lands = entry
        gl, lands = sibling_wait(send, recv, gl, lands, after, f"sibling_wait_{g}")
        ps = [add_sibling_half(v, l, self.chip_core) for v, l in zip(gl, lands)]
        send, recv, ps, l2s, last = reduce_ici_start(ps, last, f"reduce_start_{g}")
        return (g, names, send, recv, ps, l2s), last

    def _to_halves(self, entry, after, last):
        if entry is None:
            return None, last
        g, names, send, recv, ps, l2s = entry
        ps, l2s = reduce_ici_wait(send, recv, ps, l2s, after, f"reduce_wait_{g}")
        ss = [sum_chip_blocks(p, l2, self.chip_core) for p, l2 in zip(ps, l2s)]
        send, recv, ss, last = halves_start(ss, last, f"halves_start_{g}")
        return (g, names, send, recv, ss), last

    def update_matrices(self, entry, after):
        g, names, send, recv, ss = entry
        for n, grad in zip(names, halves_wait(send, recv, ss, after, f"halves_wait_{g}")):
            shape = self.a[n].shape
            two_d = lambda v: v.reshape(-1, shape[-1])
            outs = adamw(two_d(self.a[n]), grad, two_d(self.a["m_" + n]), two_d(self.a["v_" + n]))
            self.results[n] = tuple(v.reshape(shape) for v in outs)
        return outs[0]

    def finish(self, after):
        first, last = self._to_ici(self.stage[0], after, after)
        for entry in self.reduced + [self.stage[2]]:
            if entry is not None:
                last = self.update_matrices(entry, last)
        second, last = self._to_halves(self.stage[1], last, last)
        first, last = self._to_halves(first, last, last)
        return [second, first], last


def train_step(a):
    x, positions, tgt = a["x"][0], a["positions"][0], a["loss_target"][0]
    mats = [n for n in WEIGHT_NAMES if _kind(n) in ("col", "row", "grp")]
    taps = [n for n in WEIGHT_NAMES if _kind(n) == "tap"]
    reps = [n for n in WEIGHT_NAMES if _kind(n) == "rep"]
    chip = 2 * lax.axis_index("x") + lax.axis_index("y")
    chip_core = jnp.stack([chip, lax.axis_index("c")]).astype(jnp.int32)

    comm = Exchanges(a, chip_core)
    W = {n: a[n] for n in reps}
    comm.gather_first(W)
    loss, dx, grads = local_step(x, positions, tgt, W, comm)
    loss = lax.psum(loss, ("x", "y", "c"))
    left, last = comm.finish(dx)

    n_rep = _round_up(sum(a[n].size for n in reps), FLAT_ALIGN)
    flat_rep = _pad_to(jnp.concatenate([grads[n].reshape(-1) for n in reps]), n_rep)
    flat_tap = jnp.concatenate([grads[n].reshape(-1) for n in taps])
    flat = jnp.concatenate([flat_rep, _pad_to(flat_tap, _round_up(flat_tap.shape[0], FLAT_ALIGN))])
    me = (2 * chip + lax.axis_index("c")).astype(jnp.int32).reshape(1)
    send, recv, slots, token = slots_start(place_slot(flat.reshape(-1, 128), me), last, "slots_start")
    for entry in left:
        token = comm.update_matrices(entry, token)
    summed = sum_slots(slots_wait(send, recv, slots, token, "slots_wait"))
    rep_rows = n_rep // 128
    tap_flat = summed[rep_rows:].reshape(-1)

    out = dict(comm.results)
    pack = lambda pre: _pad_to(jnp.concatenate([a[pre + n].reshape(-1) for n in reps]), n_rep).reshape(-1, 128)
    g_rep, d_rep, m_rep, v_rep = adamw(pack(""), summed[:rep_rows], pack("m_"), pack("v_"))
    off = 0
    for n in reps:
        size, shape = a[n].size, a[n].shape
        out[n] = tuple(f.reshape(-1)[off:off + size].reshape(shape) for f in (g_rep, d_rep, m_rep, v_rep))
        off += size
    off = 0
    for n in taps:
        nt, cs = a[n].shape
        full = tap_flat[off:off + nt * cs * N_CHIPS].reshape(nt, cs * N_CHIPS)
        off += nt * cs * N_CHIPS
        g = lax.dynamic_slice(full, (0, chip * cs), (nt, cs))
        out[n] = tuple(adamw(a[n], g, a["m_" + n], a["v_" + n]))

    res = [loss, dx[None]]
    for part in range(4):
        res += [out[n][part] for n in WEIGHT_NAMES]
    return tuple(res)


def kernel(x, positions, l0_norm_g, l0_a_w_in, l0_a_b_in, l0_a_dw_w, l0_a_dw_b, l0_a_ln_g, l0_a_ln_b, l0_a_w_out, l0_a_b_out, l0_ffn_norm_g, l0_ffn_w_up, l0_ffn_dw_w, l0_ffn_dw_b, l0_ffn_w_down, l1_norm_g, l1_b_w_group, l1_b_scale, l1_ffn_norm_g, l1_ffn_w_up, l1_ffn_dw_w, l1_ffn_dw_b, l1_ffn_w_down, l2_norm_g, l2_c_w_qkv, l2_c_q_norm_g, l2_c_k_norm_g, l2_c_sinks, l2_c_w_o, l2_ffn_norm_g, l2_ffn_w_up, l2_ffn_dw_w, l2_ffn_dw_b, l2_ffn_w_down, l3_norm_g, l3_a_w_in, l3_a_b_in, l3_a_dw_w, l3_a_dw_b, l3_a_ln_g, l3_a_ln_b, l3_a_w_out, l3_a_b_out, l3_ffn_norm_g, l3_ffn_w_up, l3_ffn_dw_w, l3_ffn_dw_b, l3_ffn_w_down, loss_target, m_l0_norm_g, m_l0_a_w_in, m_l0_a_b_in, m_l0_a_dw_w, m_l0_a_dw_b, m_l0_a_ln_g, m_l0_a_ln_b, m_l0_a_w_out, m_l0_a_b_out, m_l0_ffn_norm_g, m_l0_ffn_w_up, m_l0_ffn_dw_w, m_l0_ffn_dw_b, m_l0_ffn_w_down, m_l1_norm_g, m_l1_b_w_group, m_l1_b_scale, m_l1_ffn_norm_g, m_l1_ffn_w_up, m_l1_ffn_dw_w, m_l1_ffn_dw_b, m_l1_ffn_w_down, m_l2_norm_g, m_l2_c_w_qkv, m_l2_c_q_norm_g, m_l2_c_k_norm_g, m_l2_c_sinks, m_l2_c_w_o, m_l2_ffn_norm_g, m_l2_ffn_w_up, m_l2_ffn_dw_w, m_l2_ffn_dw_b, m_l2_ffn_w_down, m_l3_norm_g, m_l3_a_w_in, m_l3_a_b_in, m_l3_a_dw_w, m_l3_a_dw_b, m_l3_a_ln_g, m_l3_a_ln_b, m_l3_a_w_out, m_l3_a_b_out, m_l3_ffn_norm_g, m_l3_ffn_w_up, m_l3_ffn_dw_w, m_l3_ffn_dw_b, m_l3_ffn_w_down, v_l0_norm_g, v_l0_a_w_in, v_l0_a_b_in, v_l0_a_dw_w, v_l0_a_dw_b, v_l0_a_ln_g, v_l0_a_ln_b, v_l0_a_w_out, v_l0_a_b_out, v_l0_ffn_norm_g, v_l0_ffn_w_up, v_l0_ffn_dw_w, v_l0_ffn_dw_b, v_l0_ffn_w_down, v_l1_norm_g, v_l1_b_w_group, v_l1_b_scale, v_l1_ffn_norm_g, v_l1_ffn_w_up, v_l1_ffn_dw_w, v_l1_ffn_dw_b, v_l1_ffn_w_down, v_l2_norm_g, v_l2_c_w_qkv, v_l2_c_q_norm_g, v_l2_c_k_norm_g, v_l2_c_sinks, v_l2_c_w_o, v_l2_ffn_norm_g, v_l2_ffn_w_up, v_l2_ffn_dw_w, v_l2_ffn_dw_b, v_l2_ffn_w_down, v_l3_norm_g, v_l3_a_w_in, v_l3_a_b_in, v_l3_a_dw_w, v_l3_a_dw_b, v_l3_a_ln_g, v_l3_a_ln_b, v_l3_a_w_out, v_l3_a_b_out, v_l3_ffn_norm_g, v_l3_ffn_w_up, v_l3_ffn_dw_w, v_l3_ffn_dw_b, v_l3_ffn_w_down):
    return train_step(dict(locals()))
```
